```python
import jax, jax.numpy as jnp
from jax import lax
import numpy as np

D_MODEL = 1024
BATCH = 16
SEQ = 256
DEPTH = 2
DEC_BATCH = 4
DEC_SEQ = 1024
PAST_LEN = 512

GRID_W = 64
WIN_R = 8
WIN_C = 16
H_A = 8
HD_A = 64
H_B = 4
HD_B = 128
H_C = 4
HD_C = 128
N_FG = 4
FG_W = 128
W_A = H_A * HD_A
W_B = H_B * HD_B
W_C = H_C * HD_C
W_D = N_FG * FG_W
EVEN_IN = 3 * W_A + 4 * W_B + 4 * H_B
ODD_IN = 4 * W_C + W_D
N_EXP = 32
TOP_K = 4
D_FF = D_MODEL
SWIGLU_LIMIT = 7.0
SWIGLU_ALPHA = 1.702
CHUNK = 128
QBLK = 128
ROPE_BASE = 10000.0
EPS = 1e-6
N_EVEN = (DEPTH + 1) // 2
N_ODD = DEPTH // 2

kernel_name = "hybrid_diffusion_na_mlstm_retention_fnet_moe_step"

F32 = jnp.float32


def rms_norm(x):
    xf = x.astype(F32)
    return (xf * lax.rsqrt(jnp.mean(xf * xf, -1, keepdims=True) + EPS)).astype(x.dtype)


def head_rms(x, w):
    xf = x.astype(F32)
    y = xf * lax.rsqrt(jnp.mean(xf * xf, -1, keepdims=True) + EPS) * w.astype(F32)
    return y.astype(x.dtype)


def split_cols(z, sizes):
    return jnp.split(z, np.cumsum(sizes)[:-1].tolist(), axis=-1)


def flip(a):
    return jnp.flip(a, axis=1)


def modulation(cond, w, b):
    m = jax.nn.silu(cond) @ w + b
    return jnp.split(m[..., None, :], 6, axis=-1)


def axial_rope(x):
    S, Dh = x.shape[1], x.shape[-1]
    t = jnp.arange(S)
    half = Dh // 2
    inv = ROPE_BASE ** (-jnp.arange(0, half, 2, dtype=F32) / half)
    xf = x.astype(F32)

    def rot(xa, pos):
        ang = pos.astype(F32)[:, None] * inv[None, :]
        cos = jnp.cos(ang)[None, :, None, :]
        sin = jnp.sin(ang)[None, :, None, :]
        x1, x2 = xa[..., : half // 2], xa[..., half // 2:]
        return jnp.concatenate([x1 * cos - x2 * sin, x1 * sin + x2 * cos], -1)

    y = jnp.concatenate([rot(xf[..., :half], t // GRID_W), rot(xf[..., half:], t % GRID_W)], -1)
    return y.astype(x.dtype)


def dense_attention(q, k, v):
    B, S, H, Dh = q.shape
    qb = jnp.moveaxis(q.reshape(B, S // QBLK, QBLK, H, Dh), 1, 0)

    def blk(qi):
        s = jnp.einsum('bqhd,blhd->bhql', qi, k).astype(F32) * (Dh ** -0.5)
        p = jax.nn.softmax(s, axis=-1).astype(v.dtype)
        return jnp.einsum('bhql,blhd->bqhd', p, v)

    o = lax.map(blk, qb)
    return jnp.moveaxis(o, 0, 1).reshape(B, S, H, Dh)


def na_latent(q, k, v, kc, vc, rpb):
    B, S, H, Dh = q.shape
    rows = S // GRID_W
    wr = min(WIN_R, rows)
    ncb = GRID_W // WIN_C
    band = 2 * WIN_C
    qcols = np.arange(GRID_W).reshape(ncb, WIN_C)
    bstart = np.clip(np.arange(ncb) * WIN_C - WIN_C // 2, 0, GRID_W - band)
    kcols = bstart[:, None] + np.arange(band)
    cstart = np.clip(qcols - WIN_C // 2, 0, GRID_W - WIN_C)
    col_in = jnp.asarray((kcols[:, None, :] >= cstart[..., None]) & (kcols[:, None, :] < cstart[..., None] + WIN_C))
    dc = jnp.asarray(np.clip(kcols[:, None, :] - qcols[..., None], 1 - WIN_C, WIN_C - 1) + WIN_C - 1)
    qg = q.reshape(B, rows, ncb, WIN_C, H, Dh)
    kg = k.reshape(B, rows, GRID_W, H, Dh)
    vg = v.reshape(B, rows, GRID_W, H, Dh)
    scale = Dh ** -0.5
    n_loc = wr * band

    def row_fn(r):
        rs = jnp.clip(r - wr // 2, 0, rows - wr)
        qr = lax.dynamic_index_in_dim(qg, r, axis=1, keepdims=False)
        kb = lax.dynamic_slice_in_dim(kg, rs, wr, axis=1)[:, :, kcols]
        vb = lax.dynamic_slice_in_dim(vg, rs, wr, axis=1)[:, :, kcols]
        dr = rs + jnp.arange(wr) - r + WIN_R - 1
        bias = rpb[:, dr[:, None, None, None], dc[None]].transpose(0, 2, 3, 1, 4)
        s_loc = jnp.einsum('bjchd,bsjwhd->bhjcsw', qr, kb).astype(F32) * scale + bias[None]
        s_loc = jnp.where(col_in[None, None, :, :, None, :], s_loc, -jnp.inf)
        s_ctx = jnp.einsum('bjchd,blhd->bhjcl', qr, kc).astype(F32) * scale
        s = jnp.concatenate([s_loc.reshape(B, H, ncb, WIN_C, n_loc), s_ctx], axis=-1)
        p = jax.nn.softmax(s, axis=-1).astype(v.dtype)
        p_loc = p[..., :n_loc].reshape(B, H, ncb, WIN_C, wr, band)
        o = jnp.einsum('bhjcsw,bsjwhd->bjchd', p_loc, vb) + jnp.einsum('bhjcl,blhd->bjchd', p[..., n_loc:], vc)
        return o.reshape(B, GRID_W, H, Dh)

    o = lax.map(row_fn, jnp.arange(rows))
    return jnp.moveaxis(o, 0, 1).reshape(B, S, H, Dh)


def mlstm_scan(q, k, v, ig, fg, init):
    B, S, H, _ = q.shape
    nc = S // CHUNK
    tri = jnp.tril(jnp.ones((CHUNK, CHUNK), dtype=bool))

    def chunks(a):
        return jnp.moveaxis(a.astype(F32).reshape(B, nc, CHUNK, *a.shape[2:]), 1, 0)

    def step(carry, inp):
        C, n, m = carry
        qc, kc, vc, ic, fc = inp
        b = jnp.cumsum(jax.nn.log_sigmoid(fc), axis=1).transpose(0, 2, 1)
        ii = ic.transpose(0, 2, 1)
        dmat = jnp.where(tri, b[..., :, None] - b[..., None, :] + ii[..., None, :], -jnp.inf)
        inter = b + m[..., None]
        mt = jnp.maximum(inter, dmat.max(-1))
        w = jnp.exp(dmat - mt[..., None]) * jnp.einsum('blhd,bshd->bhls', qc, kc)
        a = jnp.exp(inter - mt)
        num = jnp.einsum('bhls,bshv->blhv', w, vc) + jnp.einsum('blhk,bhkv->blhv', qc, C) * a.transpose(0, 2, 1)[..., None]
        den = w.sum(-1) + a * jnp.einsum('blhk,bhk->bhl', qc, n)
        hc = num / jnp.maximum(jnp.abs(den), jnp.exp(-mt)).transpose(0, 2, 1)[..., None]
        bl = b[..., -1]
        dl = bl[..., None] - b + ii
        m_new = jnp.maximum(bl + m, dl.max(-1))
        wl = jnp.exp(dl - m_new[..., None])
        dec = jnp.exp(bl + m - m_new)
        C_new = dec[..., None, None] * C + jnp.einsum('bhs,bshk,bshv->bhkv', wl, kc, vc)
        n_new = dec[..., None] * n + jnp.einsum('bhs,bshk->bhk', wl, kc)
        return (C_new, n_new, m_new), hc

    init = (init[0].astype(F32), init[1].astype(F32), init[2].astype(F32))
    xs = [chunks(q), chunks(k), chunks(v), chunks(ig), chunks(fg)]
    fin, hs = lax.scan(step, init, xs)
    h = jnp.moveaxis(hs, 0, 1).reshape(B, S, H, v.shape[-1]).astype(q.dtype)
    return h, fin


def retention_scan(q, k, v, log_gamma, S0):
    B, S, H, _ = q.shape
    nc = S // CHUNK
    pos = jnp.arange(CHUNK, dtype=F32)
    diff = pos[:, None] - pos[None, :]
    decay = jnp.exp(jnp.where(diff >= 0, diff[None] * log_gamma[:, None, None], -jnp.inf))
    q_dec = jnp.exp((pos[None] + 1.0) * log_gamma[:, None]).T
    k_dec = jnp.exp((CHUNK - 1.0 - pos)[None] * log_gamma[:, None])
    c_dec = jnp.exp(CHUNK * log_gamma)

    def chunks(a):
        return jnp.moveaxis(a.astype(F32).reshape(B, nc, CHUNK, *a.shape[2:]), 1, 0)

    def step(St, inp):
        qc, kc, vc = inp
        att = jnp.einsum('blhd,bshd->bhls', qc, kc) * decay[None]
        o = jnp.einsum('bhls,bshv->blhv', att, vc) + jnp.einsum('blhk,bhkv->blhv', qc, St) * q_dec[None, :, :, None]
        S_new = c_dec[None, :, None, None] * St + jnp.einsum('bshk,hs,bshv->bhkv', kc, k_dec, vc)
        return S_new, o

    fin, os_ = lax.scan(step, S0.astype(F32), [chunks(q), chunks(k), chunks(v)])
    o = jnp.moveaxis(os_, 0, 1).reshape(B, S, H, v.shape[-1]).astype(q.dtype)
    return o, fin


def mixer_even(h, w_in, gate_b, qn_w, kn_w, rpb, mnorm_w, w_out, ctx):
    B, S, _ = h.shape
    qa, ka, va, qb, kb, vb, ob, gt = split_cols(h @ w_in, [W_A] * 3 + [W_B] * 4 + [4 * H_B])
    qa = head_rms(qa.reshape(B, S, H_A, HD_A), qn_w)
    ka = head_rms(ka.reshape(B, S, H_A, HD_A), kn_w)
    va = va.reshape(B, S, H_A, HD_A)
    qb = qb.reshape(B, S, H_B, HD_B)
    kb = kb.reshape(B, S, H_B, HD_B) * (HD_B ** -0.5)
    vb = vb.reshape(B, S, H_B, HD_B)
    gt = gt.reshape(B, S, 4, H_B) + gate_b
    if ctx is None:
        oa = dense_attention(qa, ka, va)
        zC = jnp.zeros((B, H_B, HD_B, HD_B), F32)
        zn = jnp.zeros((B, H_B, HD_B), F32)
        zm = jnp.zeros((B, H_B), F32)
        init_f = (zC, zn, zm)
        init_b = (zC, zn, zm)
    else:
        kc, vc, C, n, m = ctx
        oa = na_latent(qa, ka, va, kc, vc, rpb)
        init_f = (C[:, 0], n[:, 0], m[:, 0])
        init_b = (C[:, 1], n[:, 1], m[:, 1])
    hf, fin_f = mlstm_scan(qb, kb, vb, gt[:, :, 0], gt[:, :, 1], init_f)
    hb, fin_b = mlstm_scan(flip(qb), flip(kb), flip(vb), flip(gt[:, :, 2]), flip(gt[:, :, 3]), init_b)
    hm = head_rms(hf + flip(hb), mnorm_w.reshape(H_B, HD_B)) * jax.nn.sigmoid(ob).reshape(B, S, H_B, HD_B)
    y = jnp.concatenate([oa.reshape(B, S, W_A), hm.reshape(B, S, W_B)], axis=-1) @ w_out
    if ctx is not None:
        return y, None
    new = (ka, va,
           jnp.stack([fin_f[0], fin_b[0]], 1), jnp.stack([fin_f[1], fin_b[1]], 1), jnp.stack([fin_f[2], fin_b[2]], 1))
    return y, new


def mixer_odd(h, w_in, decay_logit, rnorm_w, w_out, ctx):
    B, S, _ = h.shape
    qc, kc, vc, gc, xd = split_cols(h @ w_in, [W_C] * 4 + [W_D])
    q = qc.reshape(B, S, H_C, HD_C)
    k = kc.reshape(B, S, H_C, HD_C) * (HD_C ** -0.5)
    v = vc.reshape(B, S, H_C, HD_C)
    lg = jax.nn.log_sigmoid(decay_logit.astype(F32))
    if ctx is None:
        S0f = jnp.zeros((B, H_C, HD_C, HD_C), F32)
        S0b = S0f
    else:
        q = axial_rope(q)
        k = axial_rope(k)
        S0f, S0b = ctx[:, 0], ctx[:, 1]
    of, Sf = retention_scan(q, k, v, lg[0], S0f)
    ob, Sb = retention_scan(flip(q), flip(k), flip(v), lg[1], S0b)
    hr = head_rms(of + flip(ob), rnorm_w.reshape(H_C, HD_C)) * jax.nn.silu(gc).reshape(B, S, H_C, HD_C)
    xf = xd.reshape(B, S, N_FG, FG_W).astype(F32)
    fd = jnp.fft.fft2(xf, axes=(1, 3), norm='ortho').real.astype(h.dtype)
    y = jnp.concatenate([hr.reshape(B, S, W_C), fd.reshape(B, S, W_D)], axis=-1) @ w_out
    if ctx is not None:
        return y, None
    return y, jnp.stack([Sf, Sb], 1)


def moe(h, w_r, b_r, w1, b1, w2, b2):
    B, S, D = h.shape
    t = h.reshape(B * S, D)
    logits = (t @ w_r + b_r).astype(F32)
    top_v, top_i = lax.top_k(logits, TOP_K)
    comb = jnp.sum(jax.nn.one_hot(top_i, N_EXP, dtype=F32) * jax.nn.softmax(top_v, axis=-1)[..., None], axis=1).astype(t.dtype)
    y = jnp.zeros_like(t)
    for e in range(N_EXP):
        u = t @ w1[e] + b1[e]
        g = jnp.minimum(u[:, :D_FF], SWIGLU_LIMIT)
        up = jnp.clip(u[:, D_FF:], -SWIGLU_LIMIT, SWIGLU_LIMIT)
        act = (up + 1.0) * g * jax.nn.sigmoid(SWIGLU_ALPHA * g)
        y = y + comb[:, e:e + 1] * (act @ w2[e] + b2[e])
    return y.reshape(B, S, D)


def trunk(x, cond, P, cache):
    outs = []
    for l in range(DEPTH):
        sh1, sc1, g1, sh2, sc2, g2 = modulation(cond, P['w_mod'][l], P['b_mod'][l])
        h = rms_norm(x) * (1 + sc1) + sh1
        e = l // 2
        if l % 2 == 0:
            ctx = None if cache is None else (cache['na_k'][:, e], cache['na_v'][:, e], cache['C'][:, e], cache['n'][:, e], cache['m'][:, e])
            y, new = mixer_even(h, P['w_in_even'][e], P['mlstm_gate_b'][e], P['na_q_norm'][e], P['na_k_norm'][e],
                                P['na_rpb'][e], P['mlstm_norm'][e], P['w_out_even'][e], ctx)
        else:
            ctx = None if cache is None else cache['S'][:, e]
            y, new = mixer_odd(h, P['w_in_odd'][e], P['ret_decay'][e], P['ret_norm'][e], P['w_out_odd'][e], ctx)
        outs.append(new)
        x = x + g1 * y
        h = rms_norm(x) * (1 + sc2) + sh2
        x = x + g2 * moe(h, P['w_router'][l], P['b_router'][l], P['w_moe_in'][l], P['b_moe_in'][l],
                         P['w_moe_out'][l], P['b_moe_out'][l])
    return x, outs


def setup_inputs(seed: int = 0) -> dict:
    key = jax.random.key(seed)
    ks = iter(jax.random.split(key, 32))

    def nrm(shape, scale):
        return jax.random.normal(next(ks), shape, F32) * scale

    f_bias = jnp.linspace(3.0, 6.0, H_B)
    zb = jnp.zeros((H_B,), F32)
    gate_base = jnp.stack([zb, f_bias, zb, f_bias])[None]
    decay_base = jnp.log(2.0 ** (5.0 + jnp.arange(H_C, dtype=F32)) - 1.0)[None, None]
    return {
        'x_prompt': nrm((BATCH, SEQ, D_MODEL), 1.0),
        'x_sample': nrm((DEC_BATCH, DEC_SEQ, D_MODEL), 1.0),
        'cache_na_k': nrm((DEC_BATCH, N_EVEN, PAST_LEN, H_A, HD_A), 1.0),
        'cache_na_v': nrm((DEC_BATCH, N_EVEN, PAST_LEN, H_A, HD_A), 1.0),
        'state_mlstm_C': nrm((DEC_BATCH, N_EVEN, 2, H_B, HD_B, HD_B), 0.1),
        'state_mlstm_n': nrm((DEC_BATCH, N_EVEN, 2, H_B, HD_B), 0.1),
        'state_mlstm_m': nrm((DEC_BATCH, N_EVEN, 2, H_B), 1.0),
        'state_ret_S': nrm((DEC_BATCH, N_ODD, 2, H_C, HD_C, HD_C), 0.3),
        'c': nrm((DEC_BATCH, D_MODEL), 1.0),
        'c_ctx': nrm((D_MODEL,), 1.0),
        'w_mod': nrm((DEPTH, D_MODEL, 6 * D_MODEL), 0.5 * D_MODEL ** -0.5),
        'b_mod': nrm((DEPTH, 6 * D_MODEL), 0.02),
        'w_in_even': nrm((N_EVEN, D_MODEL, EVEN_IN), D_MODEL ** -0.5),
        'mlstm_gate_b': gate_base + nrm((N_EVEN, 4, H_B), 0.1),
        'na_q_norm': 1.0 + nrm((N_EVEN, HD_A), 0.02),
        'na_k_norm': 1.0 + nrm((N_EVEN, HD_A), 0.02),
        'na_rpb': nrm((N_EVEN, H_A, 2 * WIN_R - 1, 2 * WIN_C - 1), 0.1),
        'mlstm_norm': 1.0 + nrm((N_EVEN, W_B), 0.02),
        'w_out_even': nrm((N_EVEN, W_A + W_B, D_MODEL), (W_A + W_B) ** -0.5),
        'w_in_odd': nrm((N_ODD, D_MODEL, ODD_IN), D_MODEL ** -0.5),
        'ret_decay': decay_base + nrm((N_ODD, 2, H_C), 0.1),
        'ret_norm': 1.0 + nrm((N_ODD, W_C), 0.02),
        'w_out_odd': nrm((N_ODD, W_C + W_D, D_MODEL), (W_C + W_D) ** -0.5),
        'w_router': nrm((DEPTH, D_MODEL, N_EXP), D_MODEL ** -0.5),
        'b_router': nrm((DEPTH, N_EXP), 0.01),
        'w_moe_in': nrm((DEPTH, N_EXP, D_MODEL, 2 * D_FF), D_MODEL ** -0.5),
        'b_moe_in': nrm((DEPTH, N_EXP, 2 * D_FF), 0.02),
        'w_moe_out': nrm((DEPTH, N_EXP, D_FF, D_MODEL), D_FF ** -0.5),
        'b_moe_out': nrm((DEPTH, N_EXP, D_MODEL), 0.02),
    }


def reference(x_prompt, x_sample, cache_na_k, cache_na_v, state_mlstm_C, state_mlstm_n, state_mlstm_m, state_ret_S,
              c, c_ctx, w_mod, b_mod, w_in_even, mlstm_gate_b, na_q_norm, na_k_norm, na_rpb, mlstm_norm, w_out_even,
              w_in_odd, ret_decay, ret_norm, w_out_odd, w_router, b_router, w_moe_in, b_moe_in, w_moe_out, b_moe_out):
    P = {'w_mod': w_mod, 'b_mod': b_mod, 'w_in_even': w_in_even, 'mlstm_gate_b': mlstm_gate_b,
         'na_q_norm': na_q_norm, 'na_k_norm': na_k_norm, 'na_rpb': na_rpb, 'mlstm_norm': mlstm_norm,
         'w_out_even': w_out_even, 'w_in_odd': w_in_odd, 'ret_decay': ret_decay, 'ret_norm': ret_norm,
         'w_out_odd': w_out_odd, 'w_router': w_router, 'b_router': b_router, 'w_moe_in': w_moe_in,
         'b_moe_in': b_moe_in, 'w_moe_out': w_moe_out, 'b_moe_out': b_moe_out}
    y_prompt, outs = trunk(x_prompt, c_ctx, P, None)
    dt = x_prompt.dtype
    even = [outs[l] for l in range(0, DEPTH, 2)]
    odd = [outs[l] for l in range(1, DEPTH, 2)]
    new_na_k = jnp.stack([s[0] for s in even], 1).astype(dt)
    new_na_v = jnp.stack([s[1] for s in even], 1).astype(dt)
    new_mlstm_C = jnp.stack([s[2] for s in even], 1).astype(dt)
    new_mlstm_n = jnp.stack([s[3] for s in even], 1).astype(dt)
    new_mlstm_m = jnp.stack([s[4] for s in even], 1).astype(dt)
    new_ret_S = jnp.stack(odd, 1).astype(dt)
    cache = {'na_k': cache_na_k, 'na_v': cache_na_v, 'C': state_mlstm_C, 'n': state_mlstm_n,
             'm': state_mlstm_m, 'S': state_ret_S}
    y_sample, _ = trunk(x_sample, c, P, cache)
    return (y_prompt, y_sample, new_na_k, new_na_v, new_mlstm_C, new_mlstm_n, new_mlstm_m, new_ret_S)
```

```python
import functools
import math

import numpy as np
import jax
import jax.numpy as jnp
from jax import lax
from jax.experimental import pallas as pl
from jax.experimental.pallas import tpu as pltpu

F32 = jnp.float32
BF16 = jnp.bfloat16
HIGHEST = lax.Precision.HIGHEST

D_MODEL = 1024
GRID_W = 64
WIN_R = 8
WIN_C = 16
H_A, HD_A = 8, 64
H_B, HD_B = 4, 128
H_C, HD_C = 4, 128
N_FG, FG_W = 4, 128
W_A = H_A * HD_A
W_B = H_B * HD_B
N_EXP = 32
TOP_K = 4
D_FF = D_MODEL
SWIGLU_LIMIT = 7.0
SWIGLU_ALPHA = 1.702
CHUNK = 128
ROPE_BASE = 10000.0
EPS = 1e-6

LANES = 128
SEG = 1024
N_COND = 8
TM = 256
TM_E = 256
NEG = -1e30
VMEM_LIMIT = 56 * 1024 * 1024


def _cparams(*sem):
    return pltpu.CompilerParams(dimension_semantics=sem, vmem_limit_bytes=VMEM_LIMIT)


def _cond_row(i):
    return jnp.maximum((i * TM) // SEG - 3, 0)


def _log_sigmoid(x):
    return jnp.minimum(x, 0.0) - jnp.log1p(jnp.exp(-jnp.abs(x)))


def _dot(a, b):
    return jnp.dot(a, b, preferred_element_type=F32)


def _dot_nt(a, b):
    return lax.dot_general(a, b, (((1,), (1,)), ((), ())), preferred_element_type=F32)


def _dot_hi(a, b):
    return jnp.dot(a, b, precision=HIGHEST, preferred_element_type=F32)


def _mod_kernel(cond_ref, w_ref, b_ref, o_ref):
    c = cond_ref[...]
    s = c * jax.nn.sigmoid(c)
    o_ref[...] = _dot_hi(s, w_ref[...]) + b_ref[...]


def _modulation(cond, w_mod, b_mod):
    depth, d, n = w_mod.shape
    tn = 1536
    return pl.pallas_call(
        _mod_kernel,
        grid=(depth, n // tn),
        in_specs=[pl.BlockSpec((N_COND, d), lambda l, j: (0, 0)),
                  pl.BlockSpec((None, d, tn), lambda l, j: (l, 0, j)),
                  pl.BlockSpec((None, 1, tn), lambda l, j: (l, 0, j))],
        out_specs=pl.BlockSpec((None, N_COND, tn), lambda l, j: (l, 0, j)),
        out_shape=jax.ShapeDtypeStruct((depth, N_COND, n), F32),
        compiler_params=_cparams("arbitrary", "arbitrary"),
        name="modulation",
    )(cond, w_mod, b_mod.reshape(depth, 1, n))


def _rms_mod(x, shift, scale):
    h = x * lax.rsqrt(jnp.mean(x * x, axis=-1, keepdims=True) + EPS)
    return h * (1.0 + scale) + shift


def _inproj_kernel(x_ref, mod_ref, w_ref, z_ref):
    h = _rms_mod(x_ref[...], mod_ref[0:1, :], mod_ref[1:2, :])
    z_ref[...] = _dot(h.astype(BF16), w_ref[...])


def _inproj_gate_kernel(x_ref, mod_ref, w_ref, wg_ref, bg_ref, z_ref, g_ref):
    h = _rms_mod(x_ref[...], mod_ref[0:1, :], mod_ref[1:2, :])
    z_ref[...] = _dot(h.astype(BF16), w_ref[...])
    g_ref[...] = _dot_hi(h, wg_ref[...]) + bg_ref[...]


def _inproj(x, mod, w, wg=None, bg=None):
    t, d = x.shape
    n = w.shape[1]
    in_specs = [pl.BlockSpec((TM, d), lambda i: (i, 0)),
                pl.BlockSpec((None, 6, d), lambda i: (_cond_row(i), 0, 0)),
                pl.BlockSpec((d, n), lambda i: (0, 0))]
    z_spec = pl.BlockSpec((TM, n), lambda i: (i, 0))
    z_shape = jax.ShapeDtypeStruct((t, n), F32)
    if wg is None:
        return pl.pallas_call(
            _inproj_kernel, grid=(t // TM,), in_specs=in_specs, out_specs=z_spec, out_shape=z_shape,
            compiler_params=_cparams("arbitrary"), name="inproj",
        )(x, mod, w)
    in_specs += [pl.BlockSpec((d, LANES), lambda i: (0, 0)), pl.BlockSpec((1, LANES), lambda i: (0, 0))]
    return pl.pallas_call(
        _inproj_gate_kernel, grid=(t // TM,), in_specs=in_specs,
        out_specs=[z_spec, pl.BlockSpec((TM, LANES), lambda i: (i, 0))],
        out_shape=[z_shape, jax.ShapeDtypeStruct((t, LANES), F32)],
        compiler_params=_cparams("arbitrary"), name="inproj_gate",
    )(x, mod, w, wg, bg)


def _head_rms(x, w):
    return x * lax.rsqrt(jnp.mean(x * x, axis=-1, keepdims=True) + EPS) * w


def _ctx_attn_kernel(q_ref, k_ref, v_ref, qn_ref, kn_ref, o_ref, ko_ref):
    scale = HD_A ** -0.5
    for h in range(H_A):
        sl = slice(h * HD_A, (h + 1) * HD_A)
        q = _head_rms(q_ref[:, sl], qn_ref[...])
        k = _head_rms(k_ref[:, sl], kn_ref[...])
        ko_ref[:, sl] = k
        s = _dot_nt(q.astype(BF16), k.astype(BF16)) * scale
        p = jnp.exp(s - jnp.max(s, axis=-1, keepdims=True))
        den = jnp.sum(p, axis=-1, keepdims=True)
        o_ref[:, sl] = _dot(p.astype(BF16), v_ref[:, sl].astype(BF16)) / den


def _ctx_attention(z, n_batch, seq, qn, kn):
    spec = lambda c: pl.BlockSpec((seq, W_A), lambda b: (b, c))
    wspec = pl.BlockSpec((1, HD_A), lambda b: (0, 0))
    out = jax.ShapeDtypeStruct((n_batch * seq, W_A), F32)
    return pl.pallas_call(
        _ctx_attn_kernel, grid=(n_batch,),
        in_specs=[spec(0), spec(1), spec(2), wspec, wspec],
        out_specs=[pl.BlockSpec((seq, W_A), lambda b: (b, 0))] * 2,
        out_shape=[out, out],
        compiler_params=_cparams("arbitrary"), name="ctx_attention",
    )(z, z, z, qn, kn)


def _na_bias_table(rpb):
    qc = np.arange(GRID_W)
    kc = np.arange(GRID_W)
    cstart = np.clip(qc - WIN_C // 2, 0, GRID_W - WIN_C)
    col_in = (kc[None, :] >= cstart[:, None]) & (kc[None, :] < cstart[:, None] + WIN_C)
    dc = np.clip(kc[None, :] - qc[:, None], 1 - WIN_C, WIN_C - 1) + WIN_C - 1
    cls = np.arange(WIN_R)
    j = np.arange(WIN_R)
    dr = j[None, :] - cls[:, None] + WIN_R - 1
    tab = rpb[:, dr[:, :, None, None], dc[None, None, :, :]]
    tab = jnp.where(jnp.asarray(col_in)[None, None, None], tab, NEG)
    tab = tab.transpose(0, 1, 3, 2, 4)
    return tab.reshape(H_A, WIN_R, GRID_W, WIN_R * GRID_W)


def _na_kernel(q_ref, k_ref, v_ref, kc_ref, vc_ref, bias_ref, qn_ref, kn_ref, o_ref, kn_s, *, rows):
    r = pl.program_id(1)
    scale = HD_A ** -0.5

    @pl.when(r == 0)
    def _():
        for h in range(H_A):
            sl = slice(h * HD_A, (h + 1) * HD_A)
            kn_s[:, sl] = _head_rms(k_ref[:, sl], kn_ref[...])

    rs = jnp.clip(r - WIN_R // 2, 0, rows - WIN_R)
    start = pl.multiple_of(rs * GRID_W, GRID_W)
    n_loc = WIN_R * GRID_W
    for h in range(H_A):
        sl = slice(h * HD_A, (h + 1) * HD_A)
        q = _head_rms(q_ref[:, sl], qn_ref[...]).astype(BF16)
        kl = kn_s[pl.ds(start, n_loc), sl].astype(BF16)
        vl = v_ref[pl.ds(start, n_loc), sl].astype(BF16)
        s_loc = _dot_nt(q, kl) * scale + bias_ref[h]
        s_ctx = _dot_nt(q, kc_ref[:, sl].astype(BF16)) * scale
        m = jnp.maximum(jnp.max(s_loc, axis=-1, keepdims=True), jnp.max(s_ctx, axis=-1, keepdims=True))
        p_loc = jnp.exp(s_loc - m)
        p_ctx = jnp.exp(s_ctx - m)
        den = jnp.sum(p_loc, axis=-1, keepdims=True) + jnp.sum(p_ctx, axis=-1, keepdims=True)
        o = _dot(p_loc.astype(BF16), vl) + _dot(p_ctx.astype(BF16), vc_ref[:, sl].astype(BF16))
        o_ref[:, sl] = o / den


def _na_attention(z, row0, n_batch, seq, kc, vc, bias, qn, kn):
    rows = seq // GRID_W
    past = kc.shape[1]
    blk0 = row0 // GRID_W
    sblk0 = row0 // seq

    def cls_of(r):
        return r - jnp.clip(r - WIN_R // 2, 0, rows - WIN_R)

    full = lambda c: pl.BlockSpec((seq, W_A), lambda b, r: (sblk0 + b, c))
    cspec = pl.BlockSpec((None, past, W_A), lambda b, r: (b, 0, 0))
    wspec = pl.BlockSpec((1, HD_A), lambda b, r: (0, 0))
    return pl.pallas_call(
        functools.partial(_na_kernel, rows=rows), grid=(n_batch, rows),
        in_specs=[pl.BlockSpec((GRID_W, W_A), lambda b, r: (blk0 + b * rows + r, 0)),
                  full(1), full(2), cspec, cspec,
                  pl.BlockSpec((H_A, None, GRID_W, WIN_R * GRID_W), lambda b, r: (0, cls_of(r), 0, 0)),
                  wspec, wspec],
        out_specs=pl.BlockSpec((GRID_W, W_A), lambda b, r: (b * rows + r, 0)),
        out_shape=jax.ShapeDtypeStruct((n_batch * seq, W_A), F32),
        scratch_shapes=[pltpu.VMEM((seq, W_A), F32)],
        compiler_params=_cparams("arbitrary", "arbitrary"), name="na_attention",
    )(z, z, z, kc, vc, bias, qn, kn)


def _tri_masks():
    li = lax.broadcasted_iota(jnp.int32, (CHUNK, CHUNK), 0)
    si = lax.broadcasted_iota(jnp.int32, (CHUNK, CHUNK), 1)
    return li >= si, li <= si


def _mlstm_kernel(*refs, nc, has_init):
    if has_init:
        (q_ref, k_ref, v_ref, og_ref, g_ref, nw_ref, c0_ref, n0_ref, m0_ref,
         o_ref, cf_ref, nf_ref, mf_ref, h_s, c_s, n_s, m_s) = refs
    else:
        (q_ref, k_ref, v_ref, og_ref, g_ref, nw_ref,
         o_ref, cf_ref, nf_ref, mf_ref, h_s, c_s, n_s, m_s) = refs
    nd = 2 * H_B
    if has_init:
        c_s[...] = c0_ref[...]
        n_s[...] = n0_ref[...]
        m_s[...] = m0_ref[...]
    else:
        c_s[...] = jnp.zeros_like(c_s)
        n_s[...] = jnp.zeros_like(n_s)
        m_s[...] = jnp.zeros_like(m_s)

    causal, anti = _tri_masks()
    tri_f = causal.astype(F32)
    tri_b = anti.astype(F32)
    kscale = HD_B ** -0.5

    def chunk_step(c, carry):
        for d in range(2):
            cc = c if d == 0 else nc - 1 - c
            t0 = pl.multiple_of(cc * CHUNK, CHUNK)
            g = g_ref[pl.ds(t0, CHUNK), :]
            gt = g.T
            ls = _log_sigmoid(g)
            lst = _log_sigmoid(gt)
            tri_c, tri_r, mask = (tri_f, tri_b, causal) if d == 0 else (tri_b, tri_f, anti)
            b_cols = _dot_hi(tri_c, ls)
            b_rows = _dot_hi(lst, tri_r)
            last = CHUNK - 1 if d == 0 else 0
            for h in range(H_B):
                ci = (2 * d) * H_B + h
                cf = (2 * d + 1) * H_B + h
                hs = slice(h * HD_B, (h + 1) * HD_B)
                q = q_ref[pl.ds(t0, CHUNK), hs]
                k = k_ref[pl.ds(t0, CHUNK), hs] * kscale
                v = v_ref[pl.ds(t0, CHUNK), hs]
                qb, kb, vb = q.astype(BF16), k.astype(BF16), v.astype(BF16)
                b_col = b_cols[:, cf:cf + 1]
                b_row = b_rows[cf:cf + 1, :]
                i_row = gt[ci:ci + 1, :]
                sidx = d * H_B + h
                cst = c_s[sidx]
                nst = n_s[sidx:sidx + 1, :]
                mst = m_s[sidx:sidx + 1, 0:1]
                dmat = jnp.where(mask, b_col - b_row + i_row, -jnp.inf)
                inter = b_col + mst
                mt = jnp.maximum(inter, jnp.max(dmat, axis=-1, keepdims=True))
                w = jnp.exp(dmat - mt) * _dot_nt(qb, kb)
                a = jnp.exp(inter - mt)
                num = _dot(w.astype(BF16), vb) + _dot(qb, cst.astype(BF16)) * a
                den = jnp.sum(w, axis=-1, keepdims=True) + a * jnp.sum(q * nst, axis=-1, keepdims=True)
                hc = num / jnp.maximum(jnp.abs(den), jnp.exp(-mt))
                h_s[d, pl.ds(t0, CHUNK), hs] = hc
                bl = b_row[:, last:last + 1]
                dl = bl - b_row + i_row
                m_new = jnp.maximum(bl + mst, jnp.max(dl, axis=-1, keepdims=True))
                wl = jnp.exp(dl - m_new)
                dec = jnp.exp(bl + mst - m_new)
                kw = (k.T * wl).astype(BF16)
                c_s[sidx] = dec * cst + _dot(kw, vb)
                wl8 = jnp.broadcast_to(wl, (8, CHUNK)).astype(BF16)
                n_s[sidx:sidx + 1, :] = dec * nst + _dot(wl8, kb)[0:1, :]
                m_s[sidx:sidx + 1, :] = jnp.broadcast_to(m_new, (1, LANES))
        return carry

    lax.fori_loop(0, nc, chunk_step, 0)

    for h in range(H_B):
        hs = slice(h * HD_B, (h + 1) * HD_B)
        hsum = h_s[0, :, hs] + h_s[1, :, hs]
        o_ref[:, hs] = _head_rms(hsum, nw_ref[:, hs]) * jax.nn.sigmoid(og_ref[:, hs])
    cf_ref[...] = c_s[...]
    nf_ref[...] = n_s[...]
    mf_ref[...] = m_s[...]


def _mlstm(z, g, row0, n_batch, seq, norm_w, init=None):
    sblk0 = row0 // seq
    nd = 2 * H_B
    spec = lambda c: pl.BlockSpec((seq, W_B), lambda b: (sblk0 + b, c))
    in_specs = [spec(3), spec(4), spec(5), spec(6),
                pl.BlockSpec((seq, LANES), lambda b: (sblk0 + b, 0)),
                pl.BlockSpec((1, W_B), lambda b: (0, 0))]
    args = [z, z, z, z, g, norm_w]
    st_specs = [pl.BlockSpec((None, nd, HD_B, HD_B), lambda b: (b, 0, 0, 0)),
                pl.BlockSpec((None, nd, HD_B), lambda b: (b, 0, 0)),
                pl.BlockSpec((None, nd, LANES), lambda b: (b, 0, 0))]
    if init is not None:
        in_specs += st_specs
        args += list(init)
    return pl.pallas_call(
        functools.partial(_mlstm_kernel, nc=seq // CHUNK, has_init=init is not None), grid=(n_batch,),
        in_specs=in_specs,
        out_specs=[pl.BlockSpec((seq, W_B), lambda b: (b, 0))] + st_specs,
        out_shape=[jax.ShapeDtypeStruct((n_batch * seq, W_B), F32),
                   jax.ShapeDtypeStruct((n_batch, nd, HD_B, HD_B), F32),
                   jax.ShapeDtypeStruct((n_batch, nd, HD_B), F32),
                   jax.ShapeDtypeStruct((n_batch, nd, LANES), F32)],
        scratch_shapes=[pltpu.VMEM((2, seq, W_B), F32), pltpu.VMEM((nd, HD_B, HD_B), F32),
                        pltpu.VMEM((nd, HD_B), F32), pltpu.VMEM((nd, LANES), F32)],
        compiler_params=_cparams("arbitrary"), name="mlstm",
    )(*args)


def _rope_tables(seq):
    half = HD_C // 2
    quarter = half // 2
    t = np.arange(seq)
    inv = ROPE_BASE ** (-np.arange(0, half, 2, dtype=np.float64) / half)
    ang_r = (t // GRID_W)[:, None] * inv[None, :]
    ang_c = (t % GRID_W)[:, None] * inv[None, :]
    cos_t = np.concatenate([np.cos(ang_r), np.cos(ang_r), np.cos(ang_c), np.cos(ang_c)], -1)
    sin_t = np.concatenate([-np.sin(ang_r), np.sin(ang_r), -np.sin(ang_c), np.sin(ang_c)], -1)
    assert cos_t.shape == (seq, 4 * quarter)
    return jnp.asarray(cos_t, F32), jnp.asarray(sin_t, F32)


def _rope(x, cos_t, sin_t):
    quarter = HD_C // 4
    lane = lax.broadcasted_iota(jnp.int32, x.shape, 1)
    first = (lane % (2 * quarter)) < quarter
    swapped = jnp.where(first, pltpu.roll(x, HD_C - quarter, 1), pltpu.roll(x, quarter, 1))
    return x * cos_t + swapped * sin_t


def _ret_kernel(*refs, nc, has_init, use_rope):
    refs = list(refs)
    q_ref, k_ref, v_ref, gg_ref, dl_ref, nw_ref = refs[:6]
    pos = 6
    if use_rope:
        cos_ref, sin_ref = refs[pos:pos + 2]
        pos += 2
    if has_init:
        s0_ref = refs[pos]
        pos += 1
    o_ref, sf_ref, h_s, s_s = refs[pos:pos + 4]
    if has_init:
        s_s[...] = s0_ref[...]
    else:
        s_s[...] = jnp.zeros_like(s_s)

    causal, anti = _tri_masks()
    li = lax.broadcasted_iota(jnp.int32, (CHUNK, CHUNK), 0).astype(F32)
    si = lax.broadcasted_iota(jnp.int32, (CHUNK, CHUNK), 1).astype(F32)
    lg_all = _log_sigmoid(dl_ref[...])
    kscale = HD_C ** -0.5

    def chunk_step(c, carry):
        for d in range(2):
            cc = c if d == 0 else nc - 1 - c
            t0 = pl.multiple_of(cc * CHUNK, CHUNK)
            for h in range(H_C):
                sidx = d * H_C + h
                hs = slice(h * HD_C, (h + 1) * HD_C)
                lg = lg_all[sidx:sidx + 1, :]
                q = q_ref[pl.ds(t0, CHUNK), hs]
                k = k_ref[pl.ds(t0, CHUNK), hs] * kscale
                v = v_ref[pl.ds(t0, CHUNK), hs]
                if use_rope:
                    cos_t = cos_ref[pl.ds(t0, CHUNK), :]
                    sin_t = sin_ref[pl.ds(t0, CHUNK), :]
                    q = _rope(q, cos_t, sin_t)
                    k = _rope(k, cos_t, sin_t)
                if d == 0:
                    decay = jnp.exp(jnp.where(causal, (li - si) * lg, -jnp.inf))
                    q_dec = jnp.exp((li + 1.0) * lg)
                    k_dec = jnp.exp((CHUNK - 1.0 - li) * lg)
                else:
                    decay = jnp.exp(jnp.where(anti, (si - li) * lg, -jnp.inf))
                    q_dec = jnp.exp((CHUNK - li) * lg)
                    k_dec = jnp.exp(li * lg)
                c_dec = jnp.exp(CHUNK * lg)
                st = s_s[sidx]
                qb, kb, vb = q.astype(BF16), k.astype(BF16), v.astype(BF16)
                att = _dot_nt(qb, kb) * decay
                o = _dot(att.astype(BF16), vb) + _dot(qb, st.astype(BF16)) * q_dec
                h_s[d, pl.ds(t0, CHUNK), hs] = o
                kd = (k * k_dec).T.astype(BF16)
                s_s[sidx] = c_dec * st + _dot(kd, vb)
        return carry

    lax.fori_loop(0, nc, chunk_step, 0)

    for h in range(H_C):
        hs = slice(h * HD_C, (h + 1) * HD_C)
        osum = h_s[0, :, hs] + h_s[1, :, hs]
        gg = gg_ref[:, hs]
        o_ref[:, hs] = _head_rms(osum, nw_ref[:, hs]) * (gg * jax.nn.sigmoid(gg))
    sf_ref[...] = s_s[...]


def _retention(z, row0, n_batch, seq, decay_rep, norm_w, rope=None, init=None):
    sblk0 = row0 // seq
    nd = 2 * H_C
    w_c = H_C * HD_C
    spec = lambda c: pl.BlockSpec((seq, w_c), lambda b: (sblk0 + b, c))
    in_specs = [spec(0), spec(1), spec(2), spec(3),
                pl.BlockSpec((nd, LANES), lambda b: (0, 0)),
                pl.BlockSpec((1, w_c), lambda b: (0, 0))]
    args = [z, z, z, z, decay_rep, norm_w]
    if rope is not None:
        in_specs += [pl.BlockSpec((seq, HD_C), lambda b: (0, 0))] * 2
        args += list(rope)
    st_spec = pl.BlockSpec((None, nd, HD_C, HD_C), lambda b: (b, 0, 0, 0))
    if init is not None:
        in_specs.append(st_spec)
        args.append(init)
    return pl.pallas_call(
        functools.partial(_ret_kernel, nc=seq // CHUNK, has_init=init is not None, use_rope=rope is not None),
        grid=(n_batch,), in_specs=in_specs,
        out_specs=[pl.BlockSpec((seq, w_c), lambda b: (b, 0)), st_spec],
        out_shape=[jax.ShapeDtypeStruct((n_batch * seq, w_c), F32),
                   jax.ShapeDtypeStruct((n_batch, nd, HD_C, HD_C), F32)],
        scratch_shapes=[pltpu.VMEM((2, seq, w_c), F32), pltpu.VMEM((nd, HD_C, HD_C), F32)],
        compiler_params=_cparams("arbitrary"), name="retention",
    )(*args)


def _dft_tables(n):
    idx = (np.arange(n)[:, None] * np.arange(n)[None, :]) % n
    ang = 2.0 * np.pi * idx / n
    return np.cos(ang) / np.sqrt(n), np.sin(ang) / np.sqrt(n)


def _fnet_kernel(x_ref, cw_ref, sw_ref, cs_ref, ss_ref, o_ref):
    for g in range(N_FG):
        gs = slice(g * FG_W, (g + 1) * FG_W)
        x = x_ref[:, gs].astype(BF16)
        xc = _dot(x, cw_ref[...]).astype(BF16)
        xs = _dot(x, sw_ref[...]).astype(BF16)
        o_ref[:, gs] = _dot(cs_ref[...], xc) - _dot(ss_ref[...], xs)


def _fnet(z, row0, n_batch, seq):
    sblk0 = row0 // seq
    w_d = N_FG * FG_W
    cw, sw = _dft_tables(FG_W)
    cs, ss = _dft_tables(seq)
    tabs = [jnp.asarray(a, F32).astype(BF16) for a in (cw, sw, cs, ss)]
    wspec = pl.BlockSpec((FG_W, FG_W), lambda b: (0, 0))
    sspec = pl.BlockSpec((seq, seq), lambda b: (0, 0))
    return pl.pallas_call(
        _fnet_kernel, grid=(n_batch,),
        in_specs=[pl.BlockSpec((seq, w_d), lambda b: (sblk0 + b, 4)), wspec, wspec, sspec, sspec],
        out_specs=pl.BlockSpec((seq, w_d), lambda b: (b, 0)),
        out_shape=jax.ShapeDtypeStruct((n_batch * seq, w_d), F32),
        compiler_params=_cparams("arbitrary"), name="fnet",
    )(z, *tabs)


def _outproj_router_kernel(a_ref, b_ref, x_ref, mod_ref, wa_ref, wb_ref, wr_ref, br_ref,
                           x1_ref, h2_ref, ti_ref, tw_ref):
    y = _dot(a_ref[...].astype(BF16), wa_ref[...]) + _dot(b_ref[...].astype(BF16), wb_ref[...])
    x1 = x_ref[...] + mod_ref[2:3, :] * y
    x1_ref[...] = x1
    h2 = _rms_mod(x1, mod_ref[3:4, :], mod_ref[4:5, :])
    h2_ref[...] = h2
    logits = _dot_hi(h2, wr_ref[...]) + br_ref[...]
    lane = lax.broadcasted_iota(jnp.int32, logits.shape, 1)
    lane_f = lane.astype(F32)
    cur = logits
    vals = []
    ti = jnp.zeros(logits.shape, jnp.int32)
    for kk in range(TOP_K):
        mx = jnp.max(cur, axis=-1, keepdims=True)
        idx = jnp.min(jnp.where(cur == mx, lane_f, float(LANES)), axis=-1, keepdims=True)
        ti = jnp.where(lane == kk, idx.astype(jnp.int32), ti)
        cur = jnp.where(lane_f == idx, -jnp.inf, cur)
        vals.append(mx)
    es = [jnp.exp(v - vals[0]) for v in vals]
    tot = es[0] + es[1] + es[2] + es[3]
    tw = jnp.zeros(logits.shape, F32)
    for kk in range(TOP_K):
        tw = jnp.where(lane == kk, es[kk] / tot, tw)
    ti_ref[...] = ti
    tw_ref[...] = tw


def _outproj_router(a, b, x, mod, wa, wb, wr, br):
    t, d = x.shape
    wid = a.shape[1]
    row = lambda w: pl.BlockSpec((TM, w), lambda i: (i, 0))
    const = lambda r, c: pl.BlockSpec((r, c), lambda i: (0, 0))
    return pl.pallas_call(
        _outproj_router_kernel, grid=(t // TM,),
        in_specs=[row(wid), row(wid), row(d),
                  pl.BlockSpec((None, 6, d), lambda i: (_cond_row(i), 0, 0)),
                  const(wid, d), const(wid, d), const(d, LANES), const(1, LANES)],
        out_specs=[row(d), row(d), row(LANES), row(LANES)],
        out_shape=[jax.ShapeDtypeStruct((t, d), F32), jax.ShapeDtypeStruct((t, d), F32),
                   jax.ShapeDtypeStruct((t, LANES), jnp.int32), jax.ShapeDtypeStruct((t, LANES), F32)],
        compiler_params=_cparams("arbitrary"), name="outproj_router",
    )(a, b, x, mod, wa, wb, wr, br)


def _route_plan(top_i, n_tiles_max):
    t = top_i.shape[0]
    onehot = (top_i[:, :, None] == jnp.arange(N_EXP, dtype=jnp.int32)[None, None, :]).astype(jnp.int32)
    sel = onehot.sum(axis=1)
    before = jnp.cumsum(sel, axis=0) - sel
    counts = before[-1] + sel[-1]
    tiles = (counts + TM_E - 1) // TM_E
    tile_end = jnp.cumsum(tiles)
    tile_start = tile_end - tiles
    dest = jnp.take_along_axis(tile_start[None, :] * TM_E + before, top_i, axis=1)
    n_tiles = tile_end[-1]
    tid = jnp.minimum(jnp.arange(n_tiles_max, dtype=jnp.int32), n_tiles - 1)
    tile_expert = jnp.sum((tile_end[None, :] <= tid[:, None]).astype(jnp.int32), axis=1)
    first = jnp.concatenate([jnp.ones((1,), jnp.int32), (tile_expert[1:] != tile_expert[:-1]).astype(jnp.int32)])
    last_tile = jnp.where(tiles > 0, tile_end - 1, 0).astype(jnp.int32)
    return dest.astype(jnp.int32), tile_expert, first, n_tiles.reshape(1).astype(jnp.int32), last_tile


def _dispatch_kernel(lt_ref, nt_ref, dest_ref, h_ref, xs_ref, zero_s, sem):
    i = pl.program_id(0)
    n_tiles_max = xs_ref.shape[0] // TM_E

    @pl.when(i == 0)
    def _():
        zero_s[...] = jnp.zeros_like(zero_s)

        def zero_tile(tile):
            r0 = pl.multiple_of(tile * TM_E, TM_E)
            return pltpu.make_async_copy(zero_s, xs_ref.at[pl.ds(r0, TM_E), :], sem)

        def start_unused(j, carry):
            zero_tile(j).start()
            return carry

        def wait_unused(j, carry):
            zero_tile(j).wait()
            return carry

        for e in range(N_EXP):
            zero_tile(lt_ref[e]).start()
        lax.fori_loop(nt_ref[0], n_tiles_max, start_unused, 0)
        for e in range(N_EXP):
            zero_tile(lt_ref[e]).wait()
        lax.fori_loop(nt_ref[0], n_tiles_max, wait_unused, 0)

    def issue(t, carry):
        for kk in range(TOP_K):
            row = dest_ref[0, t * TOP_K + kk]
            pltpu.make_async_copy(h_ref.at[pl.ds(t, 1), :], xs_ref.at[pl.ds(row, 1), :], sem).start()
        return carry

    lax.fori_loop(0, TM, issue, 0)
    for kk in range(TOP_K):
        pltpu.make_async_copy(h_ref, xs_ref.at[pl.ds(0, TM), :], sem).wait()


def _dispatch(h2, dest, last_tile, n_tiles, n_rows):
    t, d = h2.shape
    assert TM == TM_E
    dest3 = dest.reshape(t // TM, 1, TM * TOP_K)
    return pl.pallas_call(
        _dispatch_kernel,
        grid_spec=pltpu.PrefetchScalarGridSpec(
            num_scalar_prefetch=2, grid=(t // TM,),
            in_specs=[pl.BlockSpec((None, 1, TM * TOP_K), lambda i, lt, nt: (i, 0, 0), memory_space=pltpu.SMEM),
                      pl.BlockSpec((TM, d), lambda i, lt, nt: (i, 0))],
            out_specs=pl.BlockSpec(memory_space=pl.ANY),
            scratch_shapes=[pltpu.VMEM((TM_E, d), F32), pltpu.SemaphoreType.DMA(())]),
        out_shape=jax.ShapeDtypeStruct((n_rows, d), F32),
        compiler_params=_cparams("arbitrary"), name="moe_dispatch",
    )(last_tile, n_tiles, dest3, h2)


def _expert_kernel(te_ref, tf_ref, nt_ref, xs_ref, w1_ref, b1_ref, w2_ref, b2_ref, ys_ref, w1_s, w2_s):
    i = pl.program_id(0)

    @pl.when(i < nt_ref[0])
    def _():
        @pl.when(tf_ref[i] == 1)
        def _():
            w1_s[...] = w1_ref[...].astype(BF16)
            w2_s[...] = w2_ref[...].astype(BF16)

        u = _dot(xs_ref[...].astype(BF16), w1_s[...]) + b1_ref[...]
        g = jnp.minimum(u[:, :D_FF], SWIGLU_LIMIT)
        up = jnp.clip(u[:, D_FF:], -SWIGLU_LIMIT, SWIGLU_LIMIT)
        act = (up + 1.0) * g * jax.nn.sigmoid(SWIGLU_ALPHA * g)
        ys_ref[...] = _dot(act.astype(BF16), w2_s[...]) + b2_ref[...]

    @pl.when(i >= nt_ref[0])
    def _():
        ys_ref[...] = jnp.zeros_like(ys_ref)


def _experts(xs, tile_expert, first, n_tiles, w1, b1, w2, b2):
    n_rows, d = xs.shape
    nt = n_rows // TM_E
    tile = lambda i, te, tf, ntl: (jnp.minimum(i, ntl[0] - 1), 0)
    otile = lambda i, te, tf, ntl: (i, 0)
    wmap = lambda i, te, tf, ntl: (te[i], 0, 0)
    return pl.pallas_call(
        _expert_kernel,
        grid_spec=pltpu.PrefetchScalarGridSpec(
            num_scalar_prefetch=3, grid=(nt,),
            in_specs=[pl.BlockSpec((TM_E, d), tile),
                      pl.BlockSpec((None, d, 2 * D_FF), wmap),
                      pl.BlockSpec((None, 1, 2 * D_FF), wmap),
                      pl.BlockSpec((None, D_FF, d), wmap),
                      pl.BlockSpec((None, 1, d), wmap)],
            out_specs=pl.BlockSpec((TM_E, d), otile),
            scratch_shapes=[pltpu.VMEM((d, 2 * D_FF), BF16), pltpu.VMEM((D_FF, d), BF16)]),
        out_shape=jax.ShapeDtypeStruct((n_rows, d), F32),
        compiler_params=_cparams("arbitrary"), name="moe_experts",
    )(tile_expert, first, n_tiles, xs, w1, b1, w2, b2)


def _combine_kernel(dest_ref, x1_ref, tw_ref, mod_ref, ys_ref, o_ref, buf, sem):
    def issue(t, carry):
        for kk in range(TOP_K):
            row = dest_ref[0, t * TOP_K + kk]
            pltpu.make_async_copy(ys_ref.at[pl.ds(row, 1), :], buf.at[kk, pl.ds(t, 1), :], sem).start()
        return carry

    lax.fori_loop(0, TM, issue, 0)
    for kk in range(TOP_K):
        pltpu.make_async_copy(ys_ref.at[pl.ds(0, TM), :], buf.at[kk], sem).wait()
    tw = tw_ref[...]
    y = tw[:, 0:1] * buf[0]
    for kk in range(1, TOP_K):
        y = y + tw[:, kk:kk + 1] * buf[kk]
    o_ref[...] = x1_ref[...] + mod_ref[5:6, :] * y


def _combine(ys, dest, x1, tw, mod):
    t, d = x1.shape
    dest3 = dest.reshape(t // TM, 1, TM * TOP_K)
    row = lambda w: pl.BlockSpec((TM, w), lambda i: (i, 0))
    return pl.pallas_call(
        _combine_kernel, grid=(t // TM,),
        in_specs=[pl.BlockSpec((None, 1, TM * TOP_K), lambda i: (i, 0, 0), memory_space=pltpu.SMEM),
                  row(d), row(LANES),
                  pl.BlockSpec((None, 6, d), lambda i: (_cond_row(i), 0, 0)),
                  pl.BlockSpec(memory_space=pl.ANY)],
        out_specs=row(d),
        out_shape=jax.ShapeDtypeStruct((t, d), F32),
        scratch_shapes=[pltpu.VMEM((TOP_K, TM, d), F32), pltpu.SemaphoreType.DMA(())],
        compiler_params=_cparams("arbitrary"), name="moe_combine",
    )(dest3, x1, tw, mod, ys)


def _moe(x1, h2, ti, tw, mod, w1, b1, w2, b2):
    t = x1.shape[0]
    n_tiles_max = t * TOP_K // TM_E + N_EXP
    dest, tile_expert, first, n_tiles, last_tile = _route_plan(ti[:, :TOP_K], n_tiles_max)
    xs = _dispatch(h2, dest, last_tile, n_tiles, n_tiles_max * TM_E)
    ys = _experts(xs, tile_expert, first, n_tiles, w1, b1.reshape(N_EXP, 1, -1), w2, b2.reshape(N_EXP, 1, -1))
    return _combine(ys, dest, x1, tw, mod)


def _pad_lanes(a, value=0.0):
    return jnp.pad(a, ((0, 0), (0, LANES - a.shape[1])), constant_values=value)


def kernel(x_prompt, x_sample, cache_na_k, cache_na_v, state_mlstm_C, state_mlstm_n, state_mlstm_m, state_ret_S, c, c_ctx, w_mod, b_mod, w_in_even, mlstm_gate_b, na_q_norm, na_k_norm, na_rpb, mlstm_norm, w_out_even, w_in_odd, ret_decay, ret_norm, w_out_odd, w_router, b_router, w_moe_in, b_moe_in, w_moe_out, b_moe_out):
    nb_c, s_c, d = x_prompt.shape
    nb_l, s_l, _ = x_sample.shape
    t_c = nb_c * s_c
    t_l = nb_l * s_l
    assert t_c == 4 * SEG and s_l == SEG and d == D_MODEL
    depth = w_mod.shape[0]
    dt = x_prompt.dtype

    x = jnp.concatenate([x_prompt.reshape(t_c, d), x_sample.reshape(t_l, d)], axis=0)
    cond = jnp.concatenate([c_ctx[None, :], c, jnp.zeros((N_COND - 1 - nb_l, d), F32)], axis=0)
    mod = _modulation(cond, w_mod, b_mod).reshape(depth, N_COND, 6, d)

    outs = {}
    for l in range(depth):
        e = l // 2
        mod_l = mod[l]
        if l % 2 == 0:
            w_in = w_in_even[e]
            n_main = 3 * W_A + 4 * W_B
            wg = _pad_lanes(w_in[:, n_main:])
            bg = _pad_lanes(mlstm_gate_b[e].reshape(1, 4 * H_B))
            z, g = _inproj(x, mod_l, w_in[:, :n_main].astype(BF16), wg, bg)
            qn = na_q_norm[e].reshape(1, HD_A)
            kn = na_k_norm[e].reshape(1, HD_A)
            oa_c, ka_c = _ctx_attention(z, nb_c, s_c, qn, kn)
            past = cache_na_k.shape[2]
            oa_l = _na_attention(z, t_c, nb_l, s_l,
                                 cache_na_k[:, e].reshape(nb_l, past, W_A), cache_na_v[:, e].reshape(nb_l, past, W_A),
                                 _na_bias_table(na_rpb[e]), qn, kn)
            nw = mlstm_norm[e].reshape(1, W_B)
            hm_c, c_fin, n_fin, m_fin = _mlstm(z, g, 0, nb_c, s_c, nw)
            init = (state_mlstm_C[:, e].reshape(nb_l, 2 * H_B, HD_B, HD_B),
                    state_mlstm_n[:, e].reshape(nb_l, 2 * H_B, HD_B),
                    jnp.broadcast_to(state_mlstm_m[:, e].reshape(nb_l, 2 * H_B, 1), (nb_l, 2 * H_B, LANES)))
            hm_l = _mlstm(z, g, t_c, nb_l, s_l, nw, init)[0]
            a = jnp.concatenate([oa_c, oa_l], axis=0)
            b = jnp.concatenate([hm_c, hm_l], axis=0)
            w_out = w_out_even[e].astype(BF16)
            wa, wb = w_out[:W_A], w_out[W_A:]
            outs.setdefault("na_k", []).append(ka_c.reshape(nb_c, s_c, H_A, HD_A))
            outs.setdefault("na_v", []).append(z[:t_c, 2 * W_A:3 * W_A].reshape(nb_c, s_c, H_A, HD_A))
            outs.setdefault("C", []).append(c_fin.reshape(nb_c, 2, H_B, HD_B, HD_B))
            outs.setdefault("n", []).append(n_fin.reshape(nb_c, 2, H_B, HD_B))
            outs.setdefault("m", []).append(m_fin[:, :, 0].reshape(nb_c, 2, H_B))
        else:
            w_c = H_C * HD_C
            z = _inproj(x, mod_l, w_in_odd[e].astype(BF16))
            dl_rep = jnp.broadcast_to(ret_decay[e].reshape(2 * H_C, 1), (2 * H_C, LANES))
            nw = ret_norm[e].reshape(1, w_c)
            hr_c, s_fin = _retention(z, 0, nb_c, s_c, dl_rep, nw)
            hr_l = _retention(z, t_c, nb_l, s_l, dl_rep, nw, rope=_rope_tables(s_l),
                              init=state_ret_S[:, e].reshape(nb_l, 2 * H_C, HD_C, HD_C))[0]
            fd_c = _fnet(z, 0, nb_c, s_c)
            fd_l = _fnet(z, t_c, nb_l, s_l)
            a = jnp.concatenate([hr_c, hr_l], axis=0)
            b = jnp.concatenate([fd_c, fd_l], axis=0)
            w_out = w_out_odd[e].astype(BF16)
            wa, wb = w_out[:w_c], w_out[w_c:]
            outs.setdefault("S", []).append(s_fin.reshape(nb_c, 2, H_C, HD_C, HD_C))
        wr = _pad_lanes(w_router[l])
        br = _pad_lanes(b_router[l].reshape(1, N_EXP), NEG)
        x1, h2, ti, tw = _outproj_router(a, b, x, mod_l, wa, wb, wr, br)
        x = _moe(x1, h2, ti, tw, mod_l, w_moe_in[l], b_moe_in[l], w_moe_out[l], b_moe_out[l])

    y_prompt = x[:t_c].reshape(nb_c, s_c, d)
    y_sample = x[t_c:].reshape(nb_l, s_l, d)
    stack = lambda key: jnp.stack(outs[key], axis=1).astype(dt)
    return (y_prompt, y_sample, stack("na_k"), stack("na_v"), stack("C"), stack("n"), stack("m"), stack("S"))
```

```python
import functools
import math

import numpy as np
import jax
import jax.numpy as jnp
from jax import lax
from jax.experimental import pallas as pl
from jax.experimental.pallas import tpu as pltpu

F32 = jnp.float32
BF16 = jnp.bfloat16
HIGHEST = lax.Precision.HIGHEST

D_MODEL = 1024
GRID_W = 64
WIN_R = 8
WIN_C = 16
H_A, HD_A = 8, 64
H_B, HD_B = 4, 128
H_C, HD_C = 4, 128
N_FG, FG_W = 4, 128
W_A = H_A * HD_A
W_B = H_B * HD_B
N_EXP = 32
TOP_K = 4
D_FF = D_MODEL
SWIGLU_LIMIT = 7.0
SWIGLU_ALPHA = 1.702
CHUNK = 128
ROPE_BASE = 10000.0
EPS = 1e-6

LANES = 128
SEG = 1024
N_COND = 8
TM = 256
TM_E = 256
NEG = -1e30
VMEM_LIMIT = 56 * 1024 * 1024


def _cparams(*sem):
    return pltpu.CompilerParams(dimension_semantics=sem, vmem_limit_bytes=VMEM_LIMIT)


def _cond_row(i):
    return jnp.maximum((i * TM) // SEG - 3, 0)


def _log_sigmoid(x):
    return jnp.minimum(x, 0.0) - jnp.log1p(jnp.exp(-jnp.abs(x)))


def _dot(a, b):
    return jnp.dot(a, b, preferred_element_type=F32)


def _dot_nt(a, b):
    return lax.dot_general(a, b, (((1,), (1,)), ((), ())), preferred_element_type=F32)


def _dot_hi(a, b):
    return jnp.dot(a, b, precision=HIGHEST, preferred_element_type=F32)


def _mod_kernel(cond_ref, w_ref, b_ref, o_ref):
    c = cond_ref[...]
    s = c * jax.nn.sigmoid(c)
    o_ref[...] = _dot_hi(s, w_ref[...]) + b_ref[...]


def _modulation(cond, w_mod, b_mod):
    depth, d, n = w_mod.shape
    tn = 1536
    return pl.pallas_call(
        _mod_kernel,
        grid=(depth, n // tn),
        in_specs=[pl.BlockSpec((N_COND, d), lambda l, j: (0, 0)),
                  pl.BlockSpec((None, d, tn), lambda l, j: (l, 0, j)),
                  pl.BlockSpec((None, 1, tn), lambda l, j: (l, 0, j))],
        out_specs=pl.BlockSpec((None, N_COND, tn), lambda l, j: (l, 0, j)),
        out_shape=jax.ShapeDtypeStruct((depth, N_COND, n), F32),
        compiler_params=_cparams("arbitrary", "arbitrary"),
        name="modulation",
    )(cond, w_mod, b_mod.reshape(depth, 1, n))


def _rms_mod(x, shift, scale):
    h = x * lax.rsqrt(jnp.mean(x * x, axis=-1, keepdims=True) + EPS)
    return h * (1.0 + scale) + shift


def _inproj_kernel(x_ref, mod_ref, w_ref, z_ref):
    h = _rms_mod(x_ref[...], mod_ref[0:1, :], mod_ref[1:2, :])
    z_ref[...] = _dot(h.astype(BF16), w_ref[...])


def _inproj_gate_kernel(x_ref, mod_ref, w_ref, wg_ref, bg_ref, z_ref, g_ref):
    h = _rms_mod(x_ref[...], mod_ref[0:1, :], mod_ref[1:2, :])
    z_ref[...] = _dot(h.astype(BF16), w_ref[...])
    g_ref[...] = _dot_hi(h, wg_ref[...]) + bg_ref[...]


def _inproj(x, mod, w, wg=None, bg=None):
    t, d = x.shape
    n = w.shape[1]
    in_specs = [pl.BlockSpec((TM, d), lambda i: (i, 0)),
                pl.BlockSpec((None, 6, d), lambda i: (_cond_row(i), 0, 0)),
                pl.BlockSpec((d, n), lambda i: (0, 0))]
    z_spec = pl.BlockSpec((TM, n), lambda i: (i, 0))
    z_shape = jax.ShapeDtypeStruct((t, n), F32)
    if wg is None:
        return pl.pallas_call(
            _inproj_kernel, grid=(t // TM,), in_specs=in_specs, out_specs=z_spec, out_shape=z_shape,
            compiler_params=_cparams("arbitrary"), name="inproj",
        )(x, mod, w)
    in_specs += [pl.BlockSpec((d, LANES), lambda i: (0, 0)), pl.BlockSpec((1, LANES), lambda i: (0, 0))]
    return pl.pallas_call(
        _inproj_gate_kernel, grid=(t // TM,), in_specs=in_specs,
        out_specs=[z_spec, pl.BlockSpec((TM, LANES), lambda i: (i, 0))],
        out_shape=[z_shape, jax.ShapeDtypeStruct((t, LANES), F32)],
        compiler_params=_cparams("arbitrary"), name="inproj_gate",
    )(x, mod, w, wg, bg)


def _head_rms(x, w):
    return x * lax.rsqrt(jnp.mean(x * x, axis=-1, keepdims=True) + EPS) * w


def _ctx_attn_kernel(q_ref, k_ref, v_ref, qn_ref, kn_ref, o_ref, ko_ref):
    scale = HD_A ** -0.5
    for h in range(H_A):
        sl = slice(h * HD_A, (h + 1) * HD_A)
        q = _head_rms(q_ref[:, sl], qn_ref[...])
        k = _head_rms(k_ref[:, sl], kn_ref[...])
        ko_ref[:, sl] = k
        s = _dot_nt(q.astype(BF16), k.astype(BF16)) * scale
        p = jnp.exp(s - jnp.max(s, axis=-1, keepdims=True))
        den = jnp.sum(p, axis=-1, keepdims=True)
        o_ref[:, sl] = _dot(p.astype(BF16), v_ref[:, sl].astype(BF16)) / den


def _ctx_attention(z, n_batch, seq, qn, kn):
    spec = lambda c: pl.BlockSpec((seq, W_A), lambda b: (b, c))
    wspec = pl.BlockSpec((1, HD_A), lambda b: (0, 0))
    out = jax.ShapeDtypeStruct((n_batch * seq, W_A), F32)
    return pl.pallas_call(
        _ctx_attn_kernel, grid=(n_batch,),
        in_specs=[spec(0), spec(1), spec(2), wspec, wspec],
        out_specs=[pl.BlockSpec((seq, W_A), lambda b: (b, 0))] * 2,
        out_shape=[out, out],
        compiler_params=_cparams("arbitrary"), name="ctx_attention",
    )(z, z, z, qn, kn)


def _na_bias_table(rpb):
    qc = np.arange(GRID_W)
    kc = np.arange(GRID_W)
    cstart = np.clip(qc - WIN_C // 2, 0, GRID_W - WIN_C)
    col_in = (kc[None, :] >= cstart[:, None]) & (kc[None, :] < cstart[:, None] + WIN_C)
    dc = np.clip(kc[None, :] - qc[:, None], 1 - WIN_C, WIN_C - 1) + WIN_C - 1
    cls = np.arange(WIN_R)
    j = np.arange(WIN_R)
    dr = j[None, :] - cls[:, None] + WIN_R - 1
    sel_r = jnp.asarray(dr[:, :, None] == np.arange(2 * WIN_R - 1)[None, None, :], F32)
    sel_c = jnp.asarray(dc[:, :, None] == np.arange(2 * WIN_C - 1)[None, None, :], F32)
    tab = jnp.einsum("hab,cja,qkb->hcqjk", rpb, sel_r, sel_c, precision=HIGHEST)
    tab = jnp.where(jnp.asarray(col_in)[None, None, :, None, :], tab, NEG)
    return tab.reshape(H_A, WIN_R, GRID_W, WIN_R * GRID_W)


def _na_kernel(q_ref, k_ref, v_ref, kc_ref, vc_ref, bias_ref, qn_ref, kn_ref, o_ref, kn_s, *, rows):
    r = pl.program_id(1)
    scale = HD_A ** -0.5

    @pl.when(r == 0)
    def _():
        for h in range(H_A):
            sl = slice(h * HD_A, (h + 1) * HD_A)
            kn_s[:, sl] = _head_rms(k_ref[:, sl], kn_ref[...])

    rs = jnp.clip(r - WIN_R // 2, 0, rows - WIN_R)
    start = pl.multiple_of(rs * GRID_W, GRID_W)
    n_loc = WIN_R * GRID_W
    for h in range(H_A):
        sl = slice(h * HD_A, (h + 1) * HD_A)
        q = _head_rms(q_ref[:, sl], qn_ref[...]).astype(BF16)
        kl = kn_s[pl.ds(start, n_loc), sl].astype(BF16)
        vl = v_ref[pl.ds(start, n_loc), sl].astype(BF16)
        s_loc = _dot_nt(q, kl) * scale + bias_ref[h]
        s_ctx = _dot_nt(q, kc_ref[:, sl].astype(BF16)) * scale
        m = jnp.maximum(jnp.max(s_loc, axis=-1, keepdims=True), jnp.max(s_ctx, axis=-1, keepdims=True))
        p_loc = jnp.exp(s_loc - m)
        p_ctx = jnp.exp(s_ctx - m)
        den = jnp.sum(p_loc, axis=-1, keepdims=True) + jnp.sum(p_ctx, axis=-1, keepdims=True)
        o = _dot(p_loc.astype(BF16), vl) + _dot(p_ctx.astype(BF16), vc_ref[:, sl].astype(BF16))
        o_ref[:, sl] = o / den


def _na_attention(z, row0, n_batch, seq, kc, vc, bias, qn, kn):
    rows = seq // GRID_W
    past = kc.shape[1]
    blk0 = row0 // GRID_W
    sblk0 = row0 // seq

    def cls_of(r):
        return r - jnp.clip(r - WIN_R // 2, 0, rows - WIN_R)

    full = lambda c: pl.BlockSpec((seq, W_A), lambda b, r: (sblk0 + b, c))
    cspec = pl.BlockSpec((None, past, W_A), lambda b, r: (b, 0, 0))
    wspec = pl.BlockSpec((1, HD_A), lambda b, r: (0, 0))
    return pl.pallas_call(
        functools.partial(_na_kernel, rows=rows), grid=(n_batch, rows),
        in_specs=[pl.BlockSpec((GRID_W, W_A), lambda b, r: (blk0 + b * rows + r, 0)),
                  full(1), full(2), cspec, cspec,
                  pl.BlockSpec((H_A, None, GRID_W, WIN_R * GRID_W), lambda b, r: (0, cls_of(r), 0, 0)),
                  wspec, wspec],
        out_specs=pl.BlockSpec((GRID_W, W_A), lambda b, r: (b * rows + r, 0)),
        out_shape=jax.ShapeDtypeStruct((n_batch * seq, W_A), F32),
        scratch_shapes=[pltpu.VMEM((seq, W_A), F32)],
        compiler_params=_cparams("arbitrary", "arbitrary"), name="na_attention",
    )(z, z, z, kc, vc, bias, qn, kn)


def _tri_masks():
    li = lax.broadcasted_iota(jnp.int32, (CHUNK, CHUNK), 0)
    si = lax.broadcasted_iota(jnp.int32, (CHUNK, CHUNK), 1)
    return li >= si, li <= si


def _mlstm_kernel(*refs, nc, has_init):
    if has_init:
        (q_ref, k_ref, v_ref, og_ref, g_ref, nw_ref, c0_ref, n0_ref, m0_ref,
         o_ref, cf_ref, nf_ref, mf_ref, h_s, c_s, n_s, m_s) = refs
    else:
        (q_ref, k_ref, v_ref, og_ref, g_ref, nw_ref,
         o_ref, cf_ref, nf_ref, mf_ref, h_s, c_s, n_s, m_s) = refs
    nd = 2 * H_B
    if has_init:
        c_s[...] = c0_ref[...]
        n_s[...] = n0_ref[...]
        m_s[...] = m0_ref[...]
    else:
        c_s[...] = jnp.zeros_like(c_s)
        n_s[...] = jnp.zeros_like(n_s)
        m_s[...] = jnp.zeros_like(m_s)

    causal, anti = _tri_masks()
    tri_f = causal.astype(F32)
    tri_b = anti.astype(F32)
    kscale = HD_B ** -0.5

    def chunk_step(c, carry):
        for d in range(2):
            cc = c if d == 0 else nc - 1 - c
            t0 = pl.multiple_of(cc * CHUNK, CHUNK)
            g = g_ref[pl.ds(t0, CHUNK), :]
            gt = g.T
            ls = _log_sigmoid(g)
            lst = _log_sigmoid(gt)
            tri_c, tri_r, mask = (tri_f, tri_b, causal) if d == 0 else (tri_b, tri_f, anti)
            b_cols = _dot_hi(tri_c, ls)
            b_rows = _dot_hi(lst, tri_r)
            last = CHUNK - 1 if d == 0 else 0
            for h in range(H_B):
                ci = (2 * d) * H_B + h
                cf = (2 * d + 1) * H_B + h
                hs = slice(h * HD_B, (h + 1) * HD_B)
                q = q_ref[pl.ds(t0, CHUNK), hs]
                k = k_ref[pl.ds(t0, CHUNK), hs] * kscale
                v = v_ref[pl.ds(t0, CHUNK), hs]
                qb, kb, vb = q.astype(BF16), k.astype(BF16), v.astype(BF16)
                b_col = b_cols[:, cf:cf + 1]
                b_row = b_rows[cf:cf + 1, :]
                i_row = gt[ci:ci + 1, :]
                sidx = d * H_B + h
                cst = c_s[sidx]
                nst = n_s[sidx:sidx + 1, :]
                mst = m_s[sidx:sidx + 1, 0:1]
                dmat = jnp.where(mask, b_col - b_row + i_row, -jnp.inf)
                inter = b_col + mst
                mt = jnp.maximum(inter, jnp.max(dmat, axis=-1, keepdims=True))
                w = jnp.exp(dmat - mt) * _dot_nt(qb, kb)
                a = jnp.exp(inter - mt)
                num = _dot(w.astype(BF16), vb) + _dot(qb, cst.astype(BF16)) * a
                den = jnp.sum(w, axis=-1, keepdims=True) + a * jnp.sum(q * nst, axis=-1, keepdims=True)
                hc = num / jnp.maximum(jnp.abs(den), jnp.exp(-mt))
                h_s[d, pl.ds(t0, CHUNK), hs] = hc
                bl = b_row[:, last:last + 1]
                dl = bl - b_row + i_row
                m_new = jnp.maximum(bl + mst, jnp.max(dl, axis=-1, keepdims=True))
                wl = jnp.exp(dl - m_new)
                dec = jnp.exp(bl + mst - m_new)
                kw = (k.T * wl).astype(BF16)
                c_s[sidx] = dec * cst + _dot(kw, vb)
                wl8 = jnp.broadcast_to(wl, (8, CHUNK)).astype(BF16)
                n_s[sidx:sidx + 1, :] = dec * nst + _dot(wl8, kb)[0:1, :]
                m_s[sidx:sidx + 1, :] = jnp.broadcast_to(m_new, (1, LANES))
        return carry

    lax.fori_loop(0, nc, chunk_step, 0)

    for h in range(H_B):
        hs = slice(h * HD_B, (h + 1) * HD_B)
        hsum = h_s[0, :, hs] + h_s[1, :, hs]
        o_ref[:, hs] = _head_rms(hsum, nw_ref[:, hs]) * jax.nn.sigmoid(og_ref[:, hs])
    cf_ref[...] = c_s[...]
    nf_ref[...] = n_s[...]
    mf_ref[...] = m_s[...]


def _mlstm(z, g, row0, n_batch, seq, norm_w, init=None):
    sblk0 = row0 // seq
    nd = 2 * H_B
    spec = lambda c: pl.BlockSpec((seq, W_B), lambda b: (sblk0 + b, c))
    in_specs = [spec(3), spec(4), spec(5), spec(6),
                pl.BlockSpec((seq, LANES), lambda b: (sblk0 + b, 0)),
                pl.BlockSpec((1, W_B), lambda b: (0, 0))]
    args = [z, z, z, z, g, norm_w]
    st_specs = [pl.BlockSpec((None, nd, HD_B, HD_B), lambda b: (b, 0, 0, 0)),
                pl.BlockSpec((None, nd, HD_B), lambda b: (b, 0, 0)),
                pl.BlockSpec((None, nd, LANES), lambda b: (b, 0, 0))]
    if init is not None:
        in_specs += st_specs
        args += list(init)
    return pl.pallas_call(
        functools.partial(_mlstm_kernel, nc=seq // CHUNK, has_init=init is not None), grid=(n_batch,),
        in_specs=in_specs,
        out_specs=[pl.BlockSpec((seq, W_B), lambda b: (b, 0))] + st_specs,
        out_shape=[jax.ShapeDtypeStruct((n_batch * seq, W_B), F32),
                   jax.ShapeDtypeStruct((n_batch, nd, HD_B, HD_B), F32),
                   jax.ShapeDtypeStruct((n_batch, nd, HD_B), F32),
                   jax.ShapeDtypeStruct((n_batch, nd, LANES), F32)],
        scratch_shapes=[pltpu.VMEM((2, seq, W_B), F32), pltpu.VMEM((nd, HD_B, HD_B), F32),
                        pltpu.VMEM((nd, HD_B), F32), pltpu.VMEM((nd, LANES), F32)],
        compiler_params=_cparams("arbitrary"), name="mlstm",
    )(*args)


def _rope_tables(seq):
    half = HD_C // 2
    quarter = half // 2
    t = np.arange(seq)
    inv = ROPE_BASE ** (-np.arange(0, half, 2, dtype=np.float64) / half)
    ang_r = (t // GRID_W)[:, None] * inv[None, :]
    ang_c = (t % GRID_W)[:, None] * inv[None, :]
    cos_t = np.concatenate([np.cos(ang_r), np.cos(ang_r), np.cos(ang_c), np.cos(ang_c)], -1)
    sin_t = np.concatenate([-np.sin(ang_r), np.sin(ang_r), -np.sin(ang_c), np.sin(ang_c)], -1)
    assert cos_t.shape == (seq, 4 * quarter)
    return jnp.asarray(cos_t, F32), jnp.asarray(sin_t, F32)


def _rope(x, cos_t, sin_t):
    quarter = HD_C // 4
    lane = lax.broadcasted_iota(jnp.int32, x.shape, 1)
    first = (lane % (2 * quarter)) < quarter
    swapped = jnp.where(first, pltpu.roll(x, HD_C - quarter, 1), pltpu.roll(x, quarter, 1))
    return x * cos_t + swapped * sin_t


def _ret_kernel(*refs, nc, has_init, use_rope):
    refs = list(refs)
    q_ref, k_ref, v_ref, gg_ref, dl_ref, nw_ref = refs[:6]
    pos = 6
    if use_rope:
        cos_ref, sin_ref = refs[pos:pos + 2]
        pos += 2
    if has_init:
        s0_ref = refs[pos]
        pos += 1
    o_ref, sf_ref, h_s, s_s = refs[pos:pos + 4]
    if has_init:
        s_s[...] = s0_ref[...]
    else:
        s_s[...] = jnp.zeros_like(s_s)

    causal, anti = _tri_masks()
    li = lax.broadcasted_iota(jnp.int32, (CHUNK, CHUNK), 0).astype(F32)
    si = lax.broadcasted_iota(jnp.int32, (CHUNK, CHUNK), 1).astype(F32)
    lg_all = _log_sigmoid(dl_ref[...])
    kscale = HD_C ** -0.5

    def chunk_step(c, carry):
        for d in range(2):
            cc = c if d == 0 else nc - 1 - c
            t0 = pl.multiple_of(cc * CHUNK, CHUNK)
            for h in range(H_C):
                sidx = d * H_C + h
                hs = slice(h * HD_C, (h + 1) * HD_C)
                lg = lg_all[sidx:sidx + 1, :]
                q = q_ref[pl.ds(t0, CHUNK), hs]
                k = k_ref[pl.ds(t0, CHUNK), hs] * kscale
                v = v_ref[pl.ds(t0, CHUNK), hs]
                if use_rope:
                    cos_t = cos_ref[pl.ds(t0, CHUNK), :]
                    sin_t = sin_ref[pl.ds(t0, CHUNK), :]
                    q = _rope(q, cos_t, sin_t)
                    k = _rope(k, cos_t, sin_t)
                if d == 0:
                    decay = jnp.exp(jnp.where(causal, (li - si) * lg, -jnp.inf))
                    q_dec = jnp.exp((li + 1.0) * lg)
                    k_dec = jnp.exp((CHUNK - 1.0 - li) * lg)
                else:
                    decay = jnp.exp(jnp.where(anti, (si - li) * lg, -jnp.inf))
                    q_dec = jnp.exp((CHUNK - li) * lg)
                    k_dec = jnp.exp(li * lg)
                c_dec = jnp.exp(CHUNK * lg)
                st = s_s[sidx]
                qb, kb, vb = q.astype(BF16), k.astype(BF16), v.astype(BF16)
                att = _dot_nt(qb, kb) * decay
                o = _dot(att.astype(BF16), vb) + _dot(qb, st.astype(BF16)) * q_dec
                h_s[d, pl.ds(t0, CHUNK), hs] = o
                kd = (k * k_dec).T.astype(BF16)
                s_s[sidx] = c_dec * st + _dot(kd, vb)
        return carry

    lax.fori_loop(0, nc, chunk_step, 0)

    for h in range(H_C):
        hs = slice(h * HD_C, (h + 1) * HD_C)
        osum = h_s[0, :, hs] + h_s[1, :, hs]
        gg = gg_ref[:, hs]
        o_ref[:, hs] = _head_rms(osum, nw_ref[:, hs]) * (gg * jax.nn.sigmoid(gg))
    sf_ref[...] = s_s[...]


def _retention(z, row0, n_batch, seq, decay_rep, norm_w, rope=None, init=None):
    sblk0 = row0 // seq
    nd = 2 * H_C
    w_c = H_C * HD_C
    spec = lambda c: pl.BlockSpec((seq, w_c), lambda b: (sblk0 + b, c))
    in_specs = [spec(0), spec(1), spec(2), spec(3),
                pl.BlockSpec((nd, LANES), lambda b: (0, 0)),
                pl.BlockSpec((1, w_c), lambda b: (0, 0))]
    args = [z, z, z, z, decay_rep, norm_w]
    if rope is not None:
        in_specs += [pl.BlockSpec((seq, HD_C), lambda b: (0, 0))] * 2
        args += list(rope)
    st_spec = pl.BlockSpec((None, nd, HD_C, HD_C), lambda b: (b, 0, 0, 0))
    if init is not None:
        in_specs.append(st_spec)
        args.append(init)
    return pl.pallas_call(
        functools.partial(_ret_kernel, nc=seq // CHUNK, has_init=init is not None, use_rope=rope is not None),
        grid=(n_batch,), in_specs=in_specs,
        out_specs=[pl.BlockSpec((seq, w_c), lambda b: (b, 0)), st_spec],
        out_shape=[jax.ShapeDtypeStruct((n_batch * seq, w_c), F32),
                   jax.ShapeDtypeStruct((n_batch, nd, HD_C, HD_C), F32)],
        scratch_shapes=[pltpu.VMEM((2, seq, w_c), F32), pltpu.VMEM((nd, HD_C, HD_C), F32)],
        compiler_params=_cparams("arbitrary"), name="retention",
    )(*args)


def _dft_tables(n):
    idx = (np.arange(n)[:, None] * np.arange(n)[None, :]) % n
    ang = 2.0 * np.pi * idx / n
    return np.cos(ang) / np.sqrt(n), np.sin(ang) / np.sqrt(n)


def _fnet_kernel(x_ref, cw_ref, sw_ref, cs_ref, ss_ref, o_ref):
    for g in range(N_FG):
        gs = slice(g * FG_W, (g + 1) * FG_W)
        x = x_ref[:, gs].astype(BF16)
        xc = _dot(x, cw_ref[...]).astype(BF16)
        xs = _dot(x, sw_ref[...]).astype(BF16)
        o_ref[:, gs] = _dot(cs_ref[...], xc) - _dot(ss_ref[...], xs)


def _fnet(z, row0, n_batch, seq):
    sblk0 = row0 // seq
    w_d = N_FG * FG_W
    cw, sw = _dft_tables(FG_W)
    cs, ss = _dft_tables(seq)
    tabs = [jnp.asarray(a, F32).astype(BF16) for a in (cw, sw, cs, ss)]
    wspec = pl.BlockSpec((FG_W, FG_W), lambda b: (0, 0))
    sspec = pl.BlockSpec((seq, seq), lambda b: (0, 0))
    return pl.pallas_call(
        _fnet_kernel, grid=(n_batch,),
        in_specs=[pl.BlockSpec((seq, w_d), lambda b: (sblk0 + b, 4)), wspec, wspec, sspec, sspec],
        out_specs=pl.BlockSpec((seq, w_d), lambda b: (b, 0)),
        out_shape=jax.ShapeDtypeStruct((n_batch * seq, w_d), F32),
        compiler_params=_cparams("arbitrary"), name="fnet",
    )(z, *tabs)


def _outproj_router_kernel(a_ref, b_ref, x_ref, mod_ref, wa_ref, wb_ref, wr_ref, br_ref,
                           x1_ref, h2_ref, ti_ref, tw_ref):
    y = _dot(a_ref[...].astype(BF16), wa_ref[...]) + _dot(b_ref[...].astype(BF16), wb_ref[...])
    x1 = x_ref[...] + mod_ref[2:3, :] * y
    x1_ref[...] = x1
    h2 = _rms_mod(x1, mod_ref[3:4, :], mod_ref[4:5, :])
    h2_ref[...] = h2
    logits = _dot_hi(h2, wr_ref[...]) + br_ref[...]
    lane = lax.broadcasted_iota(jnp.int32, logits.shape, 1)
    lane_f = lane.astype(F32)
    cur = logits
    vals = []
    ti = jnp.zeros(logits.shape, jnp.int32)
    for kk in range(TOP_K):
        mx = jnp.max(cur, axis=-1, keepdims=True)
        idx = jnp.min(jnp.where(cur == mx, lane_f, float(LANES)), axis=-1, keepdims=True)
        ti = jnp.where(lane == kk, idx.astype(jnp.int32), ti)
        cur = jnp.where(lane_f == idx, -jnp.inf, cur)
        vals.append(mx)
    es = [jnp.exp(v - vals[0]) for v in vals]
    tot = es[0] + es[1] + es[2] + es[3]
    tw = jnp.zeros(logits.shape, F32)
    for kk in range(TOP_K):
        tw = jnp.where(lane == kk, es[kk] / tot, tw)
    ti_ref[...] = ti
    tw_ref[...] = tw


def _outproj_router(a, b, x, mod, wa, wb, wr, br):
    t, d = x.shape
    wid = a.shape[1]
    row = lambda w: pl.BlockSpec((TM, w), lambda i: (i, 0))
    const = lambda r, c: pl.BlockSpec((r, c), lambda i: (0, 0))
    return pl.pallas_call(
        _outproj_router_kernel, grid=(t // TM,),
        in_specs=[row(wid), row(wid), row(d),
                  pl.BlockSpec((None, 6, d), lambda i: (_cond_row(i), 0, 0)),
                  const(wid, d), const(wid, d), const(d, LANES), const(1, LANES)],
        out_specs=[row(d), row(d), row(LANES), row(LANES)],
        out_shape=[jax.ShapeDtypeStruct((t, d), F32), jax.ShapeDtypeStruct((t, d), F32),
                   jax.ShapeDtypeStruct((t, LANES), jnp.int32), jax.ShapeDtypeStruct((t, LANES), F32)],
        compiler_params=_cparams("arbitrary"), name="outproj_router",
    )(a, b, x, mod, wa, wb, wr, br)


def _route_plan(top_i, n_tiles_max):
    t = top_i.shape[0]
    onehot = (top_i[:, :, None] == jnp.arange(N_EXP, dtype=jnp.int32)[None, None, :]).astype(jnp.int32)
    sel = onehot.sum(axis=1)
    before = jnp.cumsum(sel, axis=0) - sel
    counts = before[-1] + sel[-1]
    tiles = (counts + TM_E - 1) // TM_E
    tile_end = jnp.cumsum(tiles)
    tile_start = tile_end - tiles
    dest = jnp.take_along_axis(tile_start[None, :] * TM_E + before, top_i, axis=1)
    n_tiles = tile_end[-1]
    tid = jnp.minimum(jnp.arange(n_tiles_max, dtype=jnp.int32), n_tiles - 1)
    tile_expert = jnp.sum((tile_end[None, :] <= tid[:, None]).astype(jnp.int32), axis=1)
    first = jnp.concatenate([jnp.ones((1,), jnp.int32), (tile_expert[1:] != tile_expert[:-1]).astype(jnp.int32)])
    last_tile = jnp.where(tiles > 0, tile_end - 1, 0).astype(jnp.int32)
    return dest.astype(jnp.int32), tile_expert, first, n_tiles.reshape(1).astype(jnp.int32), last_tile


def _dispatch_kernel(lt_ref, nt_ref, dest_ref, h_ref, xs_ref, zero_s, sem):
    i = pl.program_id(0)
    n_tiles_max = xs_ref.shape[0] // TM_E

    @pl.when(i == 0)
    def _():
        zero_s[...] = jnp.zeros_like(zero_s)

        def zero_tile(tile):
            r0 = pl.multiple_of(tile * TM_E, TM_E)
            return pltpu.make_async_copy(zero_s, xs_ref.at[pl.ds(r0, TM_E), :], sem)

        def start_unused(j, carry):
            zero_tile(j).start()
            return carry

        def wait_unused(j, carry):
            zero_tile(j).wait()
            return carry

        for e in range(N_EXP):
            zero_tile(lt_ref[e]).start()
        lax.fori_loop(nt_ref[0], n_tiles_max, start_unused, 0)
        for e in range(N_EXP):
            zero_tile(lt_ref[e]).wait()
        lax.fori_loop(nt_ref[0], n_tiles_max, wait_unused, 0)

    def issue(t, carry):
        for kk in range(TOP_K):
            row = dest_ref[0, t * TOP_K + kk]
            pltpu.make_async_copy(h_ref.at[pl.ds(t, 1), :], xs_ref.at[pl.ds(row, 1), :], sem).start()
        return carry

    lax.fori_loop(0, TM, issue, 0)
    for kk in range(TOP_K):
        pltpu.make_async_copy(h_ref, xs_ref.at[pl.ds(0, TM), :], sem).wait()


def _dispatch(h2, dest, last_tile, n_tiles, n_rows):
    t, d = h2.shape
    assert TM == TM_E
    dest3 = dest.reshape(t // TM, 1, TM * TOP_K)
    return pl.pallas_call(
        _dispatch_kernel,
        grid_spec=pltpu.PrefetchScalarGridSpec(
            num_scalar_prefetch=2, grid=(t // TM,),
            in_specs=[pl.BlockSpec((None, 1, TM * TOP_K), lambda i, lt, nt: (i, 0, 0), memory_space=pltpu.SMEM),
                      pl.BlockSpec((TM, d), lambda i, lt, nt: (i, 0))],
            out_specs=pl.BlockSpec(memory_space=pl.ANY),
            scratch_shapes=[pltpu.VMEM((TM_E, d), F32), pltpu.SemaphoreType.DMA(())]),
        out_shape=jax.ShapeDtypeStruct((n_rows, d), F32),
        compiler_params=_cparams("arbitrary"), name="moe_dispatch",
    )(last_tile, n_tiles, dest3, h2)


def _expert_kernel(te_ref, tf_ref, nt_ref, xs_ref, w1_ref, b1_ref, w2_ref, b2_ref, ys_ref, w1_s, w2_s):
    i = pl.program_id(0)

    @pl.when(i < nt_ref[0])
    def _():
        @pl.when(tf_ref[i] == 1)
        def _():
            w1_s[...] = w1_ref[...].astype(BF16)
            w2_s[...] = w2_ref[...].astype(BF16)

        u = _dot(xs_ref[...].astype(BF16), w1_s[...]) + b1_ref[...]
        g = jnp.minimum(u[:, :D_FF], SWIGLU_LIMIT)
        up = jnp.clip(u[:, D_FF:], -SWIGLU_LIMIT, SWIGLU_LIMIT)
        act = (up + 1.0) * g * jax.nn.sigmoid(SWIGLU_ALPHA * g)
        ys_ref[...] = _dot(act.astype(BF16), w2_s[...]) + b2_ref[...]

    @pl.when(i >= nt_ref[0])
    def _():
        ys_ref[...] = jnp.zeros_like(ys_ref)


def _experts(xs, tile_expert, first, n_tiles, layer, w1, b1, w2, b2):
    n_rows, d = xs.shape
    nt = n_rows // TM_E
    tile = lambda i, te, tf, ntl: (jnp.minimum(i, ntl[0] - 1), 0)
    otile = lambda i, te, tf, ntl: (i, 0)
    wmap = lambda i, te, tf, ntl: (layer, te[i], 0, 0)
    return pl.pallas_call(
        _expert_kernel,
        grid_spec=pltpu.PrefetchScalarGridSpec(
            num_scalar_prefetch=3, grid=(nt,),
            in_specs=[pl.BlockSpec((TM_E, d), tile),
                      pl.BlockSpec((None, None, d, 2 * D_FF), wmap),
                      pl.BlockSpec((None, None, 1, 2 * D_FF), wmap),
                      pl.BlockSpec((None, None, D_FF, d), wmap),
                      pl.BlockSpec((None, None, 1, d), wmap)],
            out_specs=pl.BlockSpec((TM_E, d), otile),
            scratch_shapes=[pltpu.VMEM((d, 2 * D_FF), BF16), pltpu.VMEM((D_FF, d), BF16)]),
        out_shape=jax.ShapeDtypeStruct((n_rows, d), F32),
        compiler_params=_cparams("arbitrary"), name="moe_experts",
    )(tile_expert, first, n_tiles, xs, w1, b1, w2, b2)


def _combine_kernel(dest_ref, x1_ref, tw_ref, mod_ref, ys_ref, o_ref, buf, sem):
    def issue(t, carry):
        for kk in range(TOP_K):
            row = dest_ref[0, t * TOP_K + kk]
            pltpu.make_async_copy(ys_ref.at[pl.ds(row, 1), :], buf.at[kk, pl.ds(t, 1), :], sem).start()
        return carry

    lax.fori_loop(0, TM, issue, 0)
    for kk in range(TOP_K):
        pltpu.make_async_copy(ys_ref.at[pl.ds(0, TM), :], buf.at[kk], sem).wait()
    tw = tw_ref[...]
    y = tw[:, 0:1] * buf[0]
    for kk in range(1, TOP_K):
        y = y + tw[:, kk:kk + 1] * buf[kk]
    o_ref[...] = x1_ref[...] + mod_ref[5:6, :] * y


def _combine(ys, dest, x1, tw, mod):
    t, d = x1.shape
    dest3 = dest.reshape(t // TM, 1, TM * TOP_K)
    row = lambda w: pl.BlockSpec((TM, w), lambda i: (i, 0))
    return pl.pallas_call(
        _combine_kernel, grid=(t // TM,),
        in_specs=[pl.BlockSpec((None, 1, TM * TOP_K), lambda i: (i, 0, 0), memory_space=pltpu.SMEM),
                  row(d), row(LANES),
                  pl.BlockSpec((None, 6, d), lambda i: (_cond_row(i), 0, 0)),
                  pl.BlockSpec(memory_space=pl.ANY)],
        out_specs=row(d),
        out_shape=jax.ShapeDtypeStruct((t, d), F32),
        scratch_shapes=[pltpu.VMEM((TOP_K, TM, d), F32), pltpu.SemaphoreType.DMA(())],
        compiler_params=_cparams("arbitrary"), name="moe_combine",
    )(dest3, x1, tw, mod, ys)


def _moe(x1, h2, ti, tw, mod, layer, w1, b1, w2, b2):
    t = x1.shape[0]
    depth = w1.shape[0]
    n_tiles_max = t * TOP_K // TM_E + N_EXP
    dest, tile_expert, first, n_tiles, last_tile = _route_plan(ti[:, :TOP_K], n_tiles_max)
    xs = _dispatch(h2, dest, last_tile, n_tiles, n_tiles_max * TM_E)
    ys = _experts(xs, tile_expert, first, n_tiles, layer,
                  w1, b1.reshape(depth, N_EXP, 1, -1), w2, b2.reshape(depth, N_EXP, 1, -1))
    return _combine(ys, dest, x1, tw, mod)


def _pad_lanes(a, value=0.0):
    return jnp.pad(a, ((0, 0), (0, LANES - a.shape[1])), constant_values=value)


def kernel(x_prompt, x_sample, cache_na_k, cache_na_v, state_mlstm_C, state_mlstm_n, state_mlstm_m, state_ret_S, c, c_ctx, w_mod, b_mod, w_in_even, mlstm_gate_b, na_q_norm, na_k_norm, na_rpb, mlstm_norm, w_out_even, w_in_odd, ret_decay, ret_norm, w_out_odd, w_router, b_router, w_moe_in, b_moe_in, w_moe_out, b_moe_out):
    nb_c, s_c, d = x_prompt.shape
    nb_l, s_l, _ = x_sample.shape
    t_c = nb_c * s_c
    t_l = nb_l * s_l
    assert t_c == 4 * SEG and s_l == SEG and d == D_MODEL
    depth = w_mod.shape[0]
    dt = x_prompt.dtype

    x = jnp.concatenate([x_prompt.reshape(t_c, d), x_sample.reshape(t_l, d)], axis=0)
    cond = jnp.concatenate([c_ctx[None, :], c, jnp.zeros((N_COND - 1 - nb_l, d), F32)], axis=0)
    mod = _modulation(cond, w_mod, b_mod).reshape(depth, N_COND, 6, d)

    outs = {}
    for l in range(depth):
        e = l // 2
        mod_l = mod[l]
        if l % 2 == 0:
            w_in = w_in_even[e]
            n_main = 3 * W_A + 4 * W_B
            wg = _pad_lanes(w_in[:, n_main:])
            bg = _pad_lanes(mlstm_gate_b[e].reshape(1, 4 * H_B))
            z, g = _inproj(x, mod_l, w_in[:, :n_main].astype(BF16), wg, bg)
            qn = na_q_norm[e].reshape(1, HD_A)
            kn = na_k_norm[e].reshape(1, HD_A)
            oa_c, ka_c = _ctx_attention(z, nb_c, s_c, qn, kn)
            past = cache_na_k.shape[2]
            oa_l = _na_attention(z, t_c, nb_l, s_l,
                                 cache_na_k[:, e].reshape(nb_l, past, W_A), cache_na_v[:, e].reshape(nb_l, past, W_A),
                                 _na_bias_table(na_rpb[e]), qn, kn)
            nw = mlstm_norm[e].reshape(1, W_B)
            hm_c, c_fin, n_fin, m_fin = _mlstm(z, g, 0, nb_c, s_c, nw)
            init = (state_mlstm_C[:, e].reshape(nb_l, 2 * H_B, HD_B, HD_B),
                    state_mlstm_n[:, e].reshape(nb_l, 2 * H_B, HD_B),
                    jnp.broadcast_to(state_mlstm_m[:, e].reshape(nb_l, 2 * H_B, 1), (nb_l, 2 * H_B, LANES)))
            hm_l = _mlstm(z, g, t_c, nb_l, s_l, nw, init)[0]
            a = jnp.concatenate([oa_c, oa_l], axis=0)
            b = jnp.concatenate([hm_c, hm_l], axis=0)
            w_out = w_out_even[e].astype(BF16)
            wa, wb = w_out[:W_A], w_out[W_A:]
            outs.setdefault("na_k", []).append(ka_c.reshape(nb_c, s_c, H_A, HD_A))
            outs.setdefault("na_v", []).append(z[:t_c, 2 * W_A:3 * W_A].reshape(nb_c, s_c, H_A, HD_A))
            outs.setdefault("C", []).append(c_fin.reshape(nb_c, 2, H_B, HD_B, HD_B))
            outs.setdefault("n", []).append(n_fin.reshape(nb_c, 2, H_B, HD_B))
            outs.setdefault("m", []).append(m_fin[:, :, 0].reshape(nb_c, 2, H_B))
        else:
            w_c = H_C * HD_C
            z = _inproj(x, mod_l, w_in_odd[e].astype(BF16))
            dl_rep = jnp.broadcast_to(ret_decay[e].reshape(2 * H_C, 1), (2 * H_C, LANES))
            nw = ret_norm[e].reshape(1, w_c)
            hr_c, s_fin = _retention(z, 0, nb_c, s_c, dl_rep, nw)
            hr_l = _retention(z, t_c, nb_l, s_l, dl_rep, nw, rope=_rope_tables(s_l),
                              init=state_ret_S[:, e].reshape(nb_l, 2 * H_C, HD_C, HD_C))[0]
            fd_c = _fnet(z, 0, nb_c, s_c)
            fd_l = _fnet(z, t_c, nb_l, s_l)
            a = jnp.concatenate([hr_c, hr_l], axis=0)
            b = jnp.concatenate([fd_c, fd_l], axis=0)
            w_out = w_out_odd[e].astype(BF16)
            wa, wb = w_out[:w_c], w_out[w_c:]
            outs.setdefault("S", []).append(s_fin.reshape(nb_c, 2, H_C, HD_C, HD_C))
        wr = _pad_lanes(w_router[l])
        br = _pad_lanes(b_router[l].reshape(1, N_EXP), NEG)
        x1, h2, ti, tw = _outproj_router(a, b, x, mod_l, wa, wb, wr, br)
        x = _moe(x1, h2, ti, tw, mod_l, l, w_moe_in, b_moe_in, w_moe_out, b_moe_out)

    y_prompt = x[:t_c].reshape(nb_c, s_c, d)
    y_sample = x[t_c:].reshape(nb_l, s_l, d)
    stack = lambda key: jnp.stack(outs[key], axis=1).astype(dt)
    return (y_prompt, y_sample, stack("na_k"), stack("na_v"), stack("C"), stack("n"), stack("m"), stack("S"))
```

```python
import functools
import math

import numpy as np
import jax
import jax.numpy as jnp
from jax import lax
from jax.experimental import pallas as pl
from jax.experimental.pallas import tpu as pltpu

F32 = jnp.float32
BF16 = jnp.bfloat16
HIGHEST = lax.Precision.HIGHEST

D_MODEL = 1024
GRID_W = 64
WIN_R = 8
WIN_C = 16
H_A, HD_A = 8, 64
H_B, HD_B = 4, 128
H_C, HD_C = 4, 128
N_FG, FG_W = 4, 128
W_A = H_A * HD_A
W_B = H_B * HD_B
N_EXP = 32
TOP_K = 4
D_FF = D_MODEL
SWIGLU_LIMIT = 7.0
SWIGLU_ALPHA = 1.702
CHUNK = 128
ROPE_BASE = 10000.0
EPS = 1e-6

LANES = 128
SEG = 1024
N_COND = 8
TM = 256
TM_E = 256
NEG = -1e30
VMEM_LIMIT = 56 * 1024 * 1024


def _cparams(*sem):
    return pltpu.CompilerParams(dimension_semantics=sem, vmem_limit_bytes=VMEM_LIMIT)


def _cond_row(i):
    return jnp.maximum((i * TM) // SEG - 3, 0)


def _log_sigmoid(x):
    return jnp.minimum(x, 0.0) - jnp.log1p(jnp.exp(-jnp.abs(x)))


def _dot(a, b):
    return jnp.dot(a, b, preferred_element_type=F32)


def _dot_nt(a, b):
    return lax.dot_general(a, b, (((1,), (1,)), ((), ())), preferred_element_type=F32)


def _dot_hi(a, b):
    return jnp.dot(a, b, precision=HIGHEST, preferred_element_type=F32)


def _mod_kernel(cond_ref, w_ref, b_ref, o_ref):
    c = cond_ref[...]
    s = c * jax.nn.sigmoid(c)
    o_ref[...] = _dot_hi(s, w_ref[...]) + b_ref[...]


def _modulation(cond, w_mod, b_mod):
    depth, d, n = w_mod.shape
    tn = 1536
    return pl.pallas_call(
        _mod_kernel,
        grid=(depth, n // tn),
        in_specs=[pl.BlockSpec((N_COND, d), lambda l, j: (0, 0)),
                  pl.BlockSpec((None, d, tn), lambda l, j: (l, 0, j)),
                  pl.BlockSpec((None, 1, tn), lambda l, j: (l, 0, j))],
        out_specs=pl.BlockSpec((None, N_COND, tn), lambda l, j: (l, 0, j)),
        out_shape=jax.ShapeDtypeStruct((depth, N_COND, n), F32),
        compiler_params=_cparams("arbitrary", "arbitrary"),
        name="modulation",
    )(cond, w_mod, b_mod.reshape(depth, 1, n))


def _rms_mod(x, shift, scale):
    h = x * lax.rsqrt(jnp.mean(x * x, axis=-1, keepdims=True) + EPS)
    return h * (1.0 + scale) + shift


def _inproj_kernel(x_ref, mod_ref, w_ref, z_ref):
    h = _rms_mod(x_ref[...], mod_ref[0:1, :], mod_ref[1:2, :])
    z_ref[...] = _dot(h.astype(BF16), w_ref[...])


def _inproj_gate_kernel(x_ref, mod_ref, w_ref, wg_ref, bg_ref, z_ref, g_ref):
    h = _rms_mod(x_ref[...], mod_ref[0:1, :], mod_ref[1:2, :])
    z_ref[...] = _dot(h.astype(BF16), w_ref[...])
    g_ref[...] = _dot_hi(h, wg_ref[...]) + bg_ref[...]


def _inproj(x, mod, w, wg=None, bg=None):
    t, d = x.shape
    n = w.shape[1]
    in_specs = [pl.BlockSpec((TM, d), lambda i: (i, 0)),
                pl.BlockSpec((None, 6, d), lambda i: (_cond_row(i), 0, 0)),
                pl.BlockSpec((d, n), lambda i: (0, 0))]
    z_spec = pl.BlockSpec((TM, n), lambda i: (i, 0))
    z_shape = jax.ShapeDtypeStruct((t, n), F32)
    if wg is None:
        return pl.pallas_call(
            _inproj_kernel, grid=(t // TM,), in_specs=in_specs, out_specs=z_spec, out_shape=z_shape,
            compiler_params=_cparams("arbitrary"), name="inproj",
        )(x, mod, w)
    in_specs += [pl.BlockSpec((d, LANES), lambda i: (0, 0)), pl.BlockSpec((1, LANES), lambda i: (0, 0))]
    return pl.pallas_call(
        _inproj_gate_kernel, grid=(t // TM,), in_specs=in_specs,
        out_specs=[z_spec, pl.BlockSpec((TM, LANES), lambda i: (i, 0))],
        out_shape=[z_shape, jax.ShapeDtypeStruct((t, LANES), F32)],
        compiler_params=_cparams("arbitrary"), name="inproj_gate",
    )(x, mod, w, wg, bg)


def _head_rms(x, w):
    return x * lax.rsqrt(jnp.mean(x * x, axis=-1, keepdims=True) + EPS) * w


def _ctx_attn_kernel(q_ref, k_ref, v_ref, qn_ref, kn_ref, o_ref, ko_ref):
    scale = HD_A ** -0.5
    for h in range(H_A):
        sl = slice(h * HD_A, (h + 1) * HD_A)
        q = _head_rms(q_ref[:, sl], qn_ref[...])
        k = _head_rms(k_ref[:, sl], kn_ref[...])
        ko_ref[:, sl] = k
        s = _dot_nt(q.astype(BF16), k.astype(BF16)) * scale
        p = jnp.exp(s - jnp.max(s, axis=-1, keepdims=True))
        den = jnp.sum(p, axis=-1, keepdims=True)
        o_ref[:, sl] = _dot(p.astype(BF16), v_ref[:, sl].astype(BF16)) / den


def _ctx_attention(z, n_batch, seq, qn, kn):
    spec = lambda c: pl.BlockSpec((seq, W_A), lambda b: (b, c))
    wspec = pl.BlockSpec((1, HD_A), lambda b: (0, 0))
    out = jax.ShapeDtypeStruct((n_batch * seq, W_A), F32)
    return pl.pallas_call(
        _ctx_attn_kernel, grid=(n_batch,),
        in_specs=[spec(0), spec(1), spec(2), wspec, wspec],
        out_specs=[pl.BlockSpec((seq, W_A), lambda b: (b, 0))] * 2,
        out_shape=[out, out],
        compiler_params=_cparams("arbitrary"), name="ctx_attention",
    )(z, z, z, qn, kn)


def _na_bias_table(rpb):
    qc = np.arange(GRID_W)
    kc = np.arange(GRID_W)
    cstart = np.clip(qc - WIN_C // 2, 0, GRID_W - WIN_C)
    col_in = (kc[None, :] >= cstart[:, None]) & (kc[None, :] < cstart[:, None] + WIN_C)
    dc = np.clip(kc[None, :] - qc[:, None], 1 - WIN_C, WIN_C - 1) + WIN_C - 1
    cls = np.arange(WIN_R)
    j = np.arange(WIN_R)
    dr = j[None, :] - cls[:, None] + WIN_R - 1
    sel_r = jnp.asarray(dr[:, :, None] == np.arange(2 * WIN_R - 1)[None, None, :], F32)
    sel_c = jnp.asarray(dc[:, :, None] == np.arange(2 * WIN_C - 1)[None, None, :], F32)
    tab = jnp.einsum("hab,cja,qkb->hcqjk", rpb, sel_r, sel_c, precision=HIGHEST)
    tab = jnp.where(jnp.asarray(col_in)[None, None, :, None, :], tab, NEG)
    return tab.reshape(H_A, WIN_R, GRID_W, WIN_R * GRID_W)


def _na_kernel(q_ref, k_ref, v_ref, kc_ref, vc_ref, bias_ref, qn_ref, kn_ref, o_ref, kn_s, *, rows):
    r = pl.program_id(1)
    scale = HD_A ** -0.5

    @pl.when(r == 0)
    def _():
        for h in range(H_A):
            sl = slice(h * HD_A, (h + 1) * HD_A)
            kn_s[:, sl] = _head_rms(k_ref[:, sl], kn_ref[...])

    rs = jnp.clip(r - WIN_R // 2, 0, rows - WIN_R)
    start = pl.multiple_of(rs * GRID_W, GRID_W)
    n_loc = WIN_R * GRID_W
    for h in range(H_A):
        sl = slice(h * HD_A, (h + 1) * HD_A)
        q = _head_rms(q_ref[:, sl], qn_ref[...]).astype(BF16)
        kl = kn_s[pl.ds(start, n_loc), sl].astype(BF16)
        vl = v_ref[pl.ds(start, n_loc), sl].astype(BF16)
        s_loc = _dot_nt(q, kl) * scale + bias_ref[h]
        s_ctx = _dot_nt(q, kc_ref[:, sl].astype(BF16)) * scale
        m = jnp.maximum(jnp.max(s_loc, axis=-1, keepdims=True), jnp.max(s_ctx, axis=-1, keepdims=True))
        p_loc = jnp.exp(s_loc - m)
        p_ctx = jnp.exp(s_ctx - m)
        den = jnp.sum(p_loc, axis=-1, keepdims=True) + jnp.sum(p_ctx, axis=-1, keepdims=True)
        o = _dot(p_loc.astype(BF16), vl) + _dot(p_ctx.astype(BF16), vc_ref[:, sl].astype(BF16))
        o_ref[:, sl] = o / den


def _na_attention(z, row0, n_batch, seq, kc, vc, bias, qn, kn):
    rows = seq // GRID_W
    past = kc.shape[1]
    blk0 = row0 // GRID_W
    sblk0 = row0 // seq

    def cls_of(r):
        return r - jnp.clip(r - WIN_R // 2, 0, rows - WIN_R)

    full = lambda c: pl.BlockSpec((seq, W_A), lambda b, r: (sblk0 + b, c))
    cspec = pl.BlockSpec((None, past, W_A), lambda b, r: (b, 0, 0))
    wspec = pl.BlockSpec((1, HD_A), lambda b, r: (0, 0))
    return pl.pallas_call(
        functools.partial(_na_kernel, rows=rows), grid=(n_batch, rows),
        in_specs=[pl.BlockSpec((GRID_W, W_A), lambda b, r: (blk0 + b * rows + r, 0)),
                  full(1), full(2), cspec, cspec,
                  pl.BlockSpec((H_A, None, GRID_W, WIN_R * GRID_W), lambda b, r: (0, cls_of(r), 0, 0)),
                  wspec, wspec],
        out_specs=pl.BlockSpec((GRID_W, W_A), lambda b, r: (b * rows + r, 0)),
        out_shape=jax.ShapeDtypeStruct((n_batch * seq, W_A), F32),
        scratch_shapes=[pltpu.VMEM((seq, W_A), F32)],
        compiler_params=_cparams("arbitrary", "arbitrary"), name="na_attention",
    )(z, z, z, kc, vc, bias, qn, kn)


def _tri_masks():
    li = lax.broadcasted_iota(jnp.int32, (CHUNK, CHUNK), 0)
    si = lax.broadcasted_iota(jnp.int32, (CHUNK, CHUNK), 1)
    return li >= si, li <= si


def _mlstm_kernel(*refs, nc, has_init):
    if has_init:
        (q_ref, k_ref, v_ref, og_ref, g_ref, nw_ref, c0_ref, n0_ref, m0_ref,
         o_ref, cf_ref, nf_ref, mf_ref, h_s, c_s, n_s, m_s) = refs
    else:
        (q_ref, k_ref, v_ref, og_ref, g_ref, nw_ref,
         o_ref, cf_ref, nf_ref, mf_ref, h_s, c_s, n_s, m_s) = refs
    nd = 2 * H_B
    if has_init:
        c_s[...] = c0_ref[...]
        n_s[...] = n0_ref[...]
        m_s[...] = m0_ref[...]
    else:
        c_s[...] = jnp.zeros_like(c_s)
        n_s[...] = jnp.zeros_like(n_s)
        m_s[...] = jnp.zeros_like(m_s)

    causal, anti = _tri_masks()
    tri_f = causal.astype(F32)
    tri_b = anti.astype(F32)
    kscale = HD_B ** -0.5

    def chunk_step(c, carry):
        for d in range(2):
            cc = c if d == 0 else nc - 1 - c
            t0 = pl.multiple_of(cc * CHUNK, CHUNK)
            g = g_ref[pl.ds(t0, CHUNK), :]
            gt = g.T
            ls = _log_sigmoid(g)
            lst = _log_sigmoid(gt)
            tri_c, tri_r, mask = (tri_f, tri_b, causal) if d == 0 else (tri_b, tri_f, anti)
            b_cols = _dot_hi(tri_c, ls)
            b_rows = _dot_hi(lst, tri_r)
            last = CHUNK - 1 if d == 0 else 0
            for h in range(H_B):
                ci = (2 * d) * H_B + h
                cf = (2 * d + 1) * H_B + h
                hs = slice(h * HD_B, (h + 1) * HD_B)
                q = q_ref[pl.ds(t0, CHUNK), hs]
                k = k_ref[pl.ds(t0, CHUNK), hs] * kscale
                v = v_ref[pl.ds(t0, CHUNK), hs]
                qb, kb, vb = q.astype(BF16), k.astype(BF16), v.astype(BF16)
                b_col = b_cols[:, cf:cf + 1]
                b_row = b_rows[cf:cf + 1, :]
                i_row = gt[ci:ci + 1, :]
                sidx = d * H_B + h
                cst = c_s[sidx]
                nst = n_s[sidx:sidx + 1, :]
                mst = m_s[sidx:sidx + 1, 0:1]
                dmat = jnp.where(mask, b_col - b_row + i_row, -jnp.inf)
                inter = b_col + mst
                mt = jnp.maximum(inter, jnp.max(dmat, axis=-1, keepdims=True))
                w = jnp.exp(dmat - mt) * _dot_nt(qb, kb)
                a = jnp.exp(inter - mt)
                num = _dot(w.astype(BF16), vb) + _dot(qb, cst.astype(BF16)) * a
                den = jnp.sum(w, axis=-1, keepdims=True) + a * jnp.sum(q * nst, axis=-1, keepdims=True)
                hc = num / jnp.maximum(jnp.abs(den), jnp.exp(-mt))
                h_s[d, pl.ds(t0, CHUNK), hs] = hc
                bl = b_row[:, last:last + 1]
                dl = bl - b_row + i_row
                m_new = jnp.maximum(bl + mst, jnp.max(dl, axis=-1, keepdims=True))
                wl = jnp.exp(dl - m_new)
                dec = jnp.exp(bl + mst - m_new)
                kw = (k.T * wl).astype(BF16)
                c_s[sidx] = dec * cst + _dot(kw, vb)
                wl8 = jnp.broadcast_to(wl, (8, CHUNK)).astype(BF16)
                n_s[sidx:sidx + 1, :] = dec * nst + _dot(wl8, kb)[0:1, :]
                m_s[sidx:sidx + 1, :] = jnp.broadcast_to(m_new, (1, LANES))
        return carry

    lax.fori_loop(0, nc, chunk_step, 0)

    for h in range(H_B):
        hs = slice(h * HD_B, (h + 1) * HD_B)
        hsum = h_s[0, :, hs] + h_s[1, :, hs]
        o_ref[:, hs] = _head_rms(hsum, nw_ref[:, hs]) * jax.nn.sigmoid(og_ref[:, hs])
    cf_ref[...] = c_s[...]
    nf_ref[...] = n_s[...]
    mf_ref[...] = m_s[...]


def _mlstm(z, g, row0, n_batch, seq, norm_w, init=None):
    sblk0 = row0 // seq
    nd = 2 * H_B
    spec = lambda c: pl.BlockSpec((seq, W_B), lambda b: (sblk0 + b, c))
    in_specs = [spec(3), spec(4), spec(5), spec(6),
                pl.BlockSpec((seq, LANES), lambda b: (sblk0 + b, 0)),
                pl.BlockSpec((1, W_B), lambda b: (0, 0))]
    args = [z, z, z, z, g, norm_w]
    st_specs = [pl.BlockSpec((None, nd, HD_B, HD_B), lambda b: (b, 0, 0, 0)),
                pl.BlockSpec((None, nd, HD_B), lambda b: (b, 0, 0)),
                pl.BlockSpec((None, nd, LANES), lambda b: (b, 0, 0))]
    if init is not None:
        in_specs += st_specs
        args += list(init)
    return pl.pallas_call(
        functools.partial(_mlstm_kernel, nc=seq // CHUNK, has_init=init is not None), grid=(n_batch,),
        in_specs=in_specs,
        out_specs=[pl.BlockSpec((seq, W_B), lambda b: (b, 0))] + st_specs,
        out_shape=[jax.ShapeDtypeStruct((n_batch * seq, W_B), F32),
                   jax.ShapeDtypeStruct((n_batch, nd, HD_B, HD_B), F32),
                   jax.ShapeDtypeStruct((n_batch, nd, HD_B), F32),
                   jax.ShapeDtypeStruct((n_batch, nd, LANES), F32)],
        scratch_shapes=[pltpu.VMEM((2, seq, W_B), F32), pltpu.VMEM((nd, HD_B, HD_B), F32),
                        pltpu.VMEM((nd, HD_B), F32), pltpu.VMEM((nd, LANES), F32)],
        compiler_params=_cparams("arbitrary"), name="mlstm",
    )(*args)


def _rope_tables(seq):
    half = HD_C // 2
    quarter = half // 2
    t = np.arange(seq)
    inv = ROPE_BASE ** (-np.arange(0, half, 2, dtype=np.float64) / half)
    ang_r = (t // GRID_W)[:, None] * inv[None, :]
    ang_c = (t % GRID_W)[:, None] * inv[None, :]
    cos_t = np.concatenate([np.cos(ang_r), np.cos(ang_r), np.cos(ang_c), np.cos(ang_c)], -1)
    sin_t = np.concatenate([-np.sin(ang_r), np.sin(ang_r), -np.sin(ang_c), np.sin(ang_c)], -1)
    assert cos_t.shape == (seq, 4 * quarter)
    return jnp.asarray(cos_t, F32), jnp.asarray(sin_t, F32)


def _rope(x, cos_t, sin_t):
    quarter = HD_C // 4
    lane = lax.broadcasted_iota(jnp.int32, x.shape, 1)
    first = (lane % (2 * quarter)) < quarter
    swapped = jnp.where(first, pltpu.roll(x, HD_C - quarter, 1), pltpu.roll(x, quarter, 1))
    return x * cos_t + swapped * sin_t


def _ret_kernel(*refs, nc, has_init, use_rope):
    refs = list(refs)
    q_ref, k_ref, v_ref, gg_ref, dl_ref, nw_ref = refs[:6]
    pos = 6
    if use_rope:
        cos_ref, sin_ref = refs[pos:pos + 2]
        pos += 2
    if has_init:
        s0_ref = refs[pos]
        pos += 1
    o_ref, sf_ref, h_s, s_s = refs[pos:pos + 4]
    if has_init:
        s_s[...] = s0_ref[...]
    else:
        s_s[...] = jnp.zeros_like(s_s)

    causal, anti = _tri_masks()
    li = lax.broadcasted_iota(jnp.int32, (CHUNK, CHUNK), 0).astype(F32)
    si = lax.broadcasted_iota(jnp.int32, (CHUNK, CHUNK), 1).astype(F32)
    lg_all = _log_sigmoid(dl_ref[...])
    kscale = HD_C ** -0.5

    def chunk_step(c, carry):
        for d in range(2):
            cc = c if d == 0 else nc - 1 - c
            t0 = pl.multiple_of(cc * CHUNK, CHUNK)
            for h in range(H_C):
                sidx = d * H_C + h
                hs = slice(h * HD_C, (h + 1) * HD_C)
                lg = lg_all[sidx:sidx + 1, :]
                q = q_ref[pl.ds(t0, CHUNK), hs]
                k = k_ref[pl.ds(t0, CHUNK), hs] * kscale
                v = v_ref[pl.ds(t0, CHUNK), hs]
                if use_rope:
                    cos_t = cos_ref[pl.ds(t0, CHUNK), :]
                    sin_t = sin_ref[pl.ds(t0, CHUNK), :]
                    q = _rope(q, cos_t, sin_t)
                    k = _rope(k, cos_t, sin_t)
                if d == 0:
                    decay = jnp.exp(jnp.where(causal, (li - si) * lg, -jnp.inf))
                    q_dec = jnp.exp((li + 1.0) * lg)
                    k_dec = jnp.exp((CHUNK - 1.0 - li) * lg)
                else:
                    decay = jnp.exp(jnp.where(anti, (si - li) * lg, -jnp.inf))
                    q_dec = jnp.exp((CHUNK - li) * lg)
                    k_dec = jnp.exp(li * lg)
                c_dec = jnp.exp(CHUNK * lg)
                st = s_s[sidx]
                qb, kb, vb = q.astype(BF16), k.astype(BF16), v.astype(BF16)
                att = _dot_nt(qb, kb) * decay
                o = _dot(att.astype(BF16), vb) + _dot(qb, st.astype(BF16)) * q_dec
                h_s[d, pl.ds(t0, CHUNK), hs] = o
                kd = (k * k_dec).T.astype(BF16)
                s_s[sidx] = c_dec * st + _dot(kd, vb)
        return carry

    lax.fori_loop(0, nc, chunk_step, 0)

    for h in range(H_C):
        hs = slice(h * HD_C, (h + 1) * HD_C)
        osum = h_s[0, :, hs] + h_s[1, :, hs]
        gg = gg_ref[:, hs]
        o_ref[:, hs] = _head_rms(osum, nw_ref[:, hs]) * (gg * jax.nn.sigmoid(gg))
    sf_ref[...] = s_s[...]


def _retention(z, row0, n_batch, seq, decay_rep, norm_w, rope=None, init=None):
    sblk0 = row0 // seq
    nd = 2 * H_C
    w_c = H_C * HD_C
    spec = lambda c: pl.BlockSpec((seq, w_c), lambda b: (sblk0 + b, c))
    in_specs = [spec(0), spec(1), spec(2), spec(3),
                pl.BlockSpec((nd, LANES), lambda b: (0, 0)),
                pl.BlockSpec((1, w_c), lambda b: (0, 0))]
    args = [z, z, z, z, decay_rep, norm_w]
    if rope is not None:
        in_specs += [pl.BlockSpec((seq, HD_C), lambda b: (0, 0))] * 2
        args += list(rope)
    st_spec = pl.BlockSpec((None, nd, HD_C, HD_C), lambda b: (b, 0, 0, 0))
    if init is not None:
        in_specs.append(st_spec)
        args.append(init)
    return pl.pallas_call(
        functools.partial(_ret_kernel, nc=seq // CHUNK, has_init=init is not None, use_rope=rope is not None),
        grid=(n_batch,), in_specs=in_specs,
        out_specs=[pl.BlockSpec((seq, w_c), lambda b: (b, 0)), st_spec],
        out_shape=[jax.ShapeDtypeStruct((n_batch * seq, w_c), F32),
                   jax.ShapeDtypeStruct((n_batch, nd, HD_C, HD_C), F32)],
        scratch_shapes=[pltpu.VMEM((2, seq, w_c), F32), pltpu.VMEM((nd, HD_C, HD_C), F32)],
        compiler_params=_cparams("arbitrary"), name="retention",
    )(*args)


def _dft_tables(n):
    idx = (np.arange(n)[:, None] * np.arange(n)[None, :]) % n
    ang = 2.0 * np.pi * idx / n
    return np.cos(ang) / np.sqrt(n), np.sin(ang) / np.sqrt(n)


def _fnet_kernel(x_ref, cw_ref, sw_ref, cs_ref, ss_ref, o_ref):
    for g in range(N_FG):
        gs = slice(g * FG_W, (g + 1) * FG_W)
        x = x_ref[:, gs].astype(BF16)
        xc = _dot(x, cw_ref[...]).astype(BF16)
        xs = _dot(x, sw_ref[...]).astype(BF16)
        o_ref[:, gs] = _dot(cs_ref[...], xc) - _dot(ss_ref[...], xs)


def _fnet(z, row0, n_batch, seq):
    sblk0 = row0 // seq
    w_d = N_FG * FG_W
    cw, sw = _dft_tables(FG_W)
    cs, ss = _dft_tables(seq)
    tabs = [jnp.asarray(a, F32).astype(BF16) for a in (cw, sw, cs, ss)]
    wspec = pl.BlockSpec((FG_W, FG_W), lambda b: (0, 0))
    sspec = pl.BlockSpec((seq, seq), lambda b: (0, 0))
    return pl.pallas_call(
        _fnet_kernel, grid=(n_batch,),
        in_specs=[pl.BlockSpec((seq, w_d), lambda b: (sblk0 + b, 4)), wspec, wspec, sspec, sspec],
        out_specs=pl.BlockSpec((seq, w_d), lambda b: (b, 0)),
        out_shape=jax.ShapeDtypeStruct((n_batch * seq, w_d), F32),
        compiler_params=_cparams("arbitrary"), name="fnet",
    )(z, *tabs)


def _outproj_router_kernel(a_ref, b_ref, x_ref, mod_ref, wa_ref, wb_ref, wr_ref, br_ref,
                           x1_ref, h2_ref, ti_ref, tw_ref):
    y = _dot(a_ref[...].astype(BF16), wa_ref[...]) + _dot(b_ref[...].astype(BF16), wb_ref[...])
    x1 = x_ref[...] + mod_ref[2:3, :] * y
    x1_ref[...] = x1
    h2 = _rms_mod(x1, mod_ref[3:4, :], mod_ref[4:5, :])
    h2_ref[...] = h2
    logits = _dot_hi(h2, wr_ref[...]) + br_ref[...]
    lane = lax.broadcasted_iota(jnp.int32, logits.shape, 1)
    lane_f = lane.astype(F32)
    cur = logits
    vals = []
    ti = jnp.zeros(logits.shape, jnp.int32)
    for kk in range(TOP_K):
        mx = jnp.max(cur, axis=-1, keepdims=True)
        idx = jnp.min(jnp.where(cur == mx, lane_f, float(LANES)), axis=-1, keepdims=True)
        ti = jnp.where(lane == kk, idx.astype(jnp.int32), ti)
        cur = jnp.where(lane_f == idx, -jnp.inf, cur)
        vals.append(mx)
    es = [jnp.exp(v - vals[0]) for v in vals]
    tot = es[0] + es[1] + es[2] + es[3]
    tw = jnp.zeros(logits.shape, F32)
    for kk in range(TOP_K):
        tw = jnp.where(lane == kk, es[kk] / tot, tw)
    ti_ref[...] = ti
    tw_ref[...] = tw


def _outproj_router(a, b, x, mod, wa, wb, wr, br):
    t, d = x.shape
    wid = a.shape[1]
    row = lambda w: pl.BlockSpec((TM, w), lambda i: (i, 0))
    const = lambda r, c: pl.BlockSpec((r, c), lambda i: (0, 0))
    return pl.pallas_call(
        _outproj_router_kernel, grid=(t // TM,),
        in_specs=[row(wid), row(wid), row(d),
                  pl.BlockSpec((None, 6, d), lambda i: (_cond_row(i), 0, 0)),
                  const(wid, d), const(wid, d), const(d, LANES), const(1, LANES)],
        out_specs=[row(d), row(d), row(LANES), row(LANES)],
        out_shape=[jax.ShapeDtypeStruct((t, d), F32), jax.ShapeDtypeStruct((t, d), F32),
                   jax.ShapeDtypeStruct((t, LANES), jnp.int32), jax.ShapeDtypeStruct((t, LANES), F32)],
        compiler_params=_cparams("arbitrary"), name="outproj_router",
    )(a, b, x, mod, wa, wb, wr, br)


def _route_plan(top_i, n_tiles_max):
    t = top_i.shape[0]
    onehot = (top_i[:, :, None] == jnp.arange(N_EXP, dtype=jnp.int32)[None, None, :]).astype(jnp.int32)
    sel = onehot.sum(axis=1)
    before = jnp.cumsum(sel, axis=0) - sel
    counts = before[-1] + sel[-1]
    tiles = (counts + TM_E - 1) // TM_E
    tile_end = jnp.cumsum(tiles)
    tile_start = tile_end - tiles
    dest = jnp.take_along_axis(tile_start[None, :] * TM_E + before, top_i, axis=1)
    n_tiles = tile_end[-1]
    tid = jnp.minimum(jnp.arange(n_tiles_max, dtype=jnp.int32), n_tiles - 1)
    tile_expert = jnp.sum((tile_end[None, :] <= tid[:, None]).astype(jnp.int32), axis=1)
    first = jnp.concatenate([jnp.ones((1,), jnp.int32), (tile_expert[1:] != tile_expert[:-1]).astype(jnp.int32)])
    n_rows = n_tiles_max * TM_E
    flat = jnp.arange(t * TOP_K, dtype=jnp.int32)
    inv = jnp.full((n_rows,), -1, jnp.int32).at[dest.reshape(-1)].set(flat, unique_indices=True)
    row = jnp.arange(n_rows, dtype=jnp.int32)
    trash = TOP_K * t + ((row // TM_E) % 2) * TM_E + row % TM_E
    src = jnp.where(inv >= 0, inv // TOP_K, 0)
    dst = jnp.where(inv >= 0, (inv % TOP_K) * t + inv // TOP_K, trash)
    return (src.reshape(n_tiles_max, 1, TM_E), dst.reshape(n_tiles_max, 1, TM_E), tile_expert, first,
            n_tiles.reshape(1).astype(jnp.int32))


def _expert_kernel(te_ref, tf_ref, nt_ref, src_ref, dst_ref, h_ref, w1_ref, b1_ref, w2_ref, b2_ref, yk_ref,
                   xbuf, ybuf, w1_s, w2_s, gsem, ssem):
    i = pl.program_id(0)
    j = i - 1
    nt = nt_ref[0]
    n_slab_rows = yk_ref.shape[0] - 2 * TM_E

    def gather_wait(slot):
        pltpu.make_async_copy(h_ref.at[pl.ds(0, TM_E), :], xbuf.at[slot], gsem.at[slot]).wait()

    def scatter_wait(slot):
        pltpu.make_async_copy(ybuf.at[slot], yk_ref.at[pl.ds(0, TM_E), :], ssem.at[slot]).wait()

    @pl.when(i == 0)
    def _():
        ybuf[0] = jnp.zeros((TM_E, ybuf.shape[2]), F32)
        for half in range(2):
            pltpu.make_async_copy(ybuf.at[0], yk_ref.at[pl.ds(n_slab_rows + half * TM_E, TM_E), :], ssem.at[0]).start()
        for half in range(2):
            scatter_wait(0)

    @pl.when(i < nt)
    def _():
        slot = i % 2

        def issue(r, carry):
            tok = src_ref[0, r]
            pltpu.make_async_copy(h_ref.at[pl.ds(tok, 1), :], xbuf.at[slot, pl.ds(r, 1), :], gsem.at[slot]).start()
            return carry

        lax.fori_loop(0, TM_E, issue, 0, unroll=8)

    @pl.when(jnp.logical_and(j >= 0, j < nt))
    def _():
        slot = j % 2
        gather_wait(slot)

        @pl.when(j >= 2)
        def _():
            scatter_wait(slot)

        @pl.when(tf_ref[j] == 1)
        def _():
            w1_s[...] = w1_ref[...].astype(BF16)
            w2_s[...] = w2_ref[...].astype(BF16)

        u = _dot(xbuf[slot].astype(BF16), w1_s[...]) + b1_ref[...]
        g = jnp.minimum(u[:, :D_FF], SWIGLU_LIMIT)
        up = jnp.clip(u[:, D_FF:], -SWIGLU_LIMIT, SWIGLU_LIMIT)
        act = (up + 1.0) * g * jax.nn.sigmoid(SWIGLU_ALPHA * g)
        ybuf[slot] = _dot(act.astype(BF16), w2_s[...]) + b2_ref[...]

        def issue(r, carry):
            row = dst_ref[0, r]
            pltpu.make_async_copy(ybuf.at[slot, pl.ds(r, 1), :], yk_ref.at[pl.ds(row, 1), :], ssem.at[slot]).start()
            return carry

        lax.fori_loop(0, TM_E, issue, 0, unroll=8)

        @pl.when(j == nt - 1)
        def _():
            scatter_wait(1 - slot)
            scatter_wait(slot)


def _experts(h2, src, dst, tile_expert, first, n_tiles, layer, w1, b1, w2, b2):
    t, d = h2.shape
    nt_max = src.shape[0]
    assert t * TOP_K // TM_E >= 2
    prev = lambda i: jnp.maximum(i - 1, 0)
    wmap = lambda i, te, tf, ntl: (layer, te[prev(i)], 0, 0)
    smem_blk = lambda fn: pl.BlockSpec((None, 1, TM_E), fn, memory_space=pltpu.SMEM)
    return pl.pallas_call(
        _expert_kernel,
        grid_spec=pltpu.PrefetchScalarGridSpec(
            num_scalar_prefetch=3, grid=(nt_max + 1,),
            in_specs=[smem_blk(lambda i, te, tf, ntl: (jnp.minimum(i, nt_max - 1), 0, 0)),
                      smem_blk(lambda i, te, tf, ntl: (prev(i), 0, 0)),
                      pl.BlockSpec(memory_space=pl.ANY),
                      pl.BlockSpec((None, None, d, 2 * D_FF), wmap),
                      pl.BlockSpec((None, None, 1, 2 * D_FF), wmap),
                      pl.BlockSpec((None, None, D_FF, d), wmap),
                      pl.BlockSpec((None, None, 1, d), wmap)],
            out_specs=pl.BlockSpec(memory_space=pl.ANY),
            scratch_shapes=[pltpu.VMEM((2, TM_E, d), F32), pltpu.VMEM((2, TM_E, d), F32),
                            pltpu.VMEM((d, 2 * D_FF), BF16), pltpu.VMEM((D_FF, d), BF16),
                            pltpu.SemaphoreType.DMA((2,)), pltpu.SemaphoreType.DMA((2,))]),
        out_shape=jax.ShapeDtypeStruct((TOP_K * t + 2 * TM_E, d), F32),
        compiler_params=_cparams("arbitrary"), name="moe_experts",
    )(tile_expert, first, n_tiles, src, dst, h2, w1, b1, w2, b2)


def _combine_kernel(x1_ref, tw_ref, mod_ref, y0_ref, y1_ref, y2_ref, y3_ref, o_ref):
    tw = tw_ref[...]
    y = tw[:, 0:1] * y0_ref[...]
    for kk, y_ref in enumerate((y1_ref, y2_ref, y3_ref), start=1):
        y = y + tw[:, kk:kk + 1] * y_ref[...]
    o_ref[...] = x1_ref[...] + mod_ref[5:6, :] * y


def _combine(yk, x1, tw, mod):
    t, d = x1.shape
    nblk = t // TM
    row = lambda w: pl.BlockSpec((TM, w), lambda i: (i, 0))
    slab = lambda kk: pl.BlockSpec((TM, d), lambda i: (kk * nblk + i, 0))
    return pl.pallas_call(
        _combine_kernel, grid=(nblk,),
        in_specs=[row(d), row(LANES),
                  pl.BlockSpec((None, 6, d), lambda i: (_cond_row(i), 0, 0))] + [slab(kk) for kk in range(TOP_K)],
        out_specs=row(d),
        out_shape=jax.ShapeDtypeStruct((t, d), F32),
        compiler_params=_cparams("arbitrary"), name="moe_combine",
    )(x1, tw, mod, yk, yk, yk, yk)


def _moe(x1, h2, ti, tw, mod, layer, w1, b1, w2, b2):
    t = x1.shape[0]
    depth = w1.shape[0]
    assert TM == TM_E and TOP_K == 4
    n_tiles_max = t * TOP_K // TM_E + N_EXP
    src, dst, tile_expert, first, n_tiles = _route_plan(ti[:, :TOP_K], n_tiles_max)
    yk = _experts(h2, src, dst, tile_expert, first, n_tiles, layer,
                  w1, b1.reshape(depth, N_EXP, 1, -1), w2, b2.reshape(depth, N_EXP, 1, -1))
    return _combine(yk, x1, tw, mod)


def _pad_lanes(a, value=0.0):
    return jnp.pad(a, ((0, 0), (0, LANES - a.shape[1])), constant_values=value)


def kernel(x_prompt, x_sample, cache_na_k, cache_na_v, state_mlstm_C, state_mlstm_n, state_mlstm_m, state_ret_S, c, c_ctx, w_mod, b_mod, w_in_even, mlstm_gate_b, na_q_norm, na_k_norm, na_rpb, mlstm_norm, w_out_even, w_in_odd, ret_decay, ret_norm, w_out_odd, w_router, b_router, w_moe_in, b_moe_in, w_moe_out, b_moe_out):
    nb_c, s_c, d = x_prompt.shape
    nb_l, s_l, _ = x_sample.shape
    t_c = nb_c * s_c
    t_l = nb_l * s_l
    assert t_c == 4 * SEG and s_l == SEG and d == D_MODEL
    depth = w_mod.shape[0]
    dt = x_prompt.dtype

    x = jnp.concatenate([x_prompt.reshape(t_c, d), x_sample.reshape(t_l, d)], axis=0)
    cond = jnp.concatenate([c_ctx[None, :], c, jnp.zeros((N_COND - 1 - nb_l, d), F32)], axis=0)
    mod = _modulation(cond, w_mod, b_mod).reshape(depth, N_COND, 6, d)

    outs = {}
    for l in range(depth):
        e = l // 2
        mod_l = mod[l]
        if l % 2 == 0:
            w_in = w_in_even[e]
            n_main = 3 * W_A + 4 * W_B
            wg = _pad_lanes(w_in[:, n_main:])
            bg = _pad_lanes(mlstm_gate_b[e].reshape(1, 4 * H_B))
            z, g = _inproj(x, mod_l, w_in[:, :n_main].astype(BF16), wg, bg)
            qn = na_q_norm[e].reshape(1, HD_A)
            kn = na_k_norm[e].reshape(1, HD_A)
            oa_c, ka_c = _ctx_attention(z, nb_c, s_c, qn, kn)
            past = cache_na_k.shape[2]
            oa_l = _na_attention(z, t_c, nb_l, s_l,
                                 cache_na_k[:, e].reshape(nb_l, past, W_A), cache_na_v[:, e].reshape(nb_l, past, W_A),
                                 _na_bias_table(na_rpb[e]), qn, kn)
            nw = mlstm_norm[e].reshape(1, W_B)
            hm_c, c_fin, n_fin, m_fin = _mlstm(z, g, 0, nb_c, s_c, nw)
            init = (state_mlstm_C[:, e].reshape(nb_l, 2 * H_B, HD_B, HD_B),
                    state_mlstm_n[:, e].reshape(nb_l, 2 * H_B, HD_B),
                    jnp.broadcast_to(state_mlstm_m[:, e].reshape(nb_l, 2 * H_B, 1), (nb_l, 2 * H_B, LANES)))
            hm_l = _mlstm(z, g, t_c, nb_l, s_l, nw, init)[0]
            a = jnp.concatenate([oa_c, oa_l], axis=0)
            b = jnp.concatenate([hm_c, hm_l], axis=0)
            w_out = w_out_even[e].astype(BF16)
            wa, wb = w_out[:W_A], w_out[W_A:]
            outs.setdefault("na_k", []).append(ka_c.reshape(nb_c, s_c, H_A, HD_A))
            outs.setdefault("na_v", []).append(z[:t_c, 2 * W_A:3 * W_A].reshape(nb_c, s_c, H_A, HD_A))
            outs.setdefault("C", []).append(c_fin.reshape(nb_c, 2, H_B, HD_B, HD_B))
            outs.setdefault("n", []).append(n_fin.reshape(nb_c, 2, H_B, HD_B))
            outs.setdefault("m", []).append(m_fin[:, :, 0].reshape(nb_c, 2, H_B))
        else:
            w_c = H_C * HD_C
            z = _inproj(x, mod_l, w_in_odd[e].astype(BF16))
            dl_rep = jnp.broadcast_to(ret_decay[e].reshape(2 * H_C, 1), (2 * H_C, LANES))
            nw = ret_norm[e].reshape(1, w_c)
            hr_c, s_fin = _retention(z, 0, nb_c, s_c, dl_rep, nw)
            hr_l = _retention(z, t_c, nb_l, s_l, dl_rep, nw, rope=_rope_tables(s_l),
                              init=state_ret_S[:, e].reshape(nb_l, 2 * H_C, HD_C, HD_C))[0]
            fd_c = _fnet(z, 0, nb_c, s_c)
            fd_l = _fnet(z, t_c, nb_l, s_l)
            a = jnp.concatenate([hr_c, hr_l], axis=0)
            b = jnp.concatenate([fd_c, fd_l], axis=0)
            w_out = w_out_odd[e].astype(BF16)
            wa, wb = w_out[:w_c], w_out[w_c:]
            outs.setdefault("S", []).append(s_fin.reshape(nb_c, 2, H_C, HD_C, HD_C))
        wr = _pad_lanes(w_router[l])
        br = _pad_lanes(b_router[l].reshape(1, N_EXP), NEG)
        x1, h2, ti, tw = _outproj_router(a, b, x, mod_l, wa, wb, wr, br)
        x = _moe(x1, h2, ti, tw, mod_l, l, w_moe_in, b_moe_in, w_moe_out, b_moe_out)

    y_prompt = x[:t_c].reshape(nb_c, s_c, d)
    y_sample = x[t_c:].reshape(nb_l, s_l, d)
    stack = lambda key: jnp.stack(outs[key], axis=1).astype(dt)
    return (y_prompt, y_sample, stack("na_k"), stack("na_v"), stack("C"), stack("n"), stack("m"), stack("S"))
```

```python
import functools
import math

import numpy as np
import jax
import jax.numpy as jnp
from jax import lax
from jax.experimental import pallas as pl
from jax.experimental.pallas import tpu as pltpu

F32 = jnp.float32
BF16 = jnp.bfloat16
HIGHEST = lax.Precision.HIGHEST

D_MODEL = 1024
GRID_W = 64
WIN_R = 8
WIN_C = 16
H_A, HD_A = 8, 64
H_B, HD_B = 4, 128
H_C, HD_C = 4, 128
N_FG, FG_W = 4, 128
W_A = H_A * HD_A
W_B = H_B * HD_B
N_EXP = 32
TOP_K = 4
D_FF = D_MODEL
SWIGLU_LIMIT = 7.0
SWIGLU_ALPHA = 1.702
CHUNK = 128
ROPE_BASE = 10000.0
EPS = 1e-6

LANES = 128
SEG = 1024
N_COND = 8
TM = 256
TM_E = 256
NEG = -1e30
VMEM_LIMIT = 56 * 1024 * 1024


def _cparams(*sem):
    return pltpu.CompilerParams(dimension_semantics=sem, vmem_limit_bytes=VMEM_LIMIT)


def _cond_row(i):
    return jnp.maximum((i * TM) // SEG - 3, 0)


def _log_sigmoid(x):
    return jnp.minimum(x, 0.0) - jnp.log1p(jnp.exp(-jnp.abs(x)))


def _dot(a, b):
    return jnp.dot(a, b, preferred_element_type=F32)


def _dot_nt(a, b):
    return lax.dot_general(a, b, (((1,), (1,)), ((), ())), preferred_element_type=F32)


def _dot_hi(a, b):
    return jnp.dot(a, b, precision=HIGHEST, preferred_element_type=F32)


def _mod_kernel(cond_ref, w_ref, b_ref, o_ref):
    c = cond_ref[...]
    s = c * jax.nn.sigmoid(c)
    o_ref[...] = _dot_hi(s, w_ref[...]) + b_ref[...]


def _modulation(cond, w_mod, b_mod):
    depth, d, n = w_mod.shape
    tn = 1536
    return pl.pallas_call(
        _mod_kernel,
        grid=(depth, n // tn),
        in_specs=[pl.BlockSpec((N_COND, d), lambda l, j: (0, 0)),
                  pl.BlockSpec((None, d, tn), lambda l, j: (l, 0, j)),
                  pl.BlockSpec((None, 1, tn), lambda l, j: (l, 0, j))],
        out_specs=pl.BlockSpec((None, N_COND, tn), lambda l, j: (l, 0, j)),
        out_shape=jax.ShapeDtypeStruct((depth, N_COND, n), F32),
        compiler_params=_cparams("arbitrary", "arbitrary"),
        name="modulation",
    )(cond, w_mod, b_mod.reshape(depth, 1, n))


def _rms_mod(x, shift, scale):
    h = x * lax.rsqrt(jnp.mean(x * x, axis=-1, keepdims=True) + EPS)
    return h * (1.0 + scale) + shift


def _inproj_kernel(x_ref, mod_ref, w_ref, z_ref):
    h = _rms_mod(x_ref[...], mod_ref[0:1, :], mod_ref[1:2, :])
    z_ref[...] = _dot(h.astype(BF16), w_ref[...])


def _inproj_gate_kernel(x_ref, mod_ref, w_ref, wg_ref, bg_ref, z_ref, g_ref):
    h = _rms_mod(x_ref[...], mod_ref[0:1, :], mod_ref[1:2, :])
    z_ref[...] = _dot(h.astype(BF16), w_ref[...])
    g_ref[...] = _dot_hi(h, wg_ref[...]) + bg_ref[...]


def _inproj(x, mod, w, wg=None, bg=None):
    t, d = x.shape
    n = w.shape[1]
    in_specs = [pl.BlockSpec((TM, d), lambda i: (i, 0)),
                pl.BlockSpec((None, 6, d), lambda i: (_cond_row(i), 0, 0)),
                pl.BlockSpec((d, n), lambda i: (0, 0))]
    z_spec = pl.BlockSpec((TM, n), lambda i: (i, 0))
    z_shape = jax.ShapeDtypeStruct((t, n), F32)
    if wg is None:
        return pl.pallas_call(
            _inproj_kernel, grid=(t // TM,), in_specs=in_specs, out_specs=z_spec, out_shape=z_shape,
            compiler_params=_cparams("arbitrary"), name="inproj",
        )(x, mod, w)
    in_specs += [pl.BlockSpec((d, LANES), lambda i: (0, 0)), pl.BlockSpec((1, LANES), lambda i: (0, 0))]
    return pl.pallas_call(
        _inproj_gate_kernel, grid=(t // TM,), in_specs=in_specs,
        out_specs=[z_spec, pl.BlockSpec((TM, LANES), lambda i: (i, 0))],
        out_shape=[z_shape, jax.ShapeDtypeStruct((t, LANES), F32)],
        compiler_params=_cparams("arbitrary"), name="inproj_gate",
    )(x, mod, w, wg, bg)


def _head_rms(x, w):
    return x * lax.rsqrt(jnp.mean(x * x, axis=-1, keepdims=True) + EPS) * w


def _ctx_attn_kernel(q_ref, k_ref, v_ref, qn_ref, kn_ref, o_ref, ko_ref):
    scale = HD_A ** -0.5
    for h in range(H_A):
        sl = slice(h * HD_A, (h + 1) * HD_A)
        q = _head_rms(q_ref[:, sl], qn_ref[...])
        k = _head_rms(k_ref[:, sl], kn_ref[...])
        ko_ref[:, sl] = k
        s = _dot_nt(q.astype(BF16), k.astype(BF16)) * scale
        p = jnp.exp(s - jnp.max(s, axis=-1, keepdims=True))
        den = jnp.sum(p, axis=-1, keepdims=True)
        o_ref[:, sl] = _dot(p.astype(BF16), v_ref[:, sl].astype(BF16)) / den


def _ctx_attention(z, n_batch, seq, qn, kn):
    spec = lambda c: pl.BlockSpec((seq, W_A), lambda b: (b, c))
    wspec = pl.BlockSpec((1, HD_A), lambda b: (0, 0))
    out = jax.ShapeDtypeStruct((n_batch * seq, W_A), F32)
    return pl.pallas_call(
        _ctx_attn_kernel, grid=(n_batch,),
        in_specs=[spec(0), spec(1), spec(2), wspec, wspec],
        out_specs=[pl.BlockSpec((seq, W_A), lambda b: (b, 0))] * 2,
        out_shape=[out, out],
        compiler_params=_cparams("arbitrary"), name="ctx_attention",
    )(z, z, z, qn, kn)


def _na_bias_table(rpb):
    qc = np.arange(GRID_W)
    kc = np.arange(GRID_W)
    cstart = np.clip(qc - WIN_C // 2, 0, GRID_W - WIN_C)
    col_in = (kc[None, :] >= cstart[:, None]) & (kc[None, :] < cstart[:, None] + WIN_C)
    dc = np.clip(kc[None, :] - qc[:, None], 1 - WIN_C, WIN_C - 1) + WIN_C - 1
    cls = np.arange(WIN_R)
    j = np.arange(WIN_R)
    dr = j[None, :] - cls[:, None] + WIN_R - 1
    sel_r = jnp.asarray(dr[:, :, None] == np.arange(2 * WIN_R - 1)[None, None, :], F32)
    sel_c = jnp.asarray(dc[:, :, None] == np.arange(2 * WIN_C - 1)[None, None, :], F32)
    tab = jnp.einsum("hab,cja,qkb->hcqjk", rpb, sel_r, sel_c, precision=HIGHEST)
    tab = jnp.where(jnp.asarray(col_in)[None, None, :, None, :], tab, NEG)
    return tab.reshape(H_A, WIN_R, GRID_W, WIN_R * GRID_W)


def _na_kernel(q_ref, k_ref, v_ref, kc_ref, vc_ref, bias_ref, qn_ref, kn_ref, o_ref, kn_s, *, rows):
    r = pl.program_id(1)
    scale = HD_A ** -0.5

    @pl.when(r == 0)
    def _():
        for h in range(H_A):
            sl = slice(h * HD_A, (h + 1) * HD_A)
            kn_s[:, sl] = _head_rms(k_ref[:, sl], kn_ref[...])

    rs = jnp.clip(r - WIN_R // 2, 0, rows - WIN_R)
    start = pl.multiple_of(rs * GRID_W, GRID_W)
    n_loc = WIN_R * GRID_W
    for h in range(H_A):
        sl = slice(h * HD_A, (h + 1) * HD_A)
        q = _head_rms(q_ref[:, sl], qn_ref[...]).astype(BF16)
        kl = kn_s[pl.ds(start, n_loc), sl].astype(BF16)
        vl = v_ref[pl.ds(start, n_loc), sl].astype(BF16)
        s_loc = _dot_nt(q, kl) * scale + bias_ref[h]
        s_ctx = _dot_nt(q, kc_ref[:, sl].astype(BF16)) * scale
        m = jnp.maximum(jnp.max(s_loc, axis=-1, keepdims=True), jnp.max(s_ctx, axis=-1, keepdims=True))
        p_loc = jnp.exp(s_loc - m)
        p_ctx = jnp.exp(s_ctx - m)
        den = jnp.sum(p_loc, axis=-1, keepdims=True) + jnp.sum(p_ctx, axis=-1, keepdims=True)
        o = _dot(p_loc.astype(BF16), vl) + _dot(p_ctx.astype(BF16), vc_ref[:, sl].astype(BF16))
        o_ref[:, sl] = o / den


def _na_attention(z, row0, n_batch, seq, kc, vc, bias, qn, kn):
    rows = seq // GRID_W
    past = kc.shape[1]
    blk0 = row0 // GRID_W
    sblk0 = row0 // seq

    def cls_of(r):
        return r - jnp.clip(r - WIN_R // 2, 0, rows - WIN_R)

    full = lambda c: pl.BlockSpec((seq, W_A), lambda b, r: (sblk0 + b, c))
    cspec = pl.BlockSpec((None, past, W_A), lambda b, r: (b, 0, 0))
    wspec = pl.BlockSpec((1, HD_A), lambda b, r: (0, 0))
    return pl.pallas_call(
        functools.partial(_na_kernel, rows=rows), grid=(n_batch, rows),
        in_specs=[pl.BlockSpec((GRID_W, W_A), lambda b, r: (blk0 + b * rows + r, 0)),
                  full(1), full(2), cspec, cspec,
                  pl.BlockSpec((H_A, None, GRID_W, WIN_R * GRID_W), lambda b, r: (0, cls_of(r), 0, 0)),
                  wspec, wspec],
        out_specs=pl.BlockSpec((GRID_W, W_A), lambda b, r: (b * rows + r, 0)),
        out_shape=jax.ShapeDtypeStruct((n_batch * seq, W_A), F32),
        scratch_shapes=[pltpu.VMEM((seq, W_A), F32)],
        compiler_params=_cparams("arbitrary", "arbitrary"), name="na_attention",
    )(z, z, z, kc, vc, bias, qn, kn)


def _tri_masks():
    li = lax.broadcasted_iota(jnp.int32, (CHUNK, CHUNK), 0)
    si = lax.broadcasted_iota(jnp.int32, (CHUNK, CHUNK), 1)
    return li >= si, li <= si


def _mlstm_kernel(*refs, nc, has_init):
    if has_init:
        (q_ref, k_ref, v_ref, og_ref, g_ref, nw_ref, c0_ref, n0_ref, m0_ref,
         o_ref, cf_ref, nf_ref, mf_ref, h_s, c_s, n_s, m_s) = refs
    else:
        (q_ref, k_ref, v_ref, og_ref, g_ref, nw_ref,
         o_ref, cf_ref, nf_ref, mf_ref, h_s, c_s, n_s, m_s) = refs
    nd = 2 * H_B
    if has_init:
        c_s[...] = c0_ref[...]
        n_s[...] = n0_ref[...]
        m_s[...] = m0_ref[...]
    else:
        c_s[...] = jnp.zeros_like(c_s)
        n_s[...] = jnp.zeros_like(n_s)
        m_s[...] = jnp.zeros_like(m_s)

    causal, anti = _tri_masks()
    tri_f = causal.astype(F32)
    tri_b = anti.astype(F32)
    kscale = HD_B ** -0.5

    def chunk_step(c, carry):
        for d in range(2):
            cc = c if d == 0 else nc - 1 - c
            t0 = pl.multiple_of(cc * CHUNK, CHUNK)
            g = g_ref[pl.ds(t0, CHUNK), :]
            gt = g.T
            ls = _log_sigmoid(g)
            lst = _log_sigmoid(gt)
            tri_c, tri_r, mask = (tri_f, tri_b, causal) if d == 0 else (tri_b, tri_f, anti)
            b_cols = _dot_hi(tri_c, ls)
            b_rows = _dot_hi(lst, tri_r)
            last = CHUNK - 1 if d == 0 else 0
            for h in range(H_B):
                ci = (2 * d) * H_B + h
                cf = (2 * d + 1) * H_B + h
                hs = slice(h * HD_B, (h + 1) * HD_B)
                q = q_ref[pl.ds(t0, CHUNK), hs]
                k = k_ref[pl.ds(t0, CHUNK), hs] * kscale
                v = v_ref[pl.ds(t0, CHUNK), hs]
                qb, kb, vb = q.astype(BF16), k.astype(BF16), v.astype(BF16)
                b_col = b_cols[:, cf:cf + 1]
                b_row = b_rows[cf:cf + 1, :]
                i_row = gt[ci:ci + 1, :]
                sidx = d * H_B + h
                cst = c_s[sidx]
                nst = n_s[sidx:sidx + 1, :]
                mst = m_s[sidx:sidx + 1, 0:1]
                dmat = jnp.where(mask, b_col - b_row + i_row, -jnp.inf)
                inter = b_col + mst
                mt = jnp.maximum(inter, jnp.max(dmat, axis=-1, keepdims=True))
                w = jnp.exp(dmat - mt) * _dot_nt(qb, kb)
                a = jnp.exp(inter - mt)
                num = _dot(w.astype(BF16), vb) + _dot(qb, cst.astype(BF16)) * a
                den = jnp.sum(w, axis=-1, keepdims=True) + a * jnp.sum(q * nst, axis=-1, keepdims=True)
                hc = num / jnp.maximum(jnp.abs(den), jnp.exp(-mt))
                h_s[d, pl.ds(t0, CHUNK), hs] = hc
                bl = b_row[:, last:last + 1]
                dl = bl - b_row + i_row
                m_new = jnp.maximum(bl + mst, jnp.max(dl, axis=-1, keepdims=True))
                wl = jnp.exp(dl - m_new)
                dec = jnp.exp(bl + mst - m_new)
                kw = (k.T * wl).astype(BF16)
                c_s[sidx] = dec * cst + _dot(kw, vb)
                wl8 = jnp.broadcast_to(wl, (8, CHUNK)).astype(BF16)
                n_s[sidx:sidx + 1, :] = dec * nst + _dot(wl8, kb)[0:1, :]
                m_s[sidx:sidx + 1, :] = jnp.broadcast_to(m_new, (1, LANES))
        return carry

    lax.fori_loop(0, nc, chunk_step, 0)

    for h in range(H_B):
        hs = slice(h * HD_B, (h + 1) * HD_B)
        hsum = h_s[0, :, hs] + h_s[1, :, hs]
        o_ref[:, hs] = _head_rms(hsum, nw_ref[:, hs]) * jax.nn.sigmoid(og_ref[:, hs])
    cf_ref[...] = c_s[...]
    nf_ref[...] = n_s[...]
    mf_ref[...] = m_s[...]


def _mlstm(z, g, row0, n_batch, seq, norm_w, init=None):
    sblk0 = row0 // seq
    nd = 2 * H_B
    spec = lambda c: pl.BlockSpec((seq, W_B), lambda b: (sblk0 + b, c))
    in_specs = [spec(3), spec(4), spec(5), spec(6),
                pl.BlockSpec((seq, LANES), lambda b: (sblk0 + b, 0)),
                pl.BlockSpec((1, W_B), lambda b: (0, 0))]
    args = [z, z, z, z, g, norm_w]
    st_specs = [pl.BlockSpec((None, nd, HD_B, HD_B), lambda b: (b, 0, 0, 0)),
                pl.BlockSpec((None, nd, HD_B), lambda b: (b, 0, 0)),
                pl.BlockSpec((None, nd, LANES), lambda b: (b, 0, 0))]
    if init is not None:
        in_specs += st_specs
        args += list(init)
    return pl.pallas_call(
        functools.partial(_mlstm_kernel, nc=seq // CHUNK, has_init=init is not None), grid=(n_batch,),
        in_specs=in_specs,
        out_specs=[pl.BlockSpec((seq, W_B), lambda b: (b, 0))] + st_specs,
        out_shape=[jax.ShapeDtypeStruct((n_batch * seq, W_B), F32),
                   jax.ShapeDtypeStruct((n_batch, nd, HD_B, HD_B), F32),
                   jax.ShapeDtypeStruct((n_batch, nd, HD_B), F32),
                   jax.ShapeDtypeStruct((n_batch, nd, LANES), F32)],
        scratch_shapes=[pltpu.VMEM((2, seq, W_B), F32), pltpu.VMEM((nd, HD_B, HD_B), F32),
                        pltpu.VMEM((nd, HD_B), F32), pltpu.VMEM((nd, LANES), F32)],
        compiler_params=_cparams("arbitrary"), name="mlstm",
    )(*args)


def _rope_tables(seq):
    half = HD_C // 2
    quarter = half // 2
    t = np.arange(seq)
    inv = ROPE_BASE ** (-np.arange(0, half, 2, dtype=np.float64) / half)
    ang_r = (t // GRID_W)[:, None] * inv[None, :]
    ang_c = (t % GRID_W)[:, None] * inv[None, :]
    cos_t = np.concatenate([np.cos(ang_r), np.cos(ang_r), np.cos(ang_c), np.cos(ang_c)], -1)
    sin_t = np.concatenate([-np.sin(ang_r), np.sin(ang_r), -np.sin(ang_c), np.sin(ang_c)], -1)
    assert cos_t.shape == (seq, 4 * quarter)
    return jnp.asarray(cos_t, F32), jnp.asarray(sin_t, F32)


def _rope(x, cos_t, sin_t):
    quarter = HD_C // 4
    lane = lax.broadcasted_iota(jnp.int32, x.shape, 1)
    first = (lane % (2 * quarter)) < quarter
    swapped = jnp.where(first, pltpu.roll(x, HD_C - quarter, 1), pltpu.roll(x, quarter, 1))
    return x * cos_t + swapped * sin_t


def _ret_kernel(*refs, nc, has_init, use_rope):
    refs = list(refs)
    q_ref, k_ref, v_ref, gg_ref, dl_ref, nw_ref = refs[:6]
    pos = 6
    if use_rope:
        cos_ref, sin_ref = refs[pos:pos + 2]
        pos += 2
    if has_init:
        s0_ref = refs[pos]
        pos += 1
    o_ref, sf_ref, h_s, s_s = refs[pos:pos + 4]
    if has_init:
        s_s[...] = s0_ref[...]
    else:
        s_s[...] = jnp.zeros_like(s_s)

    causal, anti = _tri_masks()
    li = lax.broadcasted_iota(jnp.int32, (CHUNK, CHUNK), 0).astype(F32)
    si = lax.broadcasted_iota(jnp.int32, (CHUNK, CHUNK), 1).astype(F32)
    lg_all = _log_sigmoid(dl_ref[...])
    kscale = HD_C ** -0.5

    def chunk_step(c, carry):
        for d in range(2):
            cc = c if d == 0 else nc - 1 - c
            t0 = pl.multiple_of(cc * CHUNK, CHUNK)
            for h in range(H_C):
                sidx = d * H_C + h
                hs = slice(h * HD_C, (h + 1) * HD_C)
                lg = lg_all[sidx:sidx + 1, :]
                q = q_ref[pl.ds(t0, CHUNK), hs]
                k = k_ref[pl.ds(t0, CHUNK), hs] * kscale
                v = v_ref[pl.ds(t0, CHUNK), hs]
                if use_rope:
                    cos_t = cos_ref[pl.ds(t0, CHUNK), :]
                    sin_t = sin_ref[pl.ds(t0, CHUNK), :]
                    q = _rope(q, cos_t, sin_t)
                    k = _rope(k, cos_t, sin_t)
                if d == 0:
                    decay = jnp.exp(jnp.where(causal, (li - si) * lg, -jnp.inf))
                    q_dec = jnp.exp((li + 1.0) * lg)
                    k_dec = jnp.exp((CHUNK - 1.0 - li) * lg)
                else:
                    decay = jnp.exp(jnp.where(anti, (si - li) * lg, -jnp.inf))
                    q_dec = jnp.exp((CHUNK - li) * lg)
                    k_dec = jnp.exp(li * lg)
                c_dec = jnp.exp(CHUNK * lg)
                st = s_s[sidx]
                qb, kb, vb = q.astype(BF16), k.astype(BF16), v.astype(BF16)
                att = _dot_nt(qb, kb) * decay
                o = _dot(att.astype(BF16), vb) + _dot(qb, st.astype(BF16)) * q_dec
                h_s[d, pl.ds(t0, CHUNK), hs] = o
                kd = (k * k_dec).T.astype(BF16)
                s_s[sidx] = c_dec * st + _dot(kd, vb)
        return carry

    lax.fori_loop(0, nc, chunk_step, 0)

    for h in range(H_C):
        hs = slice(h * HD_C, (h + 1) * HD_C)
        osum = h_s[0, :, hs] + h_s[1, :, hs]
        gg = gg_ref[:, hs]
        o_ref[:, hs] = _head_rms(osum, nw_ref[:, hs]) * (gg * jax.nn.sigmoid(gg))
    sf_ref[...] = s_s[...]


def _retention(z, row0, n_batch, seq, decay_rep, norm_w, rope=None, init=None):
    sblk0 = row0 // seq
    nd = 2 * H_C
    w_c = H_C * HD_C
    spec = lambda c: pl.BlockSpec((seq, w_c), lambda b: (sblk0 + b, c))
    in_specs = [spec(0), spec(1), spec(2), spec(3),
                pl.BlockSpec((nd, LANES), lambda b: (0, 0)),
                pl.BlockSpec((1, w_c), lambda b: (0, 0))]
    args = [z, z, z, z, decay_rep, norm_w]
    if rope is not None:
        in_specs += [pl.BlockSpec((seq, HD_C), lambda b: (0, 0))] * 2
        args += list(rope)
    st_spec = pl.BlockSpec((None, nd, HD_C, HD_C), lambda b: (b, 0, 0, 0))
    if init is not None:
        in_specs.append(st_spec)
        args.append(init)
    return pl.pallas_call(
        functools.partial(_ret_kernel, nc=seq // CHUNK, has_init=init is not None, use_rope=rope is not None),
        grid=(n_batch,), in_specs=in_specs,
        out_specs=[pl.BlockSpec((seq, w_c), lambda b: (b, 0)), st_spec],
        out_shape=[jax.ShapeDtypeStruct((n_batch * seq, w_c), F32),
                   jax.ShapeDtypeStruct((n_batch, nd, HD_C, HD_C), F32)],
        scratch_shapes=[pltpu.VMEM((2, seq, w_c), F32), pltpu.VMEM((nd, HD_C, HD_C), F32)],
        compiler_params=_cparams("arbitrary"), name="retention",
    )(*args)


def _dft_tables(n):
    idx = (np.arange(n)[:, None] * np.arange(n)[None, :]) % n
    ang = 2.0 * np.pi * idx / n
    return np.cos(ang) / np.sqrt(n), np.sin(ang) / np.sqrt(n)


def _fnet_kernel(x_ref, cw_ref, sw_ref, cs_ref, ss_ref, o_ref):
    for g in range(N_FG):
        gs = slice(g * FG_W, (g + 1) * FG_W)
        x = x_ref[:, gs].astype(BF16)
        xc = _dot(x, cw_ref[...]).astype(BF16)
        xs = _dot(x, sw_ref[...]).astype(BF16)
        o_ref[:, gs] = _dot(cs_ref[...], xc) - _dot(ss_ref[...], xs)


def _fnet(z, row0, n_batch, seq):
    sblk0 = row0 // seq
    w_d = N_FG * FG_W
    cw, sw = _dft_tables(FG_W)
    cs, ss = _dft_tables(seq)
    tabs = [jnp.asarray(a, F32).astype(BF16) for a in (cw, sw, cs, ss)]
    wspec = pl.BlockSpec((FG_W, FG_W), lambda b: (0, 0))
    sspec = pl.BlockSpec((seq, seq), lambda b: (0, 0))
    return pl.pallas_call(
        _fnet_kernel, grid=(n_batch,),
        in_specs=[pl.BlockSpec((seq, w_d), lambda b: (sblk0 + b, 4)), wspec, wspec, sspec, sspec],
        out_specs=pl.BlockSpec((seq, w_d), lambda b: (b, 0)),
        out_shape=jax.ShapeDtypeStruct((n_batch * seq, w_d), F32),
        compiler_params=_cparams("arbitrary"), name="fnet",
    )(z, *tabs)


SUB = 8
assert D_MODEL == SUB * LANES


def _store_token_major(ref, lead, val):
    n = val.shape[0]
    for c in range(SUB):
        ref[lead + (pl.ds(c, n, stride=SUB), slice(None))] = val[:, c * LANES:(c + 1) * LANES]


def _load_token_major(ref, lead, n):
    return jnp.concatenate([ref[lead + (pl.ds(c, n, stride=SUB), slice(None))] for c in range(SUB)], axis=1)


def _outproj_router_kernel(a_ref, b_ref, x_ref, mod_ref, wa_ref, wb_ref, wr_ref, br_ref,
                           x1_ref, h2_ref, ti_ref, tw_ref):
    y = _dot(a_ref[...].astype(BF16), wa_ref[...]) + _dot(b_ref[...].astype(BF16), wb_ref[...])
    x1 = x_ref[...] + mod_ref[2:3, :] * y
    x1_ref[...] = x1
    h2 = _rms_mod(x1, mod_ref[3:4, :], mod_ref[4:5, :])
    _store_token_major(h2_ref, (), h2)
    logits = _dot_hi(h2, wr_ref[...]) + br_ref[...]
    lane = lax.broadcasted_iota(jnp.int32, logits.shape, 1)
    lane_f = lane.astype(F32)
    cur = logits
    vals = []
    ti = jnp.zeros(logits.shape, jnp.int32)
    for kk in range(TOP_K):
        mx = jnp.max(cur, axis=-1, keepdims=True)
        idx = jnp.min(jnp.where(cur == mx, lane_f, float(LANES)), axis=-1, keepdims=True)
        ti = jnp.where(lane == kk, idx.astype(jnp.int32), ti)
        cur = jnp.where(lane_f == idx, -jnp.inf, cur)
        vals.append(mx)
    es = [jnp.exp(v - vals[0]) for v in vals]
    tot = es[0] + es[1] + es[2] + es[3]
    tw = jnp.zeros(logits.shape, F32)
    for kk in range(TOP_K):
        tw = jnp.where(lane == kk, es[kk] / tot, tw)
    ti_ref[...] = ti
    tw_ref[...] = tw


def _outproj_router(a, b, x, mod, wa, wb, wr, br):
    t, d = x.shape
    wid = a.shape[1]
    row = lambda w: pl.BlockSpec((TM, w), lambda i: (i, 0))
    const = lambda r, c: pl.BlockSpec((r, c), lambda i: (0, 0))
    return pl.pallas_call(
        _outproj_router_kernel, grid=(t // TM,),
        in_specs=[row(wid), row(wid), row(d),
                  pl.BlockSpec((None, 6, d), lambda i: (_cond_row(i), 0, 0)),
                  const(wid, d), const(wid, d), const(d, LANES), const(1, LANES)],
        out_specs=[row(d), pl.BlockSpec((TM * SUB, LANES), lambda i: (i, 0)), row(LANES), row(LANES)],
        out_shape=[jax.ShapeDtypeStruct((t, d), F32), jax.ShapeDtypeStruct((t * SUB, LANES), F32),
                   jax.ShapeDtypeStruct((t, LANES), jnp.int32), jax.ShapeDtypeStruct((t, LANES), F32)],
        compiler_params=_cparams("arbitrary"), name="outproj_router",
    )(a, b, x, mod, wa, wb, wr, br)


INV_BLK = 4096


def _invert_kernel(dest_ref, inv_ref):
    i = pl.program_id(0)

    @pl.when(i == 0)
    def _():
        def init(r, carry):
            inv_ref[r] = -1
            return carry

        lax.fori_loop(0, inv_ref.shape[0], init, 0, unroll=8)

    def body(p, carry):
        inv_ref[dest_ref[0, p]] = i * INV_BLK + p
        return carry

    lax.fori_loop(0, INV_BLK, body, 0, unroll=8)


def _invert_rows(dest_flat, n_rows):
    n = dest_flat.shape[0]
    return pl.pallas_call(
        _invert_kernel, grid=(n // INV_BLK,),
        in_specs=[pl.BlockSpec((None, 1, INV_BLK), lambda i: (i, 0, 0), memory_space=pltpu.SMEM)],
        out_specs=pl.BlockSpec(memory_space=pltpu.SMEM),
        out_shape=jax.ShapeDtypeStruct((n_rows,), jnp.int32),
        compiler_params=_cparams("arbitrary"), name="moe_invert",
    )(dest_flat.reshape(n // INV_BLK, 1, INV_BLK))


def _route_plan(top_i, n_tiles_max):
    t = top_i.shape[0]
    onehot = (top_i[:, :, None] == jnp.arange(N_EXP, dtype=jnp.int32)[None, None, :]).astype(jnp.int32)
    sel = onehot.sum(axis=1)
    before = jnp.cumsum(sel, axis=0) - sel
    counts = before[-1] + sel[-1]
    tiles = (counts + TM_E - 1) // TM_E
    tile_end = jnp.cumsum(tiles)
    tile_start = tile_end - tiles
    dest = jnp.take_along_axis(tile_start[None, :] * TM_E + before, top_i, axis=1)
    n_tiles = tile_end[-1]
    tid = jnp.minimum(jnp.arange(n_tiles_max, dtype=jnp.int32), n_tiles - 1)
    tile_expert = jnp.sum((tile_end[None, :] <= tid[:, None]).astype(jnp.int32), axis=1)
    first = jnp.concatenate([jnp.ones((1,), jnp.int32), (tile_expert[1:] != tile_expert[:-1]).astype(jnp.int32)])
    n_rows = n_tiles_max * TM_E
    inv = _invert_rows(dest.reshape(-1).astype(jnp.int32), n_rows)
    row = jnp.arange(n_rows, dtype=jnp.int32)
    trash = TOP_K * t + ((row // TM_E) % 2) * TM_E + row % TM_E
    src = jnp.where(inv >= 0, inv // TOP_K, 0) * SUB
    dst = jnp.where(inv >= 0, (inv % TOP_K) * t + inv // TOP_K, trash) * SUB
    return (src.reshape(n_tiles_max, 1, TM_E), dst.reshape(n_tiles_max, 1, TM_E), tile_expert, first,
            n_tiles.reshape(1).astype(jnp.int32))


def _expert_kernel(te_ref, tf_ref, nt_ref, src_ref, dst_ref, h_ref, w1_ref, b1_ref, w2_ref, b2_ref, yk_ref,
                   xbuf, ybuf, w1_s, w2_s, gsem, ssem):
    i = pl.program_id(0)
    j = i - 1
    nt = nt_ref[0]
    tile_rows = TM_E * SUB
    trash0 = yk_ref.shape[0] - 2 * tile_rows

    def gather_wait(slot):
        pltpu.make_async_copy(h_ref.at[pl.ds(0, tile_rows), :], xbuf.at[slot], gsem.at[slot]).wait()

    def scatter_wait(slot):
        pltpu.make_async_copy(ybuf.at[slot], yk_ref.at[pl.ds(0, tile_rows), :], ssem.at[slot]).wait()

    @pl.when(i == 0)
    def _():
        ybuf[0] = jnp.zeros((tile_rows, LANES), F32)
        for half in range(2):
            pltpu.make_async_copy(ybuf.at[0], yk_ref.at[pl.ds(trash0 + half * tile_rows, tile_rows), :],
                                  ssem.at[0]).start()
        for half in range(2):
            scatter_wait(0)

    for s in range(2):
        @pl.when(jnp.logical_and(i < nt, i % 2 == s))
        def _():
            for r in range(TM_E):
                tok = pl.multiple_of(src_ref[0, r], SUB)
                pltpu.make_async_copy(h_ref.at[pl.ds(tok, SUB), :], xbuf.at[s, pl.ds(r * SUB, SUB), :],
                                      gsem.at[s]).start()

    @pl.when(jnp.logical_and(j >= 0, j < nt))
    def _():
        slot = j % 2
        gather_wait(slot)

        @pl.when(j >= 2)
        def _():
            scatter_wait(slot)

        @pl.when(tf_ref[j] == 1)
        def _():
            w1_s[...] = w1_ref[...].astype(BF16)
            w2_s[...] = w2_ref[...].astype(BF16)

        x = _load_token_major(xbuf, (slot,), TM_E)
        u = _dot(x.astype(BF16), w1_s[...]) + b1_ref[...]
        g = jnp.minimum(u[:, :D_FF], SWIGLU_LIMIT)
        up = jnp.clip(u[:, D_FF:], -SWIGLU_LIMIT, SWIGLU_LIMIT)
        act = (up + 1.0) * g * jax.nn.sigmoid(SWIGLU_ALPHA * g)
        _store_token_major(ybuf, (slot,), _dot(act.astype(BF16), w2_s[...]) + b2_ref[...])

    for s in range(2):
        @pl.when(jnp.logical_and(jnp.logical_and(j >= 0, j < nt), j % 2 == s))
        def _():
            for r in range(TM_E):
                row = pl.multiple_of(dst_ref[0, r], SUB)
                pltpu.make_async_copy(ybuf.at[s, pl.ds(r * SUB, SUB), :], yk_ref.at[pl.ds(row, SUB), :],
                                      ssem.at[s]).start()

    @pl.when(j == nt - 1)
    def _():
        scatter_wait(0)
        scatter_wait(1)


def _experts(h2, src, dst, tile_expert, first, n_tiles, layer, w1, b1, w2, b2):
    t, d = h2.shape[0] // SUB, D_MODEL
    nt_max = src.shape[0]
    assert t * TOP_K // TM_E >= 2
    prev = lambda i: jnp.maximum(i - 1, 0)
    wmap = lambda i, te, tf, ntl: (layer, te[prev(i)], 0, 0)
    smem_blk = lambda fn: pl.BlockSpec((None, 1, TM_E), fn, memory_space=pltpu.SMEM)
    return pl.pallas_call(
        _expert_kernel,
        grid_spec=pltpu.PrefetchScalarGridSpec(
            num_scalar_prefetch=3, grid=(nt_max + 1,),
            in_specs=[smem_blk(lambda i, te, tf, ntl: (jnp.minimum(i, nt_max - 1), 0, 0)),
                      smem_blk(lambda i, te, tf, ntl: (prev(i), 0, 0)),
                      pl.BlockSpec(memory_space=pl.ANY),
                      pl.BlockSpec((None, None, d, 2 * D_FF), wmap),
                      pl.BlockSpec((None, None, 1, 2 * D_FF), wmap),
                      pl.BlockSpec((None, None, D_FF, d), wmap),
                      pl.BlockSpec((None, None, 1, d), wmap)],
            out_specs=pl.BlockSpec(memory_space=pl.ANY),
            scratch_shapes=[pltpu.VMEM((2, TM_E * SUB, LANES), F32), pltpu.VMEM((2, TM_E * SUB, LANES), F32),
                            pltpu.VMEM((d, 2 * D_FF), BF16), pltpu.VMEM((D_FF, d), BF16),
                            pltpu.SemaphoreType.DMA((2,)), pltpu.SemaphoreType.DMA((2,))]),
        out_shape=jax.ShapeDtypeStruct(((TOP_K * t + 2 * TM_E) * SUB, LANES), F32),
        compiler_params=_cparams("arbitrary"), name="moe_experts",
    )(tile_expert, first, n_tiles, src, dst, h2, w1, b1, w2, b2)


def _combine_kernel(x1_ref, tw_ref, mod_ref, y0_ref, y1_ref, y2_ref, y3_ref, o_ref):
    tw = tw_ref[...]
    n = tw.shape[0]
    y = tw[:, 0:1] * _load_token_major(y0_ref, (), n)
    for kk, y_ref in enumerate((y1_ref, y2_ref, y3_ref), start=1):
        y = y + tw[:, kk:kk + 1] * _load_token_major(y_ref, (), n)
    o_ref[...] = x1_ref[...] + mod_ref[5:6, :] * y


def _combine(yk, x1, tw, mod):
    t, d = x1.shape
    nblk = t // TM
    row = lambda w: pl.BlockSpec((TM, w), lambda i: (i, 0))
    slab = lambda kk: pl.BlockSpec((TM * SUB, LANES), lambda i: (kk * nblk + i, 0))
    return pl.pallas_call(
        _combine_kernel, grid=(nblk,),
        in_specs=[row(d), row(LANES),
                  pl.BlockSpec((None, 6, d), lambda i: (_cond_row(i), 0, 0))] + [slab(kk) for kk in range(TOP_K)],
        out_specs=row(d),
        out_shape=jax.ShapeDtypeStruct((t, d), F32),
        compiler_params=_cparams("arbitrary"), name="moe_combine",
    )(x1, tw, mod, yk, yk, yk, yk)


def _moe(x1, h2, ti, tw, mod, layer, w1, b1, w2, b2):
    t = x1.shape[0]
    depth = w1.shape[0]
    assert TM == TM_E and TOP_K == 4
    n_tiles_max = t * TOP_K // TM_E + N_EXP
    src, dst, tile_expert, first, n_tiles = _route_plan(ti[:, :TOP_K], n_tiles_max)
    yk = _experts(h2, src, dst, tile_expert, first, n_tiles, layer,
                  w1, b1.reshape(depth, N_EXP, 1, -1), w2, b2.reshape(depth, N_EXP, 1, -1))
    return _combine(yk, x1, tw, mod)


def _pad_lanes(a, value=0.0):
    return jnp.pad(a, ((0, 0), (0, LANES - a.shape[1])), constant_values=value)


def kernel(x_prompt, x_sample, cache_na_k, cache_na_v, state_mlstm_C, state_mlstm_n, state_mlstm_m, state_ret_S, c, c_ctx, w_mod, b_mod, w_in_even, mlstm_gate_b, na_q_norm, na_k_norm, na_rpb, mlstm_norm, w_out_even, w_in_odd, ret_decay, ret_norm, w_out_odd, w_router, b_router, w_moe_in, b_moe_in, w_moe_out, b_moe_out):
    nb_c, s_c, d = x_prompt.shape
    nb_l, s_l, _ = x_sample.shape
    t_c = nb_c * s_c
    t_l = nb_l * s_l
    assert t_c == 4 * SEG and s_l == SEG and d == D_MODEL
    depth = w_mod.shape[0]
    dt = x_prompt.dtype

    x = jnp.concatenate([x_prompt.reshape(t_c, d), x_sample.reshape(t_l, d)], axis=0)
    cond = jnp.concatenate([c_ctx[None, :], c, jnp.zeros((N_COND - 1 - nb_l, d), F32)], axis=0)
    mod = _modulation(cond, w_mod, b_mod).reshape(depth, N_COND, 6, d)

    outs = {}
    for l in range(depth):
        e = l // 2
        mod_l = mod[l]
        if l % 2 == 0:
            w_in = w_in_even[e]
            n_main = 3 * W_A + 4 * W_B
            wg = _pad_lanes(w_in[:, n_main:])
            bg = _pad_lanes(mlstm_gate_b[e].reshape(1, 4 * H_B))
            z, g = _inproj(x, mod_l, w_in[:, :n_main].astype(BF16), wg, bg)
            qn = na_q_norm[e].reshape(1, HD_A)
            kn = na_k_norm[e].reshape(1, HD_A)
            oa_c, ka_c = _ctx_attention(z, nb_c, s_c, qn, kn)
            past = cache_na_k.shape[2]
            oa_l = _na_attention(z, t_c, nb_l, s_l,
                                 cache_na_k[:, e].reshape(nb_l, past, W_A), cache_na_v[:, e].reshape(nb_l, past, W_A),
                                 _na_bias_table(na_rpb[e]), qn, kn)
            nw = mlstm_norm[e].reshape(1, W_B)
            hm_c, c_fin, n_fin, m_fin = _mlstm(z, g, 0, nb_c, s_c, nw)
            init = (state_mlstm_C[:, e].reshape(nb_l, 2 * H_B, HD_B, HD_B),
                    state_mlstm_n[:, e].reshape(nb_l, 2 * H_B, HD_B),
                    jnp.broadcast_to(state_mlstm_m[:, e].reshape(nb_l, 2 * H_B, 1), (nb_l, 2 * H_B, LANES)))
            hm_l = _mlstm(z, g, t_c, nb_l, s_l, nw, init)[0]
            a = jnp.concatenate([oa_c, oa_l], axis=0)
            b = jnp.concatenate([hm_c, hm_l], axis=0)
            w_out = w_out_even[e].astype(BF16)
            wa, wb = w_out[:W_A], w_out[W_A:]
            outs.setdefault("na_k", []).append(ka_c.reshape(nb_c, s_c, H_A, HD_A))
            outs.setdefault("na_v", []).append(z[:t_c, 2 * W_A:3 * W_A].reshape(nb_c, s_c, H_A, HD_A))
            outs.setdefault("C", []).append(c_fin.reshape(nb_c, 2, H_B, HD_B, HD_B))
            outs.setdefault("n", []).append(n_fin.reshape(nb_c, 2, H_B, HD_B))
            outs.setdefault("m", []).append(m_fin[:, :, 0].reshape(nb_c, 2, H_B))
        else:
            w_c = H_C * HD_C
            z = _inproj(x, mod_l, w_in_odd[e].astype(BF16))
            dl_rep = jnp.broadcast_to(ret_decay[e].reshape(2 * H_C, 1), (2 * H_C, LANES))
            nw = ret_norm[e].reshape(1, w_c)
            hr_c, s_fin = _retention(z, 0, nb_c, s_c, dl_rep, nw)
            hr_l = _retention(z, t_c, nb_l, s_l, dl_rep, nw, rope=_rope_tables(s_l),
                              init=state_ret_S[:, e].reshape(nb_l, 2 * H_C, HD_C, HD_C))[0]
            fd_c = _fnet(z, 0, nb_c, s_c)
            fd_l = _fnet(z, t_c, nb_l, s_l)
            a = jnp.concatenate([hr_c, hr_l], axis=0)
            b = jnp.concatenate([fd_c, fd_l], axis=0)
            w_out = w_out_odd[e].astype(BF16)
            wa, wb = w_out[:w_c], w_out[w_c:]
            outs.setdefault("S", []).append(s_fin.reshape(nb_c, 2, H_C, HD_C, HD_C))
        wr = _pad_lanes(w_router[l])
        br = _pad_lanes(b_router[l].reshape(1, N_EXP), NEG)
        x1, h2, ti, tw = _outproj_router(a, b, x, mod_l, wa, wb, wr, br)
        x = _moe(x1, h2, ti, tw, mod_l, l, w_moe_in, b_moe_in, w_moe_out, b_moe_out)

    y_prompt = x[:t_c].reshape(nb_c, s_c, d)
    y_sample = x[t_c:].reshape(nb_l, s_l, d)
    stack = lambda key: jnp.stack(outs[key], axis=1).astype(dt)
    return (y_prompt, y_sample, stack("na_k"), stack("na_v"), stack("C"), stack("n"), stack("m"), stack("S"))
```

```python
import functools
import math

import numpy as np
import jax
import jax.numpy as jnp
from jax import lax
from jax.experimental import pallas as pl
from jax.experimental.pallas import tpu as pltpu

F32 = jnp.float32
BF16 = jnp.bfloat16
HIGHEST = lax.Precision.HIGHEST

D_MODEL = 1024
GRID_W = 64
WIN_R = 8
WIN_C = 16
H_A, HD_A = 8, 64
H_B, HD_B = 4, 128
H_C, HD_C = 4, 128
N_FG, FG_W = 4, 128
W_A = H_A * HD_A
W_B = H_B * HD_B
N_EXP = 32
TOP_K = 4
D_FF = D_MODEL
SWIGLU_LIMIT = 7.0
SWIGLU_ALPHA = 1.702
CHUNK = 128
ROPE_BASE = 10000.0
EPS = 1e-6

LANES = 128
SEG = 1024
N_COND = 8
TM = 256
TM_E = 256
NEG = -1e30
VMEM_LIMIT = 56 * 1024 * 1024


def _cparams(*sem):
    return pltpu.CompilerParams(dimension_semantics=sem, vmem_limit_bytes=VMEM_LIMIT)


def _cond_row(i):
    return jnp.maximum((i * TM) // SEG - 3, 0)


def _log_sigmoid(x):
    return jnp.minimum(x, 0.0) - jnp.log1p(jnp.exp(-jnp.abs(x)))


def _dot(a, b):
    return jnp.dot(a, b, preferred_element_type=F32)


def _dot_nt(a, b):
    return lax.dot_general(a, b, (((1,), (1,)), ((), ())), preferred_element_type=F32)


def _dot_hi(a, b):
    return jnp.dot(a, b, precision=HIGHEST, preferred_element_type=F32)


def _mod_kernel(cond_ref, w_ref, b_ref, o_ref):
    c = cond_ref[...]
    s = c * jax.nn.sigmoid(c)
    o_ref[...] = _dot_hi(s, w_ref[...]) + b_ref[...]


def _modulation(cond, w_mod, b_mod):
    depth, d, n = w_mod.shape
    tn = 1536
    return pl.pallas_call(
        _mod_kernel,
        grid=(depth, n // tn),
        in_specs=[pl.BlockSpec((N_COND, d), lambda l, j: (0, 0)),
                  pl.BlockSpec((None, d, tn), lambda l, j: (l, 0, j)),
                  pl.BlockSpec((None, 1, tn), lambda l, j: (l, 0, j))],
        out_specs=pl.BlockSpec((None, N_COND, tn), lambda l, j: (l, 0, j)),
        out_shape=jax.ShapeDtypeStruct((depth, N_COND, n), F32),
        compiler_params=_cparams("arbitrary", "arbitrary"),
        name="modulation",
    )(cond, w_mod, b_mod.reshape(depth, 1, n))


def _rms_mod(x, shift, scale):
    h = x * lax.rsqrt(jnp.mean(x * x, axis=-1, keepdims=True) + EPS)
    return h * (1.0 + scale) + shift


def _inproj_kernel(x_ref, mod_ref, w_ref, z_ref):
    h = _rms_mod(x_ref[...], mod_ref[0:1, :], mod_ref[1:2, :])
    z_ref[...] = _dot(h.astype(BF16), w_ref[...])


def _inproj_gate_kernel(x_ref, mod_ref, w_ref, wg_ref, bg_ref, z_ref, g_ref):
    h = _rms_mod(x_ref[...], mod_ref[0:1, :], mod_ref[1:2, :])
    z_ref[...] = _dot(h.astype(BF16), w_ref[...])
    g_ref[...] = _dot_hi(h, wg_ref[...]) + bg_ref[...]


def _inproj(x, mod, w, wg=None, bg=None):
    t, d = x.shape
    n = w.shape[1]
    in_specs = [pl.BlockSpec((TM, d), lambda i: (i, 0)),
                pl.BlockSpec((None, 6, d), lambda i: (_cond_row(i), 0, 0)),
                pl.BlockSpec((d, n), lambda i: (0, 0))]
    z_spec = pl.BlockSpec((TM, n), lambda i: (i, 0))
    z_shape = jax.ShapeDtypeStruct((t, n), F32)
    if wg is None:
        return pl.pallas_call(
            _inproj_kernel, grid=(t // TM,), in_specs=in_specs, out_specs=z_spec, out_shape=z_shape,
            compiler_params=_cparams("arbitrary"), name="inproj",
        )(x, mod, w)
    in_specs += [pl.BlockSpec((d, LANES), lambda i: (0, 0)), pl.BlockSpec((1, LANES), lambda i: (0, 0))]
    return pl.pallas_call(
        _inproj_gate_kernel, grid=(t // TM,), in_specs=in_specs,
        out_specs=[z_spec, pl.BlockSpec((TM, LANES), lambda i: (i, 0))],
        out_shape=[z_shape, jax.ShapeDtypeStruct((t, LANES), F32)],
        compiler_params=_cparams("arbitrary"), name="inproj_gate",
    )(x, mod, w, wg, bg)


def _head_rms(x, w):
    return x * lax.rsqrt(jnp.mean(x * x, axis=-1, keepdims=True) + EPS) * w


def _ctx_attn_kernel(q_ref, k_ref, v_ref, qn_ref, kn_ref, o_ref, ko_ref):
    scale = HD_A ** -0.5
    for h in range(H_A):
        sl = slice(h * HD_A, (h + 1) * HD_A)
        q = _head_rms(q_ref[:, sl], qn_ref[...])
        k = _head_rms(k_ref[:, sl], kn_ref[...])
        ko_ref[:, sl] = k
        s = _dot_nt(q.astype(BF16), k.astype(BF16)) * scale
        p = jnp.exp(s - jnp.max(s, axis=-1, keepdims=True))
        den = jnp.sum(p, axis=-1, keepdims=True)
        o_ref[:, sl] = _dot(p.astype(BF16), v_ref[:, sl].astype(BF16)) / den


def _ctx_attention(z, n_batch, seq, qn, kn):
    spec = lambda c: pl.BlockSpec((seq, W_A), lambda b: (b, c))
    wspec = pl.BlockSpec((1, HD_A), lambda b: (0, 0))
    out = jax.ShapeDtypeStruct((n_batch * seq, W_A), F32)
    return pl.pallas_call(
        _ctx_attn_kernel, grid=(n_batch,),
        in_specs=[spec(0), spec(1), spec(2), wspec, wspec],
        out_specs=[pl.BlockSpec((seq, W_A), lambda b: (b, 0))] * 2,
        out_shape=[out, out],
        compiler_params=_cparams("arbitrary"), name="ctx_attention",
    )(z, z, z, qn, kn)


def _na_bias_table(rpb):
    qc = np.arange(GRID_W)
    kc = np.arange(GRID_W)
    cstart = np.clip(qc - WIN_C // 2, 0, GRID_W - WIN_C)
    col_in = (kc[None, :] >= cstart[:, None]) & (kc[None, :] < cstart[:, None] + WIN_C)
    dc = np.clip(kc[None, :] - qc[:, None], 1 - WIN_C, WIN_C - 1) + WIN_C - 1
    cls = np.arange(WIN_R)
    j = np.arange(WIN_R)
    dr = j[None, :] - cls[:, None] + WIN_R - 1
    sel_r = jnp.asarray(dr[:, :, None] == np.arange(2 * WIN_R - 1)[None, None, :], F32)
    sel_c = jnp.asarray(dc[:, :, None] == np.arange(2 * WIN_C - 1)[None, None, :], F32)
    tab = jnp.einsum("hab,cja,qkb->hcqjk", rpb, sel_r, sel_c, precision=HIGHEST)
    tab = jnp.where(jnp.asarray(col_in)[None, None, :, None, :], tab, NEG)
    return tab.reshape(H_A, WIN_R, GRID_W, WIN_R * GRID_W)


def _na_kernel(q_ref, k_ref, v_ref, kc_ref, vc_ref, bias_ref, qn_ref, kn_ref, o_ref, kn_s, *, rows):
    r = pl.program_id(1)
    scale = HD_A ** -0.5

    @pl.when(r == 0)
    def _():
        for h in range(H_A):
            sl = slice(h * HD_A, (h + 1) * HD_A)
            kn_s[:, sl] = _head_rms(k_ref[:, sl], kn_ref[...])

    rs = jnp.clip(r - WIN_R // 2, 0, rows - WIN_R)
    start = pl.multiple_of(rs * GRID_W, GRID_W)
    n_loc = WIN_R * GRID_W
    for h in range(H_A):
        sl = slice(h * HD_A, (h + 1) * HD_A)
        q = _head_rms(q_ref[:, sl], qn_ref[...]).astype(BF16)
        kl = kn_s[pl.ds(start, n_loc), sl].astype(BF16)
        vl = v_ref[pl.ds(start, n_loc), sl].astype(BF16)
        s_loc = _dot_nt(q, kl) * scale + bias_ref[h]
        s_ctx = _dot_nt(q, kc_ref[:, sl].astype(BF16)) * scale
        m = jnp.maximum(jnp.max(s_loc, axis=-1, keepdims=True), jnp.max(s_ctx, axis=-1, keepdims=True))
        p_loc = jnp.exp(s_loc - m)
        p_ctx = jnp.exp(s_ctx - m)
        den = jnp.sum(p_loc, axis=-1, keepdims=True) + jnp.sum(p_ctx, axis=-1, keepdims=True)
        o = _dot(p_loc.astype(BF16), vl) + _dot(p_ctx.astype(BF16), vc_ref[:, sl].astype(BF16))
        o_ref[:, sl] = o / den


def _na_attention(z, row0, n_batch, seq, kc, vc, bias, qn, kn):
    rows = seq // GRID_W
    past = kc.shape[1]
    blk0 = row0 // GRID_W
    sblk0 = row0 // seq

    def cls_of(r):
        return r - jnp.clip(r - WIN_R // 2, 0, rows - WIN_R)

    full = lambda c: pl.BlockSpec((seq, W_A), lambda b, r: (sblk0 + b, c))
    cspec = pl.BlockSpec((None, past, W_A), lambda b, r: (b, 0, 0))
    wspec = pl.BlockSpec((1, HD_A), lambda b, r: (0, 0))
    return pl.pallas_call(
        functools.partial(_na_kernel, rows=rows), grid=(n_batch, rows),
        in_specs=[pl.BlockSpec((GRID_W, W_A), lambda b, r: (blk0 + b * rows + r, 0)),
                  full(1), full(2), cspec, cspec,
                  pl.BlockSpec((H_A, None, GRID_W, WIN_R * GRID_W), lambda b, r: (0, cls_of(r), 0, 0)),
                  wspec, wspec],
        out_specs=pl.BlockSpec((GRID_W, W_A), lambda b, r: (b * rows + r, 0)),
        out_shape=jax.ShapeDtypeStruct((n_batch * seq, W_A), F32),
        scratch_shapes=[pltpu.VMEM((seq, W_A), F32)],
        compiler_params=_cparams("arbitrary", "arbitrary"), name="na_attention",
    )(z, z, z, kc, vc, bias, qn, kn)


def _tri_masks():
    li = lax.broadcasted_iota(jnp.int32, (CHUNK, CHUNK), 0)
    si = lax.broadcasted_iota(jnp.int32, (CHUNK, CHUNK), 1)
    return li >= si, li <= si


def _mlstm_kernel(*refs, nc, has_init):
    if has_init:
        (q_ref, k_ref, v_ref, og_ref, g_ref, nw_ref, c0_ref, n0_ref, m0_ref,
         o_ref, cf_ref, nf_ref, mf_ref, h_s, c_s, n_s, m_s) = refs
    else:
        (q_ref, k_ref, v_ref, og_ref, g_ref, nw_ref,
         o_ref, cf_ref, nf_ref, mf_ref, h_s, c_s, n_s, m_s) = refs
    nd = 2 * H_B
    if has_init:
        c_s[...] = c0_ref[...]
        n_s[...] = n0_ref[...]
        m_s[...] = m0_ref[...]
    else:
        c_s[...] = jnp.zeros_like(c_s)
        n_s[...] = jnp.zeros_like(n_s)
        m_s[...] = jnp.zeros_like(m_s)

    causal, anti = _tri_masks()
    tri_f = causal.astype(F32)
    tri_b = anti.astype(F32)
    kscale = HD_B ** -0.5

    def chunk_step(c, carry):
        for d in range(2):
            cc = c if d == 0 else nc - 1 - c
            t0 = pl.multiple_of(cc * CHUNK, CHUNK)
            g = g_ref[pl.ds(t0, CHUNK), :]
            gt = g.T
            ls = _log_sigmoid(g)
            lst = _log_sigmoid(gt)
            tri_c, tri_r, mask = (tri_f, tri_b, causal) if d == 0 else (tri_b, tri_f, anti)
            b_cols = _dot_hi(tri_c, ls)
            b_rows = _dot_hi(lst, tri_r)
            last = CHUNK - 1 if d == 0 else 0
            for h in range(H_B):
                ci = (2 * d) * H_B + h
                cf = (2 * d + 1) * H_B + h
                hs = slice(h * HD_B, (h + 1) * HD_B)
                q = q_ref[pl.ds(t0, CHUNK), hs]
                k = k_ref[pl.ds(t0, CHUNK), hs] * kscale
                v = v_ref[pl.ds(t0, CHUNK), hs]
                qb, kb, vb = q.astype(BF16), k.astype(BF16), v.astype(BF16)
                b_col = b_cols[:, cf:cf + 1]
                b_row = b_rows[cf:cf + 1, :]
                i_row = gt[ci:ci + 1, :]
                sidx = d * H_B + h
                cst = c_s[sidx]
                nst = n_s[sidx:sidx + 1, :]
                mst = m_s[sidx:sidx + 1, 0:1]
                dmat = jnp.where(mask, b_col - b_row + i_row, -jnp.inf)
                inter = b_col + mst
                mt = jnp.maximum(inter, jnp.max(dmat, axis=-1, keepdims=True))
                w = jnp.exp(dmat - mt) * _dot_nt(qb, kb)
                a = jnp.exp(inter - mt)
                num = _dot(w.astype(BF16), vb) + _dot(qb, cst.astype(BF16)) * a
                den = jnp.sum(w, axis=-1, keepdims=True) + a * jnp.sum(q * nst, axis=-1, keepdims=True)
                hc = num / jnp.maximum(jnp.abs(den), jnp.exp(-mt))
                h_s[d, pl.ds(t0, CHUNK), hs] = hc
                bl = b_row[:, last:last + 1]
                dl = bl - b_row + i_row
                m_new = jnp.maximum(bl + mst, jnp.max(dl, axis=-1, keepdims=True))
                wl = jnp.exp(dl - m_new)
                dec = jnp.exp(bl + mst - m_new)
                kw = (k.T * wl).astype(BF16)
                c_s[sidx] = dec * cst + _dot(kw, vb)
                wl8 = jnp.broadcast_to(wl, (8, CHUNK)).astype(BF16)
                n_s[sidx:sidx + 1, :] = dec * nst + _dot(wl8, kb)[0:1, :]
                m_s[sidx:sidx + 1, :] = jnp.broadcast_to(m_new, (1, LANES))
        return carry

    lax.fori_loop(0, nc, chunk_step, 0)

    for h in range(H_B):
        hs = slice(h * HD_B, (h + 1) * HD_B)
        hsum = h_s[0, :, hs] + h_s[1, :, hs]
        o_ref[:, hs] = _head_rms(hsum, nw_ref[:, hs]) * jax.nn.sigmoid(og_ref[:, hs])
    cf_ref[...] = c_s[...]
    nf_ref[...] = n_s[...]
    mf_ref[...] = m_s[...]


def _mlstm(z, g, row0, n_batch, seq, norm_w, init=None):
    sblk0 = row0 // seq
    nd = 2 * H_B
    spec = lambda c: pl.BlockSpec((seq, W_B), lambda b: (sblk0 + b, c))
    in_specs = [spec(3), spec(4), spec(5), spec(6),
                pl.BlockSpec((seq, LANES), lambda b: (sblk0 + b, 0)),
                pl.BlockSpec((1, W_B), lambda b: (0, 0))]
    args = [z, z, z, z, g, norm_w]
    st_specs = [pl.BlockSpec((None, nd, HD_B, HD_B), lambda b: (b, 0, 0, 0)),
                pl.BlockSpec((None, nd, HD_B), lambda b: (b, 0, 0)),
                pl.BlockSpec((None, nd, LANES), lambda b: (b, 0, 0))]
    if init is not None:
        in_specs += st_specs
        args += list(init)
    return pl.pallas_call(
        functools.partial(_mlstm_kernel, nc=seq // CHUNK, has_init=init is not None), grid=(n_batch,),
        in_specs=in_specs,
        out_specs=[pl.BlockSpec((seq, W_B), lambda b: (b, 0))] + st_specs,
        out_shape=[jax.ShapeDtypeStruct((n_batch * seq, W_B), F32),
                   jax.ShapeDtypeStruct((n_batch, nd, HD_B, HD_B), F32),
                   jax.ShapeDtypeStruct((n_batch, nd, HD_B), F32),
                   jax.ShapeDtypeStruct((n_batch, nd, LANES), F32)],
        scratch_shapes=[pltpu.VMEM((2, seq, W_B), F32), pltpu.VMEM((nd, HD_B, HD_B), F32),
                        pltpu.VMEM((nd, HD_B), F32), pltpu.VMEM((nd, LANES), F32)],
        compiler_params=_cparams("arbitrary"), name="mlstm",
    )(*args)


def _rope_tables(seq):
    half = HD_C // 2
    quarter = half // 2
    t = np.arange(seq)
    inv = ROPE_BASE ** (-np.arange(0, half, 2, dtype=np.float64) / half)
    ang_r = (t // GRID_W)[:, None] * inv[None, :]
    ang_c = (t % GRID_W)[:, None] * inv[None, :]
    cos_t = np.concatenate([np.cos(ang_r), np.cos(ang_r), np.cos(ang_c), np.cos(ang_c)], -1)
    sin_t = np.concatenate([-np.sin(ang_r), np.sin(ang_r), -np.sin(ang_c), np.sin(ang_c)], -1)
    assert cos_t.shape == (seq, 4 * quarter)
    return jnp.asarray(cos_t, F32), jnp.asarray(sin_t, F32)


def _rope(x, cos_t, sin_t):
    quarter = HD_C // 4
    lane = lax.broadcasted_iota(jnp.int32, x.shape, 1)
    first = (lane % (2 * quarter)) < quarter
    swapped = jnp.where(first, pltpu.roll(x, HD_C - quarter, 1), pltpu.roll(x, quarter, 1))
    return x * cos_t + swapped * sin_t


def _ret_kernel(*refs, nc, has_init, use_rope):
    refs = list(refs)
    q_ref, k_ref, v_ref, gg_ref, dl_ref, nw_ref = refs[:6]
    pos = 6
    if use_rope:
        cos_ref, sin_ref = refs[pos:pos + 2]
        pos += 2
    if has_init:
        s0_ref = refs[pos]
        pos += 1
    o_ref, sf_ref, h_s, s_s = refs[pos:pos + 4]
    if has_init:
        s_s[...] = s0_ref[...]
    else:
        s_s[...] = jnp.zeros_like(s_s)

    causal, anti = _tri_masks()
    li = lax.broadcasted_iota(jnp.int32, (CHUNK, CHUNK), 0).astype(F32)
    si = lax.broadcasted_iota(jnp.int32, (CHUNK, CHUNK), 1).astype(F32)
    lg_all = _log_sigmoid(dl_ref[...])
    kscale = HD_C ** -0.5

    def chunk_step(c, carry):
        for d in range(2):
            cc = c if d == 0 else nc - 1 - c
            t0 = pl.multiple_of(cc * CHUNK, CHUNK)
            for h in range(H_C):
                sidx = d * H_C + h
                hs = slice(h * HD_C, (h + 1) * HD_C)
                lg = lg_all[sidx:sidx + 1, :]
                q = q_ref[pl.ds(t0, CHUNK), hs]
                k = k_ref[pl.ds(t0, CHUNK), hs] * kscale
                v = v_ref[pl.ds(t0, CHUNK), hs]
                if use_rope:
                    cos_t = cos_ref[pl.ds(t0, CHUNK), :]
                    sin_t = sin_ref[pl.ds(t0, CHUNK), :]
                    q = _rope(q, cos_t, sin_t)
                    k = _rope(k, cos_t, sin_t)
                if d == 0:
                    decay = jnp.exp(jnp.where(causal, (li - si) * lg, -jnp.inf))
                    q_dec = jnp.exp((li + 1.0) * lg)
                    k_dec = jnp.exp((CHUNK - 1.0 - li) * lg)
                else:
                    decay = jnp.exp(jnp.where(anti, (si - li) * lg, -jnp.inf))
                    q_dec = jnp.exp((CHUNK - li) * lg)
                    k_dec = jnp.exp(li * lg)
                c_dec = jnp.exp(CHUNK * lg)
                st = s_s[sidx]
                qb, kb, vb = q.astype(BF16), k.astype(BF16), v.astype(BF16)
                att = _dot_nt(qb, kb) * decay
                o = _dot(att.astype(BF16), vb) + _dot(qb, st.astype(BF16)) * q_dec
                h_s[d, pl.ds(t0, CHUNK), hs] = o
                kd = (k * k_dec).T.astype(BF16)
                s_s[sidx] = c_dec * st + _dot(kd, vb)
        return carry

    lax.fori_loop(0, nc, chunk_step, 0)

    for h in range(H_C):
        hs = slice(h * HD_C, (h + 1) * HD_C)
        osum = h_s[0, :, hs] + h_s[1, :, hs]
        gg = gg_ref[:, hs]
        o_ref[:, hs] = _head_rms(osum, nw_ref[:, hs]) * (gg * jax.nn.sigmoid(gg))
    sf_ref[...] = s_s[...]


def _retention(z, row0, n_batch, seq, decay_rep, norm_w, rope=None, init=None):
    sblk0 = row0 // seq
    nd = 2 * H_C
    w_c = H_C * HD_C
    spec = lambda c: pl.BlockSpec((seq, w_c), lambda b: (sblk0 + b, c))
    in_specs = [spec(0), spec(1), spec(2), spec(3),
                pl.BlockSpec((nd, LANES), lambda b: (0, 0)),
                pl.BlockSpec((1, w_c), lambda b: (0, 0))]
    args = [z, z, z, z, decay_rep, norm_w]
    if rope is not None:
        in_specs += [pl.BlockSpec((seq, HD_C), lambda b: (0, 0))] * 2
        args += list(rope)
    st_spec = pl.BlockSpec((None, nd, HD_C, HD_C), lambda b: (b, 0, 0, 0))
    if init is not None:
        in_specs.append(st_spec)
        args.append(init)
    return pl.pallas_call(
        functools.partial(_ret_kernel, nc=seq // CHUNK, has_init=init is not None, use_rope=rope is not None),
        grid=(n_batch,), in_specs=in_specs,
        out_specs=[pl.BlockSpec((seq, w_c), lambda b: (b, 0)), st_spec],
        out_shape=[jax.ShapeDtypeStruct((n_batch * seq, w_c), F32),
                   jax.ShapeDtypeStruct((n_batch, nd, HD_C, HD_C), F32)],
        scratch_shapes=[pltpu.VMEM((2, seq, w_c), F32), pltpu.VMEM((nd, HD_C, HD_C), F32)],
        compiler_params=_cparams("arbitrary"), name="retention",
    )(*args)


def _dft_tables(n):
    idx = (np.arange(n)[:, None] * np.arange(n)[None, :]) % n
    ang = 2.0 * np.pi * idx / n
    return np.cos(ang) / np.sqrt(n), np.sin(ang) / np.sqrt(n)


def _fnet_kernel(x_ref, cw_ref, sw_ref, cs_ref, ss_ref, o_ref):
    for g in range(N_FG):
        gs = slice(g * FG_W, (g + 1) * FG_W)
        x = x_ref[:, gs].astype(BF16)
        xc = _dot(x, cw_ref[...]).astype(BF16)
        xs = _dot(x, sw_ref[...]).astype(BF16)
        o_ref[:, gs] = _dot(cs_ref[...], xc) - _dot(ss_ref[...], xs)


def _fnet(z, row0, n_batch, seq):
    sblk0 = row0 // seq
    w_d = N_FG * FG_W
    cw, sw = _dft_tables(FG_W)
    cs, ss = _dft_tables(seq)
    tabs = [jnp.asarray(a, F32).astype(BF16) for a in (cw, sw, cs, ss)]
    wspec = pl.BlockSpec((FG_W, FG_W), lambda b: (0, 0))
    sspec = pl.BlockSpec((seq, seq), lambda b: (0, 0))
    return pl.pallas_call(
        _fnet_kernel, grid=(n_batch,),
        in_specs=[pl.BlockSpec((seq, w_d), lambda b: (sblk0 + b, 4)), wspec, wspec, sspec, sspec],
        out_specs=pl.BlockSpec((seq, w_d), lambda b: (b, 0)),
        out_shape=jax.ShapeDtypeStruct((n_batch * seq, w_d), F32),
        compiler_params=_cparams("arbitrary"), name="fnet",
    )(z, *tabs)


SUB = 8
assert D_MODEL == SUB * LANES


def _store_token_major(ref, lead, val):
    n = val.shape[0]
    for c in range(SUB):
        ref[lead + (pl.ds(c, n, stride=SUB), slice(None))] = val[:, c * LANES:(c + 1) * LANES]


def _load_token_major(ref, lead, n):
    return jnp.concatenate([ref[lead + (pl.ds(c, n, stride=SUB), slice(None))] for c in range(SUB)], axis=1)


def _outproj_router_kernel(a_ref, b_ref, x_ref, mod_ref, wa_ref, wb_ref, wr_ref, br_ref,
                           x1_ref, h2_ref, ti_ref, tw_ref):
    y = _dot(a_ref[...].astype(BF16), wa_ref[...]) + _dot(b_ref[...].astype(BF16), wb_ref[...])
    x1 = x_ref[...] + mod_ref[2:3, :] * y
    x1_ref[...] = x1
    h2 = _rms_mod(x1, mod_ref[3:4, :], mod_ref[4:5, :])
    _store_token_major(h2_ref, (), h2)
    logits = _dot_hi(h2, wr_ref[...]) + br_ref[...]
    lane = lax.broadcasted_iota(jnp.int32, logits.shape, 1)
    lane_f = lane.astype(F32)
    cur = logits
    vals = []
    ti = jnp.zeros(logits.shape, jnp.int32)
    for kk in range(TOP_K):
        mx = jnp.max(cur, axis=-1, keepdims=True)
        idx = jnp.min(jnp.where(cur == mx, lane_f, float(LANES)), axis=-1, keepdims=True)
        ti = jnp.where(lane == kk, idx.astype(jnp.int32), ti)
        cur = jnp.where(lane_f == idx, -jnp.inf, cur)
        vals.append(mx)
    es = [jnp.exp(v - vals[0]) for v in vals]
    tot = es[0] + es[1] + es[2] + es[3]
    tw = jnp.zeros(logits.shape, F32)
    for kk in range(TOP_K):
        tw = jnp.where(lane == kk, es[kk] / tot, tw)
    ti_ref[...] = ti
    tw_ref[...] = tw


def _outproj_router(a, b, x, mod, wa, wb, wr, br):
    t, d = x.shape
    wid = a.shape[1]
    row = lambda w: pl.BlockSpec((TM, w), lambda i: (i, 0))
    const = lambda r, c: pl.BlockSpec((r, c), lambda i: (0, 0))
    return pl.pallas_call(
        _outproj_router_kernel, grid=(t // TM,),
        in_specs=[row(wid), row(wid), row(d),
                  pl.BlockSpec((None, 6, d), lambda i: (_cond_row(i), 0, 0)),
                  const(wid, d), const(wid, d), const(d, LANES), const(1, LANES)],
        out_specs=[row(d), pl.BlockSpec((TM * SUB, LANES), lambda i: (i, 0)), row(LANES), row(LANES)],
        out_shape=[jax.ShapeDtypeStruct((t, d), F32), jax.ShapeDtypeStruct((t * SUB, LANES), F32),
                   jax.ShapeDtypeStruct((t, LANES), jnp.int32), jax.ShapeDtypeStruct((t, LANES), F32)],
        compiler_params=_cparams("arbitrary"), name="outproj_router",
    )(a, b, x, mod, wa, wb, wr, br)


def _route_plan(top_i, n_tiles_max):
    onehot = (top_i[:, :, None] == jnp.arange(N_EXP, dtype=jnp.int32)[None, None, :]).astype(jnp.int32)
    sel = onehot.sum(axis=1)
    before = jnp.cumsum(sel, axis=0) - sel
    counts = before[-1] + sel[-1]
    tiles = (counts + TM_E - 1) // TM_E
    tile_end = jnp.cumsum(tiles)
    tile_start = tile_end - tiles
    dest = jnp.take_along_axis(tile_start[None, :] * TM_E + before, top_i, axis=1)
    n_tiles = tile_end[-1]
    tid = jnp.minimum(jnp.arange(n_tiles_max, dtype=jnp.int32), n_tiles - 1)
    tile_expert = jnp.sum((tile_end[None, :] <= tid[:, None]).astype(jnp.int32), axis=1)
    first = jnp.concatenate([jnp.ones((1,), jnp.int32), (tile_expert[1:] != tile_expert[:-1]).astype(jnp.int32)])
    used = tiles > 0
    ids = jnp.arange(N_EXP, dtype=jnp.int32)
    later = jnp.where(used[None, :] & (ids[None, :] > ids[:, None]), ids[None, :], N_EXP)
    next_used = jnp.min(later, axis=1)
    next_used = jnp.where(next_used < N_EXP, next_used, -1).astype(jnp.int32)
    parity = ((jnp.cumsum(used.astype(jnp.int32)) - 1) % 2).astype(jnp.int32)
    last_tile = jnp.where(used, tile_end - 1, 0).astype(jnp.int32)
    plan = dict(tile_expert=tile_expert, first=first, n_tiles=n_tiles.reshape(1).astype(jnp.int32),
                tile_next=next_used[tile_expert], tile_parity=parity[tile_expert], last_tile=last_tile)
    return (dest * SUB).astype(jnp.int32), plan


DISPATCH_BLK = 1024


def _dispatch_kernel(lt_ref, nt_ref, dest_ref, h_ref, xs_ref, h_s, zero_s, sem, hsem):
    i = pl.program_id(0)
    tile_rows = TM_E * SUB
    n_tiles_max = xs_ref.shape[0] // tile_rows
    n_tok = h_s.shape[0] // SUB

    @pl.when(i == 0)
    def _():
        stage = pltpu.make_async_copy(h_ref, h_s, hsem)
        stage.start()
        zero_s[...] = jnp.zeros_like(zero_s)

        def zero_tile(tile):
            r0 = pl.multiple_of(tile * tile_rows, tile_rows)
            return pltpu.make_async_copy(zero_s, xs_ref.at[pl.ds(r0, tile_rows), :], sem)

        def start_unused(j, carry):
            zero_tile(j).start()
            return carry

        def wait_unused(j, carry):
            zero_tile(j).wait()
            return carry

        for e in range(N_EXP):
            zero_tile(lt_ref[e]).start()
        lax.fori_loop(nt_ref[0], n_tiles_max, start_unused, 0)
        for e in range(N_EXP):
            zero_tile(lt_ref[e]).wait()
        lax.fori_loop(nt_ref[0], n_tiles_max, wait_unused, 0)
        stage.wait()

    base = i * DISPATCH_BLK

    def issue(t, carry):
        src = pl.multiple_of((base + t) * SUB, SUB)
        for kk in range(TOP_K):
            row = pl.multiple_of(dest_ref[0, t * TOP_K + kk], SUB)
            pltpu.make_async_copy(h_s.at[pl.ds(src, SUB), :], xs_ref.at[pl.ds(row, SUB), :], sem).start(priority=kk % 2)
        return carry

    lax.fori_loop(0, DISPATCH_BLK, issue, 0, unroll=2)

    @pl.when(i == pl.num_programs(0) - 1)
    def _():
        for kk in range(TOP_K):
            pltpu.make_async_copy(h_s, xs_ref.at[pl.ds(0, n_tok * SUB), :], sem).wait()


def _dispatch(h2, dest, plan, n_rows):
    t = h2.shape[0] // SUB
    nblk = t // DISPATCH_BLK
    dest3 = dest.reshape(nblk, 1, DISPATCH_BLK * TOP_K)
    return pl.pallas_call(
        _dispatch_kernel,
        grid_spec=pltpu.PrefetchScalarGridSpec(
            num_scalar_prefetch=2, grid=(nblk,),
            in_specs=[pl.BlockSpec((None, 1, DISPATCH_BLK * TOP_K), lambda i, lt, nt: (i, 0, 0), memory_space=pltpu.SMEM),
                      pl.BlockSpec(memory_space=pl.ANY)],
            out_specs=pl.BlockSpec(memory_space=pl.ANY),
            scratch_shapes=[pltpu.VMEM((t * SUB, LANES), F32), pltpu.VMEM((TM_E * SUB, LANES), F32),
                            pltpu.SemaphoreType.DMA(()), pltpu.SemaphoreType.DMA(())]),
        out_shape=jax.ShapeDtypeStruct((n_rows * SUB, LANES), F32),
        compiler_params=_cparams("arbitrary"), name="moe_dispatch",
    )(plan["last_tile"], plan["n_tiles"], dest3, h2)


def _expert_kernel(te_ref, tf_ref, nt_ref, nx_ref, par_ref, xs_ref, w1_ref, b1_ref, w2_ref, b2_ref, ys_ref,
                   w1f, w2f, w1_s, w2_s, wsem, *, layer):
    i = pl.program_id(0)

    def fetch(expert, slot):
        return (pltpu.make_async_copy(w1_ref.at[layer, expert], w1f.at[slot], wsem.at[slot]),
                pltpu.make_async_copy(w2_ref.at[layer, expert], w2f.at[slot], wsem.at[slot]))

    @pl.when(i == 0)
    def _():
        for cp in fetch(te_ref[0], 0):
            cp.start(priority=1)

    @pl.when(i < nt_ref[0])
    def _():
        @pl.when(tf_ref[i] == 1)
        def _():
            slot = par_ref[i]
            for cp in fetch(te_ref[i], slot):
                cp.wait()
            w1_s[...] = w1f[slot].astype(BF16)
            w2_s[...] = w2f[slot].astype(BF16)

            @pl.when(nx_ref[i] >= 0)
            def _():
                for cp in fetch(nx_ref[i], 1 - slot):
                    cp.start(priority=1)

        x = _load_token_major(xs_ref, (), TM_E)
        u = _dot(x.astype(BF16), w1_s[...]) + b1_ref[...]
        g = jnp.minimum(u[:, :D_FF], SWIGLU_LIMIT)
        up = jnp.clip(u[:, D_FF:], -SWIGLU_LIMIT, SWIGLU_LIMIT)
        act = (up + 1.0) * g * jax.nn.sigmoid(SWIGLU_ALPHA * g)
        _store_token_major(ys_ref, (), _dot(act.astype(BF16), w2_s[...]) + b2_ref[...])

    @pl.when(i >= nt_ref[0])
    def _():
        ys_ref[...] = jnp.zeros_like(ys_ref)


def _experts(xs, plan, layer, w1, b1, w2, b2):
    d = D_MODEL
    nt = xs.shape[0] // (TM_E * SUB)
    tile = lambda i, te, tf, ntl, nx, par: (jnp.minimum(i, ntl[0] - 1), 0)
    otile = lambda i, te, tf, ntl, nx, par: (i, 0)
    bmap = lambda i, te, tf, ntl, nx, par: (layer, te[i], 0, 0)
    return pl.pallas_call(
        functools.partial(_expert_kernel, layer=layer),
        grid_spec=pltpu.PrefetchScalarGridSpec(
            num_scalar_prefetch=5, grid=(nt,),
            in_specs=[pl.BlockSpec((TM_E * SUB, LANES), tile),
                      pl.BlockSpec(memory_space=pl.ANY),
                      pl.BlockSpec((None, None, 1, 2 * D_FF), bmap),
                      pl.BlockSpec(memory_space=pl.ANY),
                      pl.BlockSpec((None, None, 1, d), bmap)],
            out_specs=pl.BlockSpec((TM_E * SUB, LANES), otile),
            scratch_shapes=[pltpu.VMEM((2, d, 2 * D_FF), F32), pltpu.VMEM((2, D_FF, d), F32),
                            pltpu.VMEM((d, 2 * D_FF), BF16), pltpu.VMEM((D_FF, d), BF16),
                            pltpu.SemaphoreType.DMA((2,))]),
        out_shape=jax.ShapeDtypeStruct(xs.shape, F32),
        compiler_params=_cparams("arbitrary"), name="moe_experts",
    )(plan["tile_expert"], plan["first"], plan["n_tiles"], plan["tile_next"], plan["tile_parity"],
      xs, w1, b1, w2, b2)


def _combine_kernel(dest_ref, x1_ref, tw_ref, mod_ref, ys_ref, o_ref, buf, sem):
    i = pl.program_id(0)
    j = i - 1
    n = pl.num_programs(0) - 1
    tile_rows = TM * SUB

    for s in range(2):
        @pl.when(jnp.logical_and(i < n, i % 2 == s))
        def _():
            def issue(t, carry):
                dst = pl.multiple_of(t * SUB, SUB)
                for kk in range(TOP_K):
                    row = pl.multiple_of(dest_ref[0, t * TOP_K + kk], SUB)
                    pltpu.make_async_copy(ys_ref.at[pl.ds(row, SUB), :], buf.at[s, kk, pl.ds(dst, SUB), :],
                                          sem.at[s]).start(priority=kk % 2)
                return carry

            lax.fori_loop(0, TM, issue, 0, unroll=2)

    @pl.when(j >= 0)
    def _():
        slot = j % 2
        for kk in range(TOP_K):
            pltpu.make_async_copy(ys_ref.at[pl.ds(0, tile_rows), :], buf.at[slot, kk], sem.at[slot]).wait()
        tw = tw_ref[...]
        y = tw[:, 0:1] * _load_token_major(buf, (slot, 0), TM)
        for kk in range(1, TOP_K):
            y = y + tw[:, kk:kk + 1] * _load_token_major(buf, (slot, kk), TM)
        o_ref[...] = x1_ref[...] + mod_ref[5:6, :] * y


def _combine(ys, dest, x1, tw, mod):
    t, d = x1.shape
    nblk = t // TM
    dest3 = dest.reshape(nblk, 1, TM * TOP_K)
    prev = lambda i: jnp.maximum(i - 1, 0)
    row = lambda w: pl.BlockSpec((TM, w), lambda i: (prev(i), 0))
    return pl.pallas_call(
        _combine_kernel, grid=(nblk + 1,),
        in_specs=[pl.BlockSpec((None, 1, TM * TOP_K), lambda i: (jnp.minimum(i, nblk - 1), 0, 0), memory_space=pltpu.SMEM),
                  row(d), row(LANES),
                  pl.BlockSpec((None, 6, d), lambda i: (_cond_row(prev(i)), 0, 0)),
                  pl.BlockSpec(memory_space=pl.ANY)],
        out_specs=row(d),
        out_shape=jax.ShapeDtypeStruct((t, d), F32),
        scratch_shapes=[pltpu.VMEM((2, TOP_K, TM * SUB, LANES), F32), pltpu.SemaphoreType.DMA((2,))],
        compiler_params=_cparams("arbitrary"), name="moe_combine",
    )(dest3, x1, tw, mod, ys)


def _moe(x1, h2, ti, tw, mod, layer, w1, b1, w2, b2):
    t = x1.shape[0]
    depth = w1.shape[0]
    assert TM == TM_E
    n_tiles_max = t * TOP_K // TM_E + N_EXP
    dest, plan = _route_plan(ti[:, :TOP_K], n_tiles_max)
    xs = _dispatch(h2, dest, plan, n_tiles_max * TM_E)
    ys = _experts(xs, plan, layer, w1, b1.reshape(depth, N_EXP, 1, -1), w2, b2.reshape(depth, N_EXP, 1, -1))
    return _combine(ys, dest, x1, tw, mod)


def _pad_lanes(a, value=0.0):
    return jnp.pad(a, ((0, 0), (0, LANES - a.shape[1])), constant_values=value)


def kernel(x_prompt, x_sample, cache_na_k, cache_na_v, state_mlstm_C, state_mlstm_n, state_mlstm_m, state_ret_S, c, c_ctx, w_mod, b_mod, w_in_even, mlstm_gate_b, na_q_norm, na_k_norm, na_rpb, mlstm_norm, w_out_even, w_in_odd, ret_decay, ret_norm, w_out_odd, w_router, b_router, w_moe_in, b_moe_in, w_moe_out, b_moe_out):
    nb_c, s_c, d = x_prompt.shape
    nb_l, s_l, _ = x_sample.shape
    t_c = nb_c * s_c
    t_l = nb_l * s_l
    assert t_c == 4 * SEG and s_l == SEG and d == D_MODEL
    depth = w_mod.shape[0]
    dt = x_prompt.dtype

    x = jnp.concatenate([x_prompt.reshape(t_c, d), x_sample.reshape(t_l, d)], axis=0)
    cond = jnp.concatenate([c_ctx[None, :], c, jnp.zeros((N_COND - 1 - nb_l, d), F32)], axis=0)
    mod = _modulation(cond, w_mod, b_mod).reshape(depth, N_COND, 6, d)

    outs = {}
    for l in range(depth):
        e = l // 2
        mod_l = mod[l]
        if l % 2 == 0:
            w_in = w_in_even[e]
            n_main = 3 * W_A + 4 * W_B
            wg = _pad_lanes(w_in[:, n_main:])
            bg = _pad_lanes(mlstm_gate_b[e].reshape(1, 4 * H_B))
            z, g = _inproj(x, mod_l, w_in[:, :n_main].astype(BF16), wg, bg)
            qn = na_q_norm[e].reshape(1, HD_A)
            kn = na_k_norm[e].reshape(1, HD_A)
            oa_c, ka_c = _ctx_attention(z, nb_c, s_c, qn, kn)
            past = cache_na_k.shape[2]
            oa_l = _na_attention(z, t_c, nb_l, s_l,
                                 cache_na_k[:, e].reshape(nb_l, past, W_A), cache_na_v[:, e].reshape(nb_l, past, W_A),
                                 _na_bias_table(na_rpb[e]), qn, kn)
            nw = mlstm_norm[e].reshape(1, W_B)
            hm_c, c_fin, n_fin, m_fin = _mlstm(z, g, 0, nb_c, s_c, nw)
            init = (state_mlstm_C[:, e].reshape(nb_l, 2 * H_B, HD_B, HD_B),
                    state_mlstm_n[:, e].reshape(nb_l, 2 * H_B, HD_B),
                    jnp.broadcast_to(state_mlstm_m[:, e].reshape(nb_l, 2 * H_B, 1), (nb_l, 2 * H_B, LANES)))
            hm_l = _mlstm(z, g, t_c, nb_l, s_l, nw, init)[0]
            a = jnp.concatenate([oa_c, oa_l], axis=0)
            b = jnp.concatenate([hm_c, hm_l], axis=0)
            w_out = w_out_even[e].astype(BF16)
            wa, wb = w_out[:W_A], w_out[W_A:]
            outs.setdefault("na_k", []).append(ka_c.reshape(nb_c, s_c, H_A, HD_A))
            outs.setdefault("na_v", []).append(z[:t_c, 2 * W_A:3 * W_A].reshape(nb_c, s_c, H_A, HD_A))
            outs.setdefault("C", []).append(c_fin.reshape(nb_c, 2, H_B, HD_B, HD_B))
            outs.setdefault("n", []).append(n_fin.reshape(nb_c, 2, H_B, HD_B))
            outs.setdefault("m", []).append(m_fin[:, :, 0].reshape(nb_c, 2, H_B))
        else:
            w_c = H_C * HD_C
            z = _inproj(x, mod_l, w_in_odd[e].astype(BF16))
            dl_rep = jnp.broadcast_to(ret_decay[e].reshape(2 * H_C, 1), (2 * H_C, LANES))
            nw = ret_norm[e].reshape(1, w_c)
            hr_c, s_fin = _retention(z, 0, nb_c, s_c, dl_rep, nw)
            hr_l = _retention(z, t_c, nb_l, s_l, dl_rep, nw, rope=_rope_tables(s_l),
                              init=state_ret_S[:, e].reshape(nb_l, 2 * H_C, HD_C, HD_C))[0]
            fd_c = _fnet(z, 0, nb_c, s_c)
            fd_l = _fnet(z, t_c, nb_l, s_l)
            a = jnp.concatenate([hr_c, hr_l], axis=0)
            b = jnp.concatenate([fd_c, fd_l], axis=0)
            w_out = w_out_odd[e].astype(BF16)
            wa, wb = w_out[:w_c], w_out[w_c:]
            outs.setdefault("S", []).append(s_fin.reshape(nb_c, 2, H_C, HD_C, HD_C))
        wr = _pad_lanes(w_router[l])
        br = _pad_lanes(b_router[l].reshape(1, N_EXP), NEG)
        x1, h2, ti, tw = _outproj_router(a, b, x, mod_l, wa, wb, wr, br)
        x = _moe(x1, h2, ti, tw, mod_l, l, w_moe_in, b_moe_in, w_moe_out, b_moe_out)

    y_prompt = x[:t_c].reshape(nb_c, s_c, d)
    y_sample = x[t_c:].reshape(nb_l, s_l, d)
    stack = lambda key: jnp.stack(outs[key], axis=1).astype(dt)
    return (y_prompt, y_sample, stack("na_k"), stack("na_v"), stack("C"), stack("n"), stack("m"), stack("S"))
```

```python
import functools
from typing import NamedTuple

import numpy as np
import jax
import jax.numpy as jnp
from jax import lax
from jax.experimental import pallas as pl
from jax.experimental.pallas import tpu as pltpu

F32 = jnp.float32
BF16 = jnp.bfloat16
HIGHEST = lax.Precision.HIGHEST

D_MODEL = 1024
GRID_W = 64
WIN_R = 8
WIN_C = 16
H_A, HD_A = 8, 64
H_B, HD_B = 4, 128
H_C, HD_C = 4, 128
N_FG, FG_W = 4, 128
W_A = H_A * HD_A
W_B = H_B * HD_B
N_EXP = 32
TOP_K = 4
D_FF = D_MODEL
SWIGLU_LIMIT = 7.0
SWIGLU_ALPHA = 1.702
CHUNK = 128
ROPE_BASE = 10000.0
EPS = 1e-6

LANES = 128
SEG = 1024
N_COND = 8
TM = 256
N_TILES = 8 * SEG // TM
N_CTX_TILES = 4 * SEG // TM
TM_E = 256
NEG = -1e30
VMEM_LIMIT = 56 * 1024 * 1024


def _cparams(*sem):
    return pltpu.CompilerParams(dimension_semantics=sem, vmem_limit_bytes=VMEM_LIMIT)


def _cond_row(i):
    return jnp.maximum((i * TM) // SEG - 3, 0)


def _log_sigmoid(x):
    return jnp.minimum(x, 0.0) - jnp.log1p(jnp.exp(-jnp.abs(x)))


def _dot(a, b):
    return jnp.dot(a, b, preferred_element_type=F32)


def _dot_nt(a, b):
    return lax.dot_general(a, b, (((1,), (1,)), ((), ())), preferred_element_type=F32)


def _dot_hi(a, b):
    return jnp.dot(a, b, precision=HIGHEST, preferred_element_type=F32)


def _split_bf16(x):
    hi = x.astype(BF16)
    return hi, (x - hi.astype(F32)).astype(BF16)


def _dot_x3(x, w_hi, w_lo):
    x_hi, x_lo = _split_bf16(x)
    return _dot(x_hi, w_hi) + (_dot(x_lo, w_hi) + _dot(x_hi, w_lo))


def _mod_kernel(cond_ref, w_ref, b_ref, o_ref):
    c = cond_ref[...]
    s = c * jax.nn.sigmoid(c)
    o_ref[...] = _dot_hi(s, w_ref[...]) + b_ref[...]


def _modulation(cond, w_mod, b_mod):
    depth, d, n = w_mod.shape
    tn = 1536
    return pl.pallas_call(
        _mod_kernel,
        grid=(depth, n // tn),
        in_specs=[pl.BlockSpec((N_COND, d), lambda l, j: (0, 0)),
                  pl.BlockSpec((None, d, tn), lambda l, j: (l, 0, j)),
                  pl.BlockSpec((None, 1, tn), lambda l, j: (l, 0, j))],
        out_specs=pl.BlockSpec((None, N_COND, tn), lambda l, j: (l, 0, j)),
        out_shape=jax.ShapeDtypeStruct((depth, N_COND, n), F32),
        compiler_params=_cparams("arbitrary", "arbitrary"),
        name="modulation",
    )(cond, w_mod, b_mod.reshape(depth, 1, n))


def _rms_mod(x, shift, scale):
    h = x * lax.rsqrt(jnp.mean(x * x, axis=-1, keepdims=True) + EPS)
    return h * (1.0 + scale) + shift


class _Rows(NamedTuple):
    a: jax.Array
    b: jax.Array
    off_b: int

    def specs(self, lag=0):
        width = self.a.shape[1]
        off_b = self.off_b
        tile = (lambda i: i) if lag == 0 else (lambda i: jnp.maximum(i - lag, 0))
        return [pl.BlockSpec((TM, width), lambda i: (jnp.minimum(tile(i), N_CTX_TILES - 1), 0)),
                pl.BlockSpec((TM, width), lambda i: (jnp.maximum(tile(i) - N_CTX_TILES, 0) + off_b, 0))]


def _whole(x):
    return _Rows(x, x, N_CTX_TILES)


def _rows_read(tile, ref_a, ref_b):
    return jnp.where(tile < N_CTX_TILES, ref_a[...], ref_b[...])


def _inproj_kernel(xa_ref, xb_ref, mod_ref, w_ref, z_ref):
    x = _rows_read(pl.program_id(0), xa_ref, xb_ref)
    h = _rms_mod(x, mod_ref[0:1, :], mod_ref[1:2, :])
    z_ref[...] = _dot(h.astype(BF16), w_ref[...])


def _inproj_gate_kernel(xa_ref, xb_ref, mod_ref, w_ref, wgh_ref, wgl_ref, bg_ref, z_ref, g_ref):
    x = _rows_read(pl.program_id(0), xa_ref, xb_ref)
    h = _rms_mod(x, mod_ref[0:1, :], mod_ref[1:2, :])
    z_ref[...] = _dot(h.astype(BF16), w_ref[...])
    g_ref[...] = _dot_x3(h, wgh_ref[...], wgl_ref[...]) + bg_ref[...]


def _inproj(x, mod, w, n, wg=None, bg=None):
    t = N_TILES * TM
    d = x.a.shape[1]
    in_specs = x.specs() + [pl.BlockSpec((None, 6, d), lambda i: (_cond_row(i), 0, 0)),
                            pl.BlockSpec((d, n), lambda i: (0, 0))]
    z_spec = pl.BlockSpec((TM, n), lambda i: (i, 0))
    z_shape = jax.ShapeDtypeStruct((t, n), F32)
    if wg is None:
        return pl.pallas_call(
            _inproj_kernel, grid=(N_TILES,), in_specs=in_specs, out_specs=z_spec, out_shape=z_shape,
            compiler_params=_cparams("arbitrary"), name="inproj",
        )(x.a, x.b, mod, w)
    wg_hi, wg_lo = _split_bf16(wg)
    in_specs += [pl.BlockSpec((d, LANES), lambda i: (0, 0))] * 2 + [pl.BlockSpec((1, LANES), lambda i: (0, 0))]
    return pl.pallas_call(
        _inproj_gate_kernel, grid=(N_TILES,), in_specs=in_specs,
        out_specs=[z_spec, pl.BlockSpec((TM, LANES), lambda i: (i, 0))],
        out_shape=[z_shape, jax.ShapeDtypeStruct((t, LANES), F32)],
        compiler_params=_cparams("arbitrary"), name="inproj_gate",
    )(x.a, x.b, mod, w, wg_hi, wg_lo, bg)


def _head_rms(x, w):
    return x * lax.rsqrt(jnp.mean(x * x, axis=-1, keepdims=True) + EPS) * w


def _fold_lanes(x, op):
    parts = [x[:, c * LANES:(c + 1) * LANES] for c in range(x.shape[1] // LANES)]
    while len(parts) > 1:
        parts = [op(parts[c], parts[c + 1]) if c + 1 < len(parts) else parts[c] for c in range(0, len(parts), 2)]
    return parts[0]


def _ctx_attn_kernel(q_ref, k_ref, v_ref, qn_ref, kn_ref, o_ref, ko_ref):
    scale = HD_A ** -0.5
    for h in range(H_A):
        sl = slice(h * HD_A, (h + 1) * HD_A)
        q = _head_rms(q_ref[:, sl], qn_ref[...]) * scale
        k = _head_rms(k_ref[:, sl], kn_ref[...])
        ko_ref[:, sl] = k
        s = _dot_nt(q.astype(BF16), k.astype(BF16))
        p = jnp.exp(s - jnp.max(_fold_lanes(s, jnp.maximum), axis=-1, keepdims=True))
        den = jnp.sum(_fold_lanes(p, jnp.add), axis=-1, keepdims=True)
        o_ref[:, sl] = _dot(p.astype(BF16), v_ref[:, sl].astype(BF16)) / den


def _ctx_attention(z, n_batch, seq, qn, kn):
    spec = lambda c: pl.BlockSpec((seq, W_A), lambda b: (b, c))
    wspec = pl.BlockSpec((1, HD_A), lambda b: (0, 0))
    out = jax.ShapeDtypeStruct((n_batch * seq, W_A), F32)
    return pl.pallas_call(
        _ctx_attn_kernel, grid=(n_batch,),
        in_specs=[spec(0), spec(1), spec(2), wspec, wspec],
        out_specs=[pl.BlockSpec((seq, W_A), lambda b: (b, 0))] * 2,
        out_shape=[out, out],
        compiler_params=_cparams("arbitrary"), name="ctx_attention",
    )(z, z, z, qn, kn)


def _na_bias_table(rpb):
    qc = np.arange(GRID_W)
    kc = np.arange(GRID_W)
    cstart = np.clip(qc - WIN_C // 2, 0, GRID_W - WIN_C)
    col_in = (kc[None, :] >= cstart[:, None]) & (kc[None, :] < cstart[:, None] + WIN_C)
    dc = np.clip(kc[None, :] - qc[:, None], 1 - WIN_C, WIN_C - 1) + WIN_C - 1
    cls = np.arange(WIN_R)
    j = np.arange(WIN_R)
    dr = j[None, :] - cls[:, None] + WIN_R - 1
    sel_r = jnp.asarray(dr[:, :, None] == np.arange(2 * WIN_R - 1)[None, None, :], F32)
    sel_c = jnp.asarray(dc[:, :, None] == np.arange(2 * WIN_C - 1)[None, None, :], F32)
    tab = jnp.einsum("hab,cja,qkb->hcqjk", rpb, sel_r, sel_c, precision=HIGHEST)
    tab = jnp.where(jnp.asarray(col_in)[None, None, :, None, :], tab, NEG)
    return tab.reshape(H_A, WIN_R, GRID_W, WIN_R * GRID_W)


def _na_kernel(q_ref, k_ref, v_ref, kc_ref, vc_ref, bias_ref, qn_ref, kn_ref, o_ref,
               kn_s, v_s, kc_s, vc_s, *, rows):
    r = pl.program_id(1)
    scale = HD_A ** -0.5

    @pl.when(r == 0)
    def _():
        for h in range(H_A):
            sl = slice(h * HD_A, (h + 1) * HD_A)
            kn_s[:, sl] = _head_rms(k_ref[:, sl], kn_ref[...]).astype(BF16)
        v_s[...] = v_ref[...].astype(BF16)
        kc_s[...] = kc_ref[...].astype(BF16)
        vc_s[...] = vc_ref[...].astype(BF16)

    rs = jnp.clip(r - WIN_R // 2, 0, rows - WIN_R)
    start = pl.multiple_of(rs * GRID_W, GRID_W)
    n_loc = WIN_R * GRID_W
    for h in range(H_A):
        sl = slice(h * HD_A, (h + 1) * HD_A)
        q = (_head_rms(q_ref[:, sl], qn_ref[...]) * scale).astype(BF16)
        s_loc = _dot_nt(q, kn_s[pl.ds(start, n_loc), sl]) + bias_ref[h]
        s_ctx = _dot_nt(q, kc_s[:, sl])
        m = jnp.max(jnp.maximum(_fold_lanes(s_loc, jnp.maximum), _fold_lanes(s_ctx, jnp.maximum)),
                    axis=-1, keepdims=True)
        p_loc = jnp.exp(s_loc - m)
        p_ctx = jnp.exp(s_ctx - m)
        den = jnp.sum(_fold_lanes(p_loc, jnp.add) + _fold_lanes(p_ctx, jnp.add), axis=-1, keepdims=True)
        o = _dot(p_loc.astype(BF16), v_s[pl.ds(start, n_loc), sl]) + _dot(p_ctx.astype(BF16), vc_s[:, sl])
        o_ref[:, sl] = o / den


def _na_attention(z, row0, n_batch, seq, kc, vc, bias, qn, kn):
    rows = seq // GRID_W
    past = kc.shape[1]
    blk0 = row0 // GRID_W
    sblk0 = row0 // seq

    def cls_of(r):
        return r - jnp.clip(r - WIN_R // 2, 0, rows - WIN_R)

    full = lambda c: pl.BlockSpec((seq, W_A), lambda b, r: (sblk0 + b, c))
    cspec = pl.BlockSpec((None, past, W_A), lambda b, r: (b, 0, 0))
    wspec = pl.BlockSpec((1, HD_A), lambda b, r: (0, 0))
    return pl.pallas_call(
        functools.partial(_na_kernel, rows=rows), grid=(n_batch, rows),
        in_specs=[pl.BlockSpec((GRID_W, W_A), lambda b, r: (blk0 + b * rows + r, 0)),
                  full(1), full(2), cspec, cspec,
                  pl.BlockSpec((H_A, None, GRID_W, WIN_R * GRID_W), lambda b, r: (0, cls_of(r), 0, 0)),
                  wspec, wspec],
        out_specs=pl.BlockSpec((GRID_W, W_A), lambda b, r: (b * rows + r, 0)),
        out_shape=jax.ShapeDtypeStruct((n_batch * seq, W_A), F32),
        scratch_shapes=[pltpu.VMEM((seq, W_A), BF16), pltpu.VMEM((seq, W_A), BF16),
                        pltpu.VMEM((past, W_A), BF16), pltpu.VMEM((past, W_A), BF16)],
        compiler_params=_cparams("arbitrary", "arbitrary"), name="na_attention",
    )(z, z, z, kc, vc, bias, qn, kn)


def _tri_masks():
    li = lax.broadcasted_iota(jnp.int32, (CHUNK, CHUNK), 0)
    si = lax.broadcasted_iota(jnp.int32, (CHUNK, CHUNK), 1)
    return li >= si, li <= si


def _mlstm_kernel(*refs, nc, has_init):
    if has_init:
        (q_ref, k_ref, v_ref, og_ref, g_ref, nw_ref, c0_ref, n0_ref, m0_ref,
         o_ref, cf_ref, nf_ref, mf_ref, h_s, c_s, n_s, m_s) = refs
    else:
        (q_ref, k_ref, v_ref, og_ref, g_ref, nw_ref,
         o_ref, cf_ref, nf_ref, mf_ref, h_s, c_s, n_s, m_s) = refs
    nd = 2 * H_B
    if has_init:
        c_s[...] = c0_ref[...]
        n_s[...] = n0_ref[...]
        m_s[...] = m0_ref[...]
    else:
        c_s[...] = jnp.zeros_like(c_s)
        n_s[...] = jnp.zeros_like(n_s)
        m_s[...] = jnp.zeros_like(m_s)

    causal, anti = _tri_masks()
    tri_f = causal.astype(F32)
    tri_b = anti.astype(F32)
    kscale = HD_B ** -0.5

    def chunk_step(c, carry):
        for d in range(2):
            cc = c if d == 0 else nc - 1 - c
            t0 = pl.multiple_of(cc * CHUNK, CHUNK)
            g = g_ref[pl.ds(t0, CHUNK), :]
            gt = g.T
            ls = _log_sigmoid(g)
            lst = _log_sigmoid(gt)
            tri_c, tri_r, mask = (tri_f, tri_b, causal) if d == 0 else (tri_b, tri_f, anti)
            b_cols = _dot_hi(tri_c, ls)
            b_rows = _dot_hi(lst, tri_r)
            last = CHUNK - 1 if d == 0 else 0
            for h in range(H_B):
                ci = (2 * d) * H_B + h
                cf = (2 * d + 1) * H_B + h
                hs = slice(h * HD_B, (h + 1) * HD_B)
                q = q_ref[pl.ds(t0, CHUNK), hs]
                k = k_ref[pl.ds(t0, CHUNK), hs] * kscale
                v = v_ref[pl.ds(t0, CHUNK), hs]
                qb, kb, vb = q.astype(BF16), k.astype(BF16), v.astype(BF16)
                b_col = b_cols[:, cf:cf + 1]
                b_row = b_rows[cf:cf + 1, :]
                i_row = gt[ci:ci + 1, :]
                sidx = d * H_B + h
                cst = c_s[sidx]
                nst = n_s[sidx:sidx + 1, :]
                mst = m_s[sidx:sidx + 1, 0:1]
                dmat = jnp.where(mask, b_col - b_row + i_row, -jnp.inf)
                inter = b_col + mst
                mt = jnp.maximum(inter, jnp.max(dmat, axis=-1, keepdims=True))
                w = jnp.exp(dmat - mt) * _dot_nt(qb, kb)
                a = jnp.exp(inter - mt)
                num = _dot(w.astype(BF16), vb) + _dot(qb, cst.astype(BF16)) * a
                den = jnp.sum(w, axis=-1, keepdims=True) + a * jnp.sum(q * nst, axis=-1, keepdims=True)
                hc = num / jnp.maximum(jnp.abs(den), jnp.exp(-mt))
                h_s[d, pl.ds(t0, CHUNK), hs] = hc
                bl = b_row[:, last:last + 1]
                dl = bl - b_row + i_row
                m_new = jnp.maximum(bl + mst, jnp.max(dl, axis=-1, keepdims=True))
                wl = jnp.exp(dl - m_new)
                dec = jnp.exp(bl + mst - m_new)
                kw = (k.T * wl).astype(BF16)
                c_s[sidx] = dec * cst + _dot(kw, vb)
                wl8 = jnp.broadcast_to(wl, (8, CHUNK)).astype(BF16)
                n_s[sidx:sidx + 1, :] = dec * nst + _dot(wl8, kb)[0:1, :]
                m_s[sidx:sidx + 1, :] = jnp.broadcast_to(m_new, (1, LANES))
        return carry

    lax.fori_loop(0, nc, chunk_step, 0)

    for h in range(H_B):
        hs = slice(h * HD_B, (h + 1) * HD_B)
        hsum = h_s[0, :, hs] + h_s[1, :, hs]
        o_ref[:, hs] = _head_rms(hsum, nw_ref[:, hs]) * jax.nn.sigmoid(og_ref[:, hs])
    cf_ref[...] = c_s[...]
    nf_ref[...] = n_s[...]
    mf_ref[...] = m_s[...]


def _mlstm(z, g, row0, n_batch, seq, norm_w, init=None):
    sblk0 = row0 // seq
    nd = 2 * H_B
    spec = lambda c: pl.BlockSpec((seq, W_B), lambda b: (sblk0 + b, c))
    in_specs = [spec(3), spec(4), spec(5), spec(6),
                pl.BlockSpec((seq, LANES), lambda b: (sblk0 + b, 0)),
                pl.BlockSpec((1, W_B), lambda b: (0, 0))]
    args = [z, z, z, z, g, norm_w]
    st_specs = [pl.BlockSpec((None, nd, HD_B, HD_B), lambda b: (b, 0, 0, 0)),
                pl.BlockSpec((None, nd, HD_B), lambda b: (b, 0, 0)),
                pl.BlockSpec((None, nd, LANES), lambda b: (b, 0, 0))]
    if init is not None:
        in_specs += st_specs
        args += list(init)
    return pl.pallas_call(
        functools.partial(_mlstm_kernel, nc=seq // CHUNK, has_init=init is not None), grid=(n_batch,),
        in_specs=in_specs,
        out_specs=[pl.BlockSpec((seq, W_B), lambda b: (b, 0))] + st_specs,
        out_shape=[jax.ShapeDtypeStruct((n_batch * seq, W_B), F32),
                   jax.ShapeDtypeStruct((n_batch, nd, HD_B, HD_B), F32),
                   jax.ShapeDtypeStruct((n_batch, nd, HD_B), F32),
                   jax.ShapeDtypeStruct((n_batch, nd, LANES), F32)],
        scratch_shapes=[pltpu.VMEM((2, seq, W_B), F32), pltpu.VMEM((nd, HD_B, HD_B), F32),
                        pltpu.VMEM((nd, HD_B), F32), pltpu.VMEM((nd, LANES), F32)],
        compiler_params=_cparams("arbitrary"), name="mlstm",
    )(*args)


def _rope_tables(seq):
    half = HD_C // 2
    quarter = half // 2
    t = np.arange(seq)
    inv = ROPE_BASE ** (-np.arange(0, half, 2, dtype=np.float64) / half)
    ang_r = (t // GRID_W)[:, None] * inv[None, :]
    ang_c = (t % GRID_W)[:, None] * inv[None, :]
    cos_t = np.concatenate([np.cos(ang_r), np.cos(ang_r), np.cos(ang_c), np.cos(ang_c)], -1)
    sin_t = np.concatenate([-np.sin(ang_r), np.sin(ang_r), -np.sin(ang_c), np.sin(ang_c)], -1)
    assert cos_t.shape == (seq, 4 * quarter)
    return jnp.asarray(cos_t, F32), jnp.asarray(sin_t, F32)


def _rope(x, cos_t, sin_t):
    quarter = HD_C // 4
    lane = lax.broadcasted_iota(jnp.int32, x.shape, 1)
    first = (lane % (2 * quarter)) < quarter
    swapped = jnp.where(first, pltpu.roll(x, HD_C - quarter, 1), pltpu.roll(x, quarter, 1))
    return x * cos_t + swapped * sin_t


def _ret_kernel(*refs, nc, has_init, use_rope):
    refs = list(refs)
    q_ref, k_ref, v_ref, gg_ref, dl_ref, nw_ref = refs[:6]
    pos = 6
    if use_rope:
        cos_ref, sin_ref = refs[pos:pos + 2]
        pos += 2
    if has_init:
        s0_ref = refs[pos]
        pos += 1
    o_ref, sf_ref, h_s, s_s = refs[pos:pos + 4]
    if has_init:
        s_s[...] = s0_ref[...]
    else:
        s_s[...] = jnp.zeros_like(s_s)

    causal, anti = _tri_masks()
    li = lax.broadcasted_iota(jnp.int32, (CHUNK, CHUNK), 0).astype(F32)
    si = lax.broadcasted_iota(jnp.int32, (CHUNK, CHUNK), 1).astype(F32)
    lg_all = _log_sigmoid(dl_ref[...])
    kscale = HD_C ** -0.5

    def chunk_step(c, carry):
        for d in range(2):
            cc = c if d == 0 else nc - 1 - c
            t0 = pl.multiple_of(cc * CHUNK, CHUNK)
            for h in range(H_C):
                sidx = d * H_C + h
                hs = slice(h * HD_C, (h + 1) * HD_C)
                lg = lg_all[sidx:sidx + 1, :]
                q = q_ref[pl.ds(t0, CHUNK), hs]
                k = k_ref[pl.ds(t0, CHUNK), hs] * kscale
                v = v_ref[pl.ds(t0, CHUNK), hs]
                if use_rope:
                    cos_t = cos_ref[pl.ds(t0, CHUNK), :]
                    sin_t = sin_ref[pl.ds(t0, CHUNK), :]
                    q = _rope(q, cos_t, sin_t)
                    k = _rope(k, cos_t, sin_t)
                if d == 0:
                    decay = jnp.exp(jnp.where(causal, (li - si) * lg, -jnp.inf))
                    q_dec = jnp.exp((li + 1.0) * lg)
                    k_dec = jnp.exp((CHUNK - 1.0 - li) * lg)
                else:
                    decay = jnp.exp(jnp.where(anti, (si - li) * lg, -jnp.inf))
                    q_dec = jnp.exp((CHUNK - li) * lg)
                    k_dec = jnp.exp(li * lg)
                c_dec = jnp.exp(CHUNK * lg)
                st = s_s[sidx]
                qb, kb, vb = q.astype(BF16), k.astype(BF16), v.astype(BF16)
                att = _dot_nt(qb, kb) * decay
                o = _dot(att.astype(BF16), vb) + _dot(qb, st.astype(BF16)) * q_dec
                h_s[d, pl.ds(t0, CHUNK), hs] = o
                kd = (k * k_dec).T.astype(BF16)
                s_s[sidx] = c_dec * st + _dot(kd, vb)
        return carry

    lax.fori_loop(0, nc, chunk_step, 0)

    for h in range(H_C):
        hs = slice(h * HD_C, (h + 1) * HD_C)
        osum = h_s[0, :, hs] + h_s[1, :, hs]
        gg = gg_ref[:, hs]
        o_ref[:, hs] = _head_rms(osum, nw_ref[:, hs]) * (gg * jax.nn.sigmoid(gg))
    sf_ref[...] = s_s[...]


def _retention(z, row0, n_batch, seq, decay_rep, norm_w, rope=None, init=None):
    sblk0 = row0 // seq
    nd = 2 * H_C
    w_c = H_C * HD_C
    spec = lambda c: pl.BlockSpec((seq, w_c), lambda b: (sblk0 + b, c))
    in_specs = [spec(0), spec(1), spec(2), spec(3),
                pl.BlockSpec((nd, LANES), lambda b: (0, 0)),
                pl.BlockSpec((1, w_c), lambda b: (0, 0))]
    args = [z, z, z, z, decay_rep, norm_w]
    if rope is not None:
        in_specs += [pl.BlockSpec((seq, HD_C), lambda b: (0, 0))] * 2
        args += list(rope)
    st_spec = pl.BlockSpec((None, nd, HD_C, HD_C), lambda b: (b, 0, 0, 0))
    if init is not None:
        in_specs.append(st_spec)
        args.append(init)
    return pl.pallas_call(
        functools.partial(_ret_kernel, nc=seq // CHUNK, has_init=init is not None, use_rope=rope is not None),
        grid=(n_batch,), in_specs=in_specs,
        out_specs=[pl.BlockSpec((seq, w_c), lambda b: (b, 0)), st_spec],
        out_shape=[jax.ShapeDtypeStruct((n_batch * seq, w_c), F32),
                   jax.ShapeDtypeStruct((n_batch, nd, HD_C, HD_C), F32)],
        scratch_shapes=[pltpu.VMEM((2, seq, w_c), F32), pltpu.VMEM((nd, HD_C, HD_C), F32)],
        compiler_params=_cparams("arbitrary"), name="retention",
    )(*args)


def _dft_tables(n):
    idx = (np.arange(n)[:, None] * np.arange(n)[None, :]) % n
    ang = 2.0 * np.pi * idx / n
    return np.cos(ang) / np.sqrt(n), np.sin(ang) / np.sqrt(n)


def _fnet_kernel(x_ref, cw_ref, sw_ref, cs_ref, ss_ref, o_ref):
    for g in range(N_FG):
        gs = slice(g * FG_W, (g + 1) * FG_W)
        x = x_ref[:, gs].astype(BF16)
        xc = _dot(x, cw_ref[...]).astype(BF16)
        xs = _dot(x, sw_ref[...]).astype(BF16)
        o_ref[:, gs] = _dot(cs_ref[...], xc) - _dot(ss_ref[...], xs)


def _fnet(z, row0, n_batch, seq):
    sblk0 = row0 // seq
    w_d = N_FG * FG_W
    cw, sw = _dft_tables(FG_W)
    cs, ss = _dft_tables(seq)
    tabs = [jnp.asarray(a, F32).astype(BF16) for a in (cw, sw, cs, ss)]
    wspec = pl.BlockSpec((FG_W, FG_W), lambda b: (0, 0))
    sspec = pl.BlockSpec((seq, seq), lambda b: (0, 0))
    return pl.pallas_call(
        _fnet_kernel, grid=(n_batch,),
        in_specs=[pl.BlockSpec((seq, w_d), lambda b: (sblk0 + b, 4)), wspec, wspec, sspec, sspec],
        out_specs=pl.BlockSpec((seq, w_d), lambda b: (b, 0)),
        out_shape=jax.ShapeDtypeStruct((n_batch * seq, w_d), F32),
        compiler_params=_cparams("arbitrary"), name="fnet",
    )(z, *tabs)


SUB = 8
assert D_MODEL == SUB * LANES


def _store_token_major(ref, lead, val):
    n = val.shape[0]
    for c in range(SUB):
        ref[lead + (pl.ds(c, n, stride=SUB), slice(None))] = val[:, c * LANES:(c + 1) * LANES]


def _load_token_major(ref, lead, n):
    return jnp.concatenate([ref[lead + (pl.ds(c, n, stride=SUB), slice(None))] for c in range(SUB)], axis=1)


def _outproj_router_kernel(aa_ref, ab_ref, ba_ref, bb_ref, xa_ref, xb_ref, mod_ref, wa_ref, wb_ref,
                           wrh_ref, wrl_ref, br_ref, x1_ref, h2_ref, ti_ref, tw_ref):
    i = pl.program_id(0)
    a = _rows_read(i, aa_ref, ab_ref)
    b = _rows_read(i, ba_ref, bb_ref)
    y = _dot(a.astype(BF16), wa_ref[...]) + _dot(b.astype(BF16), wb_ref[...])
    x1 = _rows_read(i, xa_ref, xb_ref) + mod_ref[2:3, :] * y
    x1_ref[...] = x1
    h2 = _rms_mod(x1, mod_ref[3:4, :], mod_ref[4:5, :])
    _store_token_major(h2_ref, (), h2)
    logits = _dot_x3(h2, wrh_ref[...], wrl_ref[...]) + br_ref[...]
    lane = lax.broadcasted_iota(jnp.int32, logits.shape, 1)
    lane_f = lane.astype(F32)
    cur = logits
    vals = []
    ti = jnp.zeros(logits.shape, jnp.int32)
    for kk in range(TOP_K):
        mx = jnp.max(cur, axis=-1, keepdims=True)
        idx = jnp.min(jnp.where(cur == mx, lane_f, float(LANES)), axis=-1, keepdims=True)
        ti = jnp.where(lane == kk, idx.astype(jnp.int32), ti)
        cur = jnp.where(lane_f == idx, -jnp.inf, cur)
        vals.append(mx)
    es = [jnp.exp(v - vals[0]) for v in vals]
    tot = es[0] + es[1] + es[2] + es[3]
    tw = jnp.zeros(logits.shape, F32)
    for kk in range(TOP_K):
        tw = jnp.where(lane == kk, es[kk] / tot, tw)
    ti_ref[...] = ti
    tw_ref[...] = tw


def _outproj_router(a, b, x, mod, w_out, wr, br):
    t = N_TILES * TM
    d = x.a.shape[1]
    wid = a.a.shape[1]
    row = lambda w: pl.BlockSpec((TM, w), lambda i: (i, 0))
    const = lambda r, c: pl.BlockSpec((r, c), lambda i: (0, 0))
    wr_hi, wr_lo = _split_bf16(wr)
    return pl.pallas_call(
        _outproj_router_kernel, grid=(N_TILES,),
        in_specs=a.specs() + b.specs() + x.specs() + [
                  pl.BlockSpec((None, 6, d), lambda i: (_cond_row(i), 0, 0)),
                  pl.BlockSpec((wid, d), lambda i: (0, 0)), pl.BlockSpec((wid, d), lambda i: (1, 0)),
                  const(d, LANES), const(d, LANES), const(1, LANES)],
        out_specs=[row(d), pl.BlockSpec((TM * SUB, LANES), lambda i: (i, 0)), row(LANES), row(LANES)],
        out_shape=[jax.ShapeDtypeStruct((t, d), F32), jax.ShapeDtypeStruct((t * SUB, LANES), F32),
                   jax.ShapeDtypeStruct((t, LANES), jnp.int32), jax.ShapeDtypeStruct((t, LANES), F32)],
        compiler_params=_cparams("arbitrary"), name="outproj_router",
    )(a.a, a.b, b.a, b.b, x.a, x.b, mod, w_out, w_out, wr_hi, wr_lo, br)


def _route_plan(top_i, n_tiles_max):
    onehot = (top_i[:, :, None] == jnp.arange(N_EXP, dtype=jnp.int32)[None, None, :]).astype(jnp.int32)
    sel = onehot.sum(axis=1)
    before = jnp.cumsum(sel, axis=0) - sel
    counts = before[-1] + sel[-1]
    tiles = (counts + TM_E - 1) // TM_E
    tile_end = jnp.cumsum(tiles)
    tile_start = tile_end - tiles
    dest = jnp.take_along_axis(tile_start[None, :] * TM_E + before, top_i, axis=1)
    n_tiles = tile_end[-1]
    tid = jnp.minimum(jnp.arange(n_tiles_max, dtype=jnp.int32), n_tiles - 1)
    tile_expert = jnp.sum((tile_end[None, :] <= tid[:, None]).astype(jnp.int32), axis=1)
    first = jnp.concatenate([jnp.ones((1,), jnp.int32), (tile_expert[1:] != tile_expert[:-1]).astype(jnp.int32)])
    used = tiles > 0
    ids = jnp.arange(N_EXP, dtype=jnp.int32)
    later = jnp.where(used[None, :] & (ids[None, :] > ids[:, None]), ids[None, :], N_EXP)
    next_used = jnp.min(later, axis=1)
    next_used = jnp.where(next_used < N_EXP, next_used, -1).astype(jnp.int32)
    parity = ((jnp.cumsum(used.astype(jnp.int32)) - 1) % 2).astype(jnp.int32)
    last_tile = jnp.where(used, tile_end - 1, 0).astype(jnp.int32)
    plan = dict(tile_expert=tile_expert, first=first, n_tiles=n_tiles.reshape(1).astype(jnp.int32),
                tile_next=next_used[tile_expert], tile_parity=parity[tile_expert], last_tile=last_tile)
    return (dest * SUB).astype(jnp.int32), plan


DISPATCH_BLK = 1024


def _dispatch_kernel(lt_ref, nt_ref, dest_ref, h_ref, xs_ref, h_s, zero_s, sem, hsem):
    i = pl.program_id(0)
    tile_rows = TM_E * SUB
    n_tiles_max = xs_ref.shape[0] // tile_rows
    n_tok = h_s.shape[0] // SUB

    @pl.when(i == 0)
    def _():
        stage = pltpu.make_async_copy(h_ref, h_s, hsem)
        stage.start()
        zero_s[...] = jnp.zeros_like(zero_s)

        def zero_tile(tile):
            r0 = pl.multiple_of(tile * tile_rows, tile_rows)
            return pltpu.make_async_copy(zero_s, xs_ref.at[pl.ds(r0, tile_rows), :], sem)

        def start_unused(j, carry):
            zero_tile(j).start()
            return carry

        def wait_unused(j, carry):
            zero_tile(j).wait()
            return carry

        for e in range(N_EXP):
            zero_tile(lt_ref[e]).start()
        lax.fori_loop(nt_ref[0], n_tiles_max, start_unused, 0)
        for e in range(N_EXP):
            zero_tile(lt_ref[e]).wait()
        lax.fori_loop(nt_ref[0], n_tiles_max, wait_unused, 0)
        stage.wait()

    base = i * DISPATCH_BLK

    def issue(t, carry):
        src = pl.multiple_of((base + t) * SUB, SUB)
        for kk in range(TOP_K):
            row = pl.multiple_of(dest_ref[0, t * TOP_K + kk], SUB)
            pltpu.make_async_copy(h_s.at[pl.ds(src, SUB), :], xs_ref.at[pl.ds(row, SUB), :], sem).start(priority=kk % 2)
        return carry

    lax.fori_loop(0, DISPATCH_BLK, issue, 0, unroll=2)

    @pl.when(i == pl.num_programs(0) - 1)
    def _():
        for kk in range(TOP_K):
            pltpu.make_async_copy(h_s, xs_ref.at[pl.ds(0, n_tok * SUB), :], sem).wait()


def _dispatch(h2, dest, plan, n_rows):
    t = h2.shape[0] // SUB
    nblk = t // DISPATCH_BLK
    dest3 = dest.reshape(nblk, 1, DISPATCH_BLK * TOP_K)
    return pl.pallas_call(
        _dispatch_kernel,
        grid_spec=pltpu.PrefetchScalarGridSpec(
            num_scalar_prefetch=2, grid=(nblk,),
            in_specs=[pl.BlockSpec((None, 1, DISPATCH_BLK * TOP_K), lambda i, lt, nt: (i, 0, 0), memory_space=pltpu.SMEM),
                      pl.BlockSpec(memory_space=pl.ANY)],
            out_specs=pl.BlockSpec(memory_space=pl.ANY),
            scratch_shapes=[pltpu.VMEM((t * SUB, LANES), F32), pltpu.VMEM((TM_E * SUB, LANES), F32),
                            pltpu.SemaphoreType.DMA(()), pltpu.SemaphoreType.DMA(())]),
        out_shape=jax.ShapeDtypeStruct((n_rows * SUB, LANES), F32),
        compiler_params=_cparams("arbitrary"), name="moe_dispatch",
    )(plan["last_tile"], plan["n_tiles"], dest3, h2)


def _expert_kernel(te_ref, tf_ref, nt_ref, nx_ref, par_ref, xs_ref, w1_ref, b1_ref, w2_ref, b2_ref, ys_ref,
                   w1f, w2f, w1_s, w2_s, wsem, *, layer):
    i = pl.program_id(0)

    def fetch(expert, slot):
        return (pltpu.make_async_copy(w1_ref.at[layer, expert], w1f.at[slot], wsem.at[slot]),
                pltpu.make_async_copy(w2_ref.at[layer, expert], w2f.at[slot], wsem.at[slot]))

    @pl.when(i == 0)
    def _():
        for cp in fetch(te_ref[0], 0):
            cp.start(priority=1)

    @pl.when(i < nt_ref[0])
    def _():
        @pl.when(tf_ref[i] == 1)
        def _():
            slot = par_ref[i]
            for cp in fetch(te_ref[i], slot):
                cp.wait()
            w1_s[...] = w1f[slot].astype(BF16)
            w2_s[...] = w2f[slot].astype(BF16)

            @pl.when(nx_ref[i] >= 0)
            def _():
                for cp in fetch(nx_ref[i], 1 - slot):
                    cp.start(priority=1)

        x = _load_token_major(xs_ref, (), TM_E)
        u = _dot(x.astype(BF16), w1_s[...]) + b1_ref[...]
        g = jnp.minimum(u[:, :D_FF], SWIGLU_LIMIT)
        up = jnp.clip(u[:, D_FF:], -SWIGLU_LIMIT, SWIGLU_LIMIT)
        act = (up + 1.0) * g * jax.nn.sigmoid(SWIGLU_ALPHA * g)
        _store_token_major(ys_ref, (), _dot(act.astype(BF16), w2_s[...]) + b2_ref[...])

    @pl.when(i >= nt_ref[0])
    def _():
        ys_ref[...] = jnp.zeros_like(ys_ref)


def _experts(xs, plan, layer, w1, b1, w2, b2):
    d = D_MODEL
    nt = xs.shape[0] // (TM_E * SUB)
    tile = lambda i, te, tf, ntl, nx, par: (jnp.minimum(i, ntl[0] - 1), 0)
    otile = lambda i, te, tf, ntl, nx, par: (i, 0)
    bmap = lambda i, te, tf, ntl, nx, par: (layer, te[i], 0, 0)
    return pl.pallas_call(
        functools.partial(_expert_kernel, layer=layer),
        grid_spec=pltpu.PrefetchScalarGridSpec(
            num_scalar_prefetch=5, grid=(nt,),
            in_specs=[pl.BlockSpec((TM_E * SUB, LANES), tile),
                      pl.BlockSpec(memory_space=pl.ANY),
                      pl.BlockSpec((None, None, 1, 2 * D_FF), bmap),
                      pl.BlockSpec(memory_space=pl.ANY),
                      pl.BlockSpec((None, None, 1, d), bmap)],
            out_specs=pl.BlockSpec((TM_E * SUB, LANES), otile),
            scratch_shapes=[pltpu.VMEM((2, d, 2 * D_FF), F32), pltpu.VMEM((2, D_FF, d), F32),
                            pltpu.VMEM((d, 2 * D_FF), BF16), pltpu.VMEM((D_FF, d), BF16),
                            pltpu.SemaphoreType.DMA((2,))]),
        out_shape=jax.ShapeDtypeStruct(xs.shape, F32),
        compiler_params=_cparams("arbitrary"), name="moe_experts",
    )(plan["tile_expert"], plan["first"], plan["n_tiles"], plan["tile_next"], plan["tile_parity"],
      xs, w1, b1, w2, b2)


def _combine_kernel(dest_ref, x1_ref, tw_ref, mod_ref, ys_ref, *rest):
    *o_refs, buf, sem = rest
    i = pl.program_id(0)
    j = i - 1
    n = pl.num_programs(0) - 1
    tile_rows = TM * SUB

    for s in range(2):
        @pl.when(jnp.logical_and(i < n, i % 2 == s))
        def _():
            def issue(t, carry):
                dst = pl.multiple_of(t * SUB, SUB)
                for kk in range(TOP_K):
                    row = pl.multiple_of(dest_ref[0, t * TOP_K + kk], SUB)
                    pltpu.make_async_copy(ys_ref.at[pl.ds(row, SUB), :], buf.at[s, kk, pl.ds(dst, SUB), :],
                                          sem.at[s]).start(priority=kk % 2)
                return carry

            lax.fori_loop(0, TM, issue, 0, unroll=2)

    @pl.when(j >= 0)
    def _():
        slot = j % 2
        for kk in range(TOP_K):
            pltpu.make_async_copy(ys_ref.at[pl.ds(0, tile_rows), :], buf.at[slot, kk], sem.at[slot]).wait()
        tw = tw_ref[...]
        y = tw[:, 0:1] * _load_token_major(buf, (slot, 0), TM)
        for kk in range(1, TOP_K):
            y = y + tw[:, kk:kk + 1] * _load_token_major(buf, (slot, kk), TM)
        out = x1_ref[...] + mod_ref[5:6, :] * y
        if len(o_refs) == 1:
            o_refs[0][...] = out
        else:
            @pl.when(j < N_CTX_TILES)
            def _():
                o_refs[0][...] = out

            @pl.when(j >= N_CTX_TILES)
            def _():
                o_refs[1][...] = out


def _combine(ys, dest, x1, tw, mod, split):
    t, d = x1.shape
    nblk = t // TM
    dest3 = dest.reshape(nblk, 1, TM * TOP_K)
    prev = lambda i: jnp.maximum(i - 1, 0)
    row = lambda w: pl.BlockSpec((TM, w), lambda i: (prev(i), 0))
    return pl.pallas_call(
        _combine_kernel, grid=(nblk + 1,),
        in_specs=[pl.BlockSpec((None, 1, TM * TOP_K), lambda i: (jnp.minimum(i, nblk - 1), 0, 0), memory_space=pltpu.SMEM),
                  row(d), row(LANES),
                  pl.BlockSpec((None, 6, d), lambda i: (_cond_row(prev(i)), 0, 0)),
                  pl.BlockSpec(memory_space=pl.ANY)],
        out_specs=_Rows(x1, x1, 0).specs(lag=1) if split else row(d),
        out_shape=([jax.ShapeDtypeStruct((N_CTX_TILES * TM, d), F32),
                    jax.ShapeDtypeStruct(((N_TILES - N_CTX_TILES) * TM, d), F32)] if split
                   else jax.ShapeDtypeStruct((t, d), F32)),
        scratch_shapes=[pltpu.VMEM((2, TOP_K, TM * SUB, LANES), F32), pltpu.SemaphoreType.DMA((2,))],
        compiler_params=_cparams("arbitrary"), name="moe_combine",
    )(dest3, x1, tw, mod, ys)


def _moe(x1, h2, ti, tw, mod, layer, w1, b1, w2, b2, split):
    t = x1.shape[0]
    depth = w1.shape[0]
    assert TM == TM_E
    n_tiles_max = t * TOP_K // TM_E + N_EXP
    dest, plan = _route_plan(ti[:, :TOP_K], n_tiles_max)
    xs = _dispatch(h2, dest, plan, n_tiles_max * TM_E)
    ys = _experts(xs, plan, layer, w1, b1.reshape(depth, N_EXP, 1, -1), w2, b2.reshape(depth, N_EXP, 1, -1))
    return _combine(ys, dest, x1, tw, mod, split)


def _pad_lanes(a, value=0.0):
    return jnp.pad(a, ((0, 0), (0, LANES - a.shape[1])), constant_values=value)


def kernel(x_prompt, x_sample, cache_na_k, cache_na_v, state_mlstm_C, state_mlstm_n, state_mlstm_m, state_ret_S, c, c_ctx, w_mod, b_mod, w_in_even, mlstm_gate_b, na_q_norm, na_k_norm, na_rpb, mlstm_norm, w_out_even, w_in_odd, ret_decay, ret_norm, w_out_odd, w_router, b_router, w_moe_in, b_moe_in, w_moe_out, b_moe_out):
    nb_c, s_c, d = x_prompt.shape
    nb_l, s_l, _ = x_sample.shape
    t_c = nb_c * s_c
    t_l = nb_l * s_l
    assert t_c == 4 * SEG and s_l == SEG and d == D_MODEL
    depth = w_mod.shape[0]
    dt = x_prompt.dtype

    x = _Rows(x_prompt.reshape(t_c, d), x_sample.reshape(t_l, d), 0)
    cond = jnp.concatenate([c_ctx[None, :], c, jnp.zeros((N_COND - 1 - nb_l, d), F32)], axis=0)
    mod = _modulation(cond, w_mod, b_mod).reshape(depth, N_COND, 6, d)

    outs = {}
    for l in range(depth):
        e = l // 2
        mod_l = mod[l]
        if l % 2 == 0:
            w_in = w_in_even[e]
            n_main = 3 * W_A + 4 * W_B
            wg = _pad_lanes(w_in[:, n_main:])
            bg = _pad_lanes(mlstm_gate_b[e].reshape(1, 4 * H_B))
            z, g = _inproj(x, mod_l, w_in.astype(BF16), n_main, wg, bg)
            qn = na_q_norm[e].reshape(1, HD_A)
            kn = na_k_norm[e].reshape(1, HD_A)
            oa_c, ka_c = _ctx_attention(z, nb_c, s_c, qn, kn)
            past = cache_na_k.shape[2]
            oa_l = _na_attention(z, t_c, nb_l, s_l,
                                 cache_na_k[:, e].reshape(nb_l, past, W_A), cache_na_v[:, e].reshape(nb_l, past, W_A),
                                 _na_bias_table(na_rpb[e]), qn, kn)
            nw = mlstm_norm[e].reshape(1, W_B)
            hm_c, c_fin, n_fin, m_fin = _mlstm(z, g, 0, nb_c, s_c, nw)
            init = (state_mlstm_C[:, e].reshape(nb_l, 2 * H_B, HD_B, HD_B),
                    state_mlstm_n[:, e].reshape(nb_l, 2 * H_B, HD_B),
                    jnp.broadcast_to(state_mlstm_m[:, e].reshape(nb_l, 2 * H_B, 1), (nb_l, 2 * H_B, LANES)))
            hm_l = _mlstm(z, g, t_c, nb_l, s_l, nw, init)[0]
            a = _Rows(oa_c, oa_l, 0)
            b = _Rows(hm_c, hm_l, 0)
            w_out = w_out_even[e].astype(BF16)
            outs.setdefault("na_k", []).append(ka_c.reshape(nb_c, s_c, H_A, HD_A))
            outs.setdefault("na_v", []).append(z[:t_c, 2 * W_A:3 * W_A].reshape(nb_c, s_c, H_A, HD_A))
            outs.setdefault("C", []).append(c_fin.reshape(nb_c, 2, H_B, HD_B, HD_B))
            outs.setdefault("n", []).append(n_fin.reshape(nb_c, 2, H_B, HD_B))
            outs.setdefault("m", []).append(m_fin[:, :, 0].reshape(nb_c, 2, H_B))
        else:
            w_c = H_C * HD_C
            z = _inproj(x, mod_l, w_in_odd[e].astype(BF16), 4 * w_c + N_FG * FG_W)
            dl_rep = jnp.broadcast_to(ret_decay[e].reshape(2 * H_C, 1), (2 * H_C, LANES))
            nw = ret_norm[e].reshape(1, w_c)
            hr_c, s_fin = _retention(z, 0, nb_c, s_c, dl_rep, nw)
            hr_l = _retention(z, t_c, nb_l, s_l, dl_rep, nw, rope=_rope_tables(s_l),
                              init=state_ret_S[:, e].reshape(nb_l, 2 * H_C, HD_C, HD_C))[0]
            fd_c = _fnet(z, 0, nb_c, s_c)
            fd_l = _fnet(z, t_c, nb_l, s_l)
            a = _Rows(hr_c, hr_l, 0)
            b = _Rows(fd_c, fd_l, 0)
            w_out = w_out_odd[e].astype(BF16)
            outs.setdefault("S", []).append(s_fin.reshape(nb_c, 2, H_C, HD_C, HD_C))
        wr = _pad_lanes(w_router[l])
        br = _pad_lanes(b_router[l].reshape(1, N_EXP), NEG)
        x1, h2, ti, tw = _outproj_router(a, b, x, mod_l, w_out, wr, br)
        last = l == depth - 1
        x = _moe(x1, h2, ti, tw, mod_l, l, w_moe_in, b_moe_in, w_moe_out, b_moe_out, split=last)
        if not last:
            x = _whole(x)

    y_prompt = x[0].reshape(nb_c, s_c, d)
    y_sample = x[1].reshape(nb_l, s_l, d)
    stack = lambda key: jnp.stack(outs[key], axis=1).astype(dt)
    return (y_prompt, y_sample, stack("na_k"), stack("na_v"), stack("C"), stack("n"), stack("m"), stack("S"))
```

```python
import functools
from typing import NamedTuple

import numpy as np
import jax
import jax.numpy as jnp
from jax import lax
from jax.experimental import pallas as pl
from jax.experimental.pallas import tpu as pltpu

F32 = jnp.float32
BF16 = jnp.bfloat16
HIGHEST = lax.Precision.HIGHEST

D_MODEL = 1024
GRID_W = 64
WIN_R = 8
WIN_C = 16
H_A, HD_A = 8, 64
H_B, HD_B = 4, 128
H_C, HD_C = 4, 128
N_FG, FG_W = 4, 128
W_A = H_A * HD_A
W_B = H_B * HD_B
N_EXP = 32
TOP_K = 4
D_FF = D_MODEL
SWIGLU_LIMIT = 7.0
SWIGLU_ALPHA = 1.702
CHUNK = 128
ROPE_BASE = 10000.0
EPS = 1e-6

LANES = 128
SEG = 1024
N_COND = 8
TM = 256
N_TILES = 8 * SEG // TM
N_CTX_TILES = 4 * SEG // TM
TM_E = 256
NEG = -1e30
VMEM_LIMIT = 56 * 1024 * 1024


def _cparams(*sem):
    return pltpu.CompilerParams(dimension_semantics=sem, vmem_limit_bytes=VMEM_LIMIT)


def _cond_row(i):
    return jnp.maximum((i * TM) // SEG - 3, 0)


def _log_sigmoid(x):
    return jnp.minimum(x, 0.0) - jnp.log1p(jnp.exp(-jnp.abs(x)))


def _dot(a, b):
    return jnp.dot(a, b, preferred_element_type=F32)


def _dot_nt(a, b):
    return lax.dot_general(a, b, (((1,), (1,)), ((), ())), preferred_element_type=F32)


def _dot_hi(a, b):
    return jnp.dot(a, b, precision=HIGHEST, preferred_element_type=F32)


def _split_bf16(x):
    hi = x.astype(BF16)
    return hi, (x - hi.astype(F32)).astype(BF16)


def _dot_x3(x, w_hi, w_lo):
    x_hi, x_lo = _split_bf16(x)
    return _dot(x_hi, w_hi) + (_dot(x_lo, w_hi) + _dot(x_hi, w_lo))


def _mod_kernel(cond_ref, w_ref, b_ref, o_ref):
    c = cond_ref[...]
    s = c * jax.nn.sigmoid(c)
    o_ref[...] = _dot_hi(s, w_ref[...]) + b_ref[...]


def _modulation(cond, w_mod, b_mod):
    depth, d, n = w_mod.shape
    tn = 1536
    return pl.pallas_call(
        _mod_kernel,
        grid=(depth, n // tn),
        in_specs=[pl.BlockSpec((N_COND, d), lambda l, j: (0, 0)),
                  pl.BlockSpec((None, d, tn), lambda l, j: (l, 0, j)),
                  pl.BlockSpec((None, 1, tn), lambda l, j: (l, 0, j))],
        out_specs=pl.BlockSpec((None, N_COND, tn), lambda l, j: (l, 0, j)),
        out_shape=jax.ShapeDtypeStruct((depth, N_COND, n), F32),
        compiler_params=_cparams("arbitrary", "arbitrary"),
        name="modulation",
    )(cond, w_mod, b_mod.reshape(depth, 1, n))


def _rms_mod(x, shift, scale):
    h = x * lax.rsqrt(jnp.mean(x * x, axis=-1, keepdims=True) + EPS)
    return h * (1.0 + scale) + shift


class _Rows(NamedTuple):
    a: jax.Array
    b: jax.Array
    off_b: int

    def specs(self, lag=0):
        width = self.a.shape[1]
        off_b = self.off_b
        tile = (lambda i: i) if lag == 0 else (lambda i: jnp.maximum(i - lag, 0))
        return [pl.BlockSpec((TM, width), lambda i: (jnp.minimum(tile(i), N_CTX_TILES - 1), 0)),
                pl.BlockSpec((TM, width), lambda i: (jnp.maximum(tile(i) - N_CTX_TILES, 0) + off_b, 0))]


def _whole(x):
    return _Rows(x, x, N_CTX_TILES)


def _rows_read(tile, ref_a, ref_b):
    return jnp.where(tile < N_CTX_TILES, ref_a[...], ref_b[...])


def _inproj_kernel(xa_ref, xb_ref, mod_ref, w_ref, z_ref):
    x = _rows_read(pl.program_id(0), xa_ref, xb_ref)
    h = _rms_mod(x, mod_ref[0:1, :], mod_ref[1:2, :])
    z_ref[...] = _dot(h.astype(BF16), w_ref[...])


def _inproj_gate_kernel(xa_ref, xb_ref, mod_ref, w_ref, wgh_ref, wgl_ref, bg_ref, z_ref, g_ref):
    x = _rows_read(pl.program_id(0), xa_ref, xb_ref)
    h = _rms_mod(x, mod_ref[0:1, :], mod_ref[1:2, :])
    z_ref[...] = _dot(h.astype(BF16), w_ref[...])
    g_ref[...] = _dot_x3(h, wgh_ref[...], wgl_ref[...]) + bg_ref[...]


def _inproj(x, mod, w, n, wg=None, bg=None):
    t = N_TILES * TM
    d = x.a.shape[1]
    in_specs = x.specs() + [pl.BlockSpec((None, 6, d), lambda i: (_cond_row(i), 0, 0)),
                            pl.BlockSpec((d, n), lambda i: (0, 0))]
    z_spec = pl.BlockSpec((TM, n), lambda i: (i, 0))
    z_shape = jax.ShapeDtypeStruct((t, n), F32)
    if wg is None:
        return pl.pallas_call(
            _inproj_kernel, grid=(N_TILES,), in_specs=in_specs, out_specs=z_spec, out_shape=z_shape,
            compiler_params=_cparams("arbitrary"), name="inproj",
        )(x.a, x.b, mod, w)
    wg_hi, wg_lo = _split_bf16(wg)
    in_specs += [pl.BlockSpec((d, LANES), lambda i: (0, 0))] * 2 + [pl.BlockSpec((1, LANES), lambda i: (0, 0))]
    return pl.pallas_call(
        _inproj_gate_kernel, grid=(N_TILES,), in_specs=in_specs,
        out_specs=[z_spec, pl.BlockSpec((TM, LANES), lambda i: (i, 0))],
        out_shape=[z_shape, jax.ShapeDtypeStruct((t, LANES), F32)],
        compiler_params=_cparams("arbitrary"), name="inproj_gate",
    )(x.a, x.b, mod, w, wg_hi, wg_lo, bg)


def _head_rms(x, w):
    return x * lax.rsqrt(jnp.mean(x * x, axis=-1, keepdims=True) + EPS) * w


def _fold_lanes(x, op):
    parts = [x[:, c * LANES:(c + 1) * LANES] for c in range(x.shape[1] // LANES)]
    while len(parts) > 1:
        parts = [op(parts[c], parts[c + 1]) if c + 1 < len(parts) else parts[c] for c in range(0, len(parts), 2)]
    return parts[0]


def _ctx_attn_kernel(q_ref, k_ref, v_ref, qn_ref, kn_ref, o_ref, ko_ref):
    scale = HD_A ** -0.5
    for h in range(H_A):
        sl = slice(h * HD_A, (h + 1) * HD_A)
        q = _head_rms(q_ref[:, sl], qn_ref[...]) * scale
        k = _head_rms(k_ref[:, sl], kn_ref[...])
        ko_ref[:, sl] = k
        s = _dot_nt(q.astype(BF16), k.astype(BF16))
        p = jnp.exp(s - jnp.max(_fold_lanes(s, jnp.maximum), axis=-1, keepdims=True))
        den = jnp.sum(_fold_lanes(p, jnp.add), axis=-1, keepdims=True)
        o_ref[:, sl] = _dot(p.astype(BF16), v_ref[:, sl].astype(BF16)) / den


def _ctx_attention(z, n_batch, seq, qn, kn):
    spec = lambda c: pl.BlockSpec((seq, W_A), lambda b: (b, c))
    wspec = pl.BlockSpec((1, HD_A), lambda b: (0, 0))
    out = jax.ShapeDtypeStruct((n_batch * seq, W_A), F32)
    return pl.pallas_call(
        _ctx_attn_kernel, grid=(n_batch,),
        in_specs=[spec(0), spec(1), spec(2), wspec, wspec],
        out_specs=[pl.BlockSpec((seq, W_A), lambda b: (b, 0))] * 2,
        out_shape=[out, out],
        compiler_params=_cparams("arbitrary"), name="ctx_attention",
    )(z, z, z, qn, kn)


def _na_bias_table(rpb):
    qc = np.arange(GRID_W)
    kc = np.arange(GRID_W)
    cstart = np.clip(qc - WIN_C // 2, 0, GRID_W - WIN_C)
    col_in = (kc[None, :] >= cstart[:, None]) & (kc[None, :] < cstart[:, None] + WIN_C)
    dc = np.clip(kc[None, :] - qc[:, None], 1 - WIN_C, WIN_C - 1) + WIN_C - 1
    cls = np.arange(WIN_R)
    j = np.arange(WIN_R)
    dr = j[None, :] - cls[:, None] + WIN_R - 1
    sel_r = jnp.asarray(dr[:, :, None] == np.arange(2 * WIN_R - 1)[None, None, :], F32)
    sel_c = jnp.asarray(dc[:, :, None] == np.arange(2 * WIN_C - 1)[None, None, :], F32)
    tab = jnp.einsum("hab,cja,qkb->hcqjk", rpb, sel_r, sel_c, precision=HIGHEST)
    tab = jnp.where(jnp.asarray(col_in)[None, None, :, None, :], tab, NEG)
    return tab.reshape(H_A, WIN_R, GRID_W, WIN_R * GRID_W)


def _na_kernel(q_ref, k_ref, v_ref, kc_ref, vc_ref, bias_ref, qn_ref, kn_ref, o_ref,
               kn_s, v_s, kc_s, vc_s, *, rows):
    r = pl.program_id(1)
    scale = HD_A ** -0.5

    @pl.when(r == 0)
    def _():
        for h in range(H_A):
            sl = slice(h * HD_A, (h + 1) * HD_A)
            kn_s[:, sl] = _head_rms(k_ref[:, sl], kn_ref[...]).astype(BF16)
        v_s[...] = v_ref[...].astype(BF16)
        kc_s[...] = kc_ref[...].astype(BF16)
        vc_s[...] = vc_ref[...].astype(BF16)

    rs = jnp.clip(r - WIN_R // 2, 0, rows - WIN_R)
    start = pl.multiple_of(rs * GRID_W, GRID_W)
    n_loc = WIN_R * GRID_W
    for h in range(H_A):
        sl = slice(h * HD_A, (h + 1) * HD_A)
        q = (_head_rms(q_ref[:, sl], qn_ref[...]) * scale).astype(BF16)
        s_loc = _dot_nt(q, kn_s[pl.ds(start, n_loc), sl]) + bias_ref[h]
        s_ctx = _dot_nt(q, kc_s[:, sl])
        m = jnp.max(jnp.maximum(_fold_lanes(s_loc, jnp.maximum), _fold_lanes(s_ctx, jnp.maximum)),
                    axis=-1, keepdims=True)
        p_loc = jnp.exp(s_loc - m)
        p_ctx = jnp.exp(s_ctx - m)
        den = jnp.sum(_fold_lanes(p_loc, jnp.add) + _fold_lanes(p_ctx, jnp.add), axis=-1, keepdims=True)
        o = _dot(p_loc.astype(BF16), v_s[pl.ds(start, n_loc), sl]) + _dot(p_ctx.astype(BF16), vc_s[:, sl])
        o_ref[:, sl] = o / den


def _na_attention(z, row0, n_batch, seq, kc, vc, bias, qn, kn):
    rows = seq // GRID_W
    past = kc.shape[1]
    blk0 = row0 // GRID_W
    sblk0 = row0 // seq

    def cls_of(r):
        return r - jnp.clip(r - WIN_R // 2, 0, rows - WIN_R)

    full = lambda c: pl.BlockSpec((seq, W_A), lambda b, r: (sblk0 + b, c))
    cspec = pl.BlockSpec((None, past, W_A), lambda b, r: (b, 0, 0))
    wspec = pl.BlockSpec((1, HD_A), lambda b, r: (0, 0))
    return pl.pallas_call(
        functools.partial(_na_kernel, rows=rows), grid=(n_batch, rows),
        in_specs=[pl.BlockSpec((GRID_W, W_A), lambda b, r: (blk0 + b * rows + r, 0)),
                  full(1), full(2), cspec, cspec,
                  pl.BlockSpec((H_A, None, GRID_W, WIN_R * GRID_W), lambda b, r: (0, cls_of(r), 0, 0)),
                  wspec, wspec],
        out_specs=pl.BlockSpec((GRID_W, W_A), lambda b, r: (b * rows + r, 0)),
        out_shape=jax.ShapeDtypeStruct((n_batch * seq, W_A), F32),
        scratch_shapes=[pltpu.VMEM((seq, W_A), BF16), pltpu.VMEM((seq, W_A), BF16),
                        pltpu.VMEM((past, W_A), BF16), pltpu.VMEM((past, W_A), BF16)],
        compiler_params=_cparams("arbitrary", "arbitrary"), name="na_attention",
    )(z, z, z, kc, vc, bias, qn, kn)


def _tri_masks():
    li = lax.broadcasted_iota(jnp.int32, (CHUNK, CHUNK), 0)
    si = lax.broadcasted_iota(jnp.int32, (CHUNK, CHUNK), 1)
    return li >= si, li <= si


def _mlstm_kernel(*refs, nc, has_init):
    if has_init:
        (q_ref, k_ref, v_ref, og_ref, g_ref, nw_ref, c0_ref, n0_ref, m0_ref,
         o_ref, cf_ref, nf_ref, mf_ref, h_s, c_s, n_s, m_s) = refs
    else:
        (q_ref, k_ref, v_ref, og_ref, g_ref, nw_ref,
         o_ref, cf_ref, nf_ref, mf_ref, h_s, c_s, n_s, m_s) = refs
    nd = 2 * H_B
    if has_init:
        c_s[...] = c0_ref[...]
        n_s[...] = n0_ref[...]
        m_s[...] = m0_ref[...]
    else:
        c_s[...] = jnp.zeros_like(c_s)
        n_s[...] = jnp.zeros_like(n_s)
        m_s[...] = jnp.zeros_like(m_s)

    causal, anti = _tri_masks()
    tri_f = causal.astype(F32)
    tri_b = anti.astype(F32)
    kscale = HD_B ** -0.5

    def chunk_step(c, carry):
        for d in range(2):
            cc = c if d == 0 else nc - 1 - c
            t0 = pl.multiple_of(cc * CHUNK, CHUNK)
            g = g_ref[pl.ds(t0, CHUNK), :]
            gt = g.T
            ls = _log_sigmoid(g)
            lst = _log_sigmoid(gt)
            tri_c, tri_r, mask = (tri_f, tri_b, causal) if d == 0 else (tri_b, tri_f, anti)
            b_cols = _dot_hi(tri_c, ls)
            b_rows = _dot_hi(lst, tri_r)
            last = CHUNK - 1 if d == 0 else 0
            for h in range(H_B):
                ci = (2 * d) * H_B + h
                cf = (2 * d + 1) * H_B + h
                hs = slice(h * HD_B, (h + 1) * HD_B)
                q = q_ref[pl.ds(t0, CHUNK), hs]
                k = k_ref[pl.ds(t0, CHUNK), hs] * kscale
                v = v_ref[pl.ds(t0, CHUNK), hs]
                qb, kb, vb = q.astype(BF16), k.astype(BF16), v.astype(BF16)
                b_col = b_cols[:, cf:cf + 1]
                b_row = b_rows[cf:cf + 1, :]
                i_row = gt[ci:ci + 1, :]
                sidx = d * H_B + h
                cst = c_s[sidx]
                nst = n_s[sidx:sidx + 1, :]
                mst = m_s[sidx:sidx + 1, 0:1]
                dmat = jnp.where(mask, b_col - b_row + i_row, -jnp.inf)
                inter = b_col + mst
                mt = jnp.maximum(inter, jnp.max(dmat, axis=-1, keepdims=True))
                w = jnp.exp(dmat - mt) * _dot_nt(qb, kb)
                a = jnp.exp(inter - mt)
                num = _dot(w.astype(BF16), vb) + _dot(qb, cst.astype(BF16)) * a
                den = jnp.sum(w, axis=-1, keepdims=True) + a * jnp.sum(q * nst, axis=-1, keepdims=True)
                hc = num / jnp.maximum(jnp.abs(den), jnp.exp(-mt))
                h_s[d, pl.ds(t0, CHUNK), hs] = hc
                bl = b_row[:, last:last + 1]
                dl = bl - b_row + i_row
                m_new = jnp.maximum(bl + mst, jnp.max(dl, axis=-1, keepdims=True))
                wl = jnp.exp(dl - m_new)
                dec = jnp.exp(bl + mst - m_new)
                kw = (k.T * wl).astype(BF16)
                c_s[sidx] = dec * cst + _dot(kw, vb)
                wl8 = jnp.broadcast_to(wl, (8, CHUNK)).astype(BF16)
                n_s[sidx:sidx + 1, :] = dec * nst + _dot(wl8, kb)[0:1, :]
                m_s[sidx:sidx + 1, :] = jnp.broadcast_to(m_new, (1, LANES))
        return carry

    lax.fori_loop(0, nc, chunk_step, 0)

    for h in range(H_B):
        hs = slice(h * HD_B, (h + 1) * HD_B)
        hsum = h_s[0, :, hs] + h_s[1, :, hs]
        o_ref[:, hs] = _head_rms(hsum, nw_ref[:, hs]) * jax.nn.sigmoid(og_ref[:, hs])
    cf_ref[...] = c_s[...]
    nf_ref[...] = n_s[...]
    mf_ref[...] = m_s[...]


def _mlstm(z, g, row0, n_batch, seq, norm_w, init=None):
    sblk0 = row0 // seq
    nd = 2 * H_B
    spec = lambda c: pl.BlockSpec((seq, W_B), lambda b: (sblk0 + b, c))
    in_specs = [spec(3), spec(4), spec(5), spec(6),
                pl.BlockSpec((seq, LANES), lambda b: (sblk0 + b, 0)),
                pl.BlockSpec((1, W_B), lambda b: (0, 0))]
    args = [z, z, z, z, g, norm_w]
    st_specs = [pl.BlockSpec((None, nd, HD_B, HD_B), lambda b: (b, 0, 0, 0)),
                pl.BlockSpec((None, nd, HD_B), lambda b: (b, 0, 0)),
                pl.BlockSpec((None, nd, LANES), lambda b: (b, 0, 0))]
    if init is not None:
        in_specs += st_specs
        args += list(init)
    return pl.pallas_call(
        functools.partial(_mlstm_kernel, nc=seq // CHUNK, has_init=init is not None), grid=(n_batch,),
        in_specs=in_specs,
        out_specs=[pl.BlockSpec((seq, W_B), lambda b: (b, 0))] + st_specs,
        out_shape=[jax.ShapeDtypeStruct((n_batch * seq, W_B), F32),
                   jax.ShapeDtypeStruct((n_batch, nd, HD_B, HD_B), F32),
                   jax.ShapeDtypeStruct((n_batch, nd, HD_B), F32),
                   jax.ShapeDtypeStruct((n_batch, nd, LANES), F32)],
        scratch_shapes=[pltpu.VMEM((2, seq, W_B), F32), pltpu.VMEM((nd, HD_B, HD_B), F32),
                        pltpu.VMEM((nd, HD_B), F32), pltpu.VMEM((nd, LANES), F32)],
        compiler_params=_cparams("arbitrary"), name="mlstm",
    )(*args)


def _rope_tables(seq):
    half = HD_C // 2
    quarter = half // 2
    t = np.arange(seq)
    inv = ROPE_BASE ** (-np.arange(0, half, 2, dtype=np.float64) / half)
    ang_r = (t // GRID_W)[:, None] * inv[None, :]
    ang_c = (t % GRID_W)[:, None] * inv[None, :]
    cos_t = np.concatenate([np.cos(ang_r), np.cos(ang_r), np.cos(ang_c), np.cos(ang_c)], -1)
    sin_t = np.concatenate([-np.sin(ang_r), np.sin(ang_r), -np.sin(ang_c), np.sin(ang_c)], -1)
    assert cos_t.shape == (seq, 4 * quarter)
    return jnp.asarray(cos_t, F32), jnp.asarray(sin_t, F32)


def _rope(x, cos_t, sin_t):
    quarter = HD_C // 4
    lane = lax.broadcasted_iota(jnp.int32, x.shape, 1)
    first = (lane % (2 * quarter)) < quarter
    swapped = jnp.where(first, pltpu.roll(x, HD_C - quarter, 1), pltpu.roll(x, quarter, 1))
    return x * cos_t + swapped * sin_t


def _ret_kernel(*refs, nc, has_init, use_rope):
    refs = list(refs)
    q_ref, k_ref, v_ref, gg_ref, dl_ref, nw_ref = refs[:6]
    pos = 6
    if use_rope:
        cos_ref, sin_ref = refs[pos:pos + 2]
        pos += 2
    if has_init:
        s0_ref = refs[pos]
        pos += 1
    o_ref, sf_ref, h_s, s_s = refs[pos:pos + 4]
    if has_init:
        s_s[...] = s0_ref[...]
    else:
        s_s[...] = jnp.zeros_like(s_s)

    causal, anti = _tri_masks()
    li = lax.broadcasted_iota(jnp.int32, (CHUNK, CHUNK), 0).astype(F32)
    si = lax.broadcasted_iota(jnp.int32, (CHUNK, CHUNK), 1).astype(F32)
    lg_all = _log_sigmoid(dl_ref[...])
    kscale = HD_C ** -0.5

    def chunk_step(c, carry):
        for d in range(2):
            cc = c if d == 0 else nc - 1 - c
            t0 = pl.multiple_of(cc * CHUNK, CHUNK)
            for h in range(H_C):
                sidx = d * H_C + h
                hs = slice(h * HD_C, (h + 1) * HD_C)
                lg = lg_all[sidx:sidx + 1, :]
                q = q_ref[pl.ds(t0, CHUNK), hs]
                k = k_ref[pl.ds(t0, CHUNK), hs] * kscale
                v = v_ref[pl.ds(t0, CHUNK), hs]
                if use_rope:
                    cos_t = cos_ref[pl.ds(t0, CHUNK), :]
                    sin_t = sin_ref[pl.ds(t0, CHUNK), :]
                    q = _rope(q, cos_t, sin_t)
                    k = _rope(k, cos_t, sin_t)
                if d == 0:
                    decay = jnp.exp(jnp.where(causal, (li - si) * lg, -jnp.inf))
                    q_dec = jnp.exp((li + 1.0) * lg)
                    k_dec = jnp.exp((CHUNK - 1.0 - li) * lg)
                else:
                    decay = jnp.exp(jnp.where(anti, (si - li) * lg, -jnp.inf))
                    q_dec = jnp.exp((CHUNK - li) * lg)
                    k_dec = jnp.exp(li * lg)
                c_dec = jnp.exp(CHUNK * lg)
                st = s_s[sidx]
                qb, kb, vb = q.astype(BF16), k.astype(BF16), v.astype(BF16)
                att = _dot_nt(qb, kb) * decay
                o = _dot(att.astype(BF16), vb) + _dot(qb, st.astype(BF16)) * q_dec
                h_s[d, pl.ds(t0, CHUNK), hs] = o
                kd = (k * k_dec).T.astype(BF16)
                s_s[sidx] = c_dec * st + _dot(kd, vb)
        return carry

    lax.fori_loop(0, nc, chunk_step, 0)

    for h in range(H_C):
        hs = slice(h * HD_C, (h + 1) * HD_C)
        osum = h_s[0, :, hs] + h_s[1, :, hs]
        gg = gg_ref[:, hs]
        o_ref[:, hs] = _head_rms(osum, nw_ref[:, hs]) * (gg * jax.nn.sigmoid(gg))
    sf_ref[...] = s_s[...]


def _retention(z, row0, n_batch, seq, decay_rep, norm_w, rope=None, init=None):
    sblk0 = row0 // seq
    nd = 2 * H_C
    w_c = H_C * HD_C
    spec = lambda c: pl.BlockSpec((seq, w_c), lambda b: (sblk0 + b, c))
    in_specs = [spec(0), spec(1), spec(2), spec(3),
                pl.BlockSpec((nd, LANES), lambda b: (0, 0)),
                pl.BlockSpec((1, w_c), lambda b: (0, 0))]
    args = [z, z, z, z, decay_rep, norm_w]
    if rope is not None:
        in_specs += [pl.BlockSpec((seq, HD_C), lambda b: (0, 0))] * 2
        args += list(rope)
    st_spec = pl.BlockSpec((None, nd, HD_C, HD_C), lambda b: (b, 0, 0, 0))
    if init is not None:
        in_specs.append(st_spec)
        args.append(init)
    return pl.pallas_call(
        functools.partial(_ret_kernel, nc=seq // CHUNK, has_init=init is not None, use_rope=rope is not None),
        grid=(n_batch,), in_specs=in_specs,
        out_specs=[pl.BlockSpec((seq, w_c), lambda b: (b, 0)), st_spec],
        out_shape=[jax.ShapeDtypeStruct((n_batch * seq, w_c), F32),
                   jax.ShapeDtypeStruct((n_batch, nd, HD_C, HD_C), F32)],
        scratch_shapes=[pltpu.VMEM((2, seq, w_c), F32), pltpu.VMEM((nd, HD_C, HD_C), F32)],
        compiler_params=_cparams("arbitrary"), name="retention",
    )(*args)


def _dft_tables(n):
    idx = (np.arange(n)[:, None] * np.arange(n)[None, :]) % n
    ang = 2.0 * np.pi * idx / n
    return np.cos(ang) / np.sqrt(n), np.sin(ang) / np.sqrt(n)


def _fnet_kernel(x_ref, cw_ref, sw_ref, cs_ref, ss_ref, o_ref):
    for g in range(N_FG):
        gs = slice(g * FG_W, (g + 1) * FG_W)
        x = x_ref[:, gs].astype(BF16)
        xc = _dot(x, cw_ref[...]).astype(BF16)
        xs = _dot(x, sw_ref[...]).astype(BF16)
        o_ref[:, gs] = _dot(cs_ref[...], xc) - _dot(ss_ref[...], xs)


def _fnet(z, row0, n_batch, seq):
    sblk0 = row0 // seq
    w_d = N_FG * FG_W
    cw, sw = _dft_tables(FG_W)
    cs, ss = _dft_tables(seq)
    tabs = [jnp.asarray(a, F32).astype(BF16) for a in (cw, sw, cs, ss)]
    wspec = pl.BlockSpec((FG_W, FG_W), lambda b: (0, 0))
    sspec = pl.BlockSpec((seq, seq), lambda b: (0, 0))
    return pl.pallas_call(
        _fnet_kernel, grid=(n_batch,),
        in_specs=[pl.BlockSpec((seq, w_d), lambda b: (sblk0 + b, 4)), wspec, wspec, sspec, sspec],
        out_specs=pl.BlockSpec((seq, w_d), lambda b: (b, 0)),
        out_shape=jax.ShapeDtypeStruct((n_batch * seq, w_d), F32),
        compiler_params=_cparams("arbitrary"), name="fnet",
    )(z, *tabs)


SUB = 8
assert D_MODEL == SUB * LANES


def _store_token_major(ref, lead, val):
    n = val.shape[0]
    for c in range(SUB):
        ref[lead + (pl.ds(c, n, stride=SUB), slice(None))] = val[:, c * LANES:(c + 1) * LANES]


def _load_token_major(ref, lead, n):
    return jnp.concatenate([ref[lead + (pl.ds(c, n, stride=SUB), slice(None))] for c in range(SUB)], axis=1)


def _outproj_router_kernel(aa_ref, ab_ref, ba_ref, bb_ref, xa_ref, xb_ref, mod_ref, wa_ref, wb_ref,
                           wrh_ref, wrl_ref, br_ref, x1_ref, h2_ref, ti_ref, tw_ref, rank_ref, cnt_ref, cnt_s):
    i = pl.program_id(0)
    a = _rows_read(i, aa_ref, ab_ref)
    b = _rows_read(i, ba_ref, bb_ref)
    y = _dot(a.astype(BF16), wa_ref[...]) + _dot(b.astype(BF16), wb_ref[...])
    x1 = _rows_read(i, xa_ref, xb_ref) + mod_ref[2:3, :] * y
    x1_ref[...] = x1
    h2 = _rms_mod(x1, mod_ref[3:4, :], mod_ref[4:5, :])
    _store_token_major(h2_ref, (), h2)
    logits = _dot_x3(h2, wrh_ref[...], wrl_ref[...]) + br_ref[...]
    lane = lax.broadcasted_iota(jnp.int32, logits.shape, 1)
    lane_f = lane.astype(F32)
    cur = logits
    vals, picks = [], []
    ti = jnp.zeros(logits.shape, jnp.int32)
    for kk in range(TOP_K):
        mx = jnp.max(cur, axis=-1, keepdims=True)
        idx = jnp.min(jnp.where(cur == mx, lane_f, float(LANES)), axis=-1, keepdims=True)
        ti = jnp.where(lane == kk, idx.astype(jnp.int32), ti)
        pick = lane_f == idx
        cur = jnp.where(pick, -jnp.inf, cur)
        vals.append(mx)
        picks.append(pick)
    es = [jnp.exp(v - vals[0]) for v in vals]
    tot = es[0] + es[1] + es[2] + es[3]
    tw = jnp.zeros(logits.shape, F32)
    for kk in range(TOP_K):
        tw = jnp.where(lane == kk, es[kk] / tot, tw)
    ti_ref[...] = ti
    tw_ref[...] = tw

    @pl.when(i == 0)
    def _():
        cnt_s[...] = jnp.zeros_like(cnt_s)

    onehot = jnp.zeros(logits.shape, F32)
    for pick in picks:
        onehot = onehot + jnp.where(pick, 1.0, 0.0)
    n = logits.shape[0]
    earlier = (lax.broadcasted_iota(jnp.int32, (n, n), 1) < lax.broadcasted_iota(jnp.int32, (n, n), 0))
    before = cnt_s[...] + _dot(jnp.where(earlier, 1.0, 0.0).astype(BF16), onehot.astype(BF16))
    rank = jnp.zeros(logits.shape, jnp.int32)
    for kk, pick in enumerate(picks):
        r_k = jnp.sum(jnp.where(pick, before, 0.0), axis=-1, keepdims=True)
        rank = jnp.where(lane == kk, r_k.astype(jnp.int32), rank)
    rank_ref[...] = rank
    cnt_s[...] = cnt_s[...] + jnp.sum(onehot, axis=0, keepdims=True)
    cnt_ref[...] = cnt_s[...]


def _outproj_router(a, b, x, mod, w_out, wr, br):
    t = N_TILES * TM
    d = x.a.shape[1]
    wid = a.a.shape[1]
    row = lambda w: pl.BlockSpec((TM, w), lambda i: (i, 0))
    const = lambda r, c: pl.BlockSpec((r, c), lambda i: (0, 0))
    wr_hi, wr_lo = _split_bf16(wr)
    return pl.pallas_call(
        _outproj_router_kernel, grid=(N_TILES,),
        in_specs=a.specs() + b.specs() + x.specs() + [
                  pl.BlockSpec((None, 6, d), lambda i: (_cond_row(i), 0, 0)),
                  pl.BlockSpec((wid, d), lambda i: (0, 0)), pl.BlockSpec((wid, d), lambda i: (1, 0)),
                  const(d, LANES), const(d, LANES), const(1, LANES)],
        out_specs=[row(d), pl.BlockSpec((TM * SUB, LANES), lambda i: (i, 0)), row(LANES), row(LANES), row(LANES),
                   const(1, LANES)],
        out_shape=[jax.ShapeDtypeStruct((t, d), F32), jax.ShapeDtypeStruct((t * SUB, LANES), F32),
                   jax.ShapeDtypeStruct((t, LANES), jnp.int32), jax.ShapeDtypeStruct((t, LANES), F32),
                   jax.ShapeDtypeStruct((t, LANES), jnp.int32), jax.ShapeDtypeStruct((1, LANES), F32)],
        scratch_shapes=[pltpu.VMEM((1, LANES), F32)],
        compiler_params=_cparams("arbitrary"), name="outproj_router",
    )(a.a, a.b, b.a, b.b, x.a, x.b, mod, w_out, w_out, wr_hi, wr_lo, br)


N_TILES_MAX = N_TILES * TM * TOP_K // TM_E + N_EXP
PLAN_LANES = 2 * LANES
assert N_TILES_MAX <= PLAN_LANES and N_EXP <= LANES


def _plan_kernel(cnt_ref, ti_ref, rank_ref, dest_ref, meta_ref, start_s):
    i = pl.program_id(0)

    @pl.when(i == 0)
    def _():
        cnt = cnt_ref[...]
        tiles = jnp.floor((cnt + float(TM_E - 1)) * (1.0 / TM_E))
        sub = lax.broadcasted_iota(jnp.int32, (LANES, LANES), 0)
        lane = lax.broadcasted_iota(jnp.int32, (LANES, LANES), 1)
        upto = jnp.where(sub <= lane, 1.0, 0.0).astype(BF16)
        tile_end = _dot(jnp.broadcast_to(tiles, (SUB, LANES)).astype(BF16), upto)[0:1, :]
        tile_start = tile_end - tiles
        start_s[...] = tile_start * float(TM_E * SUB)
        n_tiles = jnp.max(tile_end, axis=-1, keepdims=True)
        used = tiles > 0.0

        def column(row):
            return jnp.sum(jnp.where(sub == lane, jnp.broadcast_to(row, (LANES, LANES)), 0.0), axis=-1, keepdims=True)

        end_c, start_c, tiles_c = column(tile_end), column(tile_start), column(tiles)
        used_b = jnp.broadcast_to(jnp.where(used, 1.0, 0.0), (LANES, LANES))
        pos_c = jnp.sum(jnp.where(lane <= sub, used_b, 0.0), axis=-1, keepdims=True)
        par_c = (pos_c - 1.0) - 2.0 * jnp.floor((pos_c - 1.0) * 0.5)
        nxt_c = jnp.min(jnp.where(jnp.logical_and(lane > sub, used_b > 0.0), lane.astype(F32), float(LANES)),
                        axis=-1, keepdims=True)
        nxt_c = jnp.where(nxt_c < float(LANES), nxt_c, -1.0)

        tid = lax.broadcasted_iota(jnp.int32, (LANES, PLAN_LANES), 1).astype(F32)
        exp_id = lax.broadcasted_iota(jnp.int32, (LANES, PLAN_LANES), 0)
        tid_used = jnp.minimum(tid, n_tiles - 1.0)
        te = jnp.sum(jnp.where(jnp.logical_and(exp_id < N_EXP, end_c <= tid_used), 1.0, 0.0), axis=0, keepdims=True)
        first = jnp.sum(jnp.where(jnp.logical_and(tiles_c > 0.0, start_c == tid), 1.0, 0.0), axis=0, keepdims=True)
        mine = te == exp_id.astype(F32)
        nxt = jnp.sum(jnp.where(mine, nxt_c, 0.0), axis=0, keepdims=True)
        par = jnp.sum(jnp.where(mine, par_c, 0.0), axis=0, keepdims=True)
        last = jnp.where(used, tile_end - 1.0, 0.0)
        last = jnp.concatenate([last, jnp.zeros((1, PLAN_LANES - LANES), F32)], axis=1)

        row_id = lax.broadcasted_iota(jnp.int32, (SUB, PLAN_LANES), 0)
        meta = jnp.zeros((SUB, PLAN_LANES), F32)
        for r, val in enumerate((te, first, nxt, par, last, jnp.broadcast_to(n_tiles, (1, PLAN_LANES)))):
            meta = jnp.where(row_id == r, jnp.broadcast_to(val, (SUB, PLAN_LANES)), meta)
        meta_ref[...] = meta.astype(jnp.int32)

    ti = ti_ref[...]
    rank = rank_ref[...]
    lane = lax.broadcasted_iota(jnp.int32, ti.shape, 1)
    dest = jnp.zeros(ti.shape, jnp.int32)
    for kk in range(TOP_K):
        base = jnp.sum(jnp.where(lane == ti[:, kk:kk + 1], start_s[...], 0.0), axis=-1, keepdims=True)
        dest = jnp.where(lane == kk, base.astype(jnp.int32) + rank[:, kk:kk + 1] * SUB, dest)
    dest_ref[...] = dest


def _route_plan(cnt, ti, rank):
    t = ti.shape[0]
    row = pl.BlockSpec((TM, LANES), lambda i: (i, 0))
    dest, meta = pl.pallas_call(
        _plan_kernel, grid=(t // TM,),
        in_specs=[pl.BlockSpec((1, LANES), lambda i: (0, 0)), row, row],
        out_specs=[row, pl.BlockSpec((SUB, PLAN_LANES), lambda i: (0, 0))],
        out_shape=[jax.ShapeDtypeStruct((t, LANES), jnp.int32), jax.ShapeDtypeStruct((SUB, PLAN_LANES), jnp.int32)],
        scratch_shapes=[pltpu.VMEM((1, LANES), F32)],
        compiler_params=_cparams("arbitrary"), name="moe_plan",
    )(cnt, ti, rank)
    plan = dict(tile_expert=meta[0, :N_TILES_MAX], first=meta[1, :N_TILES_MAX], tile_next=meta[2, :N_TILES_MAX],
                tile_parity=meta[3, :N_TILES_MAX], last_tile=meta[4, :N_EXP], n_tiles=meta[5, :1])
    return dest[:, :TOP_K], plan


DISPATCH_BLK = 1024


def _dispatch_kernel(lt_ref, nt_ref, dest_ref, h_ref, xs_ref, h_s, zero_s, sem, hsem):
    i = pl.program_id(0)
    tile_rows = TM_E * SUB
    n_tiles_max = xs_ref.shape[0] // tile_rows
    n_tok = h_s.shape[0] // SUB

    @pl.when(i == 0)
    def _():
        stage = pltpu.make_async_copy(h_ref, h_s, hsem)
        stage.start()
        zero_s[...] = jnp.zeros_like(zero_s)

        def zero_tile(tile):
            r0 = pl.multiple_of(tile * tile_rows, tile_rows)
            return pltpu.make_async_copy(zero_s, xs_ref.at[pl.ds(r0, tile_rows), :], sem)

        def start_unused(j, carry):
            zero_tile(j).start()
            return carry

        def wait_unused(j, carry):
            zero_tile(j).wait()
            return carry

        for e in range(N_EXP):
            zero_tile(lt_ref[e]).start()
        lax.fori_loop(nt_ref[0], n_tiles_max, start_unused, 0)
        for e in range(N_EXP):
            zero_tile(lt_ref[e]).wait()
        lax.fori_loop(nt_ref[0], n_tiles_max, wait_unused, 0)
        stage.wait()

    base = i * DISPATCH_BLK

    def issue(t, carry):
        src = pl.multiple_of((base + t) * SUB, SUB)
        for kk in range(TOP_K):
            row = pl.multiple_of(dest_ref[0, t * TOP_K + kk], SUB)
            pltpu.make_async_copy(h_s.at[pl.ds(src, SUB), :], xs_ref.at[pl.ds(row, SUB), :], sem).start(priority=kk % 2)
        return carry

    lax.fori_loop(0, DISPATCH_BLK, issue, 0, unroll=2)

    @pl.when(i == pl.num_programs(0) - 1)
    def _():
        for kk in range(TOP_K):
            pltpu.make_async_copy(h_s, xs_ref.at[pl.ds(0, n_tok * SUB), :], sem).wait()


def _dispatch(h2, dest, plan, n_rows):
    t = h2.shape[0] // SUB
    nblk = t // DISPATCH_BLK
    dest3 = dest.reshape(nblk, 1, DISPATCH_BLK * TOP_K)
    return pl.pallas_call(
        _dispatch_kernel,
        grid_spec=pltpu.PrefetchScalarGridSpec(
            num_scalar_prefetch=2, grid=(nblk,),
            in_specs=[pl.BlockSpec((None, 1, DISPATCH_BLK * TOP_K), lambda i, lt, nt: (i, 0, 0), memory_space=pltpu.SMEM),
                      pl.BlockSpec(memory_space=pl.ANY)],
            out_specs=pl.BlockSpec(memory_space=pl.ANY),
            scratch_shapes=[pltpu.VMEM((t * SUB, LANES), F32), pltpu.VMEM((TM_E * SUB, LANES), F32),
                            pltpu.SemaphoreType.DMA(()), pltpu.SemaphoreType.DMA(())]),
        out_shape=jax.ShapeDtypeStruct((n_rows * SUB, LANES), F32),
        compiler_params=_cparams("arbitrary"), name="moe_dispatch",
    )(plan["last_tile"], plan["n_tiles"], dest3, h2)


def _expert_kernel(te_ref, tf_ref, nt_ref, nx_ref, par_ref, xs_ref, w1_ref, b1_ref, w2_ref, b2_ref, ys_ref,
                   w1f, w2f, w1_s, w2_s, wsem, *, layer):
    i = pl.program_id(0)

    def fetch(expert, slot):
        return (pltpu.make_async_copy(w1_ref.at[layer, expert], w1f.at[slot], wsem.at[slot]),
                pltpu.make_async_copy(w2_ref.at[layer, expert], w2f.at[slot], wsem.at[slot]))

    @pl.when(i == 0)
    def _():
        for cp in fetch(te_ref[0], 0):
            cp.start(priority=1)

    @pl.when(i < nt_ref[0])
    def _():
        @pl.when(tf_ref[i] == 1)
        def _():
            slot = par_ref[i]
            for cp in fetch(te_ref[i], slot):
                cp.wait()
            w1_s[...] = w1f[slot].astype(BF16)
            w2_s[...] = w2f[slot].astype(BF16)

            @pl.when(nx_ref[i] >= 0)
            def _():
                for cp in fetch(nx_ref[i], 1 - slot):
                    cp.start(priority=1)

        x = _load_token_major(xs_ref, (), TM_E)
        u = _dot(x.astype(BF16), w1_s[...]) + b1_ref[...]
        g = jnp.minimum(u[:, :D_FF], SWIGLU_LIMIT)
        up = jnp.clip(u[:, D_FF:], -SWIGLU_LIMIT, SWIGLU_LIMIT)
        act = (up + 1.0) * g * jax.nn.sigmoid(SWIGLU_ALPHA * g)
        _store_token_major(ys_ref, (), _dot(act.astype(BF16), w2_s[...]) + b2_ref[...])

    @pl.when(i >= nt_ref[0])
    def _():
        ys_ref[...] = jnp.zeros_like(ys_ref)


def _experts(xs, plan, layer, w1, b1, w2, b2):
    d = D_MODEL
    nt = xs.shape[0] // (TM_E * SUB)
    tile = lambda i, te, tf, ntl, nx, par: (jnp.minimum(i, ntl[0] - 1), 0)
    otile = lambda i, te, tf, ntl, nx, par: (i, 0)
    bmap = lambda i, te, tf, ntl, nx, par: (layer, te[i], 0, 0)
    return pl.pallas_call(
        functools.partial(_expert_kernel, layer=layer),
        grid_spec=pltpu.PrefetchScalarGridSpec(
            num_scalar_prefetch=5, grid=(nt,),
            in_specs=[pl.BlockSpec((TM_E * SUB, LANES), tile),
                      pl.BlockSpec(memory_space=pl.ANY),
                      pl.BlockSpec((None, None, 1, 2 * D_FF), bmap),
                      pl.BlockSpec(memory_space=pl.ANY),
                      pl.BlockSpec((None, None, 1, d), bmap)],
            out_specs=pl.BlockSpec((TM_E * SUB, LANES), otile),
            scratch_shapes=[pltpu.VMEM((2, d, 2 * D_FF), F32), pltpu.VMEM((2, D_FF, d), F32),
                            pltpu.VMEM((d, 2 * D_FF), BF16), pltpu.VMEM((D_FF, d), BF16),
                            pltpu.SemaphoreType.DMA((2,))]),
        out_shape=jax.ShapeDtypeStruct(xs.shape, F32),
        compiler_params=_cparams("arbitrary"), name="moe_experts",
    )(plan["tile_expert"], plan["first"], plan["n_tiles"], plan["tile_next"], plan["tile_parity"],
      xs, w1, b1, w2, b2)


def _combine_kernel(dest_ref, x1_ref, tw_ref, mod_ref, ys_ref, *rest):
    *o_refs, buf, sem = rest
    i = pl.program_id(0)
    j = i - 1
    n = pl.num_programs(0) - 1
    tile_rows = TM * SUB

    for s in range(2):
        @pl.when(jnp.logical_and(i < n, i % 2 == s))
        def _():
            def issue(t, carry):
                dst = pl.multiple_of(t * SUB, SUB)
                for kk in range(TOP_K):
                    row = pl.multiple_of(dest_ref[0, t * TOP_K + kk], SUB)
                    pltpu.make_async_copy(ys_ref.at[pl.ds(row, SUB), :], buf.at[s, kk, pl.ds(dst, SUB), :],
                                          sem.at[s]).start(priority=kk % 2)
                return carry

            lax.fori_loop(0, TM, issue, 0, unroll=2)

    @pl.when(j >= 0)
    def _():
        slot = j % 2
        for kk in range(TOP_K):
            pltpu.make_async_copy(ys_ref.at[pl.ds(0, tile_rows), :], buf.at[slot, kk], sem.at[slot]).wait()
        tw = tw_ref[...]
        y = tw[:, 0:1] * _load_token_major(buf, (slot, 0), TM)
        for kk in range(1, TOP_K):
            y = y + tw[:, kk:kk + 1] * _load_token_major(buf, (slot, kk), TM)
        out = x1_ref[...] + mod_ref[5:6, :] * y
        if len(o_refs) == 1:
            o_refs[0][...] = out
        else:
            @pl.when(j < N_CTX_TILES)
            def _():
                o_refs[0][...] = out

            @pl.when(j >= N_CTX_TILES)
            def _():
                o_refs[1][...] = out


def _combine(ys, dest, x1, tw, mod, split):
    t, d = x1.shape
    nblk = t // TM
    dest3 = dest.reshape(nblk, 1, TM * TOP_K)
    prev = lambda i: jnp.maximum(i - 1, 0)
    row = lambda w: pl.BlockSpec((TM, w), lambda i: (prev(i), 0))
    return pl.pallas_call(
        _combine_kernel, grid=(nblk + 1,),
        in_specs=[pl.BlockSpec((None, 1, TM * TOP_K), lambda i: (jnp.minimum(i, nblk - 1), 0, 0), memory_space=pltpu.SMEM),
                  row(d), row(LANES),
                  pl.BlockSpec((None, 6, d), lambda i: (_cond_row(prev(i)), 0, 0)),
                  pl.BlockSpec(memory_space=pl.ANY)],
        out_specs=_Rows(x1, x1, 0).specs(lag=1) if split else row(d),
        out_shape=([jax.ShapeDtypeStruct((N_CTX_TILES * TM, d), F32),
                    jax.ShapeDtypeStruct(((N_TILES - N_CTX_TILES) * TM, d), F32)] if split
                   else jax.ShapeDtypeStruct((t, d), F32)),
        scratch_shapes=[pltpu.VMEM((2, TOP_K, TM * SUB, LANES), F32), pltpu.SemaphoreType.DMA((2,))],
        compiler_params=_cparams("arbitrary"), name="moe_combine",
    )(dest3, x1, tw, mod, ys)


def _moe(x1, h2, ti, tw, rank, cnt, mod, layer, w1, b1, w2, b2, split):
    depth = w1.shape[0]
    assert TM == TM_E and x1.shape[0] == N_TILES * TM
    dest, plan = _route_plan(cnt, ti, rank)
    xs = _dispatch(h2, dest, plan, N_TILES_MAX * TM_E)
    ys = _experts(xs, plan, layer, w1, b1.reshape(depth, N_EXP, 1, -1), w2, b2.reshape(depth, N_EXP, 1, -1))
    return _combine(ys, dest, x1, tw, mod, split)


def _pad_lanes(a, value=0.0):
    return jnp.pad(a, ((0, 0), (0, LANES - a.shape[1])), constant_values=value)


def kernel(x_prompt, x_sample, cache_na_k, cache_na_v, state_mlstm_C, state_mlstm_n, state_mlstm_m, state_ret_S, c, c_ctx, w_mod, b_mod, w_in_even, mlstm_gate_b, na_q_norm, na_k_norm, na_rpb, mlstm_norm, w_out_even, w_in_odd, ret_decay, ret_norm, w_out_odd, w_router, b_router, w_moe_in, b_moe_in, w_moe_out, b_moe_out):
    nb_c, s_c, d = x_prompt.shape
    nb_l, s_l, _ = x_sample.shape
    t_c = nb_c * s_c
    t_l = nb_l * s_l
    assert t_c == 4 * SEG and s_l == SEG and d == D_MODEL
    depth = w_mod.shape[0]
    dt = x_prompt.dtype

    x = _Rows(x_prompt.reshape(t_c, d), x_sample.reshape(t_l, d), 0)
    cond = jnp.concatenate([c_ctx[None, :], c, jnp.zeros((N_COND - 1 - nb_l, d), F32)], axis=0)
    mod = _modulation(cond, w_mod, b_mod).reshape(depth, N_COND, 6, d)

    outs = {}
    for l in range(depth):
        e = l // 2
        mod_l = mod[l]
        if l % 2 == 0:
            w_in = w_in_even[e]
            n_main = 3 * W_A + 4 * W_B
            wg = _pad_lanes(w_in[:, n_main:])
            bg = _pad_lanes(mlstm_gate_b[e].reshape(1, 4 * H_B))
            z, g = _inproj(x, mod_l, w_in.astype(BF16), n_main, wg, bg)
            qn = na_q_norm[e].reshape(1, HD_A)
            kn = na_k_norm[e].reshape(1, HD_A)
            oa_c, ka_c = _ctx_attention(z, nb_c, s_c, qn, kn)
            past = cache_na_k.shape[2]
            oa_l = _na_attention(z, t_c, nb_l, s_l,
                                 cache_na_k[:, e].reshape(nb_l, past, W_A), cache_na_v[:, e].reshape(nb_l, past, W_A),
                                 _na_bias_table(na_rpb[e]), qn, kn)
            nw = mlstm_norm[e].reshape(1, W_B)
            hm_c, c_fin, n_fin, m_fin = _mlstm(z, g, 0, nb_c, s_c, nw)
            init = (state_mlstm_C[:, e].reshape(nb_l, 2 * H_B, HD_B, HD_B),
                    state_mlstm_n[:, e].reshape(nb_l, 2 * H_B, HD_B),
                    jnp.broadcast_to(state_mlstm_m[:, e].reshape(nb_l, 2 * H_B, 1), (nb_l, 2 * H_B, LANES)))
            hm_l = _mlstm(z, g, t_c, nb_l, s_l, nw, init)[0]
            a = _Rows(oa_c, oa_l, 0)
            b = _Rows(hm_c, hm_l, 0)
            w_out = w_out_even[e].astype(BF16)
            outs.setdefault("na_k", []).append(ka_c.reshape(nb_c, s_c, H_A, HD_A))
            outs.setdefault("na_v", []).append(z[:t_c, 2 * W_A:3 * W_A].reshape(nb_c, s_c, H_A, HD_A))
            outs.setdefault("C", []).append(c_fin.reshape(nb_c, 2, H_B, HD_B, HD_B))
            outs.setdefault("n", []).append(n_fin.reshape(nb_c, 2, H_B, HD_B))
            outs.setdefault("m", []).append(m_fin[:, :, 0].reshape(nb_c, 2, H_B))
        else:
            w_c = H_C * HD_C
            z = _inproj(x, mod_l, w_in_odd[e].astype(BF16), 4 * w_c + N_FG * FG_W)
            dl_rep = jnp.broadcast_to(ret_decay[e].reshape(2 * H_C, 1), (2 * H_C, LANES))
            nw = ret_norm[e].reshape(1, w_c)
            hr_c, s_fin = _retention(z, 0, nb_c, s_c, dl_rep, nw)
            hr_l = _retention(z, t_c, nb_l, s_l, dl_rep, nw, rope=_rope_tables(s_l),
                              init=state_ret_S[:, e].reshape(nb_l, 2 * H_C, HD_C, HD_C))[0]
            fd_c = _fnet(z, 0, nb_c, s_c)
            fd_l = _fnet(z, t_c, nb_l, s_l)
            a = _Rows(hr_c, hr_l, 0)
            b = _Rows(fd_c, fd_l, 0)
            w_out = w_out_odd[e].astype(BF16)
            outs.setdefault("S", []).append(s_fin.reshape(nb_c, 2, H_C, HD_C, HD_C))
        wr = _pad_lanes(w_router[l])
        br = _pad_lanes(b_router[l].reshape(1, N_EXP), NEG)
        x1, h2, ti, tw, rank, cnt = _outproj_router(a, b, x, mod_l, w_out, wr, br)
        last = l == depth - 1
        x = _moe(x1, h2, ti, tw, rank, cnt, mod_l, l, w_moe_in, b_moe_in, w_moe_out, b_moe_out, split=last)
        if not last:
            x = _whole(x)

    y_prompt = x[0].reshape(nb_c, s_c, d)
    y_sample = x[1].reshape(nb_l, s_l, d)
    stack = lambda key: jnp.stack(outs[key], axis=1).astype(dt)
    return (y_prompt, y_sample, stack("na_k"), stack("na_v"), stack("C"), stack("n"), stack("m"), stack("S"))
```

```python
import functools
from typing import NamedTuple

import numpy as np
import jax
import jax.numpy as jnp
from jax import lax
from jax.experimental import pallas as pl
from jax.experimental.pallas import tpu as pltpu

F32 = jnp.float32
BF16 = jnp.bfloat16
HIGHEST = lax.Precision.HIGHEST

D_MODEL = 1024
GRID_W = 64
WIN_R = 8
WIN_C = 16
H_A, HD_A = 8, 64
H_B, HD_B = 4, 128
H_C, HD_C = 4, 128
N_FG, FG_W = 4, 128
W_A = H_A * HD_A
W_B = H_B * HD_B
N_EXP = 32
TOP_K = 4
D_FF = D_MODEL
SWIGLU_LIMIT = 7.0
SWIGLU_ALPHA = 1.702
CHUNK = 128
ROPE_BASE = 10000.0
EPS = 1e-6

LANES = 128
SEG = 1024
N_COND = 8
TM = 512
N_TILES = 8 * SEG // TM
N_CTX_TILES = 4 * SEG // TM
TM_E = 256
NEG = -1e30
VMEM_LIMIT = 56 * 1024 * 1024


def _cparams(*sem):
    return pltpu.CompilerParams(dimension_semantics=sem, vmem_limit_bytes=VMEM_LIMIT)


def _cond_row(i):
    return jnp.maximum((i * TM) // SEG - 3, 0)


def _log_sigmoid(x):
    return jnp.minimum(x, 0.0) - jnp.log1p(jnp.exp(-jnp.abs(x)))


def _dot(a, b):
    return jnp.dot(a, b, preferred_element_type=F32)


def _dot_nt(a, b):
    return lax.dot_general(a, b, (((1,), (1,)), ((), ())), preferred_element_type=F32)


def _dot_hi(a, b):
    return jnp.dot(a, b, precision=HIGHEST, preferred_element_type=F32)


def _split_bf16(x):
    hi = x.astype(BF16)
    return hi, (x - hi.astype(F32)).astype(BF16)


def _dot_x3(x, w_hi, w_lo):
    x_hi, x_lo = _split_bf16(x)
    return _dot(x_hi, w_hi) + (_dot(x_lo, w_hi) + _dot(x_hi, w_lo))


def _mod_kernel(cond_ref, w_ref, b_ref, o_ref):
    c = cond_ref[...]
    s = c * jax.nn.sigmoid(c)
    o_ref[...] = _dot_hi(s, w_ref[...]) + b_ref[...]


def _modulation(cond, w_mod, b_mod):
    depth, d, n = w_mod.shape
    tn = 1536
    return pl.pallas_call(
        _mod_kernel,
        grid=(depth, n // tn),
        in_specs=[pl.BlockSpec((N_COND, d), lambda l, j: (0, 0)),
                  pl.BlockSpec((None, d, tn), lambda l, j: (l, 0, j)),
                  pl.BlockSpec((None, 1, tn), lambda l, j: (l, 0, j))],
        out_specs=pl.BlockSpec((None, N_COND, tn), lambda l, j: (l, 0, j)),
        out_shape=jax.ShapeDtypeStruct((depth, N_COND, n), F32),
        compiler_params=_cparams("arbitrary", "arbitrary"),
        name="modulation",
    )(cond, w_mod, b_mod.reshape(depth, 1, n))


def _rms_mod(x, shift, scale):
    h = x * lax.rsqrt(jnp.mean(x * x, axis=-1, keepdims=True) + EPS)
    return h * (1.0 + scale) + shift


class _Rows(NamedTuple):
    a: jax.Array
    b: jax.Array
    off_b: int

    def specs(self, lag=0):
        width = self.a.shape[1]
        off_b = self.off_b
        tile = (lambda i: i) if lag == 0 else (lambda i: jnp.maximum(i - lag, 0))
        return [pl.BlockSpec((TM, width), lambda i: (jnp.minimum(tile(i), N_CTX_TILES - 1), 0)),
                pl.BlockSpec((TM, width), lambda i: (jnp.maximum(tile(i) - N_CTX_TILES, 0) + off_b, 0))]


def _whole(x):
    return _Rows(x, x, N_CTX_TILES)


def _rows_read(tile, ref_a, ref_b):
    return jnp.where(tile < N_CTX_TILES, ref_a[...], ref_b[...])


def _inproj_kernel(xa_ref, xb_ref, mod_ref, w_ref, z_ref):
    x = _rows_read(pl.program_id(0), xa_ref, xb_ref)
    h = _rms_mod(x, mod_ref[0:1, :], mod_ref[1:2, :])
    z_ref[...] = _dot(h.astype(BF16), w_ref[...])


def _inproj_gate_kernel(xa_ref, xb_ref, mod_ref, w_ref, wgh_ref, wgl_ref, bg_ref, z_ref, g_ref):
    x = _rows_read(pl.program_id(0), xa_ref, xb_ref)
    h = _rms_mod(x, mod_ref[0:1, :], mod_ref[1:2, :])
    z_ref[...] = _dot(h.astype(BF16), w_ref[...])
    g_ref[...] = _dot_x3(h, wgh_ref[...], wgl_ref[...]) + bg_ref[...]


def _inproj(x, mod, w, n, wg=None, bg=None):
    t = N_TILES * TM
    d = x.a.shape[1]
    in_specs = x.specs() + [pl.BlockSpec((None, 6, d), lambda i: (_cond_row(i), 0, 0)),
                            pl.BlockSpec((d, n), lambda i: (0, 0))]
    z_spec = pl.BlockSpec((TM, n), lambda i: (i, 0))
    z_shape = jax.ShapeDtypeStruct((t, n), F32)
    if wg is None:
        return pl.pallas_call(
            _inproj_kernel, grid=(N_TILES,), in_specs=in_specs, out_specs=z_spec, out_shape=z_shape,
            compiler_params=_cparams("arbitrary"), name="inproj",
        )(x.a, x.b, mod, w)
    wg_hi, wg_lo = _split_bf16(wg)
    in_specs += [pl.BlockSpec((d, LANES), lambda i: (0, 0))] * 2 + [pl.BlockSpec((1, LANES), lambda i: (0, 0))]
    return pl.pallas_call(
        _inproj_gate_kernel, grid=(N_TILES,), in_specs=in_specs,
        out_specs=[z_spec, pl.BlockSpec((TM, LANES), lambda i: (i, 0))],
        out_shape=[z_shape, jax.ShapeDtypeStruct((t, LANES), F32)],
        compiler_params=_cparams("arbitrary"), name="inproj_gate",
    )(x.a, x.b, mod, w, wg_hi, wg_lo, bg)


def _head_rms(x, w):
    return x * lax.rsqrt(jnp.mean(x * x, axis=-1, keepdims=True) + EPS) * w


def _fold_lanes(x, op):
    parts = [x[:, c * LANES:(c + 1) * LANES] for c in range(x.shape[1] // LANES)]
    while len(parts) > 1:
        parts = [op(parts[c], parts[c + 1]) if c + 1 < len(parts) else parts[c] for c in range(0, len(parts), 2)]
    return parts[0]


def _ctx_attn_kernel(q_ref, k_ref, v_ref, qn_ref, kn_ref, o_ref, ko_ref):
    scale = HD_A ** -0.5
    for h in range(H_A):
        sl = slice(h * HD_A, (h + 1) * HD_A)
        q = _head_rms(q_ref[:, sl], qn_ref[...]) * scale
        k = _head_rms(k_ref[:, sl], kn_ref[...])
        ko_ref[:, sl] = k
        s = _dot_nt(q.astype(BF16), k.astype(BF16))
        p = jnp.exp(s - jnp.max(_fold_lanes(s, jnp.maximum), axis=-1, keepdims=True))
        den = jnp.sum(_fold_lanes(p, jnp.add), axis=-1, keepdims=True)
        o_ref[:, sl] = _dot(p.astype(BF16), v_ref[:, sl].astype(BF16)) / den


def _ctx_attention(z, n_batch, seq, qn, kn):
    spec = lambda c: pl.BlockSpec((seq, W_A), lambda b: (b, c))
    wspec = pl.BlockSpec((1, HD_A), lambda b: (0, 0))
    out = jax.ShapeDtypeStruct((n_batch * seq, W_A), F32)
    return pl.pallas_call(
        _ctx_attn_kernel, grid=(n_batch,),
        in_specs=[spec(0), spec(1), spec(2), wspec, wspec],
        out_specs=[pl.BlockSpec((seq, W_A), lambda b: (b, 0))] * 2,
        out_shape=[out, out],
        compiler_params=_cparams("arbitrary"), name="ctx_attention",
    )(z, z, z, qn, kn)


def _na_bias_table(rpb):
    qc = np.arange(GRID_W)
    kc = np.arange(GRID_W)
    cstart = np.clip(qc - WIN_C // 2, 0, GRID_W - WIN_C)
    col_in = (kc[None, :] >= cstart[:, None]) & (kc[None, :] < cstart[:, None] + WIN_C)
    dc = np.clip(kc[None, :] - qc[:, None], 1 - WIN_C, WIN_C - 1) + WIN_C - 1
    cls = np.arange(WIN_R)
    j = np.arange(WIN_R)
    dr = j[None, :] - cls[:, None] + WIN_R - 1
    sel_r = jnp.asarray(dr[:, :, None] == np.arange(2 * WIN_R - 1)[None, None, :], F32)
    sel_c = jnp.asarray(dc[:, :, None] == np.arange(2 * WIN_C - 1)[None, None, :], F32)
    tab = jnp.einsum("hab,cja,qkb->hcqjk", rpb, sel_r, sel_c, precision=HIGHEST)
    tab = jnp.where(jnp.asarray(col_in)[None, None, :, None, :], tab, NEG)
    return tab.reshape(H_A, WIN_R, GRID_W, WIN_R * GRID_W)


def _na_kernel(q_ref, k_ref, v_ref, kc_ref, vc_ref, bias_ref, qn_ref, kn_ref, o_ref,
               kn_s, v_s, kc_s, vc_s, *, rows):
    r = pl.program_id(1)
    scale = HD_A ** -0.5

    @pl.when(r == 0)
    def _():
        for h in range(H_A):
            sl = slice(h * HD_A, (h + 1) * HD_A)
            kn_s[:, sl] = _head_rms(k_ref[:, sl], kn_ref[...]).astype(BF16)
        v_s[...] = v_ref[...].astype(BF16)
        kc_s[...] = kc_ref[...].astype(BF16)
        vc_s[...] = vc_ref[...].astype(BF16)

    rs = jnp.clip(r - WIN_R // 2, 0, rows - WIN_R)
    start = pl.multiple_of(rs * GRID_W, GRID_W)
    n_loc = WIN_R * GRID_W
    for h in range(H_A):
        sl = slice(h * HD_A, (h + 1) * HD_A)
        q = (_head_rms(q_ref[:, sl], qn_ref[...]) * scale).astype(BF16)
        s_loc = _dot_nt(q, kn_s[pl.ds(start, n_loc), sl]) + bias_ref[h]
        s_ctx = _dot_nt(q, kc_s[:, sl])
        m = jnp.max(jnp.maximum(_fold_lanes(s_loc, jnp.maximum), _fold_lanes(s_ctx, jnp.maximum)),
                    axis=-1, keepdims=True)
        p_loc = jnp.exp(s_loc - m)
        p_ctx = jnp.exp(s_ctx - m)
        den = jnp.sum(_fold_lanes(p_loc, jnp.add) + _fold_lanes(p_ctx, jnp.add), axis=-1, keepdims=True)
        o = _dot(p_loc.astype(BF16), v_s[pl.ds(start, n_loc), sl]) + _dot(p_ctx.astype(BF16), vc_s[:, sl])
        o_ref[:, sl] = o / den


def _na_attention(z, row0, n_batch, seq, kc, vc, bias, qn, kn):
    rows = seq // GRID_W
    past = kc.shape[1]
    blk0 = row0 // GRID_W
    sblk0 = row0 // seq

    def cls_of(r):
        return r - jnp.clip(r - WIN_R // 2, 0, rows - WIN_R)

    full = lambda c: pl.BlockSpec((seq, W_A), lambda b, r: (sblk0 + b, c))
    cspec = pl.BlockSpec((None, past, W_A), lambda b, r: (b, 0, 0))
    wspec = pl.BlockSpec((1, HD_A), lambda b, r: (0, 0))
    return pl.pallas_call(
        functools.partial(_na_kernel, rows=rows), grid=(n_batch, rows),
        in_specs=[pl.BlockSpec((GRID_W, W_A), lambda b, r: (blk0 + b * rows + r, 0)),
                  full(1), full(2), cspec, cspec,
                  pl.BlockSpec((H_A, None, GRID_W, WIN_R * GRID_W), lambda b, r: (0, cls_of(r), 0, 0)),
                  wspec, wspec],
        out_specs=pl.BlockSpec((GRID_W, W_A), lambda b, r: (b * rows + r, 0)),
        out_shape=jax.ShapeDtypeStruct((n_batch * seq, W_A), F32),
        scratch_shapes=[pltpu.VMEM((seq, W_A), BF16), pltpu.VMEM((seq, W_A), BF16),
                        pltpu.VMEM((past, W_A), BF16), pltpu.VMEM((past, W_A), BF16)],
        compiler_params=_cparams("arbitrary", "arbitrary"), name="na_attention",
    )(z, z, z, kc, vc, bias, qn, kn)


def _tri_masks():
    li = lax.broadcasted_iota(jnp.int32, (CHUNK, CHUNK), 0)
    si = lax.broadcasted_iota(jnp.int32, (CHUNK, CHUNK), 1)
    return li >= si, li <= si


def _mlstm_kernel(*refs, nc, has_init):
    if has_init:
        (q_ref, k_ref, v_ref, og_ref, g_ref, nw_ref, c0_ref, n0_ref, m0_ref,
         o_ref, cf_ref, nf_ref, mf_ref, h_s, c_s, n_s, m_s) = refs
    else:
        (q_ref, k_ref, v_ref, og_ref, g_ref, nw_ref,
         o_ref, cf_ref, nf_ref, mf_ref, h_s, c_s, n_s, m_s) = refs
    nd = 2 * H_B
    if has_init:
        c_s[...] = c0_ref[...]
        n_s[...] = n0_ref[...]
        m_s[...] = m0_ref[...]
    else:
        c_s[...] = jnp.zeros_like(c_s)
        n_s[...] = jnp.zeros_like(n_s)
        m_s[...] = jnp.zeros_like(m_s)

    causal, anti = _tri_masks()
    tri_f = causal.astype(F32)
    tri_b = anti.astype(F32)
    kscale = HD_B ** -0.5

    def chunk_step(c, carry):
        for d in range(2):
            cc = c if d == 0 else nc - 1 - c
            t0 = pl.multiple_of(cc * CHUNK, CHUNK)
            g = g_ref[pl.ds(t0, CHUNK), :]
            gt = g.T
            ls = _log_sigmoid(g)
            lst = _log_sigmoid(gt)
            tri_c, tri_r, mask = (tri_f, tri_b, causal) if d == 0 else (tri_b, tri_f, anti)
            b_cols = _dot_hi(tri_c, ls)
            b_rows = _dot_hi(lst, tri_r)
            last = CHUNK - 1 if d == 0 else 0
            for h in range(H_B):
                ci = (2 * d) * H_B + h
                cf = (2 * d + 1) * H_B + h
                hs = slice(h * HD_B, (h + 1) * HD_B)
                q = q_ref[pl.ds(t0, CHUNK), hs]
                k = k_ref[pl.ds(t0, CHUNK), hs] * kscale
                v = v_ref[pl.ds(t0, CHUNK), hs]
                qb, kb, vb = q.astype(BF16), k.astype(BF16), v.astype(BF16)
                b_col = b_cols[:, cf:cf + 1]
                b_row = b_rows[cf:cf + 1, :]
                i_row = gt[ci:ci + 1, :]
                sidx = d * H_B + h
                cst = c_s[sidx]
                nst = n_s[sidx:sidx + 1, :]
                mst = m_s[sidx:sidx + 1, 0:1]
                dmat = jnp.where(mask, b_col - b_row + i_row, -jnp.inf)
                inter = b_col + mst
                mt = jnp.maximum(inter, jnp.max(dmat, axis=-1, keepdims=True))
                w = jnp.exp(dmat - mt) * _dot_nt(qb, kb)
                a = jnp.exp(inter - mt)
                num = _dot(w.astype(BF16), vb) + _dot(qb, cst.astype(BF16)) * a
                den = jnp.sum(w, axis=-1, keepdims=True) + a * jnp.sum(q * nst, axis=-1, keepdims=True)
                hc = num / jnp.maximum(jnp.abs(den), jnp.exp(-mt))
                h_s[d, pl.ds(t0, CHUNK), hs] = hc
                bl = b_row[:, last:last + 1]
                dl = bl - b_row + i_row
                m_new = jnp.maximum(bl + mst, jnp.max(dl, axis=-1, keepdims=True))
                wl = jnp.exp(dl - m_new)
                dec = jnp.exp(bl + mst - m_new)
                kw = (k.T * wl).astype(BF16)
                c_s[sidx] = dec * cst + _dot(kw, vb)
                wl8 = jnp.broadcast_to(wl, (8, CHUNK)).astype(BF16)
                n_s[sidx:sidx + 1, :] = dec * nst + _dot(wl8, kb)[0:1, :]
                m_s[sidx:sidx + 1, :] = jnp.broadcast_to(m_new, (1, LANES))
        return carry

    lax.fori_loop(0, nc, chunk_step, 0)

    for h in range(H_B):
        hs = slice(h * HD_B, (h + 1) * HD_B)
        hsum = h_s[0, :, hs] + h_s[1, :, hs]
        o_ref[:, hs] = _head_rms(hsum, nw_ref[:, hs]) * jax.nn.sigmoid(og_ref[:, hs])
    cf_ref[...] = c_s[...]
    nf_ref[...] = n_s[...]
    mf_ref[...] = m_s[...]


def _mlstm(z, g, row0, n_batch, seq, norm_w, init=None):
    sblk0 = row0 // seq
    nd = 2 * H_B
    spec = lambda c: pl.BlockSpec((seq, W_B), lambda b: (sblk0 + b, c))
    in_specs = [spec(3), spec(4), spec(5), spec(6),
                pl.BlockSpec((seq, LANES), lambda b: (sblk0 + b, 0)),
                pl.BlockSpec((1, W_B), lambda b: (0, 0))]
    args = [z, z, z, z, g, norm_w]
    st_specs = [pl.BlockSpec((None, nd, HD_B, HD_B), lambda b: (b, 0, 0, 0)),
                pl.BlockSpec((None, nd, HD_B), lambda b: (b, 0, 0)),
                pl.BlockSpec((None, nd, LANES), lambda b: (b, 0, 0))]
    if init is not None:
        in_specs += st_specs
        args += list(init)
    return pl.pallas_call(
        functools.partial(_mlstm_kernel, nc=seq // CHUNK, has_init=init is not None), grid=(n_batch,),
        in_specs=in_specs,
        out_specs=[pl.BlockSpec((seq, W_B), lambda b: (b, 0))] + st_specs,
        out_shape=[jax.ShapeDtypeStruct((n_batch * seq, W_B), F32),
                   jax.ShapeDtypeStruct((n_batch, nd, HD_B, HD_B), F32),
                   jax.ShapeDtypeStruct((n_batch, nd, HD_B), F32),
                   jax.ShapeDtypeStruct((n_batch, nd, LANES), F32)],
        scratch_shapes=[pltpu.VMEM((2, seq, W_B), F32), pltpu.VMEM((nd, HD_B, HD_B), F32),
                        pltpu.VMEM((nd, HD_B), F32), pltpu.VMEM((nd, LANES), F32)],
        compiler_params=_cparams("arbitrary"), name="mlstm",
    )(*args)


def _rope_tables(seq):
    half = HD_C // 2
    quarter = half // 2
    t = np.arange(seq)
    inv = ROPE_BASE ** (-np.arange(0, half, 2, dtype=np.float64) / half)
    ang_r = (t // GRID_W)[:, None] * inv[None, :]
    ang_c = (t % GRID_W)[:, None] * inv[None, :]
    cos_t = np.concatenate([np.cos(ang_r), np.cos(ang_r), np.cos(ang_c), np.cos(ang_c)], -1)
    sin_t = np.concatenate([-np.sin(ang_r), np.sin(ang_r), -np.sin(ang_c), np.sin(ang_c)], -1)
    assert cos_t.shape == (seq, 4 * quarter)
    return jnp.asarray(cos_t, F32), jnp.asarray(sin_t, F32)


def _rope(x, cos_t, sin_t):
    quarter = HD_C // 4
    lane = lax.broadcasted_iota(jnp.int32, x.shape, 1)
    first = (lane % (2 * quarter)) < quarter
    swapped = jnp.where(first, pltpu.roll(x, HD_C - quarter, 1), pltpu.roll(x, quarter, 1))
    return x * cos_t + swapped * sin_t


def _ret_kernel(*refs, nc, has_init, use_rope):
    refs = list(refs)
    q_ref, k_ref, v_ref, gg_ref, dl_ref, nw_ref = refs[:6]
    pos = 6
    if use_rope:
        cos_ref, sin_ref = refs[pos:pos + 2]
        pos += 2
    if has_init:
        s0_ref = refs[pos]
        pos += 1
    o_ref, sf_ref, h_s, s_s, dk_s = refs[pos:pos + 5]
    if has_init:
        s_s[...] = s0_ref[...]
    else:
        s_s[...] = jnp.zeros_like(s_s)

    causal, anti = _tri_masks()
    li = lax.broadcasted_iota(jnp.int32, (CHUNK, CHUNK), 0).astype(F32)
    si = lax.broadcasted_iota(jnp.int32, (CHUNK, CHUNK), 1).astype(F32)
    lg_all = _log_sigmoid(dl_ref[...])
    kscale = HD_C ** -0.5
    for d in range(2):
        for h in range(H_C):
            sidx = d * H_C + h
            lg = lg_all[sidx:sidx + 1, :]
            if d == 0:
                dk_s[sidx, 0] = jnp.exp(jnp.where(causal, (li - si) * lg, -jnp.inf))
                dk_s[sidx, 1] = jnp.exp((li + 1.0) * lg)
                dk_s[sidx, 2] = jnp.exp((CHUNK - 1.0 - li) * lg)
            else:
                dk_s[sidx, 0] = jnp.exp(jnp.where(anti, (si - li) * lg, -jnp.inf))
                dk_s[sidx, 1] = jnp.exp((CHUNK - li) * lg)
                dk_s[sidx, 2] = jnp.exp(li * lg)

    def chunk_step(c, carry):
        for d in range(2):
            cc = c if d == 0 else nc - 1 - c
            t0 = pl.multiple_of(cc * CHUNK, CHUNK)
            for h in range(H_C):
                sidx = d * H_C + h
                hs = slice(h * HD_C, (h + 1) * HD_C)
                lg = lg_all[sidx:sidx + 1, :]
                q = q_ref[pl.ds(t0, CHUNK), hs]
                k = k_ref[pl.ds(t0, CHUNK), hs] * kscale
                v = v_ref[pl.ds(t0, CHUNK), hs]
                if use_rope:
                    cos_t = cos_ref[pl.ds(t0, CHUNK), :]
                    sin_t = sin_ref[pl.ds(t0, CHUNK), :]
                    q = _rope(q, cos_t, sin_t)
                    k = _rope(k, cos_t, sin_t)
                decay, q_dec, k_dec = dk_s[sidx, 0], dk_s[sidx, 1], dk_s[sidx, 2]
                c_dec = jnp.exp(CHUNK * lg)
                st = s_s[sidx]
                qb, kb, vb = q.astype(BF16), k.astype(BF16), v.astype(BF16)
                att = _dot_nt(qb, kb) * decay
                o = _dot(att.astype(BF16), vb) + _dot(qb, st.astype(BF16)) * q_dec
                h_s[d, pl.ds(t0, CHUNK), hs] = o
                kd = (k * k_dec).T.astype(BF16)
                s_s[sidx] = c_dec * st + _dot(kd, vb)
        return carry

    lax.fori_loop(0, nc, chunk_step, 0)

    for h in range(H_C):
        hs = slice(h * HD_C, (h + 1) * HD_C)
        osum = h_s[0, :, hs] + h_s[1, :, hs]
        gg = gg_ref[:, hs]
        o_ref[:, hs] = _head_rms(osum, nw_ref[:, hs]) * (gg * jax.nn.sigmoid(gg))
    sf_ref[...] = s_s[...]


def _retention(z, row0, n_batch, seq, decay_rep, norm_w, rope=None, init=None):
    sblk0 = row0 // seq
    nd = 2 * H_C
    w_c = H_C * HD_C
    spec = lambda c: pl.BlockSpec((seq, w_c), lambda b: (sblk0 + b, c))
    in_specs = [spec(0), spec(1), spec(2), spec(3),
                pl.BlockSpec((nd, LANES), lambda b: (0, 0)),
                pl.BlockSpec((1, w_c), lambda b: (0, 0))]
    args = [z, z, z, z, decay_rep, norm_w]
    if rope is not None:
        in_specs += [pl.BlockSpec((seq, HD_C), lambda b: (0, 0))] * 2
        args += list(rope)
    st_spec = pl.BlockSpec((None, nd, HD_C, HD_C), lambda b: (b, 0, 0, 0))
    if init is not None:
        in_specs.append(st_spec)
        args.append(init)
    return pl.pallas_call(
        functools.partial(_ret_kernel, nc=seq // CHUNK, has_init=init is not None, use_rope=rope is not None),
        grid=(n_batch,), in_specs=in_specs,
        out_specs=[pl.BlockSpec((seq, w_c), lambda b: (b, 0)), st_spec],
        out_shape=[jax.ShapeDtypeStruct((n_batch * seq, w_c), F32),
                   jax.ShapeDtypeStruct((n_batch, nd, HD_C, HD_C), F32)],
        scratch_shapes=[pltpu.VMEM((2, seq, w_c), F32), pltpu.VMEM((nd, HD_C, HD_C), F32),
                        pltpu.VMEM((nd, 3, CHUNK, CHUNK), F32)],
        compiler_params=_cparams("arbitrary"), name="retention",
    )(*args)


def _dft_tables(n):
    idx = (np.arange(n)[:, None] * np.arange(n)[None, :]) % n
    ang = 2.0 * np.pi * idx / n
    return np.cos(ang) / np.sqrt(n), np.sin(ang) / np.sqrt(n)


def _fnet_kernel(x_ref, cw_ref, sw_ref, cs_ref, ss_ref, o_ref):
    for g in range(N_FG):
        gs = slice(g * FG_W, (g + 1) * FG_W)
        x = x_ref[:, gs].astype(BF16)
        xc = _dot(x, cw_ref[...]).astype(BF16)
        xs = _dot(x, sw_ref[...]).astype(BF16)
        o_ref[:, gs] = _dot(cs_ref[...], xc) - _dot(ss_ref[...], xs)


def _fnet(z, row0, n_batch, seq):
    sblk0 = row0 // seq
    w_d = N_FG * FG_W
    cw, sw = _dft_tables(FG_W)
    cs, ss = _dft_tables(seq)
    tabs = [jnp.asarray(a, F32).astype(BF16) for a in (cw, sw, cs, ss)]
    wspec = pl.BlockSpec((FG_W, FG_W), lambda b: (0, 0))
    sspec = pl.BlockSpec((seq, seq), lambda b: (0, 0))
    return pl.pallas_call(
        _fnet_kernel, grid=(n_batch,),
        in_specs=[pl.BlockSpec((seq, w_d), lambda b: (sblk0 + b, 4)), wspec, wspec, sspec, sspec],
        out_specs=pl.BlockSpec((seq, w_d), lambda b: (b, 0)),
        out_shape=jax.ShapeDtypeStruct((n_batch * seq, w_d), F32),
        compiler_params=_cparams("arbitrary"), name="fnet",
    )(z, *tabs)


SUB = 8
assert D_MODEL == SUB * LANES


def _store_token_major(ref, lead, val):
    n = val.shape[0]
    for c in range(SUB):
        ref[lead + (pl.ds(c, n, stride=SUB), slice(None))] = val[:, c * LANES:(c + 1) * LANES]


def _load_token_major(ref, lead, n):
    return jnp.concatenate([ref[lead + (pl.ds(c, n, stride=SUB), slice(None))] for c in range(SUB)], axis=1)


def _outproj_router_kernel(aa_ref, ab_ref, ba_ref, bb_ref, xa_ref, xb_ref, mod_ref, wa_ref, wb_ref,
                           wrh_ref, wrl_ref, br_ref, x1_ref, h2_ref, ti_ref, tw_ref, rank_ref, cnt_ref, cnt_s):
    i = pl.program_id(0)
    a = _rows_read(i, aa_ref, ab_ref)
    b = _rows_read(i, ba_ref, bb_ref)
    y = _dot(a.astype(BF16), wa_ref[...]) + _dot(b.astype(BF16), wb_ref[...])
    x1 = _rows_read(i, xa_ref, xb_ref) + mod_ref[2:3, :] * y
    x1_ref[...] = x1
    h2 = _rms_mod(x1, mod_ref[3:4, :], mod_ref[4:5, :])
    _store_token_major(h2_ref, (), h2)
    logits = _dot_x3(h2, wrh_ref[...], wrl_ref[...]) + br_ref[...]
    lane = lax.broadcasted_iota(jnp.int32, logits.shape, 1)
    lane_f = lane.astype(F32)
    cur = logits
    vals, picks = [], []
    ti = jnp.zeros(logits.shape, jnp.int32)
    for kk in range(TOP_K):
        mx = jnp.max(cur, axis=-1, keepdims=True)
        idx = jnp.min(jnp.where(cur == mx, lane_f, float(LANES)), axis=-1, keepdims=True)
        ti = jnp.where(lane == kk, idx.astype(jnp.int32), ti)
        pick = lane_f == idx
        cur = jnp.where(pick, -jnp.inf, cur)
        vals.append(mx)
        picks.append(pick)
    es = [jnp.exp(v - vals[0]) for v in vals]
    tot = es[0] + es[1] + es[2] + es[3]
    tw = jnp.zeros(logits.shape, F32)
    for kk in range(TOP_K):
        tw = jnp.where(lane == kk, es[kk] / tot, tw)
    ti_ref[...] = ti
    tw_ref[...] = tw

    @pl.when(i == 0)
    def _():
        cnt_s[...] = jnp.zeros_like(cnt_s)

    onehot = jnp.zeros(logits.shape, F32)
    for pick in picks:
        onehot = onehot + jnp.where(pick, 1.0, 0.0)
    n = logits.shape[0]
    earlier = (lax.broadcasted_iota(jnp.int32, (n, n), 1) < lax.broadcasted_iota(jnp.int32, (n, n), 0))
    before = cnt_s[...] + _dot(jnp.where(earlier, 1.0, 0.0).astype(BF16), onehot.astype(BF16))
    rank = jnp.zeros(logits.shape, jnp.int32)
    for kk, pick in enumerate(picks):
        r_k = jnp.sum(jnp.where(pick, before, 0.0), axis=-1, keepdims=True)
        rank = jnp.where(lane == kk, r_k.astype(jnp.int32), rank)
    rank_ref[...] = rank
    cnt_s[...] = cnt_s[...] + jnp.sum(onehot, axis=0, keepdims=True)
    cnt_ref[...] = cnt_s[...]


def _outproj_router(a, b, x, mod, w_out, wr, br):
    t = N_TILES * TM
    d = x.a.shape[1]
    wid = a.a.shape[1]
    row = lambda w: pl.BlockSpec((TM, w), lambda i: (i, 0))
    const = lambda r, c: pl.BlockSpec((r, c), lambda i: (0, 0))
    wr_hi, wr_lo = _split_bf16(wr)
    return pl.pallas_call(
        _outproj_router_kernel, grid=(N_TILES,),
        in_specs=a.specs() + b.specs() + x.specs() + [
                  pl.BlockSpec((None, 6, d), lambda i: (_cond_row(i), 0, 0)),
                  pl.BlockSpec((wid, d), lambda i: (0, 0)), pl.BlockSpec((wid, d), lambda i: (1, 0)),
                  const(d, LANES), const(d, LANES), const(1, LANES)],
        out_specs=[row(d), pl.BlockSpec((TM * SUB, LANES), lambda i: (i, 0)), row(LANES), row(LANES), row(LANES),
                   const(1, LANES)],
        out_shape=[jax.ShapeDtypeStruct((t, d), F32), jax.ShapeDtypeStruct((t * SUB, LANES), F32),
                   jax.ShapeDtypeStruct((t, LANES), jnp.int32), jax.ShapeDtypeStruct((t, LANES), F32),
                   jax.ShapeDtypeStruct((t, LANES), jnp.int32), jax.ShapeDtypeStruct((1, LANES), F32)],
        scratch_shapes=[pltpu.VMEM((1, LANES), F32)],
        compiler_params=_cparams("arbitrary"), name="outproj_router",
    )(a.a, a.b, b.a, b.b, x.a, x.b, mod, w_out, w_out, wr_hi, wr_lo, br)


N_TILES_MAX = N_TILES * TM * TOP_K // TM_E + N_EXP
PLAN_LANES = 2 * LANES
assert N_TILES_MAX <= PLAN_LANES and N_EXP <= LANES


def _plan_kernel(cnt_ref, ti_ref, rank_ref, dest_ref, meta_ref, start_s):
    i = pl.program_id(0)

    @pl.when(i == 0)
    def _():
        cnt = cnt_ref[...]
        tiles = jnp.floor((cnt + float(TM_E - 1)) * (1.0 / TM_E))
        sub = lax.broadcasted_iota(jnp.int32, (LANES, LANES), 0)
        lane = lax.broadcasted_iota(jnp.int32, (LANES, LANES), 1)
        upto = jnp.where(sub <= lane, 1.0, 0.0).astype(BF16)
        tile_end = _dot(jnp.broadcast_to(tiles, (SUB, LANES)).astype(BF16), upto)[0:1, :]
        tile_start = tile_end - tiles
        start_s[...] = tile_start * float(TM_E * SUB)
        n_tiles = jnp.max(tile_end, axis=-1, keepdims=True)
        used = tiles > 0.0

        def column(row):
            return jnp.sum(jnp.where(sub == lane, jnp.broadcast_to(row, (LANES, LANES)), 0.0), axis=-1, keepdims=True)

        end_c, start_c, tiles_c = column(tile_end), column(tile_start), column(tiles)
        used_b = jnp.broadcast_to(jnp.where(used, 1.0, 0.0), (LANES, LANES))
        pos_c = jnp.sum(jnp.where(lane <= sub, used_b, 0.0), axis=-1, keepdims=True)
        par_c = (pos_c - 1.0) - 2.0 * jnp.floor((pos_c - 1.0) * 0.5)
        nxt_c = jnp.min(jnp.where(jnp.logical_and(lane > sub, used_b > 0.0), lane.astype(F32), float(LANES)),
                        axis=-1, keepdims=True)
        nxt_c = jnp.where(nxt_c < float(LANES), nxt_c, -1.0)

        tid = lax.broadcasted_iota(jnp.int32, (LANES, PLAN_LANES), 1).astype(F32)
        exp_id = lax.broadcasted_iota(jnp.int32, (LANES, PLAN_LANES), 0)
        tid_used = jnp.minimum(tid, n_tiles - 1.0)
        te = jnp.sum(jnp.where(jnp.logical_and(exp_id < N_EXP, end_c <= tid_used), 1.0, 0.0), axis=0, keepdims=True)
        first = jnp.sum(jnp.where(jnp.logical_and(tiles_c > 0.0, start_c == tid), 1.0, 0.0), axis=0, keepdims=True)
        mine = te == exp_id.astype(F32)
        nxt = jnp.sum(jnp.where(mine, nxt_c, 0.0), axis=0, keepdims=True)
        par = jnp.sum(jnp.where(mine, par_c, 0.0), axis=0, keepdims=True)
        last = jnp.where(used, tile_end - 1.0, 0.0)
        last = jnp.concatenate([last, jnp.zeros((1, PLAN_LANES - LANES), F32)], axis=1)

        row_id = lax.broadcasted_iota(jnp.int32, (SUB, PLAN_LANES), 0)
        meta = jnp.zeros((SUB, PLAN_LANES), F32)
        for r, val in enumerate((te, first, nxt, par, last, jnp.broadcast_to(n_tiles, (1, PLAN_LANES)))):
            meta = jnp.where(row_id == r, jnp.broadcast_to(val, (SUB, PLAN_LANES)), meta)
        meta_ref[...] = meta.astype(jnp.int32)

    ti = ti_ref[...]
    rank = rank_ref[...]
    lane = lax.broadcasted_iota(jnp.int32, ti.shape, 1)
    dest = jnp.zeros(ti.shape, jnp.int32)
    for kk in range(TOP_K):
        base = jnp.sum(jnp.where(lane == ti[:, kk:kk + 1], start_s[...], 0.0), axis=-1, keepdims=True)
        dest = jnp.where(lane == kk, base.astype(jnp.int32) + rank[:, kk:kk + 1] * SUB, dest)
    dest_ref[...] = dest


def _route_plan(cnt, ti, rank):
    t = ti.shape[0]
    row = pl.BlockSpec((TM, LANES), lambda i: (i, 0))
    dest, meta = pl.pallas_call(
        _plan_kernel, grid=(t // TM,),
        in_specs=[pl.BlockSpec((1, LANES), lambda i: (0, 0)), row, row],
        out_specs=[row, pl.BlockSpec((SUB, PLAN_LANES), lambda i: (0, 0))],
        out_shape=[jax.ShapeDtypeStruct((t, LANES), jnp.int32), jax.ShapeDtypeStruct((SUB, PLAN_LANES), jnp.int32)],
        scratch_shapes=[pltpu.VMEM((1, LANES), F32)],
        compiler_params=_cparams("arbitrary"), name="moe_plan",
    )(cnt, ti, rank)
    plan = dict(tile_expert=meta[0, :N_TILES_MAX], first=meta[1, :N_TILES_MAX], tile_next=meta[2, :N_TILES_MAX],
                tile_parity=meta[3, :N_TILES_MAX], last_tile=meta[4, :N_EXP], n_tiles=meta[5, :1])
    return dest[:, :TOP_K], plan


DISPATCH_BLK = 1024


def _dispatch_kernel(lt_ref, nt_ref, dest_ref, h_ref, xs_ref, h_s, zero_s, sem, hsem):
    i = pl.program_id(0)
    tile_rows = TM_E * SUB
    n_tiles_max = xs_ref.shape[0] // tile_rows
    n_tok = h_s.shape[0] // SUB

    @pl.when(i == 0)
    def _():
        stage = pltpu.make_async_copy(h_ref, h_s, hsem)
        stage.start()
        zero_s[...] = jnp.zeros_like(zero_s)

        def zero_tile(tile):
            r0 = pl.multiple_of(tile * tile_rows, tile_rows)
            return pltpu.make_async_copy(zero_s, xs_ref.at[pl.ds(r0, tile_rows), :], sem)

        def start_unused(j, carry):
            zero_tile(j).start()
            return carry

        def wait_unused(j, carry):
            zero_tile(j).wait()
            return carry

        for e in range(N_EXP):
            zero_tile(lt_ref[e]).start()
        lax.fori_loop(nt_ref[0], n_tiles_max, start_unused, 0)
        for e in range(N_EXP):
            zero_tile(lt_ref[e]).wait()
        lax.fori_loop(nt_ref[0], n_tiles_max, wait_unused, 0)
        stage.wait()

    base = i * DISPATCH_BLK

    def issue(t, carry):
        src = pl.multiple_of((base + t) * SUB, SUB)
        for kk in range(TOP_K):
            row = pl.multiple_of(dest_ref[0, t * TOP_K + kk], SUB)
            pltpu.make_async_copy(h_s.at[pl.ds(src, SUB), :], xs_ref.at[pl.ds(row, SUB), :], sem).start(priority=kk % 2)
        return carry

    lax.fori_loop(0, DISPATCH_BLK, issue, 0, unroll=2)

    @pl.when(i == pl.num_programs(0) - 1)
    def _():
        for kk in range(TOP_K):
            pltpu.make_async_copy(h_s, xs_ref.at[pl.ds(0, n_tok * SUB), :], sem).wait()


def _dispatch(h2, dest, plan, n_rows):
    t = h2.shape[0] // SUB
    nblk = t // DISPATCH_BLK
    dest3 = dest.reshape(nblk, 1, DISPATCH_BLK * TOP_K)
    return pl.pallas_call(
        _dispatch_kernel,
        grid_spec=pltpu.PrefetchScalarGridSpec(
            num_scalar_prefetch=2, grid=(nblk,),
            in_specs=[pl.BlockSpec((None, 1, DISPATCH_BLK * TOP_K), lambda i, lt, nt: (i, 0, 0), memory_space=pltpu.SMEM),
                      pl.BlockSpec(memory_space=pl.ANY)],
            out_specs=pl.BlockSpec(memory_space=pl.ANY),
            scratch_shapes=[pltpu.VMEM((t * SUB, LANES), F32), pltpu.VMEM((TM_E * SUB, LANES), F32),
                            pltpu.SemaphoreType.DMA(()), pltpu.SemaphoreType.DMA(())]),
        out_shape=jax.ShapeDtypeStruct((n_rows * SUB, LANES), F32),
        compiler_params=_cparams("arbitrary"), name="moe_dispatch",
    )(plan["last_tile"], plan["n_tiles"], dest3, h2)


def _expert_kernel(te_ref, tf_ref, nt_ref, nx_ref, par_ref, xs_ref, w1_ref, b1_ref, w2_ref, b2_ref, ys_ref,
                   w1f, w2f, w1_s, w2_s, wsem, *, layer):
    i = pl.program_id(0)

    def fetch(expert, slot):
        return (pltpu.make_async_copy(w1_ref.at[layer, expert], w1f.at[slot], wsem.at[slot]),
                pltpu.make_async_copy(w2_ref.at[layer, expert], w2f.at[slot], wsem.at[slot]))

    @pl.when(i == 0)
    def _():
        for cp in fetch(te_ref[0], 0):
            cp.start(priority=1)

    @pl.when(i < nt_ref[0])
    def _():
        @pl.when(tf_ref[i] == 1)
        def _():
            slot = par_ref[i]
            for cp in fetch(te_ref[i], slot):
                cp.wait()
            w1_s[...] = w1f[slot].astype(BF16)
            w2_s[...] = w2f[slot].astype(BF16)

            @pl.when(nx_ref[i] >= 0)
            def _():
                for cp in fetch(nx_ref[i], 1 - slot):
                    cp.start(priority=1)

        x = _load_token_major(xs_ref, (), TM_E)
        u = _dot(x.astype(BF16), w1_s[...]) + b1_ref[...]
        g = jnp.minimum(u[:, :D_FF], SWIGLU_LIMIT)
        up = jnp.clip(u[:, D_FF:], -SWIGLU_LIMIT, SWIGLU_LIMIT)
        act = (up + 1.0) * g * jax.nn.sigmoid(SWIGLU_ALPHA * g)
        _store_token_major(ys_ref, (), _dot(act.astype(BF16), w2_s[...]) + b2_ref[...])

    @pl.when(i >= nt_ref[0])
    def _():
        ys_ref[...] = jnp.zeros_like(ys_ref)


def _experts(xs, plan, layer, w1, b1, w2, b2):
    d = D_MODEL
    nt = xs.shape[0] // (TM_E * SUB)
    tile = lambda i, te, tf, ntl, nx, par: (jnp.minimum(i, ntl[0] - 1), 0)
    otile = lambda i, te, tf, ntl, nx, par: (i, 0)
    bmap = lambda i, te, tf, ntl, nx, par: (layer, te[i], 0, 0)
    return pl.pallas_call(
        functools.partial(_expert_kernel, layer=layer),
        grid_spec=pltpu.PrefetchScalarGridSpec(
            num_scalar_prefetch=5, grid=(nt,),
            in_specs=[pl.BlockSpec((TM_E * SUB, LANES), tile),
                      pl.BlockSpec(memory_space=pl.ANY),
                      pl.BlockSpec((None, None, 1, 2 * D_FF), bmap),
                      pl.BlockSpec(memory_space=pl.ANY),
                      pl.BlockSpec((None, None, 1, d), bmap)],
            out_specs=pl.BlockSpec((TM_E * SUB, LANES), otile),
            scratch_shapes=[pltpu.VMEM((2, d, 2 * D_FF), F32), pltpu.VMEM((2, D_FF, d), F32),
                            pltpu.VMEM((d, 2 * D_FF), BF16), pltpu.VMEM((D_FF, d), BF16),
                            pltpu.SemaphoreType.DMA((2,))]),
        out_shape=jax.ShapeDtypeStruct(xs.shape, F32),
        compiler_params=_cparams("arbitrary"), name="moe_experts",
    )(plan["tile_expert"], plan["first"], plan["n_tiles"], plan["tile_next"], plan["tile_parity"],
      xs, w1, b1, w2, b2)


def _combine_kernel(dest_ref, x1_ref, tw_ref, mod_ref, ys_ref, *rest):
    *o_refs, buf, sem = rest
    i = pl.program_id(0)
    j = i - 1
    n = pl.num_programs(0) - 1
    tile_rows = TM * SUB

    for s in range(2):
        @pl.when(jnp.logical_and(i < n, i % 2 == s))
        def _():
            def issue(t, carry):
                dst = pl.multiple_of(t * SUB, SUB)
                for kk in range(TOP_K):
                    row = pl.multiple_of(dest_ref[0, t * TOP_K + kk], SUB)
                    pltpu.make_async_copy(ys_ref.at[pl.ds(row, SUB), :], buf.at[s, kk, pl.ds(dst, SUB), :],
                                          sem.at[s]).start(priority=kk % 2)
                return carry

            lax.fori_loop(0, TM, issue, 0, unroll=2)

    @pl.when(j >= 0)
    def _():
        slot = j % 2
        for kk in range(TOP_K):
            pltpu.make_async_copy(ys_ref.at[pl.ds(0, tile_rows), :], buf.at[slot, kk], sem.at[slot]).wait()
        tw = tw_ref[...]
        y = tw[:, 0:1] * _load_token_major(buf, (slot, 0), TM)
        for kk in range(1, TOP_K):
            y = y + tw[:, kk:kk + 1] * _load_token_major(buf, (slot, kk), TM)
        out = x1_ref[...] + mod_ref[5:6, :] * y
        if len(o_refs) == 1:
            o_refs[0][...] = out
        else:
            @pl.when(j < N_CTX_TILES)
            def _():
                o_refs[0][...] = out

            @pl.when(j >= N_CTX_TILES)
            def _():
                o_refs[1][...] = out


def _combine(ys, dest, x1, tw, mod, split):
    t, d = x1.shape
    nblk = t // TM
    dest3 = dest.reshape(nblk, 1, TM * TOP_K)
    prev = lambda i: jnp.maximum(i - 1, 0)
    row = lambda w: pl.BlockSpec((TM, w), lambda i: (prev(i), 0))
    return pl.pallas_call(
        _combine_kernel, grid=(nblk + 1,),
        in_specs=[pl.BlockSpec((None, 1, TM * TOP_K), lambda i: (jnp.minimum(i, nblk - 1), 0, 0), memory_space=pltpu.SMEM),
                  row(d), row(LANES),
                  pl.BlockSpec((None, 6, d), lambda i: (_cond_row(prev(i)), 0, 0)),
                  pl.BlockSpec(memory_space=pl.ANY)],
        out_specs=_Rows(x1, x1, 0).specs(lag=1) if split else row(d),
        out_shape=([jax.ShapeDtypeStruct((N_CTX_TILES * TM, d), F32),
                    jax.ShapeDtypeStruct(((N_TILES - N_CTX_TILES) * TM, d), F32)] if split
                   else jax.ShapeDtypeStruct((t, d), F32)),
        scratch_shapes=[pltpu.VMEM((2, TOP_K, TM * SUB, LANES), F32), pltpu.SemaphoreType.DMA((2,))],
        compiler_params=_cparams("arbitrary"), name="moe_combine",
    )(dest3, x1, tw, mod, ys)


def _moe(x1, h2, ti, tw, rank, cnt, mod, layer, w1, b1, w2, b2, split):
    depth = w1.shape[0]
    assert x1.shape[0] == N_TILES * TM
    dest, plan = _route_plan(cnt, ti, rank)
    xs = _dispatch(h2, dest, plan, N_TILES_MAX * TM_E)
    ys = _experts(xs, plan, layer, w1, b1.reshape(depth, N_EXP, 1, -1), w2, b2.reshape(depth, N_EXP, 1, -1))
    return _combine(ys, dest, x1, tw, mod, split)


def _pad_lanes(a, value=0.0):
    return jnp.pad(a, ((0, 0), (0, LANES - a.shape[1])), constant_values=value)


def kernel(x_prompt, x_sample, cache_na_k, cache_na_v, state_mlstm_C, state_mlstm_n, state_mlstm_m, state_ret_S, c, c_ctx, w_mod, b_mod, w_in_even, mlstm_gate_b, na_q_norm, na_k_norm, na_rpb, mlstm_norm, w_out_even, w_in_odd, ret_decay, ret_norm, w_out_odd, w_router, b_router, w_moe_in, b_moe_in, w_moe_out, b_moe_out):
    nb_c, s_c, d = x_prompt.shape
    nb_l, s_l, _ = x_sample.shape
    t_c = nb_c * s_c
    t_l = nb_l * s_l
    assert t_c == 4 * SEG and s_l == SEG and d == D_MODEL
    depth = w_mod.shape[0]
    dt = x_prompt.dtype

    x = _Rows(x_prompt.reshape(t_c, d), x_sample.reshape(t_l, d), 0)
    cond = jnp.concatenate([c_ctx[None, :], c, jnp.zeros((N_COND - 1 - nb_l, d), F32)], axis=0)
    mod = _modulation(cond, w_mod, b_mod).reshape(depth, N_COND, 6, d)

    outs = {}
    for l in range(depth):
        e = l // 2
        mod_l = mod[l]
        if l % 2 == 0:
            w_in = w_in_even[e]
            n_main = 3 * W_A + 4 * W_B
            wg = _pad_lanes(w_in[:, n_main:])
            bg = _pad_lanes(mlstm_gate_b[e].reshape(1, 4 * H_B))
            z, g = _inproj(x, mod_l, w_in.astype(BF16), n_main, wg, bg)
            qn = na_q_norm[e].reshape(1, HD_A)
            kn = na_k_norm[e].reshape(1, HD_A)
            oa_c, ka_c = _ctx_attention(z, nb_c, s_c, qn, kn)
            past = cache_na_k.shape[2]
            oa_l = _na_attention(z, t_c, nb_l, s_l,
                                 cache_na_k[:, e].reshape(nb_l, past, W_A), cache_na_v[:, e].reshape(nb_l, past, W_A),
                                 _na_bias_table(na_rpb[e]), qn, kn)
            nw = mlstm_norm[e].reshape(1, W_B)
            hm_c, c_fin, n_fin, m_fin = _mlstm(z, g, 0, nb_c, s_c, nw)
            init = (state_mlstm_C[:, e].reshape(nb_l, 2 * H_B, HD_B, HD_B),
                    state_mlstm_n[:, e].reshape(nb_l, 2 * H_B, HD_B),
                    jnp.broadcast_to(state_mlstm_m[:, e].reshape(nb_l, 2 * H_B, 1), (nb_l, 2 * H_B, LANES)))
            hm_l = _mlstm(z, g, t_c, nb_l, s_l, nw, init)[0]
            a = _Rows(oa_c, oa_l, 0)
            b = _Rows(hm_c, hm_l, 0)
            w_out = w_out_even[e].astype(BF16)
            outs.setdefault("na_k", []).append(ka_c.reshape(nb_c, s_c, H_A, HD_A))
            outs.setdefault("na_v", []).append(z[:t_c, 2 * W_A:3 * W_A].reshape(nb_c, s_c, H_A, HD_A))
            outs.setdefault("C", []).append(c_fin.reshape(nb_c, 2, H_B, HD_B, HD_B))
            outs.setdefault("n", []).append(n_fin.reshape(nb_c, 2, H_B, HD_B))
            outs.setdefault("m", []).append(m_fin[:, :, 0].reshape(nb_c, 2, H_B))
        else:
            w_c = H_C * HD_C
            z = _inproj(x, mod_l, w_in_odd[e].astype(BF16), 4 * w_c + N_FG * FG_W)
            dl_rep = jnp.broadcast_to(ret_decay[e].reshape(2 * H_C, 1), (2 * H_C, LANES))
            nw = ret_norm[e].reshape(1, w_c)
            hr_c, s_fin = _retention(z, 0, nb_c, s_c, dl_rep, nw)
            hr_l = _retention(z, t_c, nb_l, s_l, dl_rep, nw, rope=_rope_tables(s_l),
                              init=state_ret_S[:, e].reshape(nb_l, 2 * H_C, HD_C, HD_C))[0]
            fd_c = _fnet(z, 0, nb_c, s_c)
            fd_l = _fnet(z, t_c, nb_l, s_l)
            a = _Rows(hr_c, hr_l, 0)
            b = _Rows(fd_c, fd_l, 0)
            w_out = w_out_odd[e].astype(BF16)
            outs.setdefault("S", []).append(s_fin.reshape(nb_c, 2, H_C, HD_C, HD_C))
        wr = _pad_lanes(w_router[l])
        br = _pad_lanes(b_router[l].reshape(1, N_EXP), NEG)
        x1, h2, ti, tw, rank, cnt = _outproj_router(a, b, x, mod_l, w_out, wr, br)
        last = l == depth - 1
        x = _moe(x1, h2, ti, tw, rank, cnt, mod_l, l, w_moe_in, b_moe_in, w_moe_out, b_moe_out, split=last)
        if not last:
            x = _whole(x)

    y_prompt = x[0].reshape(nb_c, s_c, d)
    y_sample = x[1].reshape(nb_l, s_l, d)
    stack = lambda key: jnp.stack(outs[key], axis=1).astype(dt)
    return (y_prompt, y_sample, stack("na_k"), stack("na_v"), stack("C"), stack("n"), stack("m"), stack("S"))
```

```python
import functools
from typing import NamedTuple

import numpy as np
import jax
import jax.numpy as jnp
from jax import lax
from jax.experimental import pallas as pl
from jax.experimental.pallas import tpu as pltpu

F32 = jnp.float32
BF16 = jnp.bfloat16
HIGHEST = lax.Precision.HIGHEST

D_MODEL = 1024
GRID_W = 64
WIN_R = 8
WIN_C = 16
H_A, HD_A = 8, 64
H_B, HD_B = 4, 128
H_C, HD_C = 4, 128
N_FG, FG_W = 4, 128
W_A = H_A * HD_A
W_B = H_B * HD_B
N_EXP = 32
TOP_K = 4
D_FF = D_MODEL
SWIGLU_LIMIT = 7.0
SWIGLU_ALPHA = 1.702
CHUNK = 128
ROPE_BASE = 10000.0
EPS = 1e-6

LANES = 128
SEG = 1024
N_COND = 8
TM = 512
N_TILES = 8 * SEG // TM
N_CTX_TILES = 4 * SEG // TM
TM_E = 256
NEG = -1e30
VMEM_LIMIT = 56 * 1024 * 1024


def _cparams(*sem):
    return pltpu.CompilerParams(dimension_semantics=sem, vmem_limit_bytes=VMEM_LIMIT)


def _cond_row(i):
    return jnp.maximum((i * TM) // SEG - 3, 0)


def _log_sigmoid(x):
    return jnp.minimum(x, 0.0) - jnp.log1p(jnp.exp(-jnp.abs(x)))


def _dot(a, b):
    return jnp.dot(a, b, preferred_element_type=F32)


def _dot_nt(a, b):
    return lax.dot_general(a, b, (((1,), (1,)), ((), ())), preferred_element_type=F32)


def _dot_hi(a, b):
    return jnp.dot(a, b, precision=HIGHEST, preferred_element_type=F32)


def _split_bf16(x):
    hi = x.astype(BF16)
    return hi, (x - hi.astype(F32)).astype(BF16)


def _dot_x3(x, w_hi, w_lo):
    x_hi, x_lo = _split_bf16(x)
    return _dot(x_hi, w_hi) + (_dot(x_lo, w_hi) + _dot(x_hi, w_lo))


def _mod_kernel(cond_ref, w_ref, b_ref, o_ref):
    c = cond_ref[...]
    s = c * jax.nn.sigmoid(c)
    o_ref[...] = _dot_hi(s, w_ref[...]) + b_ref[...]


def _modulation(cond, w_mod, b_mod):
    depth, d, n = w_mod.shape
    tn = 1536
    return pl.pallas_call(
        _mod_kernel,
        grid=(depth, n // tn),
        in_specs=[pl.BlockSpec((N_COND, d), lambda l, j: (0, 0)),
                  pl.BlockSpec((None, d, tn), lambda l, j: (l, 0, j)),
                  pl.BlockSpec((None, 1, tn), lambda l, j: (l, 0, j))],
        out_specs=pl.BlockSpec((None, N_COND, tn), lambda l, j: (l, 0, j)),
        out_shape=jax.ShapeDtypeStruct((depth, N_COND, n), F32),
        compiler_params=_cparams("arbitrary", "arbitrary"),
        name="modulation",
    )(cond, w_mod, b_mod.reshape(depth, 1, n))


def _rms_mod(x, shift, scale):
    h = x * lax.rsqrt(jnp.mean(x * x, axis=-1, keepdims=True) + EPS)
    return h * (1.0 + scale) + shift


class _Rows(NamedTuple):
    a: jax.Array
    b: jax.Array
    off_b: int

    def specs(self, lag=0):
        width = self.a.shape[1]
        off_b = self.off_b
        tile = (lambda i: i) if lag == 0 else (lambda i: jnp.maximum(i - lag, 0))
        return [pl.BlockSpec((TM, width), lambda i: (jnp.minimum(tile(i), N_CTX_TILES - 1), 0)),
                pl.BlockSpec((TM, width), lambda i: (jnp.maximum(tile(i) - N_CTX_TILES, 0) + off_b, 0))]


def _whole(x):
    return _Rows(x, x, N_CTX_TILES)


def _rows_read(tile, ref_a, ref_b):
    return jnp.where(tile < N_CTX_TILES, ref_a[...], ref_b[...])


def _inproj_kernel(xa_ref, xb_ref, mod_ref, w_ref, z_ref):
    x = _rows_read(pl.program_id(0), xa_ref, xb_ref)
    h = _rms_mod(x, mod_ref[0:1, :], mod_ref[1:2, :])
    z_ref[...] = _dot(h.astype(BF16), w_ref[...])


def _inproj_gate_kernel(xa_ref, xb_ref, mod_ref, w_ref, wgh_ref, wgl_ref, bg_ref, z_ref, g_ref):
    x = _rows_read(pl.program_id(0), xa_ref, xb_ref)
    h = _rms_mod(x, mod_ref[0:1, :], mod_ref[1:2, :])
    z_ref[...] = _dot(h.astype(BF16), w_ref[...])
    g_ref[...] = _dot_x3(h, wgh_ref[...], wgl_ref[...]) + bg_ref[...]


def _inproj(x, mod, w, n, wg=None, bg=None):
    t = N_TILES * TM
    d = x.a.shape[1]
    in_specs = x.specs() + [pl.BlockSpec((None, 6, d), lambda i: (_cond_row(i), 0, 0)),
                            pl.BlockSpec((d, n), lambda i: (0, 0))]
    z_spec = pl.BlockSpec((TM, n), lambda i: (i, 0))
    z_shape = jax.ShapeDtypeStruct((t, n), F32)
    if wg is None:
        return pl.pallas_call(
            _inproj_kernel, grid=(N_TILES,), in_specs=in_specs, out_specs=z_spec, out_shape=z_shape,
            compiler_params=_cparams("arbitrary"), name="inproj",
        )(x.a, x.b, mod, w)
    wg_hi, wg_lo = _split_bf16(wg)
    in_specs += [pl.BlockSpec((d, LANES), lambda i: (0, 0))] * 2 + [pl.BlockSpec((1, LANES), lambda i: (0, 0))]
    return pl.pallas_call(
        _inproj_gate_kernel, grid=(N_TILES,), in_specs=in_specs,
        out_specs=[z_spec, pl.BlockSpec((TM, LANES), lambda i: (i, 0))],
        out_shape=[z_shape, jax.ShapeDtypeStruct((t, LANES), F32)],
        compiler_params=_cparams("arbitrary"), name="inproj_gate",
    )(x.a, x.b, mod, w, wg_hi, wg_lo, bg)


def _head_rms(x, w):
    return x * lax.rsqrt(jnp.mean(x * x, axis=-1, keepdims=True) + EPS) * w


def _fold_lanes(x, op):
    parts = [x[:, c * LANES:(c + 1) * LANES] for c in range(x.shape[1] // LANES)]
    while len(parts) > 1:
        parts = [op(parts[c], parts[c + 1]) if c + 1 < len(parts) else parts[c] for c in range(0, len(parts), 2)]
    return parts[0]


def _head_group_matrix():
    head = np.arange(W_A) // HD_A
    return jnp.asarray((head[:, None] == head[None, :]) / HD_A, F32).astype(BF16)


def _heads_rms(x, w_row, g):
    hi, lo = _split_bf16(x * x)
    return x * lax.rsqrt(_dot(hi, g) + _dot(lo, g) + EPS) * w_row


def _ctx_attn_kernel(q_ref, k_ref, v_ref, qn_ref, kn_ref, g_ref, o_ref, ko_ref, km_s, vm_s):
    seq = q_ref.shape[0]
    scale = HD_A ** -0.5
    g = g_ref[...]
    q = _heads_rms(q_ref[...], qn_ref[...] * scale, g).astype(BF16)
    k = _heads_rms(k_ref[...], kn_ref[...], g)
    ko_ref[...] = k
    kb = k.astype(BF16)
    vb = v_ref[...].astype(BF16)
    head = lax.broadcasted_iota(jnp.int32, (seq, W_A), 1) // HD_A
    zero = jnp.zeros((seq, W_A), BF16)
    for h in range(H_A):
        km_s[h * seq:(h + 1) * seq, :] = jnp.where(head == h, kb, zero)
        vm_s[h * seq:(h + 1) * seq, :] = jnp.where(head == h, vb, zero)
    s = _dot_nt(q, km_s[...])
    probs = []
    for h in range(H_A):
        s_h = s[:, h * seq:(h + 1) * seq]
        p = jnp.exp(s_h - jnp.max(_fold_lanes(s_h, jnp.maximum), axis=-1, keepdims=True))
        den = jnp.sum(_fold_lanes(p, jnp.add), axis=-1, keepdims=True)
        probs.append((p / den).astype(BF16))
    o_ref[...] = _dot(jnp.concatenate(probs, axis=1), vm_s[...])


def _ctx_attention(z, n_batch, seq, qn, kn):
    spec = lambda c: pl.BlockSpec((seq, W_A), lambda b: (b, c))
    wspec = pl.BlockSpec((1, W_A), lambda b: (0, 0))
    out = jax.ShapeDtypeStruct((n_batch * seq, W_A), F32)
    return pl.pallas_call(
        _ctx_attn_kernel, grid=(n_batch,),
        in_specs=[spec(0), spec(1), spec(2), wspec, wspec, pl.BlockSpec((W_A, W_A), lambda b: (0, 0))],
        out_specs=[pl.BlockSpec((seq, W_A), lambda b: (b, 0))] * 2,
        out_shape=[out, out],
        scratch_shapes=[pltpu.VMEM((H_A * seq, W_A), BF16), pltpu.VMEM((H_A * seq, W_A), BF16)],
        compiler_params=_cparams("arbitrary"), name="ctx_attention",
    )(z, z, z, qn, kn, _head_group_matrix())


def _na_bias_table(rpb):
    qc = np.arange(GRID_W)
    kc = np.arange(GRID_W)
    cstart = np.clip(qc - WIN_C // 2, 0, GRID_W - WIN_C)
    col_in = (kc[None, :] >= cstart[:, None]) & (kc[None, :] < cstart[:, None] + WIN_C)
    dc = np.clip(kc[None, :] - qc[:, None], 1 - WIN_C, WIN_C - 1) + WIN_C - 1
    cls = np.arange(WIN_R)
    j = np.arange(WIN_R)
    dr = j[None, :] - cls[:, None] + WIN_R - 1
    sel_r = jnp.asarray(dr[:, :, None] == np.arange(2 * WIN_R - 1)[None, None, :], F32)
    sel_c = jnp.asarray(dc[:, :, None] == np.arange(2 * WIN_C - 1)[None, None, :], F32)
    tab = jnp.einsum("hab,cja,qkb->hcqjk", rpb, sel_r, sel_c, precision=HIGHEST)
    tab = jnp.where(jnp.asarray(col_in)[None, None, :, None, :], tab, NEG)
    return tab.reshape(H_A, WIN_R, GRID_W, WIN_R * GRID_W)


def _na_kernel(q_ref, k_ref, v_ref, kc_ref, vc_ref, bias_ref, qn_ref, kn_ref, g_ref, o_ref,
               kn_s, v_s, kc_s, vc_s, *, rows):
    r = pl.program_id(1)
    scale = HD_A ** -0.5

    @pl.when(r == 0)
    def _():
        kn_s[...] = _heads_rms(k_ref[...], kn_ref[...], g_ref[...]).astype(BF16)
        v_s[...] = v_ref[...].astype(BF16)
        kc_s[...] = kc_ref[...].astype(BF16)
        vc_s[...] = vc_ref[...].astype(BF16)

    rs = jnp.clip(r - WIN_R // 2, 0, rows - WIN_R)
    start = pl.multiple_of(rs * GRID_W, GRID_W)
    n_loc = WIN_R * GRID_W
    for h in range(H_A):
        sl = slice(h * HD_A, (h + 1) * HD_A)
        q = (_head_rms(q_ref[:, sl], qn_ref[:, sl]) * scale).astype(BF16)
        s_loc = _dot_nt(q, kn_s[pl.ds(start, n_loc), sl]) + bias_ref[h]
        s_ctx = _dot_nt(q, kc_s[:, sl])
        m = jnp.max(jnp.maximum(_fold_lanes(s_loc, jnp.maximum), _fold_lanes(s_ctx, jnp.maximum)),
                    axis=-1, keepdims=True)
        p_loc = jnp.exp(s_loc - m)
        p_ctx = jnp.exp(s_ctx - m)
        den = jnp.sum(_fold_lanes(p_loc, jnp.add) + _fold_lanes(p_ctx, jnp.add), axis=-1, keepdims=True)
        o = _dot(p_loc.astype(BF16), v_s[pl.ds(start, n_loc), sl]) + _dot(p_ctx.astype(BF16), vc_s[:, sl])
        o_ref[:, sl] = o / den


def _na_attention(z, row0, n_batch, seq, kc, vc, bias, qn, kn):
    rows = seq // GRID_W
    past = kc.shape[1]
    blk0 = row0 // GRID_W
    sblk0 = row0 // seq

    def cls_of(r):
        return r - jnp.clip(r - WIN_R // 2, 0, rows - WIN_R)

    full = lambda c: pl.BlockSpec((seq, W_A), lambda b, r: (sblk0 + b, c))
    cspec = pl.BlockSpec((None, past, W_A), lambda b, r: (b, 0, 0))
    wspec = pl.BlockSpec((1, W_A), lambda b, r: (0, 0))
    return pl.pallas_call(
        functools.partial(_na_kernel, rows=rows), grid=(n_batch, rows),
        in_specs=[pl.BlockSpec((GRID_W, W_A), lambda b, r: (blk0 + b * rows + r, 0)),
                  full(1), full(2), cspec, cspec,
                  pl.BlockSpec((H_A, None, GRID_W, WIN_R * GRID_W), lambda b, r: (0, cls_of(r), 0, 0)),
                  wspec, wspec, pl.BlockSpec((W_A, W_A), lambda b, r: (0, 0))],
        out_specs=pl.BlockSpec((GRID_W, W_A), lambda b, r: (b * rows + r, 0)),
        out_shape=jax.ShapeDtypeStruct((n_batch * seq, W_A), F32),
        scratch_shapes=[pltpu.VMEM((seq, W_A), BF16), pltpu.VMEM((seq, W_A), BF16),
                        pltpu.VMEM((past, W_A), BF16), pltpu.VMEM((past, W_A), BF16)],
        compiler_params=_cparams("arbitrary", "arbitrary"), name="na_attention",
    )(z, z, z, kc, vc, bias, qn, kn, _head_group_matrix())


def _tri_masks():
    li = lax.broadcasted_iota(jnp.int32, (CHUNK, CHUNK), 0)
    si = lax.broadcasted_iota(jnp.int32, (CHUNK, CHUNK), 1)
    return li >= si, li <= si


def _mlstm_kernel(*refs, nc, has_init):
    if has_init:
        (q_ref, k_ref, v_ref, og_ref, g_ref, nw_ref, c0_ref, n0_ref, m0_ref,
         o_ref, cf_ref, nf_ref, mf_ref, h_s, c_s, n_s, m_s) = refs
    else:
        (q_ref, k_ref, v_ref, og_ref, g_ref, nw_ref,
         o_ref, cf_ref, nf_ref, mf_ref, h_s, c_s, n_s, m_s) = refs
    nd = 2 * H_B
    if has_init:
        c_s[...] = c0_ref[...]
        n_s[...] = n0_ref[...]
        m_s[...] = m0_ref[...]
    else:
        c_s[...] = jnp.zeros_like(c_s)
        n_s[...] = jnp.zeros_like(n_s)
        m_s[...] = jnp.zeros_like(m_s)

    causal, anti = _tri_masks()
    tri_f = causal.astype(F32)
    tri_b = anti.astype(F32)
    kscale = HD_B ** -0.5

    def chunk_step(c, carry):
        for d in range(2):
            cc = c if d == 0 else nc - 1 - c
            t0 = pl.multiple_of(cc * CHUNK, CHUNK)
            g = g_ref[pl.ds(t0, CHUNK), :]
            gt = g.T
            ls = _log_sigmoid(g)
            lst = _log_sigmoid(gt)
            tri_c, tri_r, mask = (tri_f, tri_b, causal) if d == 0 else (tri_b, tri_f, anti)
            b_cols = _dot_hi(tri_c, ls)
            b_rows = _dot_hi(lst, tri_r)
            last = CHUNK - 1 if d == 0 else 0
            for h in range(H_B):
                ci = (2 * d) * H_B + h
                cf = (2 * d + 1) * H_B + h
                hs = slice(h * HD_B, (h + 1) * HD_B)
                q = q_ref[pl.ds(t0, CHUNK), hs]
                k = k_ref[pl.ds(t0, CHUNK), hs] * kscale
                v = v_ref[pl.ds(t0, CHUNK), hs]
                qb, kb, vb = q.astype(BF16), k.astype(BF16), v.astype(BF16)
                b_col = b_cols[:, cf:cf + 1]
                b_row = b_rows[cf:cf + 1, :]
                i_row = gt[ci:ci + 1, :]
                sidx = d * H_B + h
                cst = c_s[sidx]
                nst = n_s[sidx:sidx + 1, :]
                mst = m_s[sidx:sidx + 1, 0:1]
                dmat = jnp.where(mask, b_col - b_row + i_row, -jnp.inf)
                inter = b_col + mst
                mt = jnp.maximum(inter, jnp.max(dmat, axis=-1, keepdims=True))
                w = jnp.exp(dmat - mt) * _dot_nt(qb, kb)
                a = jnp.exp(inter - mt)
                num = _dot(w.astype(BF16), vb) + _dot(qb, cst.astype(BF16)) * a
                den = jnp.sum(w, axis=-1, keepdims=True) + a * jnp.sum(q * nst, axis=-1, keepdims=True)
                hc = num / jnp.maximum(jnp.abs(den), jnp.exp(-mt))
                h_s[d, pl.ds(t0, CHUNK), hs] = hc
                bl = b_row[:, last:last + 1]
                dl = bl - b_row + i_row
                m_new = jnp.maximum(bl + mst, jnp.max(dl, axis=-1, keepdims=True))
                wl = jnp.exp(dl - m_new)
                dec = jnp.exp(bl + mst - m_new)
                kw = (k.T * wl).astype(BF16)
                c_s[sidx] = dec * cst + _dot(kw, vb)
                wl8 = jnp.broadcast_to(wl, (8, CHUNK)).astype(BF16)
                n_s[sidx:sidx + 1, :] = dec * nst + _dot(wl8, kb)[0:1, :]
                m_s[sidx:sidx + 1, :] = jnp.broadcast_to(m_new, (1, LANES))
        return carry

    lax.fori_loop(0, nc, chunk_step, 0)

    for h in range(H_B):
        hs = slice(h * HD_B, (h + 1) * HD_B)
        hsum = h_s[0, :, hs] + h_s[1, :, hs]
        o_ref[:, hs] = _head_rms(hsum, nw_ref[:, hs]) * jax.nn.sigmoid(og_ref[:, hs])
    cf_ref[...] = c_s[...]
    nf_ref[...] = n_s[...]
    mf_ref[...] = m_s[...]


def _mlstm(z, g, row0, n_batch, seq, norm_w, init=None):
    sblk0 = row0 // seq
    nd = 2 * H_B
    spec = lambda c: pl.BlockSpec((seq, W_B), lambda b: (sblk0 + b, c))
    in_specs = [spec(3), spec(4), spec(5), spec(6),
                pl.BlockSpec((seq, LANES), lambda b: (sblk0 + b, 0)),
                pl.BlockSpec((1, W_B), lambda b: (0, 0))]
    args = [z, z, z, z, g, norm_w]
    st_specs = [pl.BlockSpec((None, nd, HD_B, HD_B), lambda b: (b, 0, 0, 0)),
                pl.BlockSpec((None, nd, HD_B), lambda b: (b, 0, 0)),
                pl.BlockSpec((None, nd, LANES), lambda b: (b, 0, 0))]
    if init is not None:
        in_specs += st_specs
        args += list(init)
    return pl.pallas_call(
        functools.partial(_mlstm_kernel, nc=seq // CHUNK, has_init=init is not None), grid=(n_batch,),
        in_specs=in_specs,
        out_specs=[pl.BlockSpec((seq, W_B), lambda b: (b, 0))] + st_specs,
        out_shape=[jax.ShapeDtypeStruct((n_batch * seq, W_B), F32),
                   jax.ShapeDtypeStruct((n_batch, nd, HD_B, HD_B), F32),
                   jax.ShapeDtypeStruct((n_batch, nd, HD_B), F32),
                   jax.ShapeDtypeStruct((n_batch, nd, LANES), F32)],
        scratch_shapes=[pltpu.VMEM((2, seq, W_B), F32), pltpu.VMEM((nd, HD_B, HD_B), F32),
                        pltpu.VMEM((nd, HD_B), F32), pltpu.VMEM((nd, LANES), F32)],
        compiler_params=_cparams("arbitrary"), name="mlstm",
    )(*args)


def _rope_tables(seq):
    half = HD_C // 2
    quarter = half // 2
    t = np.arange(seq)
    inv = ROPE_BASE ** (-np.arange(0, half, 2, dtype=np.float64) / half)
    ang_r = (t // GRID_W)[:, None] * inv[None, :]
    ang_c = (t % GRID_W)[:, None] * inv[None, :]
    cos_t = np.concatenate([np.cos(ang_r), np.cos(ang_r), np.cos(ang_c), np.cos(ang_c)], -1)
    sin_t = np.concatenate([-np.sin(ang_r), np.sin(ang_r), -np.sin(ang_c), np.sin(ang_c)], -1)
    assert cos_t.shape == (seq, 4 * quarter)
    return jnp.asarray(cos_t, F32), jnp.asarray(sin_t, F32)


def _rope(x, cos_t, sin_t):
    quarter = HD_C // 4
    lane = lax.broadcasted_iota(jnp.int32, x.shape, 1)
    first = (lane % (2 * quarter)) < quarter
    swapped = jnp.where(first, pltpu.roll(x, HD_C - quarter, 1), pltpu.roll(x, quarter, 1))
    return x * cos_t + swapped * sin_t


def _ret_kernel(*refs, nc, has_init, use_rope):
    refs = list(refs)
    q_ref, k_ref, v_ref, gg_ref, dl_ref, nw_ref = refs[:6]
    pos = 6
    if use_rope:
        cos_ref, sin_ref = refs[pos:pos + 2]
        pos += 2
    if has_init:
        s0_ref = refs[pos]
        pos += 1
    o_ref, sf_ref, h_s, s_s, dk_s = refs[pos:pos + 5]
    if has_init:
        s_s[...] = s0_ref[...]
    else:
        s_s[...] = jnp.zeros_like(s_s)

    causal, anti = _tri_masks()
    li = lax.broadcasted_iota(jnp.int32, (CHUNK, CHUNK), 0).astype(F32)
    si = lax.broadcasted_iota(jnp.int32, (CHUNK, CHUNK), 1).astype(F32)
    lg_all = _log_sigmoid(dl_ref[...])
    kscale = HD_C ** -0.5
    for d in range(2):
        for h in range(H_C):
            sidx = d * H_C + h
            lg = lg_all[sidx:sidx + 1, :]
            if d == 0:
                dk_s[sidx, 0] = jnp.exp(jnp.where(causal, (li - si) * lg, -jnp.inf))
                dk_s[sidx, 1] = jnp.exp((li + 1.0) * lg)
                dk_s[sidx, 2] = jnp.exp((CHUNK - 1.0 - li) * lg)
            else:
                dk_s[sidx, 0] = jnp.exp(jnp.where(anti, (si - li) * lg, -jnp.inf))
                dk_s[sidx, 1] = jnp.exp((CHUNK - li) * lg)
                dk_s[sidx, 2] = jnp.exp(li * lg)

    def chunk_step(c, carry):
        for d in range(2):
            cc = c if d == 0 else nc - 1 - c
            t0 = pl.multiple_of(cc * CHUNK, CHUNK)
            for h in range(H_C):
                sidx = d * H_C + h
                hs = slice(h * HD_C, (h + 1) * HD_C)
                lg = lg_all[sidx:sidx + 1, :]
                q = q_ref[pl.ds(t0, CHUNK), hs]
                k = k_ref[pl.ds(t0, CHUNK), hs] * kscale
                v = v_ref[pl.ds(t0, CHUNK), hs]
                if use_rope:
                    cos_t = cos_ref[pl.ds(t0, CHUNK), :]
                    sin_t = sin_ref[pl.ds(t0, CHUNK), :]
                    q = _rope(q, cos_t, sin_t)
                    k = _rope(k, cos_t, sin_t)
                decay, q_dec, k_dec = dk_s[sidx, 0], dk_s[sidx, 1], dk_s[sidx, 2]
                c_dec = jnp.exp(CHUNK * lg)
                st = s_s[sidx]
                qb, kb, vb = q.astype(BF16), k.astype(BF16), v.astype(BF16)
                att = _dot_nt(qb, kb) * decay
                o = _dot(att.astype(BF16), vb) + _dot(qb, st.astype(BF16)) * q_dec
                h_s[d, pl.ds(t0, CHUNK), hs] = o
                kd = (k * k_dec).T.astype(BF16)
                s_s[sidx] = c_dec * st + _dot(kd, vb)
        return carry

    lax.fori_loop(0, nc, chunk_step, 0)

    for h in range(H_C):
        hs = slice(h * HD_C, (h + 1) * HD_C)
        osum = h_s[0, :, hs] + h_s[1, :, hs]
        gg = gg_ref[:, hs]
        o_ref[:, hs] = _head_rms(osum, nw_ref[:, hs]) * (gg * jax.nn.sigmoid(gg))
    sf_ref[...] = s_s[...]


def _retention(z, row0, n_batch, seq, decay_rep, norm_w, rope=None, init=None):
    sblk0 = row0 // seq
    nd = 2 * H_C
    w_c = H_C * HD_C
    spec = lambda c: pl.BlockSpec((seq, w_c), lambda b: (sblk0 + b, c))
    in_specs = [spec(0), spec(1), spec(2), spec(3),
                pl.BlockSpec((nd, LANES), lambda b: (0, 0)),
                pl.BlockSpec((1, w_c), lambda b: (0, 0))]
    args = [z, z, z, z, decay_rep, norm_w]
    if rope is not None:
        in_specs += [pl.BlockSpec((seq, HD_C), lambda b: (0, 0))] * 2
        args += list(rope)
    st_spec = pl.BlockSpec((None, nd, HD_C, HD_C), lambda b: (b, 0, 0, 0))
    if init is not None:
        in_specs.append(st_spec)
        args.append(init)
    return pl.pallas_call(
        functools.partial(_ret_kernel, nc=seq // CHUNK, has_init=init is not None, use_rope=rope is not None),
        grid=(n_batch,), in_specs=in_specs,
        out_specs=[pl.BlockSpec((seq, w_c), lambda b: (b, 0)), st_spec],
        out_shape=[jax.ShapeDtypeStruct((n_batch * seq, w_c), F32),
                   jax.ShapeDtypeStruct((n_batch, nd, HD_C, HD_C), F32)],
        scratch_shapes=[pltpu.VMEM((2, seq, w_c), F32), pltpu.VMEM((nd, HD_C, HD_C), F32),
                        pltpu.VMEM((nd, 3, CHUNK, CHUNK), F32)],
        compiler_params=_cparams("arbitrary"), name="retention",
    )(*args)


def _dft_tables(n):
    idx = (np.arange(n)[:, None] * np.arange(n)[None, :]) % n
    ang = 2.0 * np.pi * idx / n
    return np.cos(ang) / np.sqrt(n), np.sin(ang) / np.sqrt(n)


def _fnet_kernel(x_ref, cw_ref, sw_ref, cs_ref, ss_ref, o_ref):
    for g in range(N_FG):
        gs = slice(g * FG_W, (g + 1) * FG_W)
        x = x_ref[:, gs].astype(BF16)
        xc = _dot(x, cw_ref[...]).astype(BF16)
        xs = _dot(x, sw_ref[...]).astype(BF16)
        o_ref[:, gs] = _dot(cs_ref[...], xc) - _dot(ss_ref[...], xs)


def _fnet(z, row0, n_batch, seq):
    sblk0 = row0 // seq
    w_d = N_FG * FG_W
    cw, sw = _dft_tables(FG_W)
    cs, ss = _dft_tables(seq)
    tabs = [jnp.asarray(a, F32).astype(BF16) for a in (cw, sw, cs, ss)]
    wspec = pl.BlockSpec((FG_W, FG_W), lambda b: (0, 0))
    sspec = pl.BlockSpec((seq, seq), lambda b: (0, 0))
    return pl.pallas_call(
        _fnet_kernel, grid=(n_batch,),
        in_specs=[pl.BlockSpec((seq, w_d), lambda b: (sblk0 + b, 4)), wspec, wspec, sspec, sspec],
        out_specs=pl.BlockSpec((seq, w_d), lambda b: (b, 0)),
        out_shape=jax.ShapeDtypeStruct((n_batch * seq, w_d), F32),
        compiler_params=_cparams("arbitrary"), name="fnet",
    )(z, *tabs)


SUB = 8
assert D_MODEL == SUB * LANES


def _store_token_major(ref, lead, val):
    n = val.shape[0]
    for c in range(SUB):
        ref[lead + (pl.ds(c, n, stride=SUB), slice(None))] = val[:, c * LANES:(c + 1) * LANES]


def _load_token_major(ref, lead, n):
    return jnp.concatenate([ref[lead + (pl.ds(c, n, stride=SUB), slice(None))] for c in range(SUB)], axis=1)


def _outproj_router_kernel(aa_ref, ab_ref, ba_ref, bb_ref, xa_ref, xb_ref, mod_ref, wa_ref, wb_ref,
                           wrh_ref, wrl_ref, br_ref, x1_ref, h2_ref, ti_ref, tw_ref, rank_ref, cnt_ref, cnt_s):
    i = pl.program_id(0)
    a = _rows_read(i, aa_ref, ab_ref)
    b = _rows_read(i, ba_ref, bb_ref)
    y = _dot(a.astype(BF16), wa_ref[...]) + _dot(b.astype(BF16), wb_ref[...])
    x1 = _rows_read(i, xa_ref, xb_ref) + mod_ref[2:3, :] * y
    x1_ref[...] = x1
    h2 = _rms_mod(x1, mod_ref[3:4, :], mod_ref[4:5, :])
    _store_token_major(h2_ref, (), h2)
    logits = _dot_x3(h2, wrh_ref[...], wrl_ref[...]) + br_ref[...]
    lane = lax.broadcasted_iota(jnp.int32, logits.shape, 1)
    lane_f = lane.astype(F32)
    cur = logits
    vals, picks = [], []
    ti = jnp.zeros(logits.shape, jnp.int32)
    for kk in range(TOP_K):
        mx = jnp.max(cur, axis=-1, keepdims=True)
        idx = jnp.min(jnp.where(cur == mx, lane_f, float(LANES)), axis=-1, keepdims=True)
        ti = jnp.where(lane == kk, idx.astype(jnp.int32), ti)
        pick = lane_f == idx
        cur = jnp.where(pick, -jnp.inf, cur)
        vals.append(mx)
        picks.append(pick)
    es = [jnp.exp(v - vals[0]) for v in vals]
    tot = es[0] + es[1] + es[2] + es[3]
    tw = jnp.zeros(logits.shape, F32)
    for kk in range(TOP_K):
        tw = jnp.where(lane == kk, es[kk] / tot, tw)
    ti_ref[...] = ti
    tw_ref[...] = tw

    @pl.when(i == 0)
    def _():
        cnt_s[...] = jnp.zeros_like(cnt_s)

    onehot = jnp.zeros(logits.shape, F32)
    for pick in picks:
        onehot = onehot + jnp.where(pick, 1.0, 0.0)
    n = logits.shape[0]
    earlier = (lax.broadcasted_iota(jnp.int32, (n, n), 1) < lax.broadcasted_iota(jnp.int32, (n, n), 0))
    before = cnt_s[...] + _dot(jnp.where(earlier, 1.0, 0.0).astype(BF16), onehot.astype(BF16))
    rank = jnp.zeros(logits.shape, jnp.int32)
    for kk, pick in enumerate(picks):
        r_k = jnp.sum(jnp.where(pick, before, 0.0), axis=-1, keepdims=True)
        rank = jnp.where(lane == kk, r_k.astype(jnp.int32), rank)
    rank_ref[...] = rank
    cnt_s[...] = cnt_s[...] + jnp.sum(onehot, axis=0, keepdims=True)
    cnt_ref[...] = cnt_s[...]


def _outproj_router(a, b, x, mod, w_out, wr, br):
    t = N_TILES * TM
    d = x.a.shape[1]
    wid = a.a.shape[1]
    row = lambda w: pl.BlockSpec((TM, w), lambda i: (i, 0))
    const = lambda r, c: pl.BlockSpec((r, c), lambda i: (0, 0))
    wr_hi, wr_lo = _split_bf16(wr)
    return pl.pallas_call(
        _outproj_router_kernel, grid=(N_TILES,),
        in_specs=a.specs() + b.specs() + x.specs() + [
                  pl.BlockSpec((None, 6, d), lambda i: (_cond_row(i), 0, 0)),
                  pl.BlockSpec((wid, d), lambda i: (0, 0)), pl.BlockSpec((wid, d), lambda i: (1, 0)),
                  const(d, LANES), const(d, LANES), const(1, LANES)],
        out_specs=[row(d), pl.BlockSpec((TM * SUB, LANES), lambda i: (i, 0)), row(LANES), row(LANES), row(LANES),
                   const(1, LANES)],
        out_shape=[jax.ShapeDtypeStruct((t, d), F32), jax.ShapeDtypeStruct((t * SUB, LANES), F32),
                   jax.ShapeDtypeStruct((t, LANES), jnp.int32), jax.ShapeDtypeStruct((t, LANES), F32),
                   jax.ShapeDtypeStruct((t, LANES), jnp.int32), jax.ShapeDtypeStruct((1, LANES), F32)],
        scratch_shapes=[pltpu.VMEM((1, LANES), F32)],
        compiler_params=_cparams("arbitrary"), name="outproj_router",
    )(a.a, a.b, b.a, b.b, x.a, x.b, mod, w_out, w_out, wr_hi, wr_lo, br)


N_TILES_MAX = N_TILES * TM * TOP_K // TM_E + N_EXP
PLAN_LANES = 2 * LANES
assert N_TILES_MAX <= PLAN_LANES and N_EXP <= LANES


def _plan_kernel(cnt_ref, ti_ref, rank_ref, dest_ref, meta_ref, start_s):
    i = pl.program_id(0)

    @pl.when(i == 0)
    def _():
        cnt = cnt_ref[...]
        tiles = jnp.floor((cnt + float(TM_E - 1)) * (1.0 / TM_E))
        sub = lax.broadcasted_iota(jnp.int32, (LANES, LANES), 0)
        lane = lax.broadcasted_iota(jnp.int32, (LANES, LANES), 1)
        upto = jnp.where(sub <= lane, 1.0, 0.0).astype(BF16)
        tile_end = _dot(jnp.broadcast_to(tiles, (SUB, LANES)).astype(BF16), upto)[0:1, :]
        tile_start = tile_end - tiles
        start_s[...] = tile_start * float(TM_E * SUB)
        n_tiles = jnp.max(tile_end, axis=-1, keepdims=True)
        used = tiles > 0.0

        def column(row):
            return jnp.sum(jnp.where(sub == lane, jnp.broadcast_to(row, (LANES, LANES)), 0.0), axis=-1, keepdims=True)

        end_c, start_c, tiles_c = column(tile_end), column(tile_start), column(tiles)
        used_b = jnp.broadcast_to(jnp.where(used, 1.0, 0.0), (LANES, LANES))
        pos_c = jnp.sum(jnp.where(lane <= sub, used_b, 0.0), axis=-1, keepdims=True)
        par_c = (pos_c - 1.0) - 2.0 * jnp.floor((pos_c - 1.0) * 0.5)
        nxt_c = jnp.min(jnp.where(jnp.logical_and(lane > sub, used_b > 0.0), lane.astype(F32), float(LANES)),
                        axis=-1, keepdims=True)
        nxt_c = jnp.where(nxt_c < float(LANES), nxt_c, -1.0)

        tid = lax.broadcasted_iota(jnp.int32, (LANES, PLAN_LANES), 1).astype(F32)
        exp_id = lax.broadcasted_iota(jnp.int32, (LANES, PLAN_LANES), 0)
        tid_used = jnp.minimum(tid, n_tiles - 1.0)
        te = jnp.sum(jnp.where(jnp.logical_and(exp_id < N_EXP, end_c <= tid_used), 1.0, 0.0), axis=0, keepdims=True)
        first = jnp.sum(jnp.where(jnp.logical_and(tiles_c > 0.0, start_c == tid), 1.0, 0.0), axis=0, keepdims=True)
        mine = te == exp_id.astype(F32)
        nxt = jnp.sum(jnp.where(mine, nxt_c, 0.0), axis=0, keepdims=True)
        par = jnp.sum(jnp.where(mine, par_c, 0.0), axis=0, keepdims=True)
        last = jnp.where(used, tile_end - 1.0, 0.0)
        last = jnp.concatenate([last, jnp.zeros((1, PLAN_LANES - LANES), F32)], axis=1)

        row_id = lax.broadcasted_iota(jnp.int32, (SUB, PLAN_LANES), 0)
        meta = jnp.zeros((SUB, PLAN_LANES), F32)
        for r, val in enumerate((te, first, nxt, par, last, jnp.broadcast_to(n_tiles, (1, PLAN_LANES)))):
            meta = jnp.where(row_id == r, jnp.broadcast_to(val, (SUB, PLAN_LANES)), meta)
        meta_ref[...] = meta.astype(jnp.int32)

    ti = ti_ref[...]
    rank = rank_ref[...]
    lane = lax.broadcasted_iota(jnp.int32, ti.shape, 1)
    dest = jnp.zeros(ti.shape, jnp.int32)
    for kk in range(TOP_K):
        base = jnp.sum(jnp.where(lane == ti[:, kk:kk + 1], start_s[...], 0.0), axis=-1, keepdims=True)
        dest = jnp.where(lane == kk, base.astype(jnp.int32) + rank[:, kk:kk + 1] * SUB, dest)
    dest_ref[...] = dest


def _route_plan(cnt, ti, rank):
    t = ti.shape[0]
    row = pl.BlockSpec((TM, LANES), lambda i: (i, 0))
    dest, meta = pl.pallas_call(
        _plan_kernel, grid=(t // TM,),
        in_specs=[pl.BlockSpec((1, LANES), lambda i: (0, 0)), row, row],
        out_specs=[row, pl.BlockSpec((SUB, PLAN_LANES), lambda i: (0, 0))],
        out_shape=[jax.ShapeDtypeStruct((t, LANES), jnp.int32), jax.ShapeDtypeStruct((SUB, PLAN_LANES), jnp.int32)],
        scratch_shapes=[pltpu.VMEM((1, LANES), F32)],
        compiler_params=_cparams("arbitrary"), name="moe_plan",
    )(cnt, ti, rank)
    plan = dict(tile_expert=meta[0, :N_TILES_MAX], first=meta[1, :N_TILES_MAX], tile_next=meta[2, :N_TILES_MAX],
                tile_parity=meta[3, :N_TILES_MAX], last_tile=meta[4, :N_EXP], n_tiles=meta[5, :1])
    return dest[:, :TOP_K], plan


DISPATCH_BLK = 1024


def _dispatch_kernel(lt_ref, nt_ref, dest_ref, h_ref, xs_ref, h_s, zero_s, sem, hsem):
    i = pl.program_id(0)
    tile_rows = TM_E * SUB
    n_tiles_max = xs_ref.shape[0] // tile_rows
    n_tok = h_s.shape[0] // SUB

    @pl.when(i == 0)
    def _():
        stage = pltpu.make_async_copy(h_ref, h_s, hsem)
        stage.start()
        zero_s[...] = jnp.zeros_like(zero_s)

        def zero_tile(tile):
            r0 = pl.multiple_of(tile * tile_rows, tile_rows)
            return pltpu.make_async_copy(zero_s, xs_ref.at[pl.ds(r0, tile_rows), :], sem)

        def start_unused(j, carry):
            zero_tile(j).start()
            return carry

        def wait_unused(j, carry):
            zero_tile(j).wait()
            return carry

        for e in range(N_EXP):
            zero_tile(lt_ref[e]).start()
        lax.fori_loop(nt_ref[0], n_tiles_max, start_unused, 0)
        for e in range(N_EXP):
            zero_tile(lt_ref[e]).wait()
        lax.fori_loop(nt_ref[0], n_tiles_max, wait_unused, 0)
        stage.wait()

    base = i * DISPATCH_BLK

    def issue(t, carry):
        src = pl.multiple_of((base + t) * SUB, SUB)
        for kk in range(TOP_K):
            row = pl.multiple_of(dest_ref[0, t * TOP_K + kk], SUB)
            pltpu.make_async_copy(h_s.at[pl.ds(src, SUB), :], xs_ref.at[pl.ds(row, SUB), :], sem).start(priority=kk % 2)
        return carry

    lax.fori_loop(0, DISPATCH_BLK, issue, 0, unroll=2)

    @pl.when(i == pl.num_programs(0) - 1)
    def _():
        for kk in range(TOP_K):
            pltpu.make_async_copy(h_s, xs_ref.at[pl.ds(0, n_tok * SUB), :], sem).wait()


def _dispatch(h2, dest, plan, n_rows):
    t = h2.shape[0] // SUB
    nblk = t // DISPATCH_BLK
    dest3 = dest.reshape(nblk, 1, DISPATCH_BLK * TOP_K)
    return pl.pallas_call(
        _dispatch_kernel,
        grid_spec=pltpu.PrefetchScalarGridSpec(
            num_scalar_prefetch=2, grid=(nblk,),
            in_specs=[pl.BlockSpec((None, 1, DISPATCH_BLK * TOP_K), lambda i, lt, nt: (i, 0, 0), memory_space=pltpu.SMEM),
                      pl.BlockSpec(memory_space=pl.ANY)],
            out_specs=pl.BlockSpec(memory_space=pl.ANY),
            scratch_shapes=[pltpu.VMEM((t * SUB, LANES), F32), pltpu.VMEM((TM_E * SUB, LANES), F32),
                            pltpu.SemaphoreType.DMA(()), pltpu.SemaphoreType.DMA(())]),
        out_shape=jax.ShapeDtypeStruct((n_rows * SUB, LANES), F32),
        compiler_params=_cparams("arbitrary"), name="moe_dispatch",
    )(plan["last_tile"], plan["n_tiles"], dest3, h2)


def _expert_kernel(te_ref, tf_ref, nt_ref, nx_ref, par_ref, xs_ref, w1_ref, b1_ref, w2_ref, b2_ref, ys_ref,
                   w1f, w2f, w1_s, w2_s, wsem, *, layer):
    i = pl.program_id(0)

    def fetch(expert, slot):
        return (pltpu.make_async_copy(w1_ref.at[layer, expert], w1f.at[slot], wsem.at[slot]),
                pltpu.make_async_copy(w2_ref.at[layer, expert], w2f.at[slot], wsem.at[slot]))

    @pl.when(i == 0)
    def _():
        for cp in fetch(te_ref[0], 0):
            cp.start(priority=1)

    @pl.when(i < nt_ref[0])
    def _():
        @pl.when(tf_ref[i] == 1)
        def _():
            slot = par_ref[i]
            for cp in fetch(te_ref[i], slot):
                cp.wait()
            w1_s[...] = w1f[slot].astype(BF16)
            w2_s[...] = w2f[slot].astype(BF16)

            @pl.when(nx_ref[i] >= 0)
            def _():
                for cp in fetch(nx_ref[i], 1 - slot):
                    cp.start(priority=1)

        x = _load_token_major(xs_ref, (), TM_E)
        u = _dot(x.astype(BF16), w1_s[...]) + b1_ref[...]
        g = jnp.minimum(u[:, :D_FF], SWIGLU_LIMIT)
        up = jnp.clip(u[:, D_FF:], -SWIGLU_LIMIT, SWIGLU_LIMIT)
        act = (up + 1.0) * g * jax.nn.sigmoid(SWIGLU_ALPHA * g)
        _store_token_major(ys_ref, (), _dot(act.astype(BF16), w2_s[...]) + b2_ref[...])

    @pl.when(i >= nt_ref[0])
    def _():
        ys_ref[...] = jnp.zeros_like(ys_ref)


def _experts(xs, plan, layer, w1, b1, w2, b2):
    d = D_MODEL
    nt = xs.shape[0] // (TM_E * SUB)
    tile = lambda i, te, tf, ntl, nx, par: (jnp.minimum(i, ntl[0] - 1), 0)
    otile = lambda i, te, tf, ntl, nx, par: (i, 0)
    bmap = lambda i, te, tf, ntl, nx, par: (layer, te[i], 0, 0)
    return pl.pallas_call(
        functools.partial(_expert_kernel, layer=layer),
        grid_spec=pltpu.PrefetchScalarGridSpec(
            num_scalar_prefetch=5, grid=(nt,),
            in_specs=[pl.BlockSpec((TM_E * SUB, LANES), tile),
                      pl.BlockSpec(memory_space=pl.ANY),
                      pl.BlockSpec((None, None, 1, 2 * D_FF), bmap),
                      pl.BlockSpec(memory_space=pl.ANY),
                      pl.BlockSpec((None, None, 1, d), bmap)],
            out_specs=pl.BlockSpec((TM_E * SUB, LANES), otile),
            scratch_shapes=[pltpu.VMEM((2, d, 2 * D_FF), F32), pltpu.VMEM((2, D_FF, d), F32),
                            pltpu.VMEM((d, 2 * D_FF), BF16), pltpu.VMEM((D_FF, d), BF16),
                            pltpu.SemaphoreType.DMA((2,))]),
        out_shape=jax.ShapeDtypeStruct(xs.shape, F32),
        compiler_params=_cparams("arbitrary"), name="moe_experts",
    )(plan["tile_expert"], plan["first"], plan["n_tiles"], plan["tile_next"], plan["tile_parity"],
      xs, w1, b1, w2, b2)


def _combine_kernel(dest_ref, x1_ref, tw_ref, mod_ref, ys_ref, *rest):
    *o_refs, buf, sem = rest
    i = pl.program_id(0)
    j = i - 1
    n = pl.num_programs(0) - 1
    tile_rows = TM * SUB

    for s in range(2):
        @pl.when(jnp.logical_and(i < n, i % 2 == s))
        def _():
            def issue(t, carry):
                dst = pl.multiple_of(t * SUB, SUB)
                for kk in range(TOP_K):
                    row = pl.multiple_of(dest_ref[0, t * TOP_K + kk], SUB)
                    pltpu.make_async_copy(ys_ref.at[pl.ds(row, SUB), :], buf.at[s, kk, pl.ds(dst, SUB), :],
                                          sem.at[s]).start(priority=kk % 2)
                return carry

            lax.fori_loop(0, TM, issue, 0, unroll=2)

    @pl.when(j >= 0)
    def _():
        slot = j % 2
        for kk in range(TOP_K):
            pltpu.make_async_copy(ys_ref.at[pl.ds(0, tile_rows), :], buf.at[slot, kk], sem.at[slot]).wait()
        tw = tw_ref[...]
        y = tw[:, 0:1] * _load_token_major(buf, (slot, 0), TM)
        for kk in range(1, TOP_K):
            y = y + tw[:, kk:kk + 1] * _load_token_major(buf, (slot, kk), TM)
        out = x1_ref[...] + mod_ref[5:6, :] * y
        if len(o_refs) == 1:
            o_refs[0][...] = out
        else:
            @pl.when(j < N_CTX_TILES)
            def _():
                o_refs[0][...] = out

            @pl.when(j >= N_CTX_TILES)
            def _():
                o_refs[1][...] = out


def _combine(ys, dest, x1, tw, mod, split):
    t, d = x1.shape
    nblk = t // TM
    dest3 = dest.reshape(nblk, 1, TM * TOP_K)
    prev = lambda i: jnp.maximum(i - 1, 0)
    row = lambda w: pl.BlockSpec((TM, w), lambda i: (prev(i), 0))
    return pl.pallas_call(
        _combine_kernel, grid=(nblk + 1,),
        in_specs=[pl.BlockSpec((None, 1, TM * TOP_K), lambda i: (jnp.minimum(i, nblk - 1), 0, 0), memory_space=pltpu.SMEM),
                  row(d), row(LANES),
                  pl.BlockSpec((None, 6, d), lambda i: (_cond_row(prev(i)), 0, 0)),
                  pl.BlockSpec(memory_space=pl.ANY)],
        out_specs=_Rows(x1, x1, 0).specs(lag=1) if split else row(d),
        out_shape=([jax.ShapeDtypeStruct((N_CTX_TILES * TM, d), F32),
                    jax.ShapeDtypeStruct(((N_TILES - N_CTX_TILES) * TM, d), F32)] if split
                   else jax.ShapeDtypeStruct((t, d), F32)),
        scratch_shapes=[pltpu.VMEM((2, TOP_K, TM * SUB, LANES), F32), pltpu.SemaphoreType.DMA((2,))],
        compiler_params=_cparams("arbitrary"), name="moe_combine",
    )(dest3, x1, tw, mod, ys)


def _moe(x1, h2, ti, tw, rank, cnt, mod, layer, w1, b1, w2, b2, split):
    depth = w1.shape[0]
    assert x1.shape[0] == N_TILES * TM
    dest, plan = _route_plan(cnt, ti, rank)
    xs = _dispatch(h2, dest, plan, N_TILES_MAX * TM_E)
    ys = _experts(xs, plan, layer, w1, b1.reshape(depth, N_EXP, 1, -1), w2, b2.reshape(depth, N_EXP, 1, -1))
    return _combine(ys, dest, x1, tw, mod, split)


def _pad_lanes(a, value=0.0):
    return jnp.pad(a, ((0, 0), (0, LANES - a.shape[1])), constant_values=value)


def kernel(x_prompt, x_sample, cache_na_k, cache_na_v, state_mlstm_C, state_mlstm_n, state_mlstm_m, state_ret_S, c, c_ctx, w_mod, b_mod, w_in_even, mlstm_gate_b, na_q_norm, na_k_norm, na_rpb, mlstm_norm, w_out_even, w_in_odd, ret_decay, ret_norm, w_out_odd, w_router, b_router, w_moe_in, b_moe_in, w_moe_out, b_moe_out):
    nb_c, s_c, d = x_prompt.shape
    nb_l, s_l, _ = x_sample.shape
    t_c = nb_c * s_c
    t_l = nb_l * s_l
    assert t_c == 4 * SEG and s_l == SEG and d == D_MODEL
    depth = w_mod.shape[0]
    dt = x_prompt.dtype

    x = _Rows(x_prompt.reshape(t_c, d), x_sample.reshape(t_l, d), 0)
    cond = jnp.concatenate([c_ctx[None, :], c, jnp.zeros((N_COND - 1 - nb_l, d), F32)], axis=0)
    mod = _modulation(cond, w_mod, b_mod).reshape(depth, N_COND, 6, d)

    outs = {}
    for l in range(depth):
        e = l // 2
        mod_l = mod[l]
        if l % 2 == 0:
            w_in = w_in_even[e]
            n_main = 3 * W_A + 4 * W_B
            wg = _pad_lanes(w_in[:, n_main:])
            bg = _pad_lanes(mlstm_gate_b[e].reshape(1, 4 * H_B))
            z, g = _inproj(x, mod_l, w_in.astype(BF16), n_main, wg, bg)
            qn = jnp.tile(na_q_norm[e].reshape(1, HD_A), (1, H_A))
            kn = jnp.tile(na_k_norm[e].reshape(1, HD_A), (1, H_A))
            oa_c, ka_c = _ctx_attention(z, nb_c, s_c, qn, kn)
            past = cache_na_k.shape[2]
            oa_l = _na_attention(z, t_c, nb_l, s_l,
                                 cache_na_k[:, e].reshape(nb_l, past, W_A), cache_na_v[:, e].reshape(nb_l, past, W_A),
                                 _na_bias_table(na_rpb[e]), qn, kn)
            nw = mlstm_norm[e].reshape(1, W_B)
            hm_c, c_fin, n_fin, m_fin = _mlstm(z, g, 0, nb_c, s_c, nw)
            init = (state_mlstm_C[:, e].reshape(nb_l, 2 * H_B, HD_B, HD_B),
                    state_mlstm_n[:, e].reshape(nb_l, 2 * H_B, HD_B),
                    jnp.broadcast_to(state_mlstm_m[:, e].reshape(nb_l, 2 * H_B, 1), (nb_l, 2 * H_B, LANES)))
            hm_l = _mlstm(z, g, t_c, nb_l, s_l, nw, init)[0]
            a = _Rows(oa_c, oa_l, 0)
            b = _Rows(hm_c, hm_l, 0)
            w_out = w_out_even[e].astype(BF16)
            outs.setdefault("na_k", []).append(ka_c.reshape(nb_c, s_c, H_A, HD_A))
            outs.setdefault("na_v", []).append(z[:t_c, 2 * W_A:3 * W_A].reshape(nb_c, s_c, H_A, HD_A))
            outs.setdefault("C", []).append(c_fin.reshape(nb_c, 2, H_B, HD_B, HD_B))
            outs.setdefault("n", []).append(n_fin.reshape(nb_c, 2, H_B, HD_B))
            outs.setdefault("m", []).append(m_fin[:, :, 0].reshape(nb_c, 2, H_B))
        else:
            w_c = H_C * HD_C
            z = _inproj(x, mod_l, w_in_odd[e].astype(BF16), 4 * w_c + N_FG * FG_W)
            dl_rep = jnp.broadcast_to(ret_decay[e].reshape(2 * H_C, 1), (2 * H_C, LANES))
            nw = ret_norm[e].reshape(1, w_c)
            hr_c, s_fin = _retention(z, 0, nb_c, s_c, dl_rep, nw)
            hr_l = _retention(z, t_c, nb_l, s_l, dl_rep, nw, rope=_rope_tables(s_l),
                              init=state_ret_S[:, e].reshape(nb_l, 2 * H_C, HD_C, HD_C))[0]
            fd_c = _fnet(z, 0, nb_c, s_c)
            fd_l = _fnet(z, t_c, nb_l, s_l)
            a = _Rows(hr_c, hr_l, 0)
            b = _Rows(fd_c, fd_l, 0)
            w_out = w_out_odd[e].astype(BF16)
            outs.setdefault("S", []).append(s_fin.reshape(nb_c, 2, H_C, HD_C, HD_C))
        wr = _pad_lanes(w_router[l])
        br = _pad_lanes(b_router[l].reshape(1, N_EXP), NEG)
        x1, h2, ti, tw, rank, cnt = _outproj_router(a, b, x, mod_l, w_out, wr, br)
        last = l == depth - 1
        x = _moe(x1, h2, ti, tw, rank, cnt, mod_l, l, w_moe_in, b_moe_in, w_moe_out, b_moe_out, split=last)
        if not last:
            x = _whole(x)

    y_prompt = x[0].reshape(nb_c, s_c, d)
    y_sample = x[1].reshape(nb_l, s_l, d)
    stack = lambda key: jnp.stack(outs[key], axis=1).astype(dt)
    return (y_prompt, y_sample, stack("na_k"), stack("na_v"), stack("C"), stack("n"), stack("m"), stack("S"))
```

```python
import functools
from typing import NamedTuple

import numpy as np
import jax
import jax.numpy as jnp
from jax import lax
from jax.experimental import pallas as pl
from jax.experimental.pallas import tpu as pltpu

F32 = jnp.float32
BF16 = jnp.bfloat16
HIGHEST = lax.Precision.HIGHEST

D_MODEL = 1024
GRID_W = 64
WIN_R = 8
WIN_C = 16
H_A, HD_A = 8, 64
H_B, HD_B = 4, 128
H_C, HD_C = 4, 128
N_FG, FG_W = 4, 128
W_A = H_A * HD_A
W_B = H_B * HD_B
N_EXP = 32
TOP_K = 4
D_FF = D_MODEL
SWIGLU_LIMIT = 7.0
SWIGLU_ALPHA = 1.702
CHUNK = 128
ROPE_BASE = 10000.0
EPS = 1e-6

LANES = 128
SEG = 1024
N_COND = 8
TM = 512
N_TILES = 8 * SEG // TM
N_CTX_TILES = 4 * SEG // TM
TM_E = 256
NEG = -1e30
VMEM_LIMIT = 56 * 1024 * 1024


def _cparams(*sem):
    return pltpu.CompilerParams(dimension_semantics=sem, vmem_limit_bytes=VMEM_LIMIT)


def _cond_row(i):
    return jnp.maximum((i * TM) // SEG - 3, 0)


def _log_sigmoid(x):
    return jnp.minimum(x, 0.0) - jnp.log1p(jnp.exp(-jnp.abs(x)))


def _dot(a, b):
    return jnp.dot(a, b, preferred_element_type=F32)


def _dot_nt(a, b):
    return lax.dot_general(a, b, (((1,), (1,)), ((), ())), preferred_element_type=F32)


def _dot_hi(a, b):
    return jnp.dot(a, b, precision=HIGHEST, preferred_element_type=F32)


def _split_bf16(x):
    hi = x.astype(BF16)
    return hi, (x - hi.astype(F32)).astype(BF16)


def _dot_x3(x, w_hi, w_lo):
    x_hi, x_lo = _split_bf16(x)
    return _dot(x_hi, w_hi) + (_dot(x_lo, w_hi) + _dot(x_hi, w_lo))


def _mod_kernel(cond_ref, w_ref, b_ref, o_ref):
    c = cond_ref[...]
    s = c * jax.nn.sigmoid(c)
    o_ref[...] = _dot_hi(s, w_ref[...]) + b_ref[...]


def _modulation(cond, w_mod, b_mod):
    depth, d, n = w_mod.shape
    tn = 1536
    return pl.pallas_call(
        _mod_kernel,
        grid=(depth, n // tn),
        in_specs=[pl.BlockSpec((N_COND, d), lambda l, j: (0, 0)),
                  pl.BlockSpec((None, d, tn), lambda l, j: (l, 0, j)),
                  pl.BlockSpec((None, 1, tn), lambda l, j: (l, 0, j))],
        out_specs=pl.BlockSpec((None, N_COND, tn), lambda l, j: (l, 0, j)),
        out_shape=jax.ShapeDtypeStruct((depth, N_COND, n), F32),
        compiler_params=_cparams("arbitrary", "arbitrary"),
        name="modulation",
    )(cond, w_mod, b_mod.reshape(depth, 1, n))


def _rms_mod(x, shift, scale):
    h = x * lax.rsqrt(jnp.mean(x * x, axis=-1, keepdims=True) + EPS)
    return h * (1.0 + scale) + shift


class _Rows(NamedTuple):
    a: jax.Array
    b: jax.Array
    off_b: int

    def specs(self, lag=0):
        width = self.a.shape[1]
        off_b = self.off_b
        tile = (lambda i: i) if lag == 0 else (lambda i: jnp.maximum(i - lag, 0))
        return [pl.BlockSpec((TM, width), lambda i: (jnp.minimum(tile(i), N_CTX_TILES - 1), 0)),
                pl.BlockSpec((TM, width), lambda i: (jnp.maximum(tile(i) - N_CTX_TILES, 0) + off_b, 0))]


def _whole(x):
    return _Rows(x, x, N_CTX_TILES)


def _rows_read(tile, ref_a, ref_b):
    return jnp.where(tile < N_CTX_TILES, ref_a[...], ref_b[...])


def _inproj_kernel(xa_ref, xb_ref, mod_ref, w_ref, z_ref):
    x = _rows_read(pl.program_id(0), xa_ref, xb_ref)
    h = _rms_mod(x, mod_ref[0:1, :], mod_ref[1:2, :])
    z_ref[...] = _dot(h.astype(BF16), w_ref[...])


def _inproj_gate_kernel(xa_ref, xb_ref, mod_ref, w_ref, wgh_ref, wgl_ref, bg_ref, z_ref, g_ref):
    x = _rows_read(pl.program_id(0), xa_ref, xb_ref)
    h = _rms_mod(x, mod_ref[0:1, :], mod_ref[1:2, :])
    z_ref[...] = _dot(h.astype(BF16), w_ref[...])
    g_ref[...] = _dot_x3(h, wgh_ref[...], wgl_ref[...]) + bg_ref[...]


def _inproj(x, mod, w, n, wg=None, bg=None):
    t = N_TILES * TM
    d = x.a.shape[1]
    in_specs = x.specs() + [pl.BlockSpec((None, 6, d), lambda i: (_cond_row(i), 0, 0)),
                            pl.BlockSpec((d, n), lambda i: (0, 0))]
    z_spec = pl.BlockSpec((TM, n), lambda i: (i, 0))
    z_shape = jax.ShapeDtypeStruct((t, n), F32)
    if wg is None:
        return pl.pallas_call(
            _inproj_kernel, grid=(N_TILES,), in_specs=in_specs, out_specs=z_spec, out_shape=z_shape,
            compiler_params=_cparams("arbitrary"), name="inproj",
        )(x.a, x.b, mod, w)
    wg_hi, wg_lo = _split_bf16(wg)
    in_specs += [pl.BlockSpec((d, LANES), lambda i: (0, 0))] * 2 + [pl.BlockSpec((1, LANES), lambda i: (0, 0))]
    return pl.pallas_call(
        _inproj_gate_kernel, grid=(N_TILES,), in_specs=in_specs,
        out_specs=[z_spec, pl.BlockSpec((TM, LANES), lambda i: (i, 0))],
        out_shape=[z_shape, jax.ShapeDtypeStruct((t, LANES), F32)],
        compiler_params=_cparams("arbitrary"), name="inproj_gate",
    )(x.a, x.b, mod, w, wg_hi, wg_lo, bg)


def _head_rms(x, w):
    return x * lax.rsqrt(jnp.mean(x * x, axis=-1, keepdims=True) + EPS) * w


def _fold_lanes(x, op):
    parts = [x[:, c * LANES:(c + 1) * LANES] for c in range(x.shape[1] // LANES)]
    while len(parts) > 1:
        parts = [op(parts[c], parts[c + 1]) if c + 1 < len(parts) else parts[c] for c in range(0, len(parts), 2)]
    return parts[0]


def _head_group_matrix():
    head = np.arange(W_A) // HD_A
    return jnp.asarray((head[:, None] == head[None, :]) / HD_A, F32).astype(BF16)


def _heads_rms(x, w_row, g):
    hi, lo = _split_bf16(x * x)
    return x * lax.rsqrt(_dot(hi, g) + _dot(lo, g) + EPS) * w_row


def _ctx_attn_kernel(q_ref, k_ref, v_ref, qn_ref, kn_ref, g_ref, o_ref, ko_ref, vo_ref, km_s, vm_s):
    seq = q_ref.shape[0]
    scale = HD_A ** -0.5
    g = g_ref[...]
    q = _heads_rms(q_ref[...], qn_ref[...] * scale, g).astype(BF16)
    k = _heads_rms(k_ref[...], kn_ref[...], g)
    v = v_ref[...]
    for h in range(H_A):
        sl = slice(h * HD_A, (h + 1) * HD_A)
        ko_ref[pl.ds(h, seq, stride=H_A), :] = k[:, sl]
        vo_ref[pl.ds(h, seq, stride=H_A), :] = v[:, sl]
    kb = k.astype(BF16)
    vb = v.astype(BF16)
    head = lax.broadcasted_iota(jnp.int32, (seq, W_A), 1) // HD_A
    zero = jnp.zeros((seq, W_A), BF16)
    for h in range(H_A):
        km_s[h * seq:(h + 1) * seq, :] = jnp.where(head == h, kb, zero)
        vm_s[h * seq:(h + 1) * seq, :] = jnp.where(head == h, vb, zero)
    s = _dot_nt(q, km_s[...])
    probs = []
    for h in range(H_A):
        s_h = s[:, h * seq:(h + 1) * seq]
        p = jnp.exp(s_h - jnp.max(_fold_lanes(s_h, jnp.maximum), axis=-1, keepdims=True))
        den = jnp.sum(_fold_lanes(p, jnp.add), axis=-1, keepdims=True)
        probs.append((p / den).astype(BF16))
    o_ref[...] = _dot(jnp.concatenate(probs, axis=1), vm_s[...])


def _ctx_attention(z, n_batch, seq, qn, kn):
    spec = lambda c: pl.BlockSpec((seq, W_A), lambda b: (b, c))
    wspec = pl.BlockSpec((1, W_A), lambda b: (0, 0))
    out = jax.ShapeDtypeStruct((n_batch * seq, W_A), F32)
    return pl.pallas_call(
        _ctx_attn_kernel, grid=(n_batch,),
        in_specs=[spec(0), spec(1), spec(2), wspec, wspec, pl.BlockSpec((W_A, W_A), lambda b: (0, 0))],
        out_specs=[pl.BlockSpec((seq, W_A), lambda b: (b, 0))] + [pl.BlockSpec((seq * H_A, HD_A), lambda b: (b, 0))] * 2,
        out_shape=[out] + [jax.ShapeDtypeStruct((n_batch * seq * H_A, HD_A), F32)] * 2,
        scratch_shapes=[pltpu.VMEM((H_A * seq, W_A), BF16), pltpu.VMEM((H_A * seq, W_A), BF16)],
        compiler_params=_cparams("arbitrary"), name="ctx_attention",
    )(z, z, z, qn, kn, _head_group_matrix())


def _na_bias_table(rpb):
    qc = np.arange(GRID_W)
    kc = np.arange(GRID_W)
    cstart = np.clip(qc - WIN_C // 2, 0, GRID_W - WIN_C)
    col_in = (kc[None, :] >= cstart[:, None]) & (kc[None, :] < cstart[:, None] + WIN_C)
    dc = np.clip(kc[None, :] - qc[:, None], 1 - WIN_C, WIN_C - 1) + WIN_C - 1
    cls = np.arange(WIN_R)
    j = np.arange(WIN_R)
    dr = j[None, :] - cls[:, None] + WIN_R - 1
    sel_r = jnp.asarray(dr[:, :, None] == np.arange(2 * WIN_R - 1)[None, None, :], F32)
    sel_c = jnp.asarray(dc[:, :, None] == np.arange(2 * WIN_C - 1)[None, None, :], F32)
    tab = jnp.einsum("hab,cja,qkb->hcqjk", rpb, sel_r, sel_c, precision=HIGHEST)
    tab = jnp.where(jnp.asarray(col_in)[None, None, :, None, :], tab, NEG)
    return tab.reshape(H_A, WIN_R, GRID_W, WIN_R * GRID_W)


def _na_kernel(q_ref, k_ref, v_ref, kc_ref, vc_ref, bias_ref, qn_ref, kn_ref, g_ref, o_ref,
               kn_s, v_s, kc_s, vc_s, *, rows):
    r = pl.program_id(1)
    scale = HD_A ** -0.5

    @pl.when(r == 0)
    def _():
        kn_s[...] = _heads_rms(k_ref[...], kn_ref[...], g_ref[...]).astype(BF16)
        v_s[...] = v_ref[...].astype(BF16)
        kc_s[...] = kc_ref[...].astype(BF16)
        vc_s[...] = vc_ref[...].astype(BF16)

    rs = jnp.clip(r - WIN_R // 2, 0, rows - WIN_R)
    start = pl.multiple_of(rs * GRID_W, GRID_W)
    n_loc = WIN_R * GRID_W
    for h in range(H_A):
        sl = slice(h * HD_A, (h + 1) * HD_A)
        q = (_head_rms(q_ref[:, sl], qn_ref[:, sl]) * scale).astype(BF16)
        s_loc = _dot_nt(q, kn_s[pl.ds(start, n_loc), sl]) + bias_ref[h]
        s_ctx = _dot_nt(q, kc_s[:, sl])
        m = jnp.max(jnp.maximum(_fold_lanes(s_loc, jnp.maximum), _fold_lanes(s_ctx, jnp.maximum)),
                    axis=-1, keepdims=True)
        p_loc = jnp.exp(s_loc - m)
        p_ctx = jnp.exp(s_ctx - m)
        den = jnp.sum(_fold_lanes(p_loc, jnp.add) + _fold_lanes(p_ctx, jnp.add), axis=-1, keepdims=True)
        o = _dot(p_loc.astype(BF16), v_s[pl.ds(start, n_loc), sl]) + _dot(p_ctx.astype(BF16), vc_s[:, sl])
        o_ref[:, sl] = o / den


def _na_attention(z, row0, n_batch, seq, kc, vc, bias, qn, kn):
    rows = seq // GRID_W
    past = kc.shape[1]
    blk0 = row0 // GRID_W
    sblk0 = row0 // seq

    def cls_of(r):
        return r - jnp.clip(r - WIN_R // 2, 0, rows - WIN_R)

    full = lambda c: pl.BlockSpec((seq, W_A), lambda b, r: (sblk0 + b, c))
    cspec = pl.BlockSpec((None, past, W_A), lambda b, r: (b, 0, 0))
    wspec = pl.BlockSpec((1, W_A), lambda b, r: (0, 0))
    return pl.pallas_call(
        functools.partial(_na_kernel, rows=rows), grid=(n_batch, rows),
        in_specs=[pl.BlockSpec((GRID_W, W_A), lambda b, r: (blk0 + b * rows + r, 0)),
                  full(1), full(2), cspec, cspec,
                  pl.BlockSpec((H_A, None, GRID_W, WIN_R * GRID_W), lambda b, r: (0, cls_of(r), 0, 0)),
                  wspec, wspec, pl.BlockSpec((W_A, W_A), lambda b, r: (0, 0))],
        out_specs=pl.BlockSpec((GRID_W, W_A), lambda b, r: (b * rows + r, 0)),
        out_shape=jax.ShapeDtypeStruct((n_batch * seq, W_A), F32),
        scratch_shapes=[pltpu.VMEM((seq, W_A), BF16), pltpu.VMEM((seq, W_A), BF16),
                        pltpu.VMEM((past, W_A), BF16), pltpu.VMEM((past, W_A), BF16)],
        compiler_params=_cparams("arbitrary", "arbitrary"), name="na_attention",
    )(z, z, z, kc, vc, bias, qn, kn, _head_group_matrix())


def _tri_masks():
    li = lax.broadcasted_iota(jnp.int32, (CHUNK, CHUNK), 0)
    si = lax.broadcasted_iota(jnp.int32, (CHUNK, CHUNK), 1)
    return li >= si, li <= si


def _mlstm_kernel(*refs, nc, has_init):
    if has_init:
        (q_ref, k_ref, v_ref, og_ref, g_ref, nw_ref, c0_ref, n0_ref, m0_ref,
         o_ref, cf_ref, nf_ref, mf_ref, h_s, c_s, n_s, m_s) = refs
    else:
        (q_ref, k_ref, v_ref, og_ref, g_ref, nw_ref,
         o_ref, cf_ref, nf_ref, mf_ref, h_s, c_s, n_s, m_s) = refs
    nd = 2 * H_B
    if has_init:
        c_s[...] = c0_ref[...]
        n_s[...] = n0_ref[...]
        m_s[...] = m0_ref[...]
    else:
        c_s[...] = jnp.zeros_like(c_s)
        n_s[...] = jnp.zeros_like(n_s)
        m_s[...] = jnp.zeros_like(m_s)

    causal, anti = _tri_masks()
    tri_f = causal.astype(F32)
    tri_b = anti.astype(F32)
    kscale = HD_B ** -0.5

    def chunk_step(c, carry):
        for d in range(2):
            cc = c if d == 0 else nc - 1 - c
            t0 = pl.multiple_of(cc * CHUNK, CHUNK)
            g = g_ref[pl.ds(t0, CHUNK), :]
            gt = g.T
            ls = _log_sigmoid(g)
            lst = _log_sigmoid(gt)
            tri_c, tri_r, mask = (tri_f, tri_b, causal) if d == 0 else (tri_b, tri_f, anti)
            b_cols = _dot_hi(tri_c, ls)
            b_rows = _dot_hi(lst, tri_r)
            last = CHUNK - 1 if d == 0 else 0
            for h in range(H_B):
                ci = (2 * d) * H_B + h
                cf = (2 * d + 1) * H_B + h
                hs = slice(h * HD_B, (h + 1) * HD_B)
                q = q_ref[pl.ds(t0, CHUNK), hs]
                k = k_ref[pl.ds(t0, CHUNK), hs] * kscale
                v = v_ref[pl.ds(t0, CHUNK), hs]
                qb, kb, vb = q.astype(BF16), k.astype(BF16), v.astype(BF16)
                b_col = b_cols[:, cf:cf + 1]
                b_row = b_rows[cf:cf + 1, :]
                i_row = gt[ci:ci + 1, :]
                sidx = d * H_B + h
                cst = c_s[sidx]
                nst = n_s[sidx:sidx + 1, :]
                mst = m_s[sidx:sidx + 1, 0:1]
                dmat = jnp.where(mask, b_col - b_row + i_row, -jnp.inf)
                inter = b_col + mst
                mt = jnp.maximum(inter, jnp.max(dmat, axis=-1, keepdims=True))
                w = jnp.exp(dmat - mt) * _dot_nt(qb, kb)
                a = jnp.exp(inter - mt)
                num = _dot(w.astype(BF16), vb) + _dot(qb, cst.astype(BF16)) * a
                den = jnp.sum(w, axis=-1, keepdims=True) + a * jnp.sum(q * nst, axis=-1, keepdims=True)
                hc = num / jnp.maximum(jnp.abs(den), jnp.exp(-mt))
                h_s[d, pl.ds(t0, CHUNK), hs] = hc
                bl = b_row[:, last:last + 1]
                dl = bl - b_row + i_row
                m_new = jnp.maximum(bl + mst, jnp.max(dl, axis=-1, keepdims=True))
                wl = jnp.exp(dl - m_new)
                dec = jnp.exp(bl + mst - m_new)
                kw = (k.T * wl).astype(BF16)
                c_s[sidx] = dec * cst + _dot(kw, vb)
                wl8 = jnp.broadcast_to(wl, (8, CHUNK)).astype(BF16)
                n_s[sidx:sidx + 1, :] = dec * nst + _dot(wl8, kb)[0:1, :]
                m_s[sidx:sidx + 1, :] = jnp.broadcast_to(m_new, (1, LANES))
        return carry

    lax.fori_loop(0, nc, chunk_step, 0)

    for h in range(H_B):
        hs = slice(h * HD_B, (h + 1) * HD_B)
        hsum = h_s[0, :, hs] + h_s[1, :, hs]
        o_ref[:, hs] = _head_rms(hsum, nw_ref[:, hs]) * jax.nn.sigmoid(og_ref[:, hs])
    cf_ref[...] = c_s[...]
    nf_ref[...] = n_s[...]
    mf_ref[...] = m_s[...]


def _mlstm(z, g, row0, n_batch, seq, norm_w, init=None):
    sblk0 = row0 // seq
    nd = 2 * H_B
    spec = lambda c: pl.BlockSpec((seq, W_B), lambda b: (sblk0 + b, c))
    in_specs = [spec(3), spec(4), spec(5), spec(6),
                pl.BlockSpec((seq, LANES), lambda b: (sblk0 + b, 0)),
                pl.BlockSpec((1, W_B), lambda b: (0, 0))]
    args = [z, z, z, z, g, norm_w]
    st_specs = [pl.BlockSpec((None, nd, HD_B, HD_B), lambda b: (b, 0, 0, 0)),
                pl.BlockSpec((None, nd, HD_B), lambda b: (b, 0, 0)),
                pl.BlockSpec((None, nd, LANES), lambda b: (b, 0, 0))]
    if init is not None:
        in_specs += st_specs
        args += list(init)
    return pl.pallas_call(
        functools.partial(_mlstm_kernel, nc=seq // CHUNK, has_init=init is not None), grid=(n_batch,),
        in_specs=in_specs,
        out_specs=[pl.BlockSpec((seq, W_B), lambda b: (b, 0))] + st_specs,
        out_shape=[jax.ShapeDtypeStruct((n_batch * seq, W_B), F32),
                   jax.ShapeDtypeStruct((n_batch, nd, HD_B, HD_B), F32),
                   jax.ShapeDtypeStruct((n_batch, nd, HD_B), F32),
                   jax.ShapeDtypeStruct((n_batch, nd, LANES), F32)],
        scratch_shapes=[pltpu.VMEM((2, seq, W_B), F32), pltpu.VMEM((nd, HD_B, HD_B), F32),
                        pltpu.VMEM((nd, HD_B), F32), pltpu.VMEM((nd, LANES), F32)],
        compiler_params=_cparams("arbitrary"), name="mlstm",
    )(*args)


def _rope_tables(seq):
    half = HD_C // 2
    quarter = half // 2
    t = np.arange(seq)
    inv = ROPE_BASE ** (-np.arange(0, half, 2, dtype=np.float64) / half)
    ang_r = (t // GRID_W)[:, None] * inv[None, :]
    ang_c = (t % GRID_W)[:, None] * inv[None, :]
    cos_t = np.concatenate([np.cos(ang_r), np.cos(ang_r), np.cos(ang_c), np.cos(ang_c)], -1)
    sin_t = np.concatenate([-np.sin(ang_r), np.sin(ang_r), -np.sin(ang_c), np.sin(ang_c)], -1)
    assert cos_t.shape == (seq, 4 * quarter)
    return jnp.asarray(cos_t, F32), jnp.asarray(sin_t, F32)


def _rope(x, cos_t, sin_t):
    quarter = HD_C // 4
    lane = lax.broadcasted_iota(jnp.int32, x.shape, 1)
    first = (lane % (2 * quarter)) < quarter
    swapped = jnp.where(first, pltpu.roll(x, HD_C - quarter, 1), pltpu.roll(x, quarter, 1))
    return x * cos_t + swapped * sin_t


def _ret_kernel(*refs, nc, has_init, use_rope):
    refs = list(refs)
    q_ref, k_ref, v_ref, gg_ref, dl_ref, nw_ref = refs[:6]
    pos = 6
    if use_rope:
        cos_ref, sin_ref = refs[pos:pos + 2]
        pos += 2
    if has_init:
        s0_ref = refs[pos]
        pos += 1
    o_ref, sf_ref, h_s, s_s, dk_s = refs[pos:pos + 5]
    if has_init:
        s_s[...] = s0_ref[...]
    else:
        s_s[...] = jnp.zeros_like(s_s)

    causal, anti = _tri_masks()
    li = lax.broadcasted_iota(jnp.int32, (CHUNK, CHUNK), 0).astype(F32)
    si = lax.broadcasted_iota(jnp.int32, (CHUNK, CHUNK), 1).astype(F32)
    lg_all = _log_sigmoid(dl_ref[...])
    kscale = HD_C ** -0.5
    for d in range(2):
        for h in range(H_C):
            sidx = d * H_C + h
            lg = lg_all[sidx:sidx + 1, :]
            if d == 0:
                dk_s[sidx, 0] = jnp.exp(jnp.where(causal, (li - si) * lg, -jnp.inf))
                dk_s[sidx, 1] = jnp.exp((li + 1.0) * lg)
                dk_s[sidx, 2] = jnp.exp((CHUNK - 1.0 - li) * lg)
            else:
                dk_s[sidx, 0] = jnp.exp(jnp.where(anti, (si - li) * lg, -jnp.inf))
                dk_s[sidx, 1] = jnp.exp((CHUNK - li) * lg)
                dk_s[sidx, 2] = jnp.exp(li * lg)

    def chunk_step(c, carry):
        for d in range(2):
            cc = c if d == 0 else nc - 1 - c
            t0 = pl.multiple_of(cc * CHUNK, CHUNK)
            for h in range(H_C):
                sidx = d * H_C + h
                hs = slice(h * HD_C, (h + 1) * HD_C)
                lg = lg_all[sidx:sidx + 1, :]
                q = q_ref[pl.ds(t0, CHUNK), hs]
                k = k_ref[pl.ds(t0, CHUNK), hs] * kscale
                v = v_ref[pl.ds(t0, CHUNK), hs]
                if use_rope:
                    cos_t = cos_ref[pl.ds(t0, CHUNK), :]
                    sin_t = sin_ref[pl.ds(t0, CHUNK), :]
                    q = _rope(q, cos_t, sin_t)
                    k = _rope(k, cos_t, sin_t)
                decay, q_dec, k_dec = dk_s[sidx, 0], dk_s[sidx, 1], dk_s[sidx, 2]
                c_dec = jnp.exp(CHUNK * lg)
                st = s_s[sidx]
                qb, kb, vb = q.astype(BF16), k.astype(BF16), v.astype(BF16)
                att = _dot_nt(qb, kb) * decay
                o = _dot(att.astype(BF16), vb) + _dot(qb, st.astype(BF16)) * q_dec
                h_s[d, pl.ds(t0, CHUNK), hs] = o
                kd = (k * k_dec).T.astype(BF16)
                s_s[sidx] = c_dec * st + _dot(kd, vb)
        return carry

    lax.fori_loop(0, nc, chunk_step, 0)

    for h in range(H_C):
        hs = slice(h * HD_C, (h + 1) * HD_C)
        osum = h_s[0, :, hs] + h_s[1, :, hs]
        gg = gg_ref[:, hs]
        o_ref[:, hs] = _head_rms(osum, nw_ref[:, hs]) * (gg * jax.nn.sigmoid(gg))
    sf_ref[...] = s_s[...]


def _retention(z, row0, n_batch, seq, decay_rep, norm_w, rope=None, init=None):
    sblk0 = row0 // seq
    nd = 2 * H_C
    w_c = H_C * HD_C
    spec = lambda c: pl.BlockSpec((seq, w_c), lambda b: (sblk0 + b, c))
    in_specs = [spec(0), spec(1), spec(2), spec(3),
                pl.BlockSpec((nd, LANES), lambda b: (0, 0)),
                pl.BlockSpec((1, w_c), lambda b: (0, 0))]
    args = [z, z, z, z, decay_rep, norm_w]
    if rope is not None:
        in_specs += [pl.BlockSpec((seq, HD_C), lambda b: (0, 0))] * 2
        args += list(rope)
    st_spec = pl.BlockSpec((None, nd, HD_C, HD_C), lambda b: (b, 0, 0, 0))
    if init is not None:
        in_specs.append(st_spec)
        args.append(init)
    return pl.pallas_call(
        functools.partial(_ret_kernel, nc=seq // CHUNK, has_init=init is not None, use_rope=rope is not None),
        grid=(n_batch,), in_specs=in_specs,
        out_specs=[pl.BlockSpec((seq, w_c), lambda b: (b, 0)), st_spec],
        out_shape=[jax.ShapeDtypeStruct((n_batch * seq, w_c), F32),
                   jax.ShapeDtypeStruct((n_batch, nd, HD_C, HD_C), F32)],
        scratch_shapes=[pltpu.VMEM((2, seq, w_c), F32), pltpu.VMEM((nd, HD_C, HD_C), F32),
                        pltpu.VMEM((nd, 3, CHUNK, CHUNK), F32)],
        compiler_params=_cparams("arbitrary"), name="retention",
    )(*args)


def _dft_tables(n):
    idx = (np.arange(n)[:, None] * np.arange(n)[None, :]) % n
    ang = 2.0 * np.pi * idx / n
    return np.cos(ang) / np.sqrt(n), np.sin(ang) / np.sqrt(n)


def _fnet_kernel(x_ref, cw_ref, sw_ref, cs_ref, ss_ref, o_ref):
    for g in range(N_FG):
        gs = slice(g * FG_W, (g + 1) * FG_W)
        x = x_ref[:, gs].astype(BF16)
        xc = _dot(x, cw_ref[...]).astype(BF16)
        xs = _dot(x, sw_ref[...]).astype(BF16)
        o_ref[:, gs] = _dot(cs_ref[...], xc) - _dot(ss_ref[...], xs)


def _fnet(z, row0, n_batch, seq):
    sblk0 = row0 // seq
    w_d = N_FG * FG_W
    cw, sw = _dft_tables(FG_W)
    cs, ss = _dft_tables(seq)
    tabs = [jnp.asarray(a, F32).astype(BF16) for a in (cw, sw, cs, ss)]
    wspec = pl.BlockSpec((FG_W, FG_W), lambda b: (0, 0))
    sspec = pl.BlockSpec((seq, seq), lambda b: (0, 0))
    return pl.pallas_call(
        _fnet_kernel, grid=(n_batch,),
        in_specs=[pl.BlockSpec((seq, w_d), lambda b: (sblk0 + b, 4)), wspec, wspec, sspec, sspec],
        out_specs=pl.BlockSpec((seq, w_d), lambda b: (b, 0)),
        out_shape=jax.ShapeDtypeStruct((n_batch * seq, w_d), F32),
        compiler_params=_cparams("arbitrary"), name="fnet",
    )(z, *tabs)


SUB = 8
assert D_MODEL == SUB * LANES


def _store_token_major(ref, lead, val):
    n = val.shape[0]
    for c in range(SUB):
        ref[lead + (pl.ds(c, n, stride=SUB), slice(None))] = val[:, c * LANES:(c + 1) * LANES]


def _load_token_major(ref, lead, n):
    return jnp.concatenate([ref[lead + (pl.ds(c, n, stride=SUB), slice(None))] for c in range(SUB)], axis=1)


def _outproj_router_kernel(aa_ref, ab_ref, ba_ref, bb_ref, xa_ref, xb_ref, mod_ref, wa_ref, wb_ref,
                           wrh_ref, wrl_ref, br_ref, x1_ref, h2_ref, ti_ref, tw_ref, rank_ref, cnt_ref, cnt_s):
    i = pl.program_id(0)
    a = _rows_read(i, aa_ref, ab_ref)
    b = _rows_read(i, ba_ref, bb_ref)
    y = _dot(a.astype(BF16), wa_ref[...]) + _dot(b.astype(BF16), wb_ref[...])
    x1 = _rows_read(i, xa_ref, xb_ref) + mod_ref[2:3, :] * y
    x1_ref[...] = x1
    h2 = _rms_mod(x1, mod_ref[3:4, :], mod_ref[4:5, :])
    _store_token_major(h2_ref, (), h2)
    logits = _dot_x3(h2, wrh_ref[...], wrl_ref[...]) + br_ref[...]
    lane = lax.broadcasted_iota(jnp.int32, logits.shape, 1)
    lane_f = lane.astype(F32)
    cur = logits
    vals, picks = [], []
    ti = jnp.zeros(logits.shape, jnp.int32)
    for kk in range(TOP_K):
        mx = jnp.max(cur, axis=-1, keepdims=True)
        idx = jnp.min(jnp.where(cur == mx, lane_f, float(LANES)), axis=-1, keepdims=True)
        ti = jnp.where(lane == kk, idx.astype(jnp.int32), ti)
        pick = lane_f == idx
        cur = jnp.where(pick, -jnp.inf, cur)
        vals.append(mx)
        picks.append(pick)
    es = [jnp.exp(v - vals[0]) for v in vals]
    tot = es[0] + es[1] + es[2] + es[3]
    tw = jnp.zeros(logits.shape, F32)
    for kk in range(TOP_K):
        tw = jnp.where(lane == kk, es[kk] / tot, tw)
    ti_ref[...] = ti
    tw_ref[...] = tw

    @pl.when(i == 0)
    def _():
        cnt_s[...] = jnp.zeros_like(cnt_s)

    onehot = jnp.zeros(logits.shape, F32)
    for pick in picks:
        onehot = onehot + jnp.where(pick, 1.0, 0.0)
    n = logits.shape[0]
    earlier = (lax.broadcasted_iota(jnp.int32, (n, n), 1) < lax.broadcasted_iota(jnp.int32, (n, n), 0))
    before = cnt_s[...] + _dot(jnp.where(earlier, 1.0, 0.0).astype(BF16), onehot.astype(BF16))
    rank = jnp.zeros(logits.shape, jnp.int32)
    for kk, pick in enumerate(picks):
        r_k = jnp.sum(jnp.where(pick, before, 0.0), axis=-1, keepdims=True)
        rank = jnp.where(lane == kk, r_k.astype(jnp.int32), rank)
    rank_ref[...] = rank
    cnt_s[...] = cnt_s[...] + jnp.sum(onehot, axis=0, keepdims=True)
    cnt_ref[...] = cnt_s[...]


def _outproj_router(a, b, x, mod, w_out, wr, br):
    t = N_TILES * TM
    d = x.a.shape[1]
    wid = a.a.shape[1]
    row = lambda w: pl.BlockSpec((TM, w), lambda i: (i, 0))
    const = lambda r, c: pl.BlockSpec((r, c), lambda i: (0, 0))
    wr_hi, wr_lo = _split_bf16(wr)
    return pl.pallas_call(
        _outproj_router_kernel, grid=(N_TILES,),
        in_specs=a.specs() + b.specs() + x.specs() + [
                  pl.BlockSpec((None, 6, d), lambda i: (_cond_row(i), 0, 0)),
                  pl.BlockSpec((wid, d), lambda i: (0, 0)), pl.BlockSpec((wid, d), lambda i: (1, 0)),
                  const(d, LANES), const(d, LANES), const(1, LANES)],
        out_specs=[row(d), pl.BlockSpec((TM * SUB, LANES), lambda i: (i, 0)), row(LANES), row(LANES), row(LANES),
                   const(1, LANES)],
        out_shape=[jax.ShapeDtypeStruct((t, d), F32), jax.ShapeDtypeStruct((t * SUB, LANES), F32),
                   jax.ShapeDtypeStruct((t, LANES), jnp.int32), jax.ShapeDtypeStruct((t, LANES), F32),
                   jax.ShapeDtypeStruct((t, LANES), jnp.int32), jax.ShapeDtypeStruct((1, LANES), F32)],
        scratch_shapes=[pltpu.VMEM((1, LANES), F32)],
        compiler_params=_cparams("arbitrary"), name="outproj_router",
    )(a.a, a.b, b.a, b.b, x.a, x.b, mod, w_out, w_out, wr_hi, wr_lo, br)


N_TILES_MAX = N_TILES * TM * TOP_K // TM_E + N_EXP
PLAN_LANES = 2 * LANES
assert N_TILES_MAX <= PLAN_LANES and N_EXP <= LANES


def _plan_kernel(cnt_ref, ti_ref, rank_ref, dest_ref, meta_ref, start_s):
    i = pl.program_id(0)

    @pl.when(i == 0)
    def _():
        cnt = cnt_ref[...]
        tiles = jnp.floor((cnt + float(TM_E - 1)) * (1.0 / TM_E))
        sub = lax.broadcasted_iota(jnp.int32, (LANES, LANES), 0)
        lane = lax.broadcasted_iota(jnp.int32, (LANES, LANES), 1)
        upto = jnp.where(sub <= lane, 1.0, 0.0).astype(BF16)
        tile_end = _dot(jnp.broadcast_to(tiles, (SUB, LANES)).astype(BF16), upto)[0:1, :]
        tile_start = tile_end - tiles
        start_s[...] = tile_start * float(TM_E * SUB)
        n_tiles = jnp.max(tile_end, axis=-1, keepdims=True)
        used = tiles > 0.0

        def column(row):
            return jnp.sum(jnp.where(sub == lane, jnp.broadcast_to(row, (LANES, LANES)), 0.0), axis=-1, keepdims=True)

        end_c, start_c, tiles_c = column(tile_end), column(tile_start), column(tiles)
        used_b = jnp.broadcast_to(jnp.where(used, 1.0, 0.0), (LANES, LANES))
        pos_c = jnp.sum(jnp.where(lane <= sub, used_b, 0.0), axis=-1, keepdims=True)
        par_c = (pos_c - 1.0) - 2.0 * jnp.floor((pos_c - 1.0) * 0.5)
        nxt_c = jnp.min(jnp.where(jnp.logical_and(lane > sub, used_b > 0.0), lane.astype(F32), float(LANES)),
                        axis=-1, keepdims=True)
        nxt_c = jnp.where(nxt_c < float(LANES), nxt_c, -1.0)

        tid = lax.broadcasted_iota(jnp.int32, (LANES, PLAN_LANES), 1).astype(F32)
        exp_id = lax.broadcasted_iota(jnp.int32, (LANES, PLAN_LANES), 0)
        tid_used = jnp.minimum(tid, n_tiles - 1.0)
        te = jnp.sum(jnp.where(jnp.logical_and(exp_id < N_EXP, end_c <= tid_used), 1.0, 0.0), axis=0, keepdims=True)
        first = jnp.sum(jnp.where(jnp.logical_and(tiles_c > 0.0, start_c == tid), 1.0, 0.0), axis=0, keepdims=True)
        mine = te == exp_id.astype(F32)
        nxt = jnp.sum(jnp.where(mine, nxt_c, 0.0), axis=0, keepdims=True)
        par = jnp.sum(jnp.where(mine, par_c, 0.0), axis=0, keepdims=True)
        last = jnp.where(used, tile_end - 1.0, 0.0)
        last = jnp.concatenate([last, jnp.zeros((1, PLAN_LANES - LANES), F32)], axis=1)

        row_id = lax.broadcasted_iota(jnp.int32, (SUB, PLAN_LANES), 0)
        meta = jnp.zeros((SUB, PLAN_LANES), F32)
        for r, val in enumerate((te, first, nxt, par, last, jnp.broadcast_to(n_tiles, (1, PLAN_LANES)))):
            meta = jnp.where(row_id == r, jnp.broadcast_to(val, (SUB, PLAN_LANES)), meta)
        meta_ref[...] = meta.astype(jnp.int32)

    ti = ti_ref[...]
    rank = rank_ref[...]
    lane = lax.broadcasted_iota(jnp.int32, ti.shape, 1)
    dest = jnp.zeros(ti.shape, jnp.int32)
    for kk in range(TOP_K):
        base = jnp.sum(jnp.where(lane == ti[:, kk:kk + 1], start_s[...], 0.0), axis=-1, keepdims=True)
        dest = jnp.where(lane == kk, base.astype(jnp.int32) + rank[:, kk:kk + 1] * SUB, dest)
    dest_ref[...] = dest.T[0:SUB, :]


def _route_plan(cnt, ti, rank):
    t = ti.shape[0]
    row = pl.BlockSpec((TM, LANES), lambda i: (i, 0))
    dest, meta = pl.pallas_call(
        _plan_kernel, grid=(t // TM,),
        in_specs=[pl.BlockSpec((1, LANES), lambda i: (0, 0)), row, row],
        out_specs=[pl.BlockSpec((SUB, TM), lambda i: (0, i)), pl.BlockSpec((SUB, PLAN_LANES), lambda i: (0, 0))],
        out_shape=[jax.ShapeDtypeStruct((SUB, t), jnp.int32), jax.ShapeDtypeStruct((SUB, PLAN_LANES), jnp.int32)],
        scratch_shapes=[pltpu.VMEM((1, LANES), F32)],
        compiler_params=_cparams("arbitrary"), name="moe_plan",
    )(cnt, ti, rank)
    plan = dict(tile_expert=meta[0, :N_TILES_MAX], first=meta[1, :N_TILES_MAX], tile_next=meta[2, :N_TILES_MAX],
                tile_parity=meta[3, :N_TILES_MAX], last_tile=meta[4, :N_EXP], n_tiles=meta[5, :1])
    return dest, plan


DISPATCH_BLK = 1024


def _dispatch_kernel(lt_ref, nt_ref, dest_ref, h_ref, xs_ref, h_s, zero_s, sem, hsem):
    i = pl.program_id(0)
    tile_rows = TM_E * SUB
    n_tiles_max = xs_ref.shape[0] // tile_rows
    n_tok = h_s.shape[0] // SUB

    @pl.when(i == 0)
    def _():
        stage = pltpu.make_async_copy(h_ref, h_s, hsem)
        stage.start()
        zero_s[...] = jnp.zeros_like(zero_s)

        def zero_tile(tile):
            r0 = pl.multiple_of(tile * tile_rows, tile_rows)
            return pltpu.make_async_copy(zero_s, xs_ref.at[pl.ds(r0, tile_rows), :], sem)

        def start_unused(j, carry):
            zero_tile(j).start()
            return carry

        def wait_unused(j, carry):
            zero_tile(j).wait()
            return carry

        for e in range(N_EXP):
            zero_tile(lt_ref[e]).start()
        lax.fori_loop(nt_ref[0], n_tiles_max, start_unused, 0)
        for e in range(N_EXP):
            zero_tile(lt_ref[e]).wait()
        lax.fori_loop(nt_ref[0], n_tiles_max, wait_unused, 0)
        stage.wait()

    base = i * DISPATCH_BLK

    def issue(t, carry):
        src = pl.multiple_of((base + t) * SUB, SUB)
        for kk in range(TOP_K):
            row = pl.multiple_of(dest_ref[kk, t], SUB)
            pltpu.make_async_copy(h_s.at[pl.ds(src, SUB), :], xs_ref.at[pl.ds(row, SUB), :], sem).start(priority=kk % 2)
        return carry

    lax.fori_loop(0, DISPATCH_BLK, issue, 0, unroll=2)

    @pl.when(i == pl.num_programs(0) - 1)
    def _():
        for kk in range(TOP_K):
            pltpu.make_async_copy(h_s, xs_ref.at[pl.ds(0, n_tok * SUB), :], sem).wait()


def _dispatch(h2, dest, plan, n_rows):
    t = h2.shape[0] // SUB
    nblk = t // DISPATCH_BLK
    return pl.pallas_call(
        _dispatch_kernel,
        grid_spec=pltpu.PrefetchScalarGridSpec(
            num_scalar_prefetch=2, grid=(nblk,),
            in_specs=[pl.BlockSpec((SUB, DISPATCH_BLK), lambda i, lt, nt: (0, i), memory_space=pltpu.SMEM),
                      pl.BlockSpec(memory_space=pl.ANY)],
            out_specs=pl.BlockSpec(memory_space=pl.ANY),
            scratch_shapes=[pltpu.VMEM((t * SUB, LANES), F32), pltpu.VMEM((TM_E * SUB, LANES), F32),
                            pltpu.SemaphoreType.DMA(()), pltpu.SemaphoreType.DMA(())]),
        out_shape=jax.ShapeDtypeStruct((n_rows * SUB, LANES), F32),
        compiler_params=_cparams("arbitrary"), name="moe_dispatch",
    )(plan["last_tile"], plan["n_tiles"], dest, h2)


def _expert_kernel(te_ref, tf_ref, nt_ref, nx_ref, par_ref, xs_ref, w1_ref, b1_ref, w2_ref, b2_ref, ys_ref,
                   w1f, w2f, w1_s, w2_s, wsem, *, layer):
    i = pl.program_id(0)

    def fetch(expert, slot):
        return (pltpu.make_async_copy(w1_ref.at[layer, expert], w1f.at[slot], wsem.at[slot]),
                pltpu.make_async_copy(w2_ref.at[layer, expert], w2f.at[slot], wsem.at[slot]))

    @pl.when(i == 0)
    def _():
        for cp in fetch(te_ref[0], 0):
            cp.start(priority=1)

    @pl.when(i < nt_ref[0])
    def _():
        @pl.when(tf_ref[i] == 1)
        def _():
            slot = par_ref[i]
            for cp in fetch(te_ref[i], slot):
                cp.wait()
            w1_s[...] = w1f[slot].astype(BF16)
            w2_s[...] = w2f[slot].astype(BF16)

            @pl.when(nx_ref[i] >= 0)
            def _():
                for cp in fetch(nx_ref[i], 1 - slot):
                    cp.start(priority=1)

        x = _load_token_major(xs_ref, (), TM_E)
        u = _dot(x.astype(BF16), w1_s[...]) + b1_ref[...]
        g = jnp.minimum(u[:, :D_FF], SWIGLU_LIMIT)
        up = jnp.clip(u[:, D_FF:], -SWIGLU_LIMIT, SWIGLU_LIMIT)
        act = (up + 1.0) * g * jax.nn.sigmoid(SWIGLU_ALPHA * g)
        _store_token_major(ys_ref, (), _dot(act.astype(BF16), w2_s[...]) + b2_ref[...])

    @pl.when(i >= nt_ref[0])
    def _():
        ys_ref[...] = jnp.zeros_like(ys_ref)


def _experts(xs, plan, layer, w1, b1, w2, b2):
    d = D_MODEL
    nt = xs.shape[0] // (TM_E * SUB)
    tile = lambda i, te, tf, ntl, nx, par: (jnp.minimum(i, ntl[0] - 1), 0)
    otile = lambda i, te, tf, ntl, nx, par: (i, 0)
    bmap = lambda i, te, tf, ntl, nx, par: (layer, te[i], 0, 0)
    return pl.pallas_call(
        functools.partial(_expert_kernel, layer=layer),
        grid_spec=pltpu.PrefetchScalarGridSpec(
            num_scalar_prefetch=5, grid=(nt,),
            in_specs=[pl.BlockSpec((TM_E * SUB, LANES), tile),
                      pl.BlockSpec(memory_space=pl.ANY),
                      pl.BlockSpec((None, None, 1, 2 * D_FF), bmap),
                      pl.BlockSpec(memory_space=pl.ANY),
                      pl.BlockSpec((None, None, 1, d), bmap)],
            out_specs=pl.BlockSpec((TM_E * SUB, LANES), otile),
            scratch_shapes=[pltpu.VMEM((2, d, 2 * D_FF), F32), pltpu.VMEM((2, D_FF, d), F32),
                            pltpu.VMEM((d, 2 * D_FF), BF16), pltpu.VMEM((D_FF, d), BF16),
                            pltpu.SemaphoreType.DMA((2,))]),
        out_shape=jax.ShapeDtypeStruct(xs.shape, F32),
        compiler_params=_cparams("arbitrary"), name="moe_experts",
    )(plan["tile_expert"], plan["first"], plan["n_tiles"], plan["tile_next"], plan["tile_parity"],
      xs, w1, b1, w2, b2)


def _combine_kernel(dest_ref, x1_ref, tw_ref, mod_ref, ys_ref, *rest):
    *o_refs, buf, sem = rest
    i = pl.program_id(0)
    j = i - 1
    n = pl.num_programs(0) - 1
    tile_rows = TM * SUB

    for s in range(2):
        @pl.when(jnp.logical_and(i < n, i % 2 == s))
        def _():
            def issue(t, carry):
                dst = pl.multiple_of(t * SUB, SUB)
                for kk in range(TOP_K):
                    row = pl.multiple_of(dest_ref[kk, t], SUB)
                    pltpu.make_async_copy(ys_ref.at[pl.ds(row, SUB), :], buf.at[s, kk, pl.ds(dst, SUB), :],
                                          sem.at[s]).start(priority=kk % 2)
                return carry

            lax.fori_loop(0, TM, issue, 0, unroll=2)

    @pl.when(j >= 0)
    def _():
        slot = j % 2
        for kk in range(TOP_K):
            pltpu.make_async_copy(ys_ref.at[pl.ds(0, tile_rows), :], buf.at[slot, kk], sem.at[slot]).wait()
        tw = tw_ref[...]
        y = tw[:, 0:1] * _load_token_major(buf, (slot, 0), TM)
        for kk in range(1, TOP_K):
            y = y + tw[:, kk:kk + 1] * _load_token_major(buf, (slot, kk), TM)
        out = x1_ref[...] + mod_ref[5:6, :] * y
        if len(o_refs) == 1:
            o_refs[0][...] = out
        else:
            @pl.when(j < N_CTX_TILES)
            def _():
                o_refs[0][...] = out

            @pl.when(j >= N_CTX_TILES)
            def _():
                o_refs[1][...] = out


def _combine(ys, dest, x1, tw, mod, split):
    t, d = x1.shape
    nblk = t // TM
    prev = lambda i: jnp.maximum(i - 1, 0)
    row = lambda w: pl.BlockSpec((TM, w), lambda i: (prev(i), 0))
    return pl.pallas_call(
        _combine_kernel, grid=(nblk + 1,),
        in_specs=[pl.BlockSpec((SUB, TM), lambda i: (0, jnp.minimum(i, nblk - 1)), memory_space=pltpu.SMEM),
                  row(d), row(LANES),
                  pl.BlockSpec((None, 6, d), lambda i: (_cond_row(prev(i)), 0, 0)),
                  pl.BlockSpec(memory_space=pl.ANY)],
        out_specs=_Rows(x1, x1, 0).specs(lag=1) if split else row(d),
        out_shape=([jax.ShapeDtypeStruct((N_CTX_TILES * TM, d), F32),
                    jax.ShapeDtypeStruct(((N_TILES - N_CTX_TILES) * TM, d), F32)] if split
                   else jax.ShapeDtypeStruct((t, d), F32)),
        scratch_shapes=[pltpu.VMEM((2, TOP_K, TM * SUB, LANES), F32), pltpu.SemaphoreType.DMA((2,))],
        compiler_params=_cparams("arbitrary"), name="moe_combine",
    )(dest, x1, tw, mod, ys)


def _moe(x1, h2, ti, tw, rank, cnt, mod, layer, w1, b1, w2, b2, split):
    depth = w1.shape[0]
    assert x1.shape[0] == N_TILES * TM
    dest, plan = _route_plan(cnt, ti, rank)
    xs = _dispatch(h2, dest, plan, N_TILES_MAX * TM_E)
    ys = _experts(xs, plan, layer, w1, b1.reshape(depth, N_EXP, 1, -1), w2, b2.reshape(depth, N_EXP, 1, -1))
    return _combine(ys, dest, x1, tw, mod, split)


def _pad_lanes(a, value=0.0):
    return jnp.pad(a, ((0, 0), (0, LANES - a.shape[1])), constant_values=value)


def kernel(x_prompt, x_sample, cache_na_k, cache_na_v, state_mlstm_C, state_mlstm_n, state_mlstm_m, state_ret_S, c, c_ctx, w_mod, b_mod, w_in_even, mlstm_gate_b, na_q_norm, na_k_norm, na_rpb, mlstm_norm, w_out_even, w_in_odd, ret_decay, ret_norm, w_out_odd, w_router, b_router, w_moe_in, b_moe_in, w_moe_out, b_moe_out):
    nb_c, s_c, d = x_prompt.shape
    nb_l, s_l, _ = x_sample.shape
    t_c = nb_c * s_c
    t_l = nb_l * s_l
    assert t_c == 4 * SEG and s_l == SEG and d == D_MODEL
    depth = w_mod.shape[0]
    dt = x_prompt.dtype

    x = _Rows(x_prompt.reshape(t_c, d), x_sample.reshape(t_l, d), 0)
    cond = jnp.concatenate([c_ctx[None, :], c, jnp.zeros((N_COND - 1 - nb_l, d), F32)], axis=0)
    mod = _modulation(cond, w_mod, b_mod).reshape(depth, N_COND, 6, d)

    outs = {}
    for l in range(depth):
        e = l // 2
        mod_l = mod[l]
        if l % 2 == 0:
            w_in = w_in_even[e]
            n_main = 3 * W_A + 4 * W_B
            wg = _pad_lanes(w_in[:, n_main:])
            bg = _pad_lanes(mlstm_gate_b[e].reshape(1, 4 * H_B))
            z, g = _inproj(x, mod_l, w_in.astype(BF16), n_main, wg, bg)
            qn = jnp.tile(na_q_norm[e].reshape(1, HD_A), (1, H_A))
            kn = jnp.tile(na_k_norm[e].reshape(1, HD_A), (1, H_A))
            oa_c, ka_c, va_c = _ctx_attention(z, nb_c, s_c, qn, kn)
            past = cache_na_k.shape[2]
            oa_l = _na_attention(z, t_c, nb_l, s_l,
                                 cache_na_k[:, e].reshape(nb_l, past, W_A), cache_na_v[:, e].reshape(nb_l, past, W_A),
                                 _na_bias_table(na_rpb[e]), qn, kn)
            nw = mlstm_norm[e].reshape(1, W_B)
            hm_c, c_fin, n_fin, m_fin = _mlstm(z, g, 0, nb_c, s_c, nw)
            init = (state_mlstm_C[:, e].reshape(nb_l, 2 * H_B, HD_B, HD_B),
                    state_mlstm_n[:, e].reshape(nb_l, 2 * H_B, HD_B),
                    jnp.broadcast_to(state_mlstm_m[:, e].reshape(nb_l, 2 * H_B, 1), (nb_l, 2 * H_B, LANES)))
            hm_l = _mlstm(z, g, t_c, nb_l, s_l, nw, init)[0]
            a = _Rows(oa_c, oa_l, 0)
            b = _Rows(hm_c, hm_l, 0)
            w_out = w_out_even[e].astype(BF16)
            outs.setdefault("na_k", []).append(ka_c.reshape(nb_c, s_c, H_A, HD_A))
            outs.setdefault("na_v", []).append(va_c.reshape(nb_c, s_c, H_A, HD_A))
            outs.setdefault("C", []).append(c_fin.reshape(nb_c, 2, H_B, HD_B, HD_B))
            outs.setdefault("n", []).append(n_fin.reshape(nb_c, 2, H_B, HD_B))
            outs.setdefault("m", []).append(m_fin[:, :, 0].reshape(nb_c, 2, H_B))
        else:
            w_c = H_C * HD_C
            z = _inproj(x, mod_l, w_in_odd[e].astype(BF16), 4 * w_c + N_FG * FG_W)
            dl_rep = jnp.broadcast_to(ret_decay[e].reshape(2 * H_C, 1), (2 * H_C, LANES))
            nw = ret_norm[e].reshape(1, w_c)
            hr_c, s_fin = _retention(z, 0, nb_c, s_c, dl_rep, nw)
            hr_l = _retention(z, t_c, nb_l, s_l, dl_rep, nw, rope=_rope_tables(s_l),
                              init=state_ret_S[:, e].reshape(nb_l, 2 * H_C, HD_C, HD_C))[0]
            fd_c = _fnet(z, 0, nb_c, s_c)
            fd_l = _fnet(z, t_c, nb_l, s_l)
            a = _Rows(hr_c, hr_l, 0)
            b = _Rows(fd_c, fd_l, 0)
            w_out = w_out_odd[e].astype(BF16)
            outs.setdefault("S", []).append(s_fin.reshape(nb_c, 2, H_C, HD_C, HD_C))
        wr = _pad_lanes(w_router[l])
        br = _pad_lanes(b_router[l].reshape(1, N_EXP), NEG)
        x1, h2, ti, tw, rank, cnt = _outproj_router(a, b, x, mod_l, w_out, wr, br)
        last = l == depth - 1
        x = _moe(x1, h2, ti, tw, rank, cnt, mod_l, l, w_moe_in, b_moe_in, w_moe_out, b_moe_out, split=last)
        if not last:
            x = _whole(x)

    y_prompt = x[0].reshape(nb_c, s_c, d)
    y_sample = x[1].reshape(nb_l, s_l, d)
    stack = lambda key: jnp.stack(outs[key], axis=1).astype(dt)
    return (y_prompt, y_sample, stack("na_k"), stack("na_v"), stack("C"), stack("n"), stack("m"), stack("S"))
```

```python
import functools
from typing import NamedTuple

import numpy as np
import jax
import jax.numpy as jnp
from jax import lax
from jax.experimental import pallas as pl
from jax.experimental.pallas import tpu as pltpu

F32 = jnp.float32
BF16 = jnp.bfloat16
HIGHEST = lax.Precision.HIGHEST

D_MODEL = 1024
GRID_W = 64
WIN_R = 8
WIN_C = 16
H_A, HD_A = 8, 64
H_B, HD_B = 4, 128
H_C, HD_C = 4, 128
N_FG, FG_W = 4, 128
W_A = H_A * HD_A
W_B = H_B * HD_B
N_EXP = 32
TOP_K = 4
D_FF = D_MODEL
SWIGLU_LIMIT = 7.0
SWIGLU_ALPHA = 1.702
CHUNK = 128
ROPE_BASE = 10000.0
EPS = 1e-6

LANES = 128
SEG = 1024
N_COND = 8
TM = 512
N_TILES = 8 * SEG // TM
N_CTX_TILES = 4 * SEG // TM
TM_E = 256
NEG = -1e30
VMEM_LIMIT = 56 * 1024 * 1024


def _cparams(*sem):
    return pltpu.CompilerParams(dimension_semantics=sem, vmem_limit_bytes=VMEM_LIMIT)


def _cond_row(i):
    return jnp.maximum((i * TM) // SEG - 3, 0)


def _log_sigmoid(x):
    return jnp.minimum(x, 0.0) - jnp.log1p(jnp.exp(-jnp.abs(x)))


def _dot(a, b):
    return jnp.dot(a, b, preferred_element_type=F32)


def _dot_nt(a, b):
    return lax.dot_general(a, b, (((1,), (1,)), ((), ())), preferred_element_type=F32)


def _dot_hi(a, b):
    return jnp.dot(a, b, precision=HIGHEST, preferred_element_type=F32)


def _split_bf16(x):
    hi = x.astype(BF16)
    return hi, (x - hi.astype(F32)).astype(BF16)


def _dot_x3(x, w_hi, w_lo):
    x_hi, x_lo = _split_bf16(x)
    return _dot(x_hi, w_hi) + (_dot(x_lo, w_hi) + _dot(x_hi, w_lo))


def _mod_kernel(cond_ref, w_ref, b_ref, o_ref):
    c = cond_ref[...]
    s = c * jax.nn.sigmoid(c)
    o_ref[...] = _dot_hi(s, w_ref[...]) + b_ref[...]


def _modulation(cond, w_mod, b_mod):
    depth, d, n = w_mod.shape
    tn = 1536
    return pl.pallas_call(
        _mod_kernel,
        grid=(depth, n // tn),
        in_specs=[pl.BlockSpec((N_COND, d), lambda l, j: (0, 0)),
                  pl.BlockSpec((None, d, tn), lambda l, j: (l, 0, j)),
                  pl.BlockSpec((None, 1, tn), lambda l, j: (l, 0, j))],
        out_specs=pl.BlockSpec((None, N_COND, tn), lambda l, j: (l, 0, j)),
        out_shape=jax.ShapeDtypeStruct((depth, N_COND, n), F32),
        compiler_params=_cparams("arbitrary", "arbitrary"),
        name="modulation",
    )(cond, w_mod, b_mod.reshape(depth, 1, n))


def _rms_mod(x, shift, scale):
    h = x * lax.rsqrt(jnp.mean(x * x, axis=-1, keepdims=True) + EPS)
    return h * (1.0 + scale) + shift


class _Rows(NamedTuple):
    a: jax.Array
    b: jax.Array
    off_b: int

    def specs(self, lag=0):
        width = self.a.shape[1]
        off_b = self.off_b
        tile = (lambda i: i) if lag == 0 else (lambda i: jnp.maximum(i - lag, 0))
        return [pl.BlockSpec((TM, width), lambda i: (jnp.minimum(tile(i), N_CTX_TILES - 1), 0)),
                pl.BlockSpec((TM, width), lambda i: (jnp.maximum(tile(i) - N_CTX_TILES, 0) + off_b, 0))]


def _whole(x):
    return _Rows(x, x, N_CTX_TILES)


def _rows_read(tile, ref_a, ref_b):
    return jnp.where(tile < N_CTX_TILES, ref_a[...], ref_b[...])


def _inproj_kernel(xa_ref, xb_ref, mod_ref, w_ref, z_ref):
    x = _rows_read(pl.program_id(0), xa_ref, xb_ref)
    h = _rms_mod(x, mod_ref[0:1, :], mod_ref[1:2, :])
    z_ref[...] = _dot(h.astype(BF16), w_ref[...])


def _inproj_gate_kernel(xa_ref, xb_ref, mod_ref, w_ref, wgh_ref, wgl_ref, bg_ref, z_ref, g_ref):
    x = _rows_read(pl.program_id(0), xa_ref, xb_ref)
    h = _rms_mod(x, mod_ref[0:1, :], mod_ref[1:2, :])
    z_ref[...] = _dot(h.astype(BF16), w_ref[...])
    g_ref[...] = _dot_x3(h, wgh_ref[...], wgl_ref[...]) + bg_ref[...]


def _inproj(x, mod, w, n, wg=None, bg=None):
    t = N_TILES * TM
    d = x.a.shape[1]
    in_specs = x.specs() + [pl.BlockSpec((None, 6, d), lambda i: (_cond_row(i), 0, 0)),
                            pl.BlockSpec((d, n), lambda i: (0, 0))]
    z_spec = pl.BlockSpec((TM, n), lambda i: (i, 0))
    z_shape = jax.ShapeDtypeStruct((t, n), F32)
    if wg is None:
        return pl.pallas_call(
            _inproj_kernel, grid=(N_TILES,), in_specs=in_specs, out_specs=z_spec, out_shape=z_shape,
            compiler_params=_cparams("arbitrary"), name="inproj",
        )(x.a, x.b, mod, w)
    wg_hi, wg_lo = _split_bf16(wg)
    in_specs += [pl.BlockSpec((d, LANES), lambda i: (0, 0))] * 2 + [pl.BlockSpec((1, LANES), lambda i: (0, 0))]
    return pl.pallas_call(
        _inproj_gate_kernel, grid=(N_TILES,), in_specs=in_specs,
        out_specs=[z_spec, pl.BlockSpec((TM, LANES), lambda i: (i, 0))],
        out_shape=[z_shape, jax.ShapeDtypeStruct((t, LANES), F32)],
        compiler_params=_cparams("arbitrary"), name="inproj_gate",
    )(x.a, x.b, mod, w, wg_hi, wg_lo, bg)


def _head_rms(x, w):
    return x * lax.rsqrt(jnp.mean(x * x, axis=-1, keepdims=True) + EPS) * w


def _fold_lanes(x, op):
    parts = [x[:, c * LANES:(c + 1) * LANES] for c in range(x.shape[1] // LANES)]
    while len(parts) > 1:
        parts = [op(parts[c], parts[c + 1]) if c + 1 < len(parts) else parts[c] for c in range(0, len(parts), 2)]
    return parts[0]


def _head_group_matrix():
    head = np.arange(W_A) // HD_A
    return jnp.asarray((head[:, None] == head[None, :]) / HD_A, F32).astype(BF16)


def _heads_rms(x, w_row, g):
    hi, lo = _split_bf16(x * x)
    return x * lax.rsqrt(_dot(hi, g) + _dot(lo, g) + EPS) * w_row


def _ctx_attn_kernel(q_ref, k_ref, v_ref, qn_ref, kn_ref, g_ref, o_ref, ko_ref, vo_ref, km_s, vm_s):
    seq = q_ref.shape[0]
    scale = HD_A ** -0.5
    g = g_ref[...]
    q = _heads_rms(q_ref[...], qn_ref[...] * scale, g).astype(BF16)
    k = _heads_rms(k_ref[...], kn_ref[...], g)
    v = v_ref[...]
    for h in range(H_A):
        sl = slice(h * HD_A, (h + 1) * HD_A)
        ko_ref[pl.ds(h, seq, stride=H_A), :] = k[:, sl]
        vo_ref[pl.ds(h, seq, stride=H_A), :] = v[:, sl]
    kb = k.astype(BF16)
    vb = v.astype(BF16)
    head = lax.broadcasted_iota(jnp.int32, (seq, W_A), 1) // HD_A
    zero = jnp.zeros((seq, W_A), BF16)
    for h in range(H_A):
        km_s[h * seq:(h + 1) * seq, :] = jnp.where(head == h, kb, zero)
        vm_s[h * seq:(h + 1) * seq, :] = jnp.where(head == h, vb, zero)
    s = _dot_nt(q, km_s[...])
    probs = []
    for h in range(H_A):
        s_h = s[:, h * seq:(h + 1) * seq]
        p = jnp.exp(s_h - jnp.max(_fold_lanes(s_h, jnp.maximum), axis=-1, keepdims=True))
        den = jnp.sum(_fold_lanes(p, jnp.add), axis=-1, keepdims=True)
        probs.append((p / den).astype(BF16))
    o_ref[...] = _dot(jnp.concatenate(probs, axis=1), vm_s[...])


def _ctx_attention(z, n_batch, seq, qn, kn):
    spec = lambda c: pl.BlockSpec((seq, W_A), lambda b: (b, c))
    wspec = pl.BlockSpec((1, W_A), lambda b: (0, 0))
    out = jax.ShapeDtypeStruct((n_batch * seq, W_A), F32)
    return pl.pallas_call(
        _ctx_attn_kernel, grid=(n_batch,),
        in_specs=[spec(0), spec(1), spec(2), wspec, wspec, pl.BlockSpec((W_A, W_A), lambda b: (0, 0))],
        out_specs=[pl.BlockSpec((seq, W_A), lambda b: (b, 0))] + [pl.BlockSpec((seq * H_A, HD_A), lambda b: (b, 0))] * 2,
        out_shape=[out] + [jax.ShapeDtypeStruct((n_batch * seq * H_A, HD_A), F32)] * 2,
        scratch_shapes=[pltpu.VMEM((H_A * seq, W_A), BF16), pltpu.VMEM((H_A * seq, W_A), BF16)],
        compiler_params=_cparams("arbitrary"), name="ctx_attention",
    )(z, z, z, qn, kn, _head_group_matrix())


def _na_bias_table(rpb):
    qc = np.arange(GRID_W)
    kc = np.arange(GRID_W)
    cstart = np.clip(qc - WIN_C // 2, 0, GRID_W - WIN_C)
    col_in = (kc[None, :] >= cstart[:, None]) & (kc[None, :] < cstart[:, None] + WIN_C)
    dc = np.clip(kc[None, :] - qc[:, None], 1 - WIN_C, WIN_C - 1) + WIN_C - 1
    cls = np.arange(WIN_R)
    j = np.arange(WIN_R)
    dr = j[None, :] - cls[:, None] + WIN_R - 1
    sel_r = jnp.asarray(dr[:, :, None] == np.arange(2 * WIN_R - 1)[None, None, :], F32)
    sel_c = jnp.asarray(dc[:, :, None] == np.arange(2 * WIN_C - 1)[None, None, :], F32)
    tab = jnp.einsum("hab,cja,qkb->hcqjk", rpb, sel_r, sel_c, precision=HIGHEST)
    tab = jnp.where(jnp.asarray(col_in)[None, None, :, None, :], tab, NEG)
    return tab.reshape(H_A, WIN_R, GRID_W, WIN_R * GRID_W)


def _na_kernel(q_ref, k_ref, v_ref, kc_ref, vc_ref, bias_ref, qn_ref, kn_ref, g_ref, o_ref,
               kn_s, v_s, kc_s, vc_s, *, rows):
    r = pl.program_id(1)
    scale = HD_A ** -0.5

    @pl.when(r == 0)
    def _():
        kn_s[...] = _heads_rms(k_ref[...], kn_ref[...], g_ref[...]).astype(BF16)
        v_s[...] = v_ref[...].astype(BF16)
        kc_s[...] = kc_ref[...].astype(BF16)
        vc_s[...] = vc_ref[...].astype(BF16)

    rs = jnp.clip(r - WIN_R // 2, 0, rows - WIN_R)
    start = pl.multiple_of(rs * GRID_W, GRID_W)
    n_loc = WIN_R * GRID_W
    for h in range(H_A):
        sl = slice(h * HD_A, (h + 1) * HD_A)
        q = (_head_rms(q_ref[:, sl], qn_ref[:, sl]) * scale).astype(BF16)
        s_loc = _dot_nt(q, kn_s[pl.ds(start, n_loc), sl]) + bias_ref[h]
        s_ctx = _dot_nt(q, kc_s[:, sl])
        m = jnp.max(jnp.maximum(_fold_lanes(s_loc, jnp.maximum), _fold_lanes(s_ctx, jnp.maximum)),
                    axis=-1, keepdims=True)
        p_loc = jnp.exp(s_loc - m)
        p_ctx = jnp.exp(s_ctx - m)
        den = jnp.sum(_fold_lanes(p_loc, jnp.add) + _fold_lanes(p_ctx, jnp.add), axis=-1, keepdims=True)
        o = _dot(p_loc.astype(BF16), v_s[pl.ds(start, n_loc), sl]) + _dot(p_ctx.astype(BF16), vc_s[:, sl])
        o_ref[:, sl] = o / den


def _na_attention(z, row0, n_batch, seq, kc, vc, bias, qn, kn):
    rows = seq // GRID_W
    past = kc.shape[1]
    blk0 = row0 // GRID_W
    sblk0 = row0 // seq

    def cls_of(r):
        return r - jnp.clip(r - WIN_R // 2, 0, rows - WIN_R)

    full = lambda c: pl.BlockSpec((seq, W_A), lambda b, r: (sblk0 + b, c))
    cspec = pl.BlockSpec((None, past, W_A), lambda b, r: (b, 0, 0))
    wspec = pl.BlockSpec((1, W_A), lambda b, r: (0, 0))
    return pl.pallas_call(
        functools.partial(_na_kernel, rows=rows), grid=(n_batch, rows),
        in_specs=[pl.BlockSpec((GRID_W, W_A), lambda b, r: (blk0 + b * rows + r, 0)),
                  full(1), full(2), cspec, cspec,
                  pl.BlockSpec((H_A, None, GRID_W, WIN_R * GRID_W), lambda b, r: (0, cls_of(r), 0, 0)),
                  wspec, wspec, pl.BlockSpec((W_A, W_A), lambda b, r: (0, 0))],
        out_specs=pl.BlockSpec((GRID_W, W_A), lambda b, r: (b * rows + r, 0)),
        out_shape=jax.ShapeDtypeStruct((n_batch * seq, W_A), F32),
        scratch_shapes=[pltpu.VMEM((seq, W_A), BF16), pltpu.VMEM((seq, W_A), BF16),
                        pltpu.VMEM((past, W_A), BF16), pltpu.VMEM((past, W_A), BF16)],
        compiler_params=_cparams("arbitrary", "arbitrary"), name="na_attention",
    )(z, z, z, kc, vc, bias, qn, kn, _head_group_matrix())


def _tri_masks():
    li = lax.broadcasted_iota(jnp.int32, (CHUNK, CHUNK), 0)
    si = lax.broadcasted_iota(jnp.int32, (CHUNK, CHUNK), 1)
    return li >= si, li <= si


def _mlstm_kernel(*refs, nc, has_init):
    if has_init:
        (q_ref, k_ref, v_ref, og_ref, g_ref, nw_ref, c0_ref, n0_ref, m0_ref,
         o_ref, cf_ref, nf_ref, mf_ref, h_s, c_s, n_s, m_s) = refs
    else:
        (q_ref, k_ref, v_ref, og_ref, g_ref, nw_ref,
         o_ref, cf_ref, nf_ref, mf_ref, h_s, c_s, n_s, m_s) = refs
    nd = 2 * H_B
    if has_init:
        c_s[...] = c0_ref[...]
        n_s[...] = n0_ref[...]
        m_s[...] = m0_ref[...]
    else:
        c_s[...] = jnp.zeros_like(c_s)
        n_s[...] = jnp.zeros_like(n_s)
        m_s[...] = jnp.zeros_like(m_s)

    causal, anti = _tri_masks()
    tri_f = causal.astype(F32)
    tri_b = anti.astype(F32)
    kscale = HD_B ** -0.5

    def chunk_step(c, carry):
        for d in range(2):
            cc = c if d == 0 else nc - 1 - c
            t0 = pl.multiple_of(cc * CHUNK, CHUNK)
            g = g_ref[pl.ds(t0, CHUNK), :]
            gt = g.T
            ls = _log_sigmoid(g)
            lst = _log_sigmoid(gt)
            tri_c, tri_r, mask = (tri_f, tri_b, causal) if d == 0 else (tri_b, tri_f, anti)
            b_cols = _dot_hi(tri_c, ls)
            b_rows = _dot_hi(lst, tri_r)
            last = CHUNK - 1 if d == 0 else 0
            for h in range(H_B):
                ci = (2 * d) * H_B + h
                cf = (2 * d + 1) * H_B + h
                hs = slice(h * HD_B, (h + 1) * HD_B)
                q = q_ref[pl.ds(t0, CHUNK), hs]
                k = k_ref[pl.ds(t0, CHUNK), hs] * kscale
                v = v_ref[pl.ds(t0, CHUNK), hs]
                qb, kb, vb = q.astype(BF16), k.astype(BF16), v.astype(BF16)
                b_col = b_cols[:, cf:cf + 1]
                b_row = b_rows[cf:cf + 1, :]
                i_row = gt[ci:ci + 1, :]
                sidx = d * H_B + h
                cst = c_s[sidx]
                nst = n_s[sidx:sidx + 1, :]
                mst = m_s[sidx:sidx + 1, 0:1]
                dmat = jnp.where(mask, b_col - b_row + i_row, -jnp.inf)
                inter = b_col + mst
                mt = jnp.maximum(inter, jnp.max(dmat, axis=-1, keepdims=True))
                w = jnp.exp(dmat - mt) * _dot_nt(qb, kb)
                a = jnp.exp(inter - mt)
                num = _dot(w.astype(BF16), vb) + _dot(qb, cst.astype(BF16)) * a
                den = jnp.sum(w, axis=-1, keepdims=True) + a * jnp.sum(q * nst, axis=-1, keepdims=True)
                hc = num / jnp.maximum(jnp.abs(den), jnp.exp(-mt))
                h_s[d, pl.ds(t0, CHUNK), hs] = hc
                bl = b_row[:, last:last + 1]
                dl = bl - b_row + i_row
                m_new = jnp.maximum(bl + mst, jnp.max(dl, axis=-1, keepdims=True))
                wl = jnp.exp(dl - m_new)
                dec = jnp.exp(bl + mst - m_new)
                kw = (k.T * wl).astype(BF16)
                c_s[sidx] = dec * cst + _dot(kw, vb)
                wl8 = jnp.broadcast_to(wl, (8, CHUNK)).astype(BF16)
                n_s[sidx:sidx + 1, :] = dec * nst + _dot(wl8, kb)[0:1, :]
                m_s[sidx:sidx + 1, :] = jnp.broadcast_to(m_new, (1, LANES))
        return carry

    lax.fori_loop(0, nc, chunk_step, 0)

    for h in range(H_B):
        hs = slice(h * HD_B, (h + 1) * HD_B)
        hsum = h_s[0, :, hs] + h_s[1, :, hs]
        o_ref[:, hs] = _head_rms(hsum, nw_ref[:, hs]) * jax.nn.sigmoid(og_ref[:, hs])
    cf_ref[...] = c_s[...]
    nf_ref[...] = n_s[...]
    mf_ref[...] = m_s[...]


def _mlstm(z, g, row0, n_batch, seq, norm_w, init=None):
    sblk0 = row0 // seq
    nd = 2 * H_B
    spec = lambda c: pl.BlockSpec((seq, W_B), lambda b: (sblk0 + b, c))
    in_specs = [spec(3), spec(4), spec(5), spec(6),
                pl.BlockSpec((seq, LANES), lambda b: (sblk0 + b, 0)),
                pl.BlockSpec((1, W_B), lambda b: (0, 0))]
    args = [z, z, z, z, g, norm_w]
    st_specs = [pl.BlockSpec((None, nd, HD_B, HD_B), lambda b: (b, 0, 0, 0)),
                pl.BlockSpec((None, nd, HD_B), lambda b: (b, 0, 0)),
                pl.BlockSpec((None, nd, LANES), lambda b: (b, 0, 0))]
    if init is not None:
        in_specs += st_specs
        args += list(init)
    return pl.pallas_call(
        functools.partial(_mlstm_kernel, nc=seq // CHUNK, has_init=init is not None), grid=(n_batch,),
        in_specs=in_specs,
        out_specs=[pl.BlockSpec((seq, W_B), lambda b: (b, 0))] + st_specs,
        out_shape=[jax.ShapeDtypeStruct((n_batch * seq, W_B), F32),
                   jax.ShapeDtypeStruct((n_batch, nd, HD_B, HD_B), F32),
                   jax.ShapeDtypeStruct((n_batch, nd, HD_B), F32),
                   jax.ShapeDtypeStruct((n_batch, nd, LANES), F32)],
        scratch_shapes=[pltpu.VMEM((2, seq, W_B), F32), pltpu.VMEM((nd, HD_B, HD_B), F32),
                        pltpu.VMEM((nd, HD_B), F32), pltpu.VMEM((nd, LANES), F32)],
        compiler_params=_cparams("arbitrary"), name="mlstm",
    )(*args)


def _rope_tables(seq):
    half = HD_C // 2
    quarter = half // 2
    t = np.arange(seq)
    inv = ROPE_BASE ** (-np.arange(0, half, 2, dtype=np.float64) / half)
    ang_r = (t // GRID_W)[:, None] * inv[None, :]
    ang_c = (t % GRID_W)[:, None] * inv[None, :]
    cos_t = np.concatenate([np.cos(ang_r), np.cos(ang_r), np.cos(ang_c), np.cos(ang_c)], -1)
    sin_t = np.concatenate([-np.sin(ang_r), np.sin(ang_r), -np.sin(ang_c), np.sin(ang_c)], -1)
    assert cos_t.shape == (seq, 4 * quarter)
    return jnp.asarray(cos_t, F32), jnp.asarray(sin_t, F32)


def _rope(x, cos_t, sin_t):
    quarter = HD_C // 4
    lane = lax.broadcasted_iota(jnp.int32, x.shape, 1)
    first = (lane % (2 * quarter)) < quarter
    swapped = jnp.where(first, pltpu.roll(x, HD_C - quarter, 1), pltpu.roll(x, quarter, 1))
    return x * cos_t + swapped * sin_t


def _ret_kernel(*refs, nc, has_init, use_rope):
    refs = list(refs)
    q_ref, k_ref, v_ref, gg_ref, dl_ref, nw_ref = refs[:6]
    pos = 6
    if use_rope:
        cos_ref, sin_ref = refs[pos:pos + 2]
        pos += 2
    if has_init:
        s0_ref = refs[pos]
        pos += 1
    o_ref, sf_ref, h_s, s_s, dk_s = refs[pos:pos + 5]
    if has_init:
        s_s[...] = s0_ref[...]
    else:
        s_s[...] = jnp.zeros_like(s_s)

    causal, anti = _tri_masks()
    li = lax.broadcasted_iota(jnp.int32, (CHUNK, CHUNK), 0).astype(F32)
    si = lax.broadcasted_iota(jnp.int32, (CHUNK, CHUNK), 1).astype(F32)
    lg_all = _log_sigmoid(dl_ref[...])
    kscale = HD_C ** -0.5
    for d in range(2):
        for h in range(H_C):
            sidx = d * H_C + h
            lg = lg_all[sidx:sidx + 1, :]
            if d == 0:
                dk_s[sidx, 0] = jnp.exp(jnp.where(causal, (li - si) * lg, -jnp.inf))
                dk_s[sidx, 1] = jnp.exp((li + 1.0) * lg)
                dk_s[sidx, 2] = jnp.exp((CHUNK - 1.0 - li) * lg)
            else:
                dk_s[sidx, 0] = jnp.exp(jnp.where(anti, (si - li) * lg, -jnp.inf))
                dk_s[sidx, 1] = jnp.exp((CHUNK - li) * lg)
                dk_s[sidx, 2] = jnp.exp(li * lg)

    def chunk_step(c, carry):
        for d in range(2):
            cc = c if d == 0 else nc - 1 - c
            t0 = pl.multiple_of(cc * CHUNK, CHUNK)
            for h in range(H_C):
                sidx = d * H_C + h
                hs = slice(h * HD_C, (h + 1) * HD_C)
                lg = lg_all[sidx:sidx + 1, :]
                q = q_ref[pl.ds(t0, CHUNK), hs]
                k = k_ref[pl.ds(t0, CHUNK), hs] * kscale
                v = v_ref[pl.ds(t0, CHUNK), hs]
                if use_rope:
                    cos_t = cos_ref[pl.ds(t0, CHUNK), :]
                    sin_t = sin_ref[pl.ds(t0, CHUNK), :]
                    q = _rope(q, cos_t, sin_t)
                    k = _rope(k, cos_t, sin_t)
                decay, q_dec, k_dec = dk_s[sidx, 0], dk_s[sidx, 1], dk_s[sidx, 2]
                c_dec = jnp.exp(CHUNK * lg)
                st = s_s[sidx]
                qb, kb, vb = q.astype(BF16), k.astype(BF16), v.astype(BF16)
                att = _dot_nt(qb, kb) * decay
                o = _dot(att.astype(BF16), vb) + _dot(qb, st.astype(BF16)) * q_dec
                h_s[d, pl.ds(t0, CHUNK), hs] = o
                kd = (k * k_dec).T.astype(BF16)
                s_s[sidx] = c_dec * st + _dot(kd, vb)
        return carry

    lax.fori_loop(0, nc, chunk_step, 0)

    for h in range(H_C):
        hs = slice(h * HD_C, (h + 1) * HD_C)
        osum = h_s[0, :, hs] + h_s[1, :, hs]
        gg = gg_ref[:, hs]
        o_ref[:, hs] = _head_rms(osum, nw_ref[:, hs]) * (gg * jax.nn.sigmoid(gg))
    sf_ref[...] = s_s[...]


def _retention(z, row0, n_batch, seq, decay_rep, norm_w, rope=None, init=None):
    sblk0 = row0 // seq
    nd = 2 * H_C
    w_c = H_C * HD_C
    spec = lambda c: pl.BlockSpec((seq, w_c), lambda b: (sblk0 + b, c))
    in_specs = [spec(0), spec(1), spec(2), spec(3),
                pl.BlockSpec((nd, LANES), lambda b: (0, 0)),
                pl.BlockSpec((1, w_c), lambda b: (0, 0))]
    args = [z, z, z, z, decay_rep, norm_w]
    if rope is not None:
        in_specs += [pl.BlockSpec((seq, HD_C), lambda b: (0, 0))] * 2
        args += list(rope)
    st_spec = pl.BlockSpec((None, nd, HD_C, HD_C), lambda b: (b, 0, 0, 0))
    if init is not None:
        in_specs.append(st_spec)
        args.append(init)
    return pl.pallas_call(
        functools.partial(_ret_kernel, nc=seq // CHUNK, has_init=init is not None, use_rope=rope is not None),
        grid=(n_batch,), in_specs=in_specs,
        out_specs=[pl.BlockSpec((seq, w_c), lambda b: (b, 0)), st_spec],
        out_shape=[jax.ShapeDtypeStruct((n_batch * seq, w_c), F32),
                   jax.ShapeDtypeStruct((n_batch, nd, HD_C, HD_C), F32)],
        scratch_shapes=[pltpu.VMEM((2, seq, w_c), F32), pltpu.VMEM((nd, HD_C, HD_C), F32),
                        pltpu.VMEM((nd, 3, CHUNK, CHUNK), F32)],
        compiler_params=_cparams("arbitrary"), name="retention",
    )(*args)


def _dft_tables(n):
    idx = (np.arange(n)[:, None] * np.arange(n)[None, :]) % n
    ang = 2.0 * np.pi * idx / n
    return np.cos(ang) / np.sqrt(n), np.sin(ang) / np.sqrt(n)


def _fnet_kernel(x_ref, cw_ref, sw_ref, cs_ref, ss_ref, o_ref):
    for g in range(N_FG):
        gs = slice(g * FG_W, (g + 1) * FG_W)
        x = x_ref[:, gs].astype(BF16)
        xc = _dot(x, cw_ref[...]).astype(BF16)
        xs = _dot(x, sw_ref[...]).astype(BF16)
        o_ref[:, gs] = _dot(cs_ref[...], xc) - _dot(ss_ref[...], xs)


def _fnet(z, row0, n_batch, seq):
    sblk0 = row0 // seq
    w_d = N_FG * FG_W
    cw, sw = _dft_tables(FG_W)
    cs, ss = _dft_tables(seq)
    tabs = [jnp.asarray(a, F32).astype(BF16) for a in (cw, sw, cs, ss)]
    wspec = pl.BlockSpec((FG_W, FG_W), lambda b: (0, 0))
    sspec = pl.BlockSpec((seq, seq), lambda b: (0, 0))
    return pl.pallas_call(
        _fnet_kernel, grid=(n_batch,),
        in_specs=[pl.BlockSpec((seq, w_d), lambda b: (sblk0 + b, 4)), wspec, wspec, sspec, sspec],
        out_specs=pl.BlockSpec((seq, w_d), lambda b: (b, 0)),
        out_shape=jax.ShapeDtypeStruct((n_batch * seq, w_d), F32),
        compiler_params=_cparams("arbitrary"), name="fnet",
    )(z, *tabs)


SUB = 8
assert D_MODEL == SUB * LANES


def _store_token_major(ref, lead, val):
    n = val.shape[0]
    for c in range(SUB):
        ref[lead + (pl.ds(c, n, stride=SUB), slice(None))] = val[:, c * LANES:(c + 1) * LANES]


def _load_token_major(ref, lead, n):
    return jnp.concatenate([ref[lead + (pl.ds(c, n, stride=SUB), slice(None))] for c in range(SUB)], axis=1)


def _outproj_router_kernel(aa_ref, ab_ref, ba_ref, bb_ref, xa_ref, xb_ref, mod_ref, wa_ref, wb_ref,
                           wrh_ref, wrl_ref, br_ref, x1_ref, h2_ref, ti_ref, tw_ref, rank_ref, cnt_ref, cnt_s):
    i = pl.program_id(0)
    a = _rows_read(i, aa_ref, ab_ref)
    b = _rows_read(i, ba_ref, bb_ref)
    y = _dot(a.astype(BF16), wa_ref[...]) + _dot(b.astype(BF16), wb_ref[...])
    x1 = _rows_read(i, xa_ref, xb_ref) + mod_ref[2:3, :] * y
    x1_ref[...] = x1
    h2 = _rms_mod(x1, mod_ref[3:4, :], mod_ref[4:5, :])
    _store_token_major(h2_ref, (), h2)
    logits = _dot_x3(h2, wrh_ref[...], wrl_ref[...]) + br_ref[...]
    lane = lax.broadcasted_iota(jnp.int32, logits.shape, 1)
    lane_f = lane.astype(F32)
    cur = logits
    vals, picks = [], []
    ti = jnp.zeros(logits.shape, jnp.int32)
    for kk in range(TOP_K):
        mx = jnp.max(cur, axis=-1, keepdims=True)
        idx = jnp.min(jnp.where(cur == mx, lane_f, float(LANES)), axis=-1, keepdims=True)
        ti = jnp.where(lane == kk, idx.astype(jnp.int32), ti)
        pick = lane_f == idx
        cur = jnp.where(pick, -jnp.inf, cur)
        vals.append(mx)
        picks.append(pick)
    es = [jnp.exp(v - vals[0]) for v in vals]
    tot = es[0] + es[1] + es[2] + es[3]
    tw = jnp.zeros(logits.shape, F32)
    for kk in range(TOP_K):
        tw = jnp.where(lane == kk, es[kk] / tot, tw)
    ti_ref[...] = ti
    tw_ref[...] = tw

    @pl.when(i == 0)
    def _():
        cnt_s[...] = jnp.zeros_like(cnt_s)

    onehot = jnp.zeros(logits.shape, F32)
    for pick in picks:
        onehot = onehot + jnp.where(pick, 1.0, 0.0)
    n = logits.shape[0]
    earlier = (lax.broadcasted_iota(jnp.int32, (n, n), 1) < lax.broadcasted_iota(jnp.int32, (n, n), 0))
    before = cnt_s[...] + _dot(jnp.where(earlier, 1.0, 0.0).astype(BF16), onehot.astype(BF16))
    rank = jnp.zeros(logits.shape, jnp.int32)
    for kk, pick in enumerate(picks):
        r_k = jnp.sum(jnp.where(pick, before, 0.0), axis=-1, keepdims=True)
        rank = jnp.where(lane == kk, r_k.astype(jnp.int32), rank)
    rank_ref[...] = rank
    cnt_s[...] = cnt_s[...] + jnp.sum(onehot, axis=0, keepdims=True)
    cnt_ref[...] = cnt_s[...]


def _outproj_router(a, b, x, mod, w_out, wr, br):
    t = N_TILES * TM
    d = x.a.shape[1]
    wid = a.a.shape[1]
    row = lambda w: pl.BlockSpec((TM, w), lambda i: (i, 0))
    const = lambda r, c: pl.BlockSpec((r, c), lambda i: (0, 0))
    wr_hi, wr_lo = _split_bf16(wr)
    return pl.pallas_call(
        _outproj_router_kernel, grid=(N_TILES,),
        in_specs=a.specs() + b.specs() + x.specs() + [
                  pl.BlockSpec((None, 6, d), lambda i: (_cond_row(i), 0, 0)),
                  pl.BlockSpec((wid, d), lambda i: (0, 0)), pl.BlockSpec((wid, d), lambda i: (1, 0)),
                  const(d, LANES), const(d, LANES), const(1, LANES)],
        out_specs=[row(d), pl.BlockSpec((TM * SUB, LANES), lambda i: (i, 0)), row(LANES), row(LANES), row(LANES),
                   const(1, LANES)],
        out_shape=[jax.ShapeDtypeStruct((t, d), F32), jax.ShapeDtypeStruct((t * SUB, LANES), F32),
                   jax.ShapeDtypeStruct((t, LANES), jnp.int32), jax.ShapeDtypeStruct((t, LANES), F32),
                   jax.ShapeDtypeStruct((t, LANES), jnp.int32), jax.ShapeDtypeStruct((1, LANES), F32)],
        scratch_shapes=[pltpu.VMEM((1, LANES), F32)],
        compiler_params=_cparams("arbitrary"), name="outproj_router",
    )(a.a, a.b, b.a, b.b, x.a, x.b, mod, w_out, w_out, wr_hi, wr_lo, br)


N_TILES_MAX = N_TILES * TM * TOP_K // TM_E + N_EXP
PLAN_LANES = 2 * LANES
assert N_TILES_MAX <= PLAN_LANES and N_EXP <= LANES


def _plan_kernel(cnt_ref, ti_ref, rank_ref, dest_ref, meta_ref, start_s):
    i = pl.program_id(0)

    @pl.when(i == 0)
    def _():
        cnt = cnt_ref[...]
        tiles = jnp.floor((cnt + float(TM_E - 1)) * (1.0 / TM_E))
        sub = lax.broadcasted_iota(jnp.int32, (LANES, LANES), 0)
        lane = lax.broadcasted_iota(jnp.int32, (LANES, LANES), 1)
        upto = jnp.where(sub <= lane, 1.0, 0.0).astype(BF16)
        tile_end = _dot(jnp.broadcast_to(tiles, (SUB, LANES)).astype(BF16), upto)[0:1, :]
        tile_start = tile_end - tiles
        start_s[...] = tile_start * float(TM_E * SUB)
        n_tiles = jnp.max(tile_end, axis=-1, keepdims=True)
        used = tiles > 0.0

        def column(row):
            return jnp.sum(jnp.where(sub == lane, jnp.broadcast_to(row, (LANES, LANES)), 0.0), axis=-1, keepdims=True)

        end_c, start_c, tiles_c = column(tile_end), column(tile_start), column(tiles)
        used_b = jnp.broadcast_to(jnp.where(used, 1.0, 0.0), (LANES, LANES))
        pos_c = jnp.sum(jnp.where(lane <= sub, used_b, 0.0), axis=-1, keepdims=True)
        par_c = (pos_c - 1.0) - 2.0 * jnp.floor((pos_c - 1.0) * 0.5)
        nxt_c = jnp.min(jnp.where(jnp.logical_and(lane > sub, used_b > 0.0), lane.astype(F32), float(LANES)),
                        axis=-1, keepdims=True)
        nxt_c = jnp.where(nxt_c < float(LANES), nxt_c, -1.0)

        tid = lax.broadcasted_iota(jnp.int32, (LANES, PLAN_LANES), 1).astype(F32)
        exp_id = lax.broadcasted_iota(jnp.int32, (LANES, PLAN_LANES), 0)
        tid_used = jnp.minimum(tid, n_tiles - 1.0)
        te = jnp.sum(jnp.where(jnp.logical_and(exp_id < N_EXP, end_c <= tid_used), 1.0, 0.0), axis=0, keepdims=True)
        first = jnp.sum(jnp.where(jnp.logical_and(tiles_c > 0.0, start_c == tid), 1.0, 0.0), axis=0, keepdims=True)
        mine = te == exp_id.astype(F32)
        nxt = jnp.sum(jnp.where(mine, nxt_c, 0.0), axis=0, keepdims=True)
        par = jnp.sum(jnp.where(mine, par_c, 0.0), axis=0, keepdims=True)
        last = jnp.where(used, tile_end - 1.0, 0.0)
        last = jnp.concatenate([last, jnp.zeros((1, PLAN_LANES - LANES), F32)], axis=1)

        row_id = lax.broadcasted_iota(jnp.int32, (SUB, PLAN_LANES), 0)
        meta = jnp.zeros((SUB, PLAN_LANES), F32)
        for r, val in enumerate((te, first, nxt, par, last, jnp.broadcast_to(n_tiles, (1, PLAN_LANES)))):
            meta = jnp.where(row_id == r, jnp.broadcast_to(val, (SUB, PLAN_LANES)), meta)
        meta_ref[...] = meta.astype(jnp.int32)

    ti = ti_ref[...]
    rank = rank_ref[...]
    lane = lax.broadcasted_iota(jnp.int32, ti.shape, 1)
    dest = jnp.zeros(ti.shape, jnp.int32)
    for kk in range(TOP_K):
        base = jnp.sum(jnp.where(lane == ti[:, kk:kk + 1], start_s[...], 0.0), axis=-1, keepdims=True)
        dest = jnp.where(lane == kk, base.astype(jnp.int32) + rank[:, kk:kk + 1] * SUB, dest)
    dest_ref[...] = dest.T[0:SUB, :]


def _route_plan(cnt, ti, rank):
    t = ti.shape[0]
    row = pl.BlockSpec((TM, LANES), lambda i: (i, 0))
    dest, meta = pl.pallas_call(
        _plan_kernel, grid=(t // TM,),
        in_specs=[pl.BlockSpec((1, LANES), lambda i: (0, 0)), row, row],
        out_specs=[pl.BlockSpec((SUB, TM), lambda i: (0, i)), pl.BlockSpec((SUB, PLAN_LANES), lambda i: (0, 0))],
        out_shape=[jax.ShapeDtypeStruct((SUB, t), jnp.int32), jax.ShapeDtypeStruct((SUB, PLAN_LANES), jnp.int32)],
        scratch_shapes=[pltpu.VMEM((1, LANES), F32)],
        compiler_params=_cparams("arbitrary"), name="moe_plan",
    )(cnt, ti, rank)
    plan = dict(tile_expert=meta[0, :N_TILES_MAX], first=meta[1, :N_TILES_MAX], tile_next=meta[2, :N_TILES_MAX],
                tile_parity=meta[3, :N_TILES_MAX], last_tile=meta[4, :N_EXP], n_tiles=meta[5, :1])
    return dest, plan


DISPATCH_BLK = 1024


def _dispatch_kernel(lt_ref, nt_ref, *rest):
    dest_refs, (h_ref, xs_ref, h_s, zero_s, sem, hsem) = rest[:TOP_K], rest[TOP_K:]
    i = pl.program_id(0)
    tile_rows = TM_E * SUB
    n_tiles_max = xs_ref.shape[0] // tile_rows
    n_tok = h_s.shape[0] // SUB

    @pl.when(i == 0)
    def _():
        stage = pltpu.make_async_copy(h_ref, h_s, hsem)
        stage.start()
        zero_s[...] = jnp.zeros_like(zero_s)

        def zero_tile(tile):
            r0 = pl.multiple_of(tile * tile_rows, tile_rows)
            return pltpu.make_async_copy(zero_s, xs_ref.at[pl.ds(r0, tile_rows), :], sem)

        def start_unused(j, carry):
            zero_tile(j).start()
            return carry

        def wait_unused(j, carry):
            zero_tile(j).wait()
            return carry

        for e in range(N_EXP):
            zero_tile(lt_ref[e]).start()
        lax.fori_loop(nt_ref[0], n_tiles_max, start_unused, 0)
        for e in range(N_EXP):
            zero_tile(lt_ref[e]).wait()
        lax.fori_loop(nt_ref[0], n_tiles_max, wait_unused, 0)
        stage.wait()

    base = i * DISPATCH_BLK

    def issue(t, carry):
        src = pl.multiple_of((base + t) * SUB, SUB)
        for kk in range(TOP_K):
            row = pl.multiple_of(dest_refs[kk][0, t], SUB)
            pltpu.make_async_copy(h_s.at[pl.ds(src, SUB), :], xs_ref.at[pl.ds(row, SUB), :], sem).start(priority=kk % 2)
        return carry

    lax.fori_loop(0, DISPATCH_BLK, issue, 0, unroll=2)

    @pl.when(i == pl.num_programs(0) - 1)
    def _():
        for kk in range(TOP_K):
            pltpu.make_async_copy(h_s, xs_ref.at[pl.ds(0, n_tok * SUB), :], sem).wait()


def _dispatch(h2, dest, plan, n_rows):
    t = h2.shape[0] // SUB
    nblk = t // DISPATCH_BLK
    return pl.pallas_call(
        _dispatch_kernel,
        grid_spec=pltpu.PrefetchScalarGridSpec(
            num_scalar_prefetch=2, grid=(nblk,),
            in_specs=[pl.BlockSpec((None, 1, DISPATCH_BLK), lambda i, lt, nt: (i, 0, 0), memory_space=pltpu.SMEM)] * TOP_K
                     + [pl.BlockSpec(memory_space=pl.ANY)],
            out_specs=pl.BlockSpec(memory_space=pl.ANY),
            scratch_shapes=[pltpu.VMEM((t * SUB, LANES), F32), pltpu.VMEM((TM_E * SUB, LANES), F32),
                            pltpu.SemaphoreType.DMA(()), pltpu.SemaphoreType.DMA(())]),
        out_shape=jax.ShapeDtypeStruct((n_rows * SUB, LANES), F32),
        compiler_params=_cparams("arbitrary"), name="moe_dispatch",
    )(plan["last_tile"], plan["n_tiles"], *[dest[kk].reshape(nblk, 1, DISPATCH_BLK) for kk in range(TOP_K)], h2)


def _expert_kernel(te_ref, tf_ref, nt_ref, nx_ref, par_ref, xs_ref, w1_ref, b1_ref, w2_ref, b2_ref, ys_ref,
                   w1f, w2f, w1_s, w2_s, wsem, *, layer):
    i = pl.program_id(0)

    def fetch(expert, slot):
        return (pltpu.make_async_copy(w1_ref.at[layer, expert], w1f.at[slot], wsem.at[slot]),
                pltpu.make_async_copy(w2_ref.at[layer, expert], w2f.at[slot], wsem.at[slot]))

    @pl.when(i == 0)
    def _():
        for cp in fetch(te_ref[0], 0):
            cp.start(priority=1)

    @pl.when(i < nt_ref[0])
    def _():
        @pl.when(tf_ref[i] == 1)
        def _():
            slot = par_ref[i]
            for cp in fetch(te_ref[i], slot):
                cp.wait()
            w1_s[...] = w1f[slot].astype(BF16)
            w2_s[...] = w2f[slot].astype(BF16)

            @pl.when(nx_ref[i] >= 0)
            def _():
                for cp in fetch(nx_ref[i], 1 - slot):
                    cp.start(priority=1)

        x = _load_token_major(xs_ref, (), TM_E)
        u = _dot(x.astype(BF16), w1_s[...]) + b1_ref[...]
        g = jnp.minimum(u[:, :D_FF], SWIGLU_LIMIT)
        up = jnp.clip(u[:, D_FF:], -SWIGLU_LIMIT, SWIGLU_LIMIT)
        act = (up + 1.0) * g * jax.nn.sigmoid(SWIGLU_ALPHA * g)
        _store_token_major(ys_ref, (), _dot(act.astype(BF16), w2_s[...]) + b2_ref[...])

    @pl.when(i >= nt_ref[0])
    def _():
        ys_ref[...] = jnp.zeros_like(ys_ref)


def _experts(xs, plan, layer, w1, b1, w2, b2):
    d = D_MODEL
    nt = xs.shape[0] // (TM_E * SUB)
    tile = lambda i, te, tf, ntl, nx, par: (jnp.minimum(i, ntl[0] - 1), 0)
    otile = lambda i, te, tf, ntl, nx, par: (i, 0)
    bmap = lambda i, te, tf, ntl, nx, par: (layer, te[i], 0, 0)
    return pl.pallas_call(
        functools.partial(_expert_kernel, layer=layer),
        grid_spec=pltpu.PrefetchScalarGridSpec(
            num_scalar_prefetch=5, grid=(nt,),
            in_specs=[pl.BlockSpec((TM_E * SUB, LANES), tile),
                      pl.BlockSpec(memory_space=pl.ANY),
                      pl.BlockSpec((None, None, 1, 2 * D_FF), bmap),
                      pl.BlockSpec(memory_space=pl.ANY),
                      pl.BlockSpec((None, None, 1, d), bmap)],
            out_specs=pl.BlockSpec((TM_E * SUB, LANES), otile),
            scratch_shapes=[pltpu.VMEM((2, d, 2 * D_FF), F32), pltpu.VMEM((2, D_FF, d), F32),
                            pltpu.VMEM((d, 2 * D_FF), BF16), pltpu.VMEM((D_FF, d), BF16),
                            pltpu.SemaphoreType.DMA((2,))]),
        out_shape=jax.ShapeDtypeStruct(xs.shape, F32),
        compiler_params=_cparams("arbitrary"), name="moe_experts",
    )(plan["tile_expert"], plan["first"], plan["n_tiles"], plan["tile_next"], plan["tile_parity"],
      xs, w1, b1, w2, b2)


def _combine_kernel(*refs):
    dest_refs = refs[:TOP_K]
    x1_ref, tw_ref, mod_ref, ys_ref, *o_refs, buf, sem = refs[TOP_K:]
    i = pl.program_id(0)
    j = i - 1
    n = pl.num_programs(0) - 1
    tile_rows = TM * SUB

    for s in range(2):
        @pl.when(jnp.logical_and(i < n, i % 2 == s))
        def _():
            def issue(t, carry):
                dst = pl.multiple_of(t * SUB, SUB)
                for kk in range(TOP_K):
                    row = pl.multiple_of(dest_refs[kk][0, t], SUB)
                    pltpu.make_async_copy(ys_ref.at[pl.ds(row, SUB), :], buf.at[s, kk, pl.ds(dst, SUB), :],
                                          sem.at[s]).start(priority=kk % 2)
                return carry

            lax.fori_loop(0, TM, issue, 0, unroll=2)

    @pl.when(j >= 0)
    def _():
        slot = j % 2
        for kk in range(TOP_K):
            pltpu.make_async_copy(ys_ref.at[pl.ds(0, tile_rows), :], buf.at[slot, kk], sem.at[slot]).wait()
        tw = tw_ref[...]
        y = tw[:, 0:1] * _load_token_major(buf, (slot, 0), TM)
        for kk in range(1, TOP_K):
            y = y + tw[:, kk:kk + 1] * _load_token_major(buf, (slot, kk), TM)
        out = x1_ref[...] + mod_ref[5:6, :] * y
        if len(o_refs) == 1:
            o_refs[0][...] = out
        else:
            @pl.when(j < N_CTX_TILES)
            def _():
                o_refs[0][...] = out

            @pl.when(j >= N_CTX_TILES)
            def _():
                o_refs[1][...] = out


def _combine(ys, dest, x1, tw, mod, split):
    t, d = x1.shape
    nblk = t // TM
    prev = lambda i: jnp.maximum(i - 1, 0)
    row = lambda w: pl.BlockSpec((TM, w), lambda i: (prev(i), 0))
    return pl.pallas_call(
        _combine_kernel, grid=(nblk + 1,),
        in_specs=[pl.BlockSpec((None, 1, TM), lambda i: (jnp.minimum(i, nblk - 1), 0, 0), memory_space=pltpu.SMEM)] * TOP_K + [
                  row(d), row(LANES),
                  pl.BlockSpec((None, 6, d), lambda i: (_cond_row(prev(i)), 0, 0)),
                  pl.BlockSpec(memory_space=pl.ANY)],
        out_specs=_Rows(x1, x1, 0).specs(lag=1) if split else row(d),
        out_shape=([jax.ShapeDtypeStruct((N_CTX_TILES * TM, d), F32),
                    jax.ShapeDtypeStruct(((N_TILES - N_CTX_TILES) * TM, d), F32)] if split
                   else jax.ShapeDtypeStruct((t, d), F32)),
        scratch_shapes=[pltpu.VMEM((2, TOP_K, TM * SUB, LANES), F32), pltpu.SemaphoreType.DMA((2,))],
        compiler_params=_cparams("arbitrary"), name="moe_combine",
    )(*[dest[kk].reshape(nblk, 1, TM) for kk in range(TOP_K)], x1, tw, mod, ys)


def _moe(x1, h2, ti, tw, rank, cnt, mod, layer, w1, b1, w2, b2, split):
    depth = w1.shape[0]
    assert x1.shape[0] == N_TILES * TM
    dest, plan = _route_plan(cnt, ti, rank)
    xs = _dispatch(h2, dest, plan, N_TILES_MAX * TM_E)
    ys = _experts(xs, plan, layer, w1, b1.reshape(depth, N_EXP, 1, -1), w2, b2.reshape(depth, N_EXP, 1, -1))
    return _combine(ys, dest, x1, tw, mod, split)


def _pad_lanes(a, value=0.0):
    return jnp.pad(a, ((0, 0), (0, LANES - a.shape[1])), constant_values=value)


def kernel(x_prompt, x_sample, cache_na_k, cache_na_v, state_mlstm_C, state_mlstm_n, state_mlstm_m, state_ret_S, c, c_ctx, w_mod, b_mod, w_in_even, mlstm_gate_b, na_q_norm, na_k_norm, na_rpb, mlstm_norm, w_out_even, w_in_odd, ret_decay, ret_norm, w_out_odd, w_router, b_router, w_moe_in, b_moe_in, w_moe_out, b_moe_out):
    nb_c, s_c, d = x_prompt.shape
    nb_l, s_l, _ = x_sample.shape
    t_c = nb_c * s_c
    t_l = nb_l * s_l
    assert t_c == 4 * SEG and s_l == SEG and d == D_MODEL
    depth = w_mod.shape[0]
    dt = x_prompt.dtype

    x = _Rows(x_prompt.reshape(t_c, d), x_sample.reshape(t_l, d), 0)
    cond = jnp.concatenate([c_ctx[None, :], c, jnp.zeros((N_COND - 1 - nb_l, d), F32)], axis=0)
    mod = _modulation(cond, w_mod, b_mod).reshape(depth, N_COND, 6, d)

    outs = {}
    for l in range(depth):
        e = l // 2
        mod_l = mod[l]
        if l % 2 == 0:
            w_in = w_in_even[e]
            n_main = 3 * W_A + 4 * W_B
            wg = _pad_lanes(w_in[:, n_main:])
            bg = _pad_lanes(mlstm_gate_b[e].reshape(1, 4 * H_B))
            z, g = _inproj(x, mod_l, w_in.astype(BF16), n_main, wg, bg)
            qn = jnp.tile(na_q_norm[e].reshape(1, HD_A), (1, H_A))
            kn = jnp.tile(na_k_norm[e].reshape(1, HD_A), (1, H_A))
            oa_c, ka_c, va_c = _ctx_attention(z, nb_c, s_c, qn, kn)
            past = cache_na_k.shape[2]
            oa_l = _na_attention(z, t_c, nb_l, s_l,
                                 cache_na_k[:, e].reshape(nb_l, past, W_A), cache_na_v[:, e].reshape(nb_l, past, W_A),
                                 _na_bias_table(na_rpb[e]), qn, kn)
            nw = mlstm_norm[e].reshape(1, W_B)
            hm_c, c_fin, n_fin, m_fin = _mlstm(z, g, 0, nb_c, s_c, nw)
            init = (state_mlstm_C[:, e].reshape(nb_l, 2 * H_B, HD_B, HD_B),
                    state_mlstm_n[:, e].reshape(nb_l, 2 * H_B, HD_B),
                    jnp.broadcast_to(state_mlstm_m[:, e].reshape(nb_l, 2 * H_B, 1), (nb_l, 2 * H_B, LANES)))
            hm_l = _mlstm(z, g, t_c, nb_l, s_l, nw, init)[0]
            a = _Rows(oa_c, oa_l, 0)
            b = _Rows(hm_c, hm_l, 0)
            w_out = w_out_even[e].astype(BF16)
            outs.setdefault("na_k", []).append(ka_c.reshape(nb_c, s_c, H_A, HD_A))
            outs.setdefault("na_v", []).append(va_c.reshape(nb_c, s_c, H_A, HD_A))
            outs.setdefault("C", []).append(c_fin.reshape(nb_c, 2, H_B, HD_B, HD_B))
            outs.setdefault("n", []).append(n_fin.reshape(nb_c, 2, H_B, HD_B))
            outs.setdefault("m", []).append(m_fin[:, :, 0].reshape(nb_c, 2, H_B))
        else:
            w_c = H_C * HD_C
            z = _inproj(x, mod_l, w_in_odd[e].astype(BF16), 4 * w_c + N_FG * FG_W)
            dl_rep = jnp.broadcast_to(ret_decay[e].reshape(2 * H_C, 1), (2 * H_C, LANES))
            nw = ret_norm[e].reshape(1, w_c)
            hr_c, s_fin = _retention(z, 0, nb_c, s_c, dl_rep, nw)
            hr_l = _retention(z, t_c, nb_l, s_l, dl_rep, nw, rope=_rope_tables(s_l),
                              init=state_ret_S[:, e].reshape(nb_l, 2 * H_C, HD_C, HD_C))[0]
            fd_c = _fnet(z, 0, nb_c, s_c)
            fd_l = _fnet(z, t_c, nb_l, s_l)
            a = _Rows(hr_c, hr_l, 0)
            b = _Rows(fd_c, fd_l, 0)
            w_out = w_out_odd[e].astype(BF16)
            outs.setdefault("S", []).append(s_fin.reshape(nb_c, 2, H_C, HD_C, HD_C))
        wr = _pad_lanes(w_router[l])
        br = _pad_lanes(b_router[l].reshape(1, N_EXP), NEG)
        x1, h2, ti, tw, rank, cnt = _outproj_router(a, b, x, mod_l, w_out, wr, br)
        last = l == depth - 1
        x = _moe(x1, h2, ti, tw, rank, cnt, mod_l, l, w_moe_in, b_moe_in, w_moe_out, b_moe_out, split=last)
        if not last:
            x = _whole(x)

    y_prompt = x[0].reshape(nb_c, s_c, d)
    y_sample = x[1].reshape(nb_l, s_l, d)
    stack = lambda key: jnp.stack(outs[key], axis=1).astype(dt)
    return (y_prompt, y_sample, stack("na_k"), stack("na_v"), stack("C"), stack("n"), stack("m"), stack("S"))
```

```python
import functools
from typing import NamedTuple

import numpy as np
import jax
import jax.numpy as jnp
from jax import lax
from jax.experimental import pallas as pl
from jax.experimental.pallas import tpu as pltpu

F32 = jnp.float32
BF16 = jnp.bfloat16
HIGHEST = lax.Precision.HIGHEST

D_MODEL = 1024
GRID_W = 64
WIN_R = 8
WIN_C = 16
H_A, HD_A = 8, 64
H_B, HD_B = 4, 128
H_C, HD_C = 4, 128
N_FG, FG_W = 4, 128
W_A = H_A * HD_A
W_B = H_B * HD_B
N_EXP = 32
TOP_K = 4
D_FF = D_MODEL
SWIGLU_LIMIT = 7.0
SWIGLU_ALPHA = 1.702
CHUNK = 128
ROPE_BASE = 10000.0
EPS = 1e-6

LANES = 128
SEG = 1024
N_COND = 8
TM = 512
N_TILES = 8 * SEG // TM
N_CTX_TILES = 4 * SEG // TM
TM_E = 256
NEG = -1e30
VMEM_LIMIT = 56 * 1024 * 1024


def _cparams(*sem):
    return pltpu.CompilerParams(dimension_semantics=sem, vmem_limit_bytes=VMEM_LIMIT)


def _cond_row(i):
    return jnp.maximum((i * TM) // SEG - 3, 0)


def _log_sigmoid(x):
    return jnp.minimum(x, 0.0) - jnp.log1p(jnp.exp(-jnp.abs(x)))


def _dot(a, b):
    return jnp.dot(a, b, preferred_element_type=F32)


def _dot_nt(a, b):
    return lax.dot_general(a, b, (((1,), (1,)), ((), ())), preferred_element_type=F32)


def _dot_hi(a, b):
    return jnp.dot(a, b, precision=HIGHEST, preferred_element_type=F32)


def _split_bf16(x):
    hi = x.astype(BF16)
    return hi, (x - hi.astype(F32)).astype(BF16)


def _dot_x3(x, w_hi, w_lo):
    x_hi, x_lo = _split_bf16(x)
    return _dot(x_hi, w_hi) + (_dot(x_lo, w_hi) + _dot(x_hi, w_lo))


def _mod_kernel(cond_ref, w_ref, b_ref, o_ref):
    c = cond_ref[...]
    s = c * jax.nn.sigmoid(c)
    o_ref[...] = _dot_hi(s, w_ref[...]) + b_ref[...]


def _modulation(cond, w_mod, b_mod):
    depth, d, n = w_mod.shape
    tn = 1536
    return pl.pallas_call(
        _mod_kernel,
        grid=(depth, n // tn),
        in_specs=[pl.BlockSpec((N_COND, d), lambda l, j: (0, 0)),
                  pl.BlockSpec((None, d, tn), lambda l, j: (l, 0, j)),
                  pl.BlockSpec((None, 1, tn), lambda l, j: (l, 0, j))],
        out_specs=pl.BlockSpec((None, N_COND, tn), lambda l, j: (l, 0, j)),
        out_shape=jax.ShapeDtypeStruct((depth, N_COND, n), F32),
        compiler_params=_cparams("arbitrary", "arbitrary"),
        name="modulation",
    )(cond, w_mod, b_mod.reshape(depth, 1, n))


def _rms_mod(x, shift, scale):
    h = x * lax.rsqrt(jnp.mean(x * x, axis=-1, keepdims=True) + EPS)
    return h * (1.0 + scale) + shift


class _Rows(NamedTuple):
    a: jax.Array
    b: jax.Array
    off_b: int

    def specs(self, lag=0):
        width = self.a.shape[1]
        off_b = self.off_b
        tile = (lambda i: i) if lag == 0 else (lambda i: jnp.maximum(i - lag, 0))
        return [pl.BlockSpec((TM, width), lambda i: (jnp.minimum(tile(i), N_CTX_TILES - 1), 0)),
                pl.BlockSpec((TM, width), lambda i: (jnp.maximum(tile(i) - N_CTX_TILES, 0) + off_b, 0))]


def _whole(x):
    return _Rows(x, x, N_CTX_TILES)


def _rows_read(tile, ref_a, ref_b):
    return jnp.where(tile < N_CTX_TILES, ref_a[...], ref_b[...])


def _inproj_kernel(xa_ref, xb_ref, mod_ref, w_ref, z_ref):
    x = _rows_read(pl.program_id(0), xa_ref, xb_ref)
    h = _rms_mod(x, mod_ref[0:1, :], mod_ref[1:2, :])
    z_ref[...] = _dot(h.astype(BF16), w_ref[...])


def _inproj_gate_kernel(xa_ref, xb_ref, mod_ref, w_ref, wgh_ref, wgl_ref, bg_ref, z_ref, g_ref):
    x = _rows_read(pl.program_id(0), xa_ref, xb_ref)
    h = _rms_mod(x, mod_ref[0:1, :], mod_ref[1:2, :])
    z_ref[...] = _dot(h.astype(BF16), w_ref[...])
    g_ref[...] = _dot_x3(h, wgh_ref[...], wgl_ref[...]) + bg_ref[...]


def _inproj(x, mod, w, n, wg=None, bg=None):
    t = N_TILES * TM
    d = x.a.shape[1]
    in_specs = x.specs() + [pl.BlockSpec((None, 6, d), lambda i: (_cond_row(i), 0, 0)),
                            pl.BlockSpec((d, n), lambda i: (0, 0))]
    z_spec = pl.BlockSpec((TM, n), lambda i: (i, 0))
    z_shape = jax.ShapeDtypeStruct((t, n), F32)
    if wg is None:
        return pl.pallas_call(
            _inproj_kernel, grid=(N_TILES,), in_specs=in_specs, out_specs=z_spec, out_shape=z_shape,
            compiler_params=_cparams("arbitrary"), name="inproj",
        )(x.a, x.b, mod, w)
    wg_hi, wg_lo = _split_bf16(wg)
    in_specs += [pl.BlockSpec((d, LANES), lambda i: (0, 0))] * 2 + [pl.BlockSpec((1, LANES), lambda i: (0, 0))]
    return pl.pallas_call(
        _inproj_gate_kernel, grid=(N_TILES,), in_specs=in_specs,
        out_specs=[z_spec, pl.BlockSpec((TM, LANES), lambda i: (i, 0))],
        out_shape=[z_shape, jax.ShapeDtypeStruct((t, LANES), F32)],
        compiler_params=_cparams("arbitrary"), name="inproj_gate",
    )(x.a, x.b, mod, w, wg_hi, wg_lo, bg)


def _head_rms(x, w):
    return x * lax.rsqrt(jnp.mean(x * x, axis=-1, keepdims=True) + EPS) * w


def _fold_lanes(x, op):
    parts = [x[:, c * LANES:(c + 1) * LANES] for c in range(x.shape[1] // LANES)]
    while len(parts) > 1:
        parts = [op(parts[c], parts[c + 1]) if c + 1 < len(parts) else parts[c] for c in range(0, len(parts), 2)]
    return parts[0]


def _head_group_matrix():
    head = np.arange(W_A) // HD_A
    return jnp.asarray((head[:, None] == head[None, :]) / HD_A, F32).astype(BF16)


def _heads_rms(x, w_row, g):
    hi, lo = _split_bf16(x * x)
    return x * lax.rsqrt(_dot(hi, g) + _dot(lo, g) + EPS) * w_row


def _ctx_attn_kernel(q_ref, k_ref, v_ref, qn_ref, kn_ref, g_ref, o_ref, ko_ref, vo_ref, km_s, vm_s):
    seq = q_ref.shape[0]
    scale = HD_A ** -0.5
    g = g_ref[...]
    q = _heads_rms(q_ref[...], qn_ref[...] * scale, g).astype(BF16)
    k = _heads_rms(k_ref[...], kn_ref[...], g)
    v = v_ref[...]
    for h in range(H_A):
        sl = slice(h * HD_A, (h + 1) * HD_A)
        ko_ref[pl.ds(h, seq, stride=H_A), :] = k[:, sl]
        vo_ref[pl.ds(h, seq, stride=H_A), :] = v[:, sl]
    kb = k.astype(BF16)
    vb = v.astype(BF16)
    head = lax.broadcasted_iota(jnp.int32, (seq, W_A), 1) // HD_A
    zero = jnp.zeros((seq, W_A), BF16)
    for h in range(H_A):
        km_s[h * seq:(h + 1) * seq, :] = jnp.where(head == h, kb, zero)
        vm_s[h * seq:(h + 1) * seq, :] = jnp.where(head == h, vb, zero)
    s = _dot_nt(q, km_s[...])
    probs = []
    for h in range(H_A):
        s_h = s[:, h * seq:(h + 1) * seq]
        p = jnp.exp(s_h - jnp.max(_fold_lanes(s_h, jnp.maximum), axis=-1, keepdims=True))
        den = jnp.sum(_fold_lanes(p, jnp.add), axis=-1, keepdims=True)
        probs.append((p / den).astype(BF16))
    o_ref[...] = _dot(jnp.concatenate(probs, axis=1), vm_s[...])


def _ctx_attention(z, n_batch, seq, qn, kn):
    spec = lambda c: pl.BlockSpec((seq, W_A), lambda b: (b, c))
    wspec = pl.BlockSpec((1, W_A), lambda b: (0, 0))
    out = jax.ShapeDtypeStruct((n_batch * seq, W_A), F32)
    return pl.pallas_call(
        _ctx_attn_kernel, grid=(n_batch,),
        in_specs=[spec(0), spec(1), spec(2), wspec, wspec, pl.BlockSpec((W_A, W_A), lambda b: (0, 0))],
        out_specs=[pl.BlockSpec((seq, W_A), lambda b: (b, 0))] + [pl.BlockSpec((seq * H_A, HD_A), lambda b: (b, 0))] * 2,
        out_shape=[out] + [jax.ShapeDtypeStruct((n_batch * seq * H_A, HD_A), F32)] * 2,
        scratch_shapes=[pltpu.VMEM((H_A * seq, W_A), BF16), pltpu.VMEM((H_A * seq, W_A), BF16)],
        compiler_params=_cparams("arbitrary"), name="ctx_attention",
    )(z, z, z, qn, kn, _head_group_matrix())


def _na_bias_table(rpb):
    qc = np.arange(GRID_W)
    kc = np.arange(GRID_W)
    cstart = np.clip(qc - WIN_C // 2, 0, GRID_W - WIN_C)
    col_in = (kc[None, :] >= cstart[:, None]) & (kc[None, :] < cstart[:, None] + WIN_C)
    dc = np.clip(kc[None, :] - qc[:, None], 1 - WIN_C, WIN_C - 1) + WIN_C - 1
    cls = np.arange(WIN_R)
    j = np.arange(WIN_R)
    dr = j[None, :] - cls[:, None] + WIN_R - 1
    sel_r = jnp.asarray(dr[:, :, None] == np.arange(2 * WIN_R - 1)[None, None, :], F32)
    sel_c = jnp.asarray(dc[:, :, None] == np.arange(2 * WIN_C - 1)[None, None, :], F32)
    tab = jnp.einsum("hab,cja,qkb->hcqjk", rpb, sel_r, sel_c, precision=HIGHEST)
    tab = jnp.where(jnp.asarray(col_in)[None, None, :, None, :], tab, NEG)
    return tab.reshape(H_A, WIN_R, GRID_W, WIN_R * GRID_W)


def _na_kernel(q_ref, k_ref, v_ref, kc_ref, vc_ref, bias_ref, qn_ref, kn_ref, g_ref, o_ref,
               kn_s, v_s, kc_s, vc_s, *, rows):
    r = pl.program_id(1)
    scale = HD_A ** -0.5

    @pl.when(r == 0)
    def _():
        kn_s[...] = _heads_rms(k_ref[...], kn_ref[...], g_ref[...]).astype(BF16)
        v_s[...] = v_ref[...].astype(BF16)
        kc_s[...] = kc_ref[...].astype(BF16)
        vc_s[...] = vc_ref[...].astype(BF16)

    rs = jnp.clip(r - WIN_R // 2, 0, rows - WIN_R)
    start = pl.multiple_of(rs * GRID_W, GRID_W)
    n_loc = WIN_R * GRID_W
    for h in range(H_A):
        sl = slice(h * HD_A, (h + 1) * HD_A)
        q = (_head_rms(q_ref[:, sl], qn_ref[:, sl]) * scale).astype(BF16)
        s_loc = _dot_nt(q, kn_s[pl.ds(start, n_loc), sl]) + bias_ref[h]
        s_ctx = _dot_nt(q, kc_s[:, sl])
        m = jnp.max(jnp.maximum(_fold_lanes(s_loc, jnp.maximum), _fold_lanes(s_ctx, jnp.maximum)),
                    axis=-1, keepdims=True)
        p_loc = jnp.exp(s_loc - m)
        p_ctx = jnp.exp(s_ctx - m)
        den = jnp.sum(_fold_lanes(p_loc, jnp.add) + _fold_lanes(p_ctx, jnp.add), axis=-1, keepdims=True)
        o = _dot(p_loc.astype(BF16), v_s[pl.ds(start, n_loc), sl]) + _dot(p_ctx.astype(BF16), vc_s[:, sl])
        o_ref[:, sl] = o / den


def _na_attention(z, row0, n_batch, seq, kc, vc, bias, qn, kn):
    rows = seq // GRID_W
    past = kc.shape[1]
    blk0 = row0 // GRID_W
    sblk0 = row0 // seq

    def cls_of(r):
        return r - jnp.clip(r - WIN_R // 2, 0, rows - WIN_R)

    full = lambda c: pl.BlockSpec((seq, W_A), lambda b, r: (sblk0 + b, c))
    cspec = pl.BlockSpec((None, past, W_A), lambda b, r: (b, 0, 0))
    wspec = pl.BlockSpec((1, W_A), lambda b, r: (0, 0))
    return pl.pallas_call(
        functools.partial(_na_kernel, rows=rows), grid=(n_batch, rows),
        in_specs=[pl.BlockSpec((GRID_W, W_A), lambda b, r: (blk0 + b * rows + r, 0)),
                  full(1), full(2), cspec, cspec,
                  pl.BlockSpec((H_A, None, GRID_W, WIN_R * GRID_W), lambda b, r: (0, cls_of(r), 0, 0)),
                  wspec, wspec, pl.BlockSpec((W_A, W_A), lambda b, r: (0, 0))],
        out_specs=pl.BlockSpec((GRID_W, W_A), lambda b, r: (b * rows + r, 0)),
        out_shape=jax.ShapeDtypeStruct((n_batch * seq, W_A), F32),
        scratch_shapes=[pltpu.VMEM((seq, W_A), BF16), pltpu.VMEM((seq, W_A), BF16),
                        pltpu.VMEM((past, W_A), BF16), pltpu.VMEM((past, W_A), BF16)],
        compiler_params=_cparams("arbitrary", "arbitrary"), name="na_attention",
    )(z, z, z, kc, vc, bias, qn, kn, _head_group_matrix())


def _tri_masks():
    li = lax.broadcasted_iota(jnp.int32, (CHUNK, CHUNK), 0)
    si = lax.broadcasted_iota(jnp.int32, (CHUNK, CHUNK), 1)
    return li >= si, li <= si


def _mlstm_kernel(*refs, nc, has_init):
    if has_init:
        (q_ref, k_ref, v_ref, og_ref, g_ref, nw_ref, c0_ref, n0_ref, m0_ref,
         o_ref, cf_ref, nf_ref, mf_ref, h_s, c_s, n_s, m_s) = refs
    else:
        (q_ref, k_ref, v_ref, og_ref, g_ref, nw_ref,
         o_ref, cf_ref, nf_ref, mf_ref, h_s, c_s, n_s, m_s) = refs
    nd = 2 * H_B
    if has_init:
        for sidx in range(nd):
            c_s[sidx] = c0_ref[sidx].T
        n_s[...] = n0_ref[...]
        m_s[...] = m0_ref[...]
    else:
        c_s[...] = jnp.zeros_like(c_s)
        n_s[...] = jnp.zeros_like(n_s)
        m_s[...] = jnp.zeros_like(m_s)

    causal, anti = _tri_masks()
    tri_f = causal.astype(F32)
    tri_b = anti.astype(F32)
    kscale = HD_B ** -0.5

    def chunk_step(c, carry):
        for d in range(2):
            cc = c if d == 0 else nc - 1 - c
            t0 = pl.multiple_of(cc * CHUNK, CHUNK)
            g = g_ref[pl.ds(t0, CHUNK), :]
            gt = g.T
            ls = _log_sigmoid(g)
            lst = _log_sigmoid(gt)
            tri_c, tri_r, mask_t = (tri_f, tri_b, anti) if d == 0 else (tri_b, tri_f, causal)
            b_cols = _dot_hi(tri_c, ls)
            b_rows = _dot_hi(lst, tri_r)
            last = CHUNK - 1 if d == 0 else 0
            for h in range(H_B):
                ci = (2 * d) * H_B + h
                cf = (2 * d + 1) * H_B + h
                hs = slice(h * HD_B, (h + 1) * HD_B)
                q = q_ref[pl.ds(t0, CHUNK), hs]
                k = k_ref[pl.ds(t0, CHUNK), hs] * kscale
                v = v_ref[pl.ds(t0, CHUNK), hs]
                qb, kb = q.astype(BF16), k.astype(BF16)
                qtb = q.T.astype(BF16)
                vt = v.T
                b_row = b_rows[cf:cf + 1, :]
                i_row = gt[ci:ci + 1, :]
                ib_col = g[:, ci:ci + 1] - b_cols[:, cf:cf + 1]
                sidx = d * H_B + h
                cst = c_s[sidx]
                nst = n_s[sidx:sidx + 1, :]
                mst = m_s[sidx:sidx + 1, 0:1]
                dmat_t = jnp.where(mask_t, b_row + ib_col, -jnp.inf)
                inter = b_row + mst
                mt = jnp.maximum(inter, jnp.max(dmat_t, axis=0, keepdims=True))
                w_t = jnp.exp(dmat_t - mt) * _dot_nt(kb, qb)
                a = jnp.exp(inter - mt)
                num_t = _dot(vt.astype(BF16), w_t.astype(BF16)) + _dot(cst.astype(BF16), qtb) * a
                nq = _dot(jnp.broadcast_to(nst, (SUB, HD_B)).astype(BF16), qtb)[0:1, :]
                den = jnp.sum(w_t, axis=0, keepdims=True) + a * nq
                h_t = num_t / jnp.maximum(jnp.abs(den), jnp.exp(-mt))
                h_s[d, pl.ds(t0, CHUNK), hs] = h_t.T
                bl = b_row[:, last:last + 1]
                dl = bl - b_row + i_row
                m_new = jnp.maximum(bl + mst, jnp.max(dl, axis=-1, keepdims=True))
                wl = jnp.exp(dl - m_new)
                dec = jnp.exp(bl + mst - m_new)
                c_s[sidx] = dec * cst + _dot((vt * wl).astype(BF16), kb)
                wl8 = jnp.broadcast_to(wl, (SUB, CHUNK)).astype(BF16)
                n_s[sidx:sidx + 1, :] = dec * nst + _dot(wl8, kb)[0:1, :]
                m_s[sidx:sidx + 1, :] = jnp.broadcast_to(m_new, (1, LANES))
        return carry

    lax.fori_loop(0, nc, chunk_step, 0)

    for h in range(H_B):
        hs = slice(h * HD_B, (h + 1) * HD_B)
        hsum = h_s[0, :, hs] + h_s[1, :, hs]
        o_ref[:, hs] = _head_rms(hsum, nw_ref[:, hs]) * jax.nn.sigmoid(og_ref[:, hs])
    for sidx in range(nd):
        cf_ref[sidx] = c_s[sidx].T
    nf_ref[...] = n_s[...]
    mf_ref[...] = m_s[...]


def _mlstm(z, g, row0, n_batch, seq, norm_w, init=None):
    sblk0 = row0 // seq
    nd = 2 * H_B
    spec = lambda c: pl.BlockSpec((seq, W_B), lambda b: (sblk0 + b, c))
    in_specs = [spec(3), spec(4), spec(5), spec(6),
                pl.BlockSpec((seq, LANES), lambda b: (sblk0 + b, 0)),
                pl.BlockSpec((1, W_B), lambda b: (0, 0))]
    args = [z, z, z, z, g, norm_w]
    st_specs = [pl.BlockSpec((None, nd, HD_B, HD_B), lambda b: (b, 0, 0, 0)),
                pl.BlockSpec((None, nd, HD_B), lambda b: (b, 0, 0)),
                pl.BlockSpec((None, nd, LANES), lambda b: (b, 0, 0))]
    if init is not None:
        in_specs += st_specs
        args += list(init)
    return pl.pallas_call(
        functools.partial(_mlstm_kernel, nc=seq // CHUNK, has_init=init is not None), grid=(n_batch,),
        in_specs=in_specs,
        out_specs=[pl.BlockSpec((seq, W_B), lambda b: (b, 0))] + st_specs,
        out_shape=[jax.ShapeDtypeStruct((n_batch * seq, W_B), F32),
                   jax.ShapeDtypeStruct((n_batch, nd, HD_B, HD_B), F32),
                   jax.ShapeDtypeStruct((n_batch, nd, HD_B), F32),
                   jax.ShapeDtypeStruct((n_batch, nd, LANES), F32)],
        scratch_shapes=[pltpu.VMEM((2, seq, W_B), F32), pltpu.VMEM((nd, HD_B, HD_B), F32),
                        pltpu.VMEM((nd, HD_B), F32), pltpu.VMEM((nd, LANES), F32)],
        compiler_params=_cparams("arbitrary"), name="mlstm",
    )(*args)


def _rope_tables(seq):
    half = HD_C // 2
    quarter = half // 2
    t = np.arange(seq)
    inv = ROPE_BASE ** (-np.arange(0, half, 2, dtype=np.float64) / half)
    ang_r = (t // GRID_W)[:, None] * inv[None, :]
    ang_c = (t % GRID_W)[:, None] * inv[None, :]
    cos_t = np.concatenate([np.cos(ang_r), np.cos(ang_r), np.cos(ang_c), np.cos(ang_c)], -1)
    sin_t = np.concatenate([-np.sin(ang_r), np.sin(ang_r), -np.sin(ang_c), np.sin(ang_c)], -1)
    assert cos_t.shape == (seq, 4 * quarter)
    return jnp.asarray(cos_t, F32), jnp.asarray(sin_t, F32)


def _rope(x, cos_t, sin_t):
    quarter = HD_C // 4
    lane = lax.broadcasted_iota(jnp.int32, x.shape, 1)
    first = (lane % (2 * quarter)) < quarter
    swapped = jnp.where(first, pltpu.roll(x, HD_C - quarter, 1), pltpu.roll(x, quarter, 1))
    return x * cos_t + swapped * sin_t


def _ret_kernel(*refs, nc, has_init, use_rope):
    refs = list(refs)
    q_ref, k_ref, v_ref, gg_ref, dl_ref, nw_ref = refs[:6]
    pos = 6
    if use_rope:
        cos_ref, sin_ref = refs[pos:pos + 2]
        pos += 2
    if has_init:
        s0_ref = refs[pos]
        pos += 1
    o_ref, sf_ref, h_s, s_s, dk_s = refs[pos:pos + 5]
    if has_init:
        s_s[...] = s0_ref[...]
    else:
        s_s[...] = jnp.zeros_like(s_s)

    causal, anti = _tri_masks()
    li = lax.broadcasted_iota(jnp.int32, (CHUNK, CHUNK), 0).astype(F32)
    si = lax.broadcasted_iota(jnp.int32, (CHUNK, CHUNK), 1).astype(F32)
    lg_all = _log_sigmoid(dl_ref[...])
    kscale = HD_C ** -0.5
    for d in range(2):
        for h in range(H_C):
            sidx = d * H_C + h
            lg = lg_all[sidx:sidx + 1, :]
            if d == 0:
                dk_s[sidx, 0] = jnp.exp(jnp.where(causal, (li - si) * lg, -jnp.inf))
                dk_s[sidx, 1] = jnp.exp((li + 1.0) * lg)
                dk_s[sidx, 2] = jnp.exp((CHUNK - 1.0 - li) * lg)
            else:
                dk_s[sidx, 0] = jnp.exp(jnp.where(anti, (si - li) * lg, -jnp.inf))
                dk_s[sidx, 1] = jnp.exp((CHUNK - li) * lg)
                dk_s[sidx, 2] = jnp.exp(li * lg)

    def chunk_step(c, carry):
        for d in range(2):
            cc = c if d == 0 else nc - 1 - c
            t0 = pl.multiple_of(cc * CHUNK, CHUNK)
            for h in range(H_C):
                sidx = d * H_C + h
                hs = slice(h * HD_C, (h + 1) * HD_C)
                lg = lg_all[sidx:sidx + 1, :]
                q = q_ref[pl.ds(t0, CHUNK), hs]
                k = k_ref[pl.ds(t0, CHUNK), hs] * kscale
                v = v_ref[pl.ds(t0, CHUNK), hs]
                if use_rope:
                    cos_t = cos_ref[pl.ds(t0, CHUNK), :]
                    sin_t = sin_ref[pl.ds(t0, CHUNK), :]
                    q = _rope(q, cos_t, sin_t)
                    k = _rope(k, cos_t, sin_t)
                decay, q_dec, k_dec = dk_s[sidx, 0], dk_s[sidx, 1], dk_s[sidx, 2]
                c_dec = jnp.exp(CHUNK * lg)
                st = s_s[sidx]
                qb, kb, vb = q.astype(BF16), k.astype(BF16), v.astype(BF16)
                att = _dot_nt(qb, kb) * decay
                o = _dot(att.astype(BF16), vb) + _dot(qb, st.astype(BF16)) * q_dec
                h_s[d, pl.ds(t0, CHUNK), hs] = o
                kd = (k * k_dec).T.astype(BF16)
                s_s[sidx] = c_dec * st + _dot(kd, vb)
        return carry

    lax.fori_loop(0, nc, chunk_step, 0)

    for h in range(H_C):
        hs = slice(h * HD_C, (h + 1) * HD_C)
        osum = h_s[0, :, hs] + h_s[1, :, hs]
        gg = gg_ref[:, hs]
        o_ref[:, hs] = _head_rms(osum, nw_ref[:, hs]) * (gg * jax.nn.sigmoid(gg))
    sf_ref[...] = s_s[...]


def _retention(z, row0, n_batch, seq, decay_rep, norm_w, rope=None, init=None):
    sblk0 = row0 // seq
    nd = 2 * H_C
    w_c = H_C * HD_C
    spec = lambda c: pl.BlockSpec((seq, w_c), lambda b: (sblk0 + b, c))
    in_specs = [spec(0), spec(1), spec(2), spec(3),
                pl.BlockSpec((nd, LANES), lambda b: (0, 0)),
                pl.BlockSpec((1, w_c), lambda b: (0, 0))]
    args = [z, z, z, z, decay_rep, norm_w]
    if rope is not None:
        in_specs += [pl.BlockSpec((seq, HD_C), lambda b: (0, 0))] * 2
        args += list(rope)
    st_spec = pl.BlockSpec((None, nd, HD_C, HD_C), lambda b: (b, 0, 0, 0))
    if init is not None:
        in_specs.append(st_spec)
        args.append(init)
    return pl.pallas_call(
        functools.partial(_ret_kernel, nc=seq // CHUNK, has_init=init is not None, use_rope=rope is not None),
        grid=(n_batch,), in_specs=in_specs,
        out_specs=[pl.BlockSpec((seq, w_c), lambda b: (b, 0)), st_spec],
        out_shape=[jax.ShapeDtypeStruct((n_batch * seq, w_c), F32),
                   jax.ShapeDtypeStruct((n_batch, nd, HD_C, HD_C), F32)],
        scratch_shapes=[pltpu.VMEM((2, seq, w_c), F32), pltpu.VMEM((nd, HD_C, HD_C), F32),
                        pltpu.VMEM((nd, 3, CHUNK, CHUNK), F32)],
        compiler_params=_cparams("arbitrary"), name="retention",
    )(*args)


def _dft_tables(n):
    idx = (np.arange(n)[:, None] * np.arange(n)[None, :]) % n
    ang = 2.0 * np.pi * idx / n
    return np.cos(ang) / np.sqrt(n), np.sin(ang) / np.sqrt(n)


def _fnet_kernel(x_ref, cw_ref, sw_ref, cs_ref, ss_ref, o_ref):
    for g in range(N_FG):
        gs = slice(g * FG_W, (g + 1) * FG_W)
        x = x_ref[:, gs].astype(BF16)
        xc = _dot(x, cw_ref[...]).astype(BF16)
        xs = _dot(x, sw_ref[...]).astype(BF16)
        o_ref[:, gs] = _dot(cs_ref[...], xc) - _dot(ss_ref[...], xs)


def _fnet(z, row0, n_batch, seq):
    sblk0 = row0 // seq
    w_d = N_FG * FG_W
    cw, sw = _dft_tables(FG_W)
    cs, ss = _dft_tables(seq)
    tabs = [jnp.asarray(a, F32).astype(BF16) for a in (cw, sw, cs, ss)]
    wspec = pl.BlockSpec((FG_W, FG_W), lambda b: (0, 0))
    sspec = pl.BlockSpec((seq, seq), lambda b: (0, 0))
    return pl.pallas_call(
        _fnet_kernel, grid=(n_batch,),
        in_specs=[pl.BlockSpec((seq, w_d), lambda b: (sblk0 + b, 4)), wspec, wspec, sspec, sspec],
        out_specs=pl.BlockSpec((seq, w_d), lambda b: (b, 0)),
        out_shape=jax.ShapeDtypeStruct((n_batch * seq, w_d), F32),
        compiler_params=_cparams("arbitrary"), name="fnet",
    )(z, *tabs)


SUB = 8
assert D_MODEL == SUB * LANES


def _store_token_major(ref, lead, val):
    n = val.shape[0]
    for c in range(SUB):
        ref[lead + (pl.ds(c, n, stride=SUB), slice(None))] = val[:, c * LANES:(c + 1) * LANES]


def _load_token_major(ref, lead, n):
    return jnp.concatenate([ref[lead + (pl.ds(c, n, stride=SUB), slice(None))] for c in range(SUB)], axis=1)


def _outproj_router_kernel(aa_ref, ab_ref, ba_ref, bb_ref, xa_ref, xb_ref, mod_ref, wa_ref, wb_ref,
                           wrh_ref, wrl_ref, br_ref, x1_ref, h2_ref, ti_ref, tw_ref, rank_ref, cnt_ref, cnt_s):
    i = pl.program_id(0)
    a = _rows_read(i, aa_ref, ab_ref)
    b = _rows_read(i, ba_ref, bb_ref)
    y = _dot(a.astype(BF16), wa_ref[...]) + _dot(b.astype(BF16), wb_ref[...])
    x1 = _rows_read(i, xa_ref, xb_ref) + mod_ref[2:3, :] * y
    x1_ref[...] = x1
    h2 = _rms_mod(x1, mod_ref[3:4, :], mod_ref[4:5, :])
    _store_token_major(h2_ref, (), h2)
    logits = _dot_x3(h2, wrh_ref[...], wrl_ref[...]) + br_ref[...]
    lane = lax.broadcasted_iota(jnp.int32, logits.shape, 1)
    lane_f = lane.astype(F32)
    cur = logits
    vals, picks = [], []
    ti = jnp.zeros(logits.shape, jnp.int32)
    for kk in range(TOP_K):
        mx = jnp.max(cur, axis=-1, keepdims=True)
        idx = jnp.min(jnp.where(cur == mx, lane_f, float(LANES)), axis=-1, keepdims=True)
        ti = jnp.where(lane == kk, idx.astype(jnp.int32), ti)
        pick = lane_f == idx
        cur = jnp.where(pick, -jnp.inf, cur)
        vals.append(mx)
        picks.append(pick)
    es = [jnp.exp(v - vals[0]) for v in vals]
    tot = es[0] + es[1] + es[2] + es[3]
    tw = jnp.zeros(logits.shape, F32)
    for kk in range(TOP_K):
        tw = jnp.where(lane == kk, es[kk] / tot, tw)
    ti_ref[...] = ti
    tw_ref[...] = tw

    @pl.when(i == 0)
    def _():
        cnt_s[...] = jnp.zeros_like(cnt_s)

    onehot = jnp.zeros(logits.shape, F32)
    for pick in picks:
        onehot = onehot + jnp.where(pick, 1.0, 0.0)
    n = logits.shape[0]
    earlier = (lax.broadcasted_iota(jnp.int32, (n, n), 1) < lax.broadcasted_iota(jnp.int32, (n, n), 0))
    before = cnt_s[...] + _dot(jnp.where(earlier, 1.0, 0.0).astype(BF16), onehot.astype(BF16))
    rank = jnp.zeros(logits.shape, jnp.int32)
    for kk, pick in enumerate(picks):
        r_k = jnp.sum(jnp.where(pick, before, 0.0), axis=-1, keepdims=True)
        rank = jnp.where(lane == kk, r_k.astype(jnp.int32), rank)
    rank_ref[...] = rank
    cnt_s[...] = cnt_s[...] + jnp.sum(onehot, axis=0, keepdims=True)
    cnt_ref[...] = cnt_s[...]


def _outproj_router(a, b, x, mod, w_out, wr, br):
    t = N_TILES * TM
    d = x.a.shape[1]
    wid = a.a.shape[1]
    row = lambda w: pl.BlockSpec((TM, w), lambda i: (i, 0))
    const = lambda r, c: pl.BlockSpec((r, c), lambda i: (0, 0))
    wr_hi, wr_lo = _split_bf16(wr)
    return pl.pallas_call(
        _outproj_router_kernel, grid=(N_TILES,),
        in_specs=a.specs() + b.specs() + x.specs() + [
                  pl.BlockSpec((None, 6, d), lambda i: (_cond_row(i), 0, 0)),
                  pl.BlockSpec((wid, d), lambda i: (0, 0)), pl.BlockSpec((wid, d), lambda i: (1, 0)),
                  const(d, LANES), const(d, LANES), const(1, LANES)],
        out_specs=[row(d), pl.BlockSpec((TM * SUB, LANES), lambda i: (i, 0)), row(LANES), row(LANES), row(LANES),
                   const(1, LANES)],
        out_shape=[jax.ShapeDtypeStruct((t, d), F32), jax.ShapeDtypeStruct((t * SUB, LANES), F32),
                   jax.ShapeDtypeStruct((t, LANES), jnp.int32), jax.ShapeDtypeStruct((t, LANES), F32),
                   jax.ShapeDtypeStruct((t, LANES), jnp.int32), jax.ShapeDtypeStruct((1, LANES), F32)],
        scratch_shapes=[pltpu.VMEM((1, LANES), F32)],
        compiler_params=_cparams("arbitrary"), name="outproj_router",
    )(a.a, a.b, b.a, b.b, x.a, x.b, mod, w_out, w_out, wr_hi, wr_lo, br)


N_TILES_MAX = N_TILES * TM * TOP_K // TM_E + N_EXP
PLAN_LANES = 2 * LANES
assert N_TILES_MAX <= PLAN_LANES and N_EXP <= LANES


def _plan_kernel(cnt_ref, ti_ref, rank_ref, dest_ref, meta_ref, start_s):
    i = pl.program_id(0)

    @pl.when(i == 0)
    def _():
        cnt = cnt_ref[...]
        tiles = jnp.floor((cnt + float(TM_E - 1)) * (1.0 / TM_E))
        sub = lax.broadcasted_iota(jnp.int32, (LANES, LANES), 0)
        lane = lax.broadcasted_iota(jnp.int32, (LANES, LANES), 1)
        upto = jnp.where(sub <= lane, 1.0, 0.0).astype(BF16)
        tile_end = _dot(jnp.broadcast_to(tiles, (SUB, LANES)).astype(BF16), upto)[0:1, :]
        tile_start = tile_end - tiles
        start_s[...] = tile_start * float(TM_E * SUB)
        n_tiles = jnp.max(tile_end, axis=-1, keepdims=True)
        used = tiles > 0.0

        def column(row):
            return jnp.sum(jnp.where(sub == lane, jnp.broadcast_to(row, (LANES, LANES)), 0.0), axis=-1, keepdims=True)

        end_c, start_c, tiles_c = column(tile_end), column(tile_start), column(tiles)
        used_b = jnp.broadcast_to(jnp.where(used, 1.0, 0.0), (LANES, LANES))
        pos_c = jnp.sum(jnp.where(lane <= sub, used_b, 0.0), axis=-1, keepdims=True)
        par_c = (pos_c - 1.0) - 2.0 * jnp.floor((pos_c - 1.0) * 0.5)
        nxt_c = jnp.min(jnp.where(jnp.logical_and(lane > sub, used_b > 0.0), lane.astype(F32), float(LANES)),
                        axis=-1, keepdims=True)
        nxt_c = jnp.where(nxt_c < float(LANES), nxt_c, -1.0)

        tid = lax.broadcasted_iota(jnp.int32, (LANES, PLAN_LANES), 1).astype(F32)
        exp_id = lax.broadcasted_iota(jnp.int32, (LANES, PLAN_LANES), 0)
        tid_used = jnp.minimum(tid, n_tiles - 1.0)
        te = jnp.sum(jnp.where(jnp.logical_and(exp_id < N_EXP, end_c <= tid_used), 1.0, 0.0), axis=0, keepdims=True)
        first = jnp.sum(jnp.where(jnp.logical_and(tiles_c > 0.0, start_c == tid), 1.0, 0.0), axis=0, keepdims=True)
        mine = te == exp_id.astype(F32)
        nxt = jnp.sum(jnp.where(mine, nxt_c, 0.0), axis=0, keepdims=True)
        par = jnp.sum(jnp.where(mine, par_c, 0.0), axis=0, keepdims=True)
        last = jnp.where(used, tile_end - 1.0, 0.0)
        last = jnp.concatenate([last, jnp.zeros((1, PLAN_LANES - LANES), F32)], axis=1)

        row_id = lax.broadcasted_iota(jnp.int32, (SUB, PLAN_LANES), 0)
        meta = jnp.zeros((SUB, PLAN_LANES), F32)
        for r, val in enumerate((te, first, nxt, par, last, jnp.broadcast_to(n_tiles, (1, PLAN_LANES)))):
            meta = jnp.where(row_id == r, jnp.broadcast_to(val, (SUB, PLAN_LANES)), meta)
        meta_ref[...] = meta.astype(jnp.int32)

    ti = ti_ref[...]
    rank = rank_ref[...]
    lane = lax.broadcasted_iota(jnp.int32, ti.shape, 1)
    dest = jnp.zeros(ti.shape, jnp.int32)
    for kk in range(TOP_K):
        base = jnp.sum(jnp.where(lane == ti[:, kk:kk + 1], start_s[...], 0.0), axis=-1, keepdims=True)
        dest = jnp.where(lane == kk, base.astype(jnp.int32) + rank[:, kk:kk + 1] * SUB, dest)
    dest_ref[...] = dest.T[0:SUB, :]


def _route_plan(cnt, ti, rank):
    t = ti.shape[0]
    row = pl.BlockSpec((TM, LANES), lambda i: (i, 0))
    dest, meta = pl.pallas_call(
        _plan_kernel, grid=(t // TM,),
        in_specs=[pl.BlockSpec((1, LANES), lambda i: (0, 0)), row, row],
        out_specs=[pl.BlockSpec((SUB, TM), lambda i: (0, i)), pl.BlockSpec((SUB, PLAN_LANES), lambda i: (0, 0))],
        out_shape=[jax.ShapeDtypeStruct((SUB, t), jnp.int32), jax.ShapeDtypeStruct((SUB, PLAN_LANES), jnp.int32)],
        scratch_shapes=[pltpu.VMEM((1, LANES), F32)],
        compiler_params=_cparams("arbitrary"), name="moe_plan",
    )(cnt, ti, rank)
    plan = dict(tile_expert=meta[0, :N_TILES_MAX], first=meta[1, :N_TILES_MAX], tile_next=meta[2, :N_TILES_MAX],
                tile_parity=meta[3, :N_TILES_MAX], last_tile=meta[4, :N_EXP], n_tiles=meta[5, :1])
    return dest, plan


DISPATCH_BLK = 1024


def _dispatch_kernel(lt_ref, nt_ref, *rest):
    dest_refs, (h_ref, xs_ref, h_s, zero_s, sem, hsem) = rest[:TOP_K], rest[TOP_K:]
    i = pl.program_id(0)
    tile_rows = TM_E * SUB
    n_tiles_max = xs_ref.shape[0] // tile_rows
    n_tok = h_s.shape[0] // SUB

    @pl.when(i == 0)
    def _():
        stage = pltpu.make_async_copy(h_ref, h_s, hsem)
        stage.start()
        zero_s[...] = jnp.zeros_like(zero_s)

        def zero_tile(tile):
            r0 = pl.multiple_of(tile * tile_rows, tile_rows)
            return pltpu.make_async_copy(zero_s, xs_ref.at[pl.ds(r0, tile_rows), :], sem)

        def start_unused(j, carry):
            zero_tile(j).start()
            return carry

        def wait_unused(j, carry):
            zero_tile(j).wait()
            return carry

        for e in range(N_EXP):
            zero_tile(lt_ref[e]).start()
        lax.fori_loop(nt_ref[0], n_tiles_max, start_unused, 0)
        for e in range(N_EXP):
            zero_tile(lt_ref[e]).wait()
        lax.fori_loop(nt_ref[0], n_tiles_max, wait_unused, 0)
        stage.wait()

    base = i * DISPATCH_BLK

    def issue(t, carry):
        src = pl.multiple_of((base + t) * SUB, SUB)
        for kk in range(TOP_K):
            row = pl.multiple_of(dest_refs[kk][0, t], SUB)
            pltpu.make_async_copy(h_s.at[pl.ds(src, SUB), :], xs_ref.at[pl.ds(row, SUB), :], sem).start(priority=kk % 2)
        return carry

    lax.fori_loop(0, DISPATCH_BLK, issue, 0, unroll=2)

    @pl.when(i == pl.num_programs(0) - 1)
    def _():
        for kk in range(TOP_K):
            pltpu.make_async_copy(h_s, xs_ref.at[pl.ds(0, n_tok * SUB), :], sem).wait()


def _dispatch(h2, dest, plan, n_rows):
    t = h2.shape[0] // SUB
    nblk = t // DISPATCH_BLK
    return pl.pallas_call(
        _dispatch_kernel,
        grid_spec=pltpu.PrefetchScalarGridSpec(
            num_scalar_prefetch=2, grid=(nblk,),
            in_specs=[pl.BlockSpec((None, 1, DISPATCH_BLK), lambda i, lt, nt: (i, 0, 0), memory_space=pltpu.SMEM)] * TOP_K
                     + [pl.BlockSpec(memory_space=pl.ANY)],
            out_specs=pl.BlockSpec(memory_space=pl.ANY),
            scratch_shapes=[pltpu.VMEM((t * SUB, LANES), F32), pltpu.VMEM((TM_E * SUB, LANES), F32),
                            pltpu.SemaphoreType.DMA(()), pltpu.SemaphoreType.DMA(())]),
        out_shape=jax.ShapeDtypeStruct((n_rows * SUB, LANES), F32),
        compiler_params=_cparams("arbitrary"), name="moe_dispatch",
    )(plan["last_tile"], plan["n_tiles"], *[dest[kk].reshape(nblk, 1, DISPATCH_BLK) for kk in range(TOP_K)], h2)


def _expert_kernel(te_ref, tf_ref, nt_ref, nx_ref, par_ref, xs_ref, w1_ref, b1_ref, w2_ref, b2_ref, ys_ref,
                   w1f, w2f, w1_s, w2_s, wsem, *, layer):
    i = pl.program_id(0)

    def fetch(expert, slot):
        return (pltpu.make_async_copy(w1_ref.at[layer, expert], w1f.at[slot], wsem.at[slot]),
                pltpu.make_async_copy(w2_ref.at[layer, expert], w2f.at[slot], wsem.at[slot]))

    @pl.when(i == 0)
    def _():
        for cp in fetch(te_ref[0], 0):
            cp.start(priority=1)

    @pl.when(i < nt_ref[0])
    def _():
        @pl.when(tf_ref[i] == 1)
        def _():
            slot = par_ref[i]
            for cp in fetch(te_ref[i], slot):
                cp.wait()
            w1_s[...] = w1f[slot].astype(BF16)
            w2_s[...] = w2f[slot].astype(BF16)

            @pl.when(nx_ref[i] >= 0)
            def _():
                for cp in fetch(nx_ref[i], 1 - slot):
                    cp.start(priority=1)

        x = _load_token_major(xs_ref, (), TM_E)
        u = _dot(x.astype(BF16), w1_s[...]) + b1_ref[...]
        g = jnp.minimum(u[:, :D_FF], SWIGLU_LIMIT)
        up = jnp.clip(u[:, D_FF:], -SWIGLU_LIMIT, SWIGLU_LIMIT)
        act = (up + 1.0) * g * jax.nn.sigmoid(SWIGLU_ALPHA * g)
        _store_token_major(ys_ref, (), _dot(act.astype(BF16), w2_s[...]) + b2_ref[...])

    @pl.when(i >= nt_ref[0])
    def _():
        ys_ref[...] = jnp.zeros_like(ys_ref)


def _experts(xs, plan, layer, w1, b1, w2, b2):
    d = D_MODEL
    nt = xs.shape[0] // (TM_E * SUB)
    tile = lambda i, te, tf, ntl, nx, par: (jnp.minimum(i, ntl[0] - 1), 0)
    otile = lambda i, te, tf, ntl, nx, par: (i, 0)
    bmap = lambda i, te, tf, ntl, nx, par: (layer, te[i], 0, 0)
    return pl.pallas_call(
        functools.partial(_expert_kernel, layer=layer),
        grid_spec=pltpu.PrefetchScalarGridSpec(
            num_scalar_prefetch=5, grid=(nt,),
            in_specs=[pl.BlockSpec((TM_E * SUB, LANES), tile),
                      pl.BlockSpec(memory_space=pl.ANY),
                      pl.BlockSpec((None, None, 1, 2 * D_FF), bmap),
                      pl.BlockSpec(memory_space=pl.ANY),
                      pl.BlockSpec((None, None, 1, d), bmap)],
            out_specs=pl.BlockSpec((TM_E * SUB, LANES), otile),
            scratch_shapes=[pltpu.VMEM((2, d, 2 * D_FF), F32), pltpu.VMEM((2, D_FF, d), F32),
                            pltpu.VMEM((d, 2 * D_FF), BF16), pltpu.VMEM((D_FF, d), BF16),
                            pltpu.SemaphoreType.DMA((2,))]),
        out_shape=jax.ShapeDtypeStruct(xs.shape, F32),
        compiler_params=_cparams("arbitrary"), name="moe_experts",
    )(plan["tile_expert"], plan["first"], plan["n_tiles"], plan["tile_next"], plan["tile_parity"],
      xs, w1, b1, w2, b2)


def _combine_kernel(*refs):
    dest_refs = refs[:TOP_K]
    x1_ref, tw_ref, mod_ref, ys_ref, *o_refs, buf, sem = refs[TOP_K:]
    i = pl.program_id(0)
    j = i - 1
    n = pl.num_programs(0) - 1
    tile_rows = TM * SUB

    for s in range(2):
        @pl.when(jnp.logical_and(i < n, i % 2 == s))
        def _():
            def issue(t, carry):
                dst = pl.multiple_of(t * SUB, SUB)
                for kk in range(TOP_K):
                    row = pl.multiple_of(dest_refs[kk][0, t], SUB)
                    pltpu.make_async_copy(ys_ref.at[pl.ds(row, SUB), :], buf.at[s, kk, pl.ds(dst, SUB), :],
                                          sem.at[s]).start(priority=kk % 2)
                return carry

            lax.fori_loop(0, TM, issue, 0, unroll=2)

    @pl.when(j >= 0)
    def _():
        slot = j % 2
        for kk in range(TOP_K):
            pltpu.make_async_copy(ys_ref.at[pl.ds(0, tile_rows), :], buf.at[slot, kk], sem.at[slot]).wait()
        tw = tw_ref[...]
        y = tw[:, 0:1] * _load_token_major(buf, (slot, 0), TM)
        for kk in range(1, TOP_K):
            y = y + tw[:, kk:kk + 1] * _load_token_major(buf, (slot, kk), TM)
        out = x1_ref[...] + mod_ref[5:6, :] * y
        if len(o_refs) == 1:
            o_refs[0][...] = out
        else:
            @pl.when(j < N_CTX_TILES)
            def _():
                o_refs[0][...] = out

            @pl.when(j >= N_CTX_TILES)
            def _():
                o_refs[1][...] = out


def _combine(ys, dest, x1, tw, mod, split):
    t, d = x1.shape
    nblk = t // TM
    prev = lambda i: jnp.maximum(i - 1, 0)
    row = lambda w: pl.BlockSpec((TM, w), lambda i: (prev(i), 0))
    return pl.pallas_call(
        _combine_kernel, grid=(nblk + 1,),
        in_specs=[pl.BlockSpec((None, 1, TM), lambda i: (jnp.minimum(i, nblk - 1), 0, 0), memory_space=pltpu.SMEM)] * TOP_K + [
                  row(d), row(LANES),
                  pl.BlockSpec((None, 6, d), lambda i: (_cond_row(prev(i)), 0, 0)),
                  pl.BlockSpec(memory_space=pl.ANY)],
        out_specs=_Rows(x1, x1, 0).specs(lag=1) if split else row(d),
        out_shape=([jax.ShapeDtypeStruct((N_CTX_TILES * TM, d), F32),
                    jax.ShapeDtypeStruct(((N_TILES - N_CTX_TILES) * TM, d), F32)] if split
                   else jax.ShapeDtypeStruct((t, d), F32)),
        scratch_shapes=[pltpu.VMEM((2, TOP_K, TM * SUB, LANES), F32), pltpu.SemaphoreType.DMA((2,))],
        compiler_params=_cparams("arbitrary"), name="moe_combine",
    )(*[dest[kk].reshape(nblk, 1, TM) for kk in range(TOP_K)], x1, tw, mod, ys)


def _moe(x1, h2, ti, tw, rank, cnt, mod, layer, w1, b1, w2, b2, split):
    depth = w1.shape[0]
    assert x1.shape[0] == N_TILES * TM
    dest, plan = _route_plan(cnt, ti, rank)
    xs = _dispatch(h2, dest, plan, N_TILES_MAX * TM_E)
    ys = _experts(xs, plan, layer, w1, b1.reshape(depth, N_EXP, 1, -1), w2, b2.reshape(depth, N_EXP, 1, -1))
    return _combine(ys, dest, x1, tw, mod, split)


def _pad_lanes(a, value=0.0):
    return jnp.pad(a, ((0, 0), (0, LANES - a.shape[1])), constant_values=value)


def kernel(x_prompt, x_sample, cache_na_k, cache_na_v, state_mlstm_C, state_mlstm_n, state_mlstm_m, state_ret_S, c, c_ctx, w_mod, b_mod, w_in_even, mlstm_gate_b, na_q_norm, na_k_norm, na_rpb, mlstm_norm, w_out_even, w_in_odd, ret_decay, ret_norm, w_out_odd, w_router, b_router, w_moe_in, b_moe_in, w_moe_out, b_moe_out):
    nb_c, s_c, d = x_prompt.shape
    nb_l, s_l, _ = x_sample.shape
    t_c = nb_c * s_c
    t_l = nb_l * s_l
    assert t_c == 4 * SEG and s_l == SEG and d == D_MODEL
    depth = w_mod.shape[0]
    dt = x_prompt.dtype

    x = _Rows(x_prompt.reshape(t_c, d), x_sample.reshape(t_l, d), 0)
    cond = jnp.concatenate([c_ctx[None, :], c, jnp.zeros((N_COND - 1 - nb_l, d), F32)], axis=0)
    mod = _modulation(cond, w_mod, b_mod).reshape(depth, N_COND, 6, d)

    outs = {}
    for l in range(depth):
        e = l // 2
        mod_l = mod[l]
        if l % 2 == 0:
            w_in = w_in_even[e]
            n_main = 3 * W_A + 4 * W_B
            wg = _pad_lanes(w_in[:, n_main:])
            bg = _pad_lanes(mlstm_gate_b[e].reshape(1, 4 * H_B))
            z, g = _inproj(x, mod_l, w_in.astype(BF16), n_main, wg, bg)
            qn = jnp.tile(na_q_norm[e].reshape(1, HD_A), (1, H_A))
            kn = jnp.tile(na_k_norm[e].reshape(1, HD_A), (1, H_A))
            oa_c, ka_c, va_c = _ctx_attention(z, nb_c, s_c, qn, kn)
            past = cache_na_k.shape[2]
            oa_l = _na_attention(z, t_c, nb_l, s_l,
                                 cache_na_k[:, e].reshape(nb_l, past, W_A), cache_na_v[:, e].reshape(nb_l, past, W_A),
                                 _na_bias_table(na_rpb[e]), qn, kn)
            nw = mlstm_norm[e].reshape(1, W_B)
            hm_c, c_fin, n_fin, m_fin = _mlstm(z, g, 0, nb_c, s_c, nw)
            init = (state_mlstm_C[:, e].reshape(nb_l, 2 * H_B, HD_B, HD_B),
                    state_mlstm_n[:, e].reshape(nb_l, 2 * H_B, HD_B),
                    jnp.broadcast_to(state_mlstm_m[:, e].reshape(nb_l, 2 * H_B, 1), (nb_l, 2 * H_B, LANES)))
            hm_l = _mlstm(z, g, t_c, nb_l, s_l, nw, init)[0]
            a = _Rows(oa_c, oa_l, 0)
            b = _Rows(hm_c, hm_l, 0)
            w_out = w_out_even[e].astype(BF16)
            outs.setdefault("na_k", []).append(ka_c.reshape(nb_c, s_c, H_A, HD_A))
            outs.setdefault("na_v", []).append(va_c.reshape(nb_c, s_c, H_A, HD_A))
            outs.setdefault("C", []).append(c_fin.reshape(nb_c, 2, H_B, HD_B, HD_B))
            outs.setdefault("n", []).append(n_fin.reshape(nb_c, 2, H_B, HD_B))
            outs.setdefault("m", []).append(m_fin[:, :, 0].reshape(nb_c, 2, H_B))
        else:
            w_c = H_C * HD_C
            z = _inproj(x, mod_l, w_in_odd[e].astype(BF16), 4 * w_c + N_FG * FG_W)
            dl_rep = jnp.broadcast_to(ret_decay[e].reshape(2 * H_C, 1), (2 * H_C, LANES))
            nw = ret_norm[e].reshape(1, w_c)
            hr_c, s_fin = _retention(z, 0, nb_c, s_c, dl_rep, nw)
            hr_l = _retention(z, t_c, nb_l, s_l, dl_rep, nw, rope=_rope_tables(s_l),
                              init=state_ret_S[:, e].reshape(nb_l, 2 * H_C, HD_C, HD_C))[0]
            fd_c = _fnet(z, 0, nb_c, s_c)
            fd_l = _fnet(z, t_c, nb_l, s_l)
            a = _Rows(hr_c, hr_l, 0)
            b = _Rows(fd_c, fd_l, 0)
            w_out = w_out_odd[e].astype(BF16)
            outs.setdefault("S", []).append(s_fin.reshape(nb_c, 2, H_C, HD_C, HD_C))
        wr = _pad_lanes(w_router[l])
        br = _pad_lanes(b_router[l].reshape(1, N_EXP), NEG)
        x1, h2, ti, tw, rank, cnt = _outproj_router(a, b, x, mod_l, w_out, wr, br)
        last = l == depth - 1
        x = _moe(x1, h2, ti, tw, rank, cnt, mod_l, l, w_moe_in, b_moe_in, w_moe_out, b_moe_out, split=last)
        if not last:
            x = _whole(x)

    y_prompt = x[0].reshape(nb_c, s_c, d)
    y_sample = x[1].reshape(nb_l, s_l, d)
    stack = lambda key: jnp.stack(outs[key], axis=1).astype(dt)
    return (y_prompt, y_sample, stack("na_k"), stack("na_v"), stack("C"), stack("n"), stack("m"), stack("S"))
```

```python
import functools
from typing import NamedTuple

import numpy as np
import jax
import jax.numpy as jnp
from jax import lax
from jax.experimental import pallas as pl
from jax.experimental.pallas import tpu as pltpu

F32 = jnp.float32
BF16 = jnp.bfloat16
HIGHEST = lax.Precision.HIGHEST

D_MODEL = 1024
GRID_W = 64
WIN_R = 8
WIN_C = 16
H_A, HD_A = 8, 64
H_B, HD_B = 4, 128
H_C, HD_C = 4, 128
N_FG, FG_W = 4, 128
W_A = H_A * HD_A
W_B = H_B * HD_B
N_EXP = 32
TOP_K = 4
D_FF = D_MODEL
SWIGLU_LIMIT = 7.0
SWIGLU_ALPHA = 1.702
CHUNK = 128
ROPE_BASE = 10000.0
EPS = 1e-6

LANES = 128
SEG = 1024
N_COND = 8
TM = 512
N_TILES = 8 * SEG // TM
N_CTX_TILES = 4 * SEG // TM
TM_E = 256
NEG = -1e30
VMEM_LIMIT = 56 * 1024 * 1024


def _cparams(*sem):
    return pltpu.CompilerParams(dimension_semantics=sem, vmem_limit_bytes=VMEM_LIMIT)


def _cond_row(i):
    return jnp.maximum((i * TM) // SEG - 3, 0)


def _log_sigmoid(x):
    return jnp.minimum(x, 0.0) - jnp.log1p(jnp.exp(-jnp.abs(x)))


def _dot(a, b):
    return jnp.dot(a, b, preferred_element_type=F32)


def _dot_nt(a, b):
    return lax.dot_general(a, b, (((1,), (1,)), ((), ())), preferred_element_type=F32)


def _dot_hi(a, b):
    return jnp.dot(a, b, precision=HIGHEST, preferred_element_type=F32)


def _split_bf16(x):
    hi = x.astype(BF16)
    return hi, (x - hi.astype(F32)).astype(BF16)


def _dot_x3(x, w_hi, w_lo):
    x_hi, x_lo = _split_bf16(x)
    return _dot(x_hi, w_hi) + (_dot(x_lo, w_hi) + _dot(x_hi, w_lo))


def _mod_kernel(cond_ref, w_ref, b_ref, o_ref):
    c = cond_ref[...]
    s = c * jax.nn.sigmoid(c)
    o_ref[...] = _dot_hi(s, w_ref[...]) + b_ref[...]


def _modulation(cond, w_mod, b_mod):
    depth, d, n = w_mod.shape
    tn = 1536
    return pl.pallas_call(
        _mod_kernel,
        grid=(depth, n // tn),
        in_specs=[pl.BlockSpec((N_COND, d), lambda l, j: (0, 0)),
                  pl.BlockSpec((None, d, tn), lambda l, j: (l, 0, j)),
                  pl.BlockSpec((None, 1, tn), lambda l, j: (l, 0, j))],
        out_specs=pl.BlockSpec((None, N_COND, tn), lambda l, j: (l, 0, j)),
        out_shape=jax.ShapeDtypeStruct((depth, N_COND, n), F32),
        compiler_params=_cparams("arbitrary", "arbitrary"),
        name="modulation",
    )(cond, w_mod, b_mod.reshape(depth, 1, n))


def _rms_mod(x, shift, scale):
    h = x * lax.rsqrt(jnp.mean(x * x, axis=-1, keepdims=True) + EPS)
    return h * (1.0 + scale) + shift


class _Rows(NamedTuple):
    a: jax.Array
    b: jax.Array
    off_b: int

    def specs(self, lag=0):
        width = self.a.shape[1]
        off_b = self.off_b
        tile = (lambda i: i) if lag == 0 else (lambda i: jnp.maximum(i - lag, 0))
        return [pl.BlockSpec((TM, width), lambda i: (jnp.minimum(tile(i), N_CTX_TILES - 1), 0)),
                pl.BlockSpec((TM, width), lambda i: (jnp.maximum(tile(i) - N_CTX_TILES, 0) + off_b, 0))]


def _whole(x):
    return _Rows(x, x, N_CTX_TILES)


def _rows_read(tile, ref_a, ref_b):
    return jnp.where(tile < N_CTX_TILES, ref_a[...], ref_b[...])


def _inproj_kernel(xa_ref, xb_ref, mod_ref, w_ref, z_ref):
    x = _rows_read(pl.program_id(0), xa_ref, xb_ref)
    h = _rms_mod(x, mod_ref[0:1, :], mod_ref[1:2, :])
    z_ref[...] = _dot(h.astype(BF16), w_ref[...])


def _inproj_gate_kernel(xa_ref, xb_ref, mod_ref, w_ref, wgh_ref, wgl_ref, bg_ref, z_ref, g_ref):
    x = _rows_read(pl.program_id(0), xa_ref, xb_ref)
    h = _rms_mod(x, mod_ref[0:1, :], mod_ref[1:2, :])
    z_ref[...] = _dot(h.astype(BF16), w_ref[...])
    g_ref[...] = _dot_x3(h, wgh_ref[...], wgl_ref[...]) + bg_ref[...]


def _inproj(x, mod, w, n, wg=None, bg=None):
    t = N_TILES * TM
    d = x.a.shape[1]
    in_specs = x.specs() + [pl.BlockSpec((None, 6, d), lambda i: (_cond_row(i), 0, 0)),
                            pl.BlockSpec((d, n), lambda i: (0, 0))]
    z_spec = pl.BlockSpec((TM, n), lambda i: (i, 0))
    z_shape = jax.ShapeDtypeStruct((t, n), F32)
    if wg is None:
        return pl.pallas_call(
            _inproj_kernel, grid=(N_TILES,), in_specs=in_specs, out_specs=z_spec, out_shape=z_shape,
            compiler_params=_cparams("arbitrary"), name="inproj",
        )(x.a, x.b, mod, w)
    wg_hi, wg_lo = _split_bf16(wg)
    in_specs += [pl.BlockSpec((d, LANES), lambda i: (0, 0))] * 2 + [pl.BlockSpec((1, LANES), lambda i: (0, 0))]
    return pl.pallas_call(
        _inproj_gate_kernel, grid=(N_TILES,), in_specs=in_specs,
        out_specs=[z_spec, pl.BlockSpec((TM, LANES), lambda i: (i, 0))],
        out_shape=[z_shape, jax.ShapeDtypeStruct((t, LANES), F32)],
        compiler_params=_cparams("arbitrary"), name="inproj_gate",
    )(x.a, x.b, mod, w, wg_hi, wg_lo, bg)


def _head_rms(x, w):
    return x * lax.rsqrt(jnp.mean(x * x, axis=-1, keepdims=True) + EPS) * w


def _fold_lanes(x, op):
    parts = [x[:, c * LANES:(c + 1) * LANES] for c in range(x.shape[1] // LANES)]
    while len(parts) > 1:
        parts = [op(parts[c], parts[c + 1]) if c + 1 < len(parts) else parts[c] for c in range(0, len(parts), 2)]
    return parts[0]


def _head_group_matrix():
    head = np.arange(W_A) // HD_A
    return jnp.asarray((head[:, None] == head[None, :]) / HD_A, F32).astype(BF16)


def _heads_rms(x, w_row, g):
    hi, lo = _split_bf16(x * x)
    return x * lax.rsqrt(_dot(hi, g) + _dot(lo, g) + EPS) * w_row


def _ctx_attn_kernel(q_ref, k_ref, v_ref, qn_ref, kn_ref, g_ref, o_ref, ko_ref, vo_ref, km_s, vm_s):
    seq = q_ref.shape[0]
    scale = HD_A ** -0.5
    g = g_ref[...]
    q = _heads_rms(q_ref[...], qn_ref[...] * scale, g).astype(BF16)
    k = _heads_rms(k_ref[...], kn_ref[...], g)
    v = v_ref[...]
    for h in range(H_A):
        sl = slice(h * HD_A, (h + 1) * HD_A)
        ko_ref[pl.ds(h, seq, stride=H_A), :] = k[:, sl]
        vo_ref[pl.ds(h, seq, stride=H_A), :] = v[:, sl]
    kb = k.astype(BF16)
    vb = v.astype(BF16)
    head = lax.broadcasted_iota(jnp.int32, (seq, W_A), 1) // HD_A
    zero = jnp.zeros((seq, W_A), BF16)
    for h in range(H_A):
        km_s[h * seq:(h + 1) * seq, :] = jnp.where(head == h, kb, zero)
        vm_s[h * seq:(h + 1) * seq, :] = jnp.where(head == h, vb, zero)
    s = _dot_nt(q, km_s[...])
    probs = []
    for h in range(H_A):
        s_h = s[:, h * seq:(h + 1) * seq]
        p = jnp.exp(s_h - jnp.max(_fold_lanes(s_h, jnp.maximum), axis=-1, keepdims=True))
        den = jnp.sum(_fold_lanes(p, jnp.add), axis=-1, keepdims=True)
        probs.append((p / den).astype(BF16))
    o_ref[...] = _dot(jnp.concatenate(probs, axis=1), vm_s[...])


def _ctx_attention(z, n_batch, seq, qn, kn):
    spec = lambda c: pl.BlockSpec((seq, W_A), lambda b: (b, c))
    wspec = pl.BlockSpec((1, W_A), lambda b: (0, 0))
    out = jax.ShapeDtypeStruct((n_batch * seq, W_A), F32)
    return pl.pallas_call(
        _ctx_attn_kernel, grid=(n_batch,),
        in_specs=[spec(0), spec(1), spec(2), wspec, wspec, pl.BlockSpec((W_A, W_A), lambda b: (0, 0))],
        out_specs=[pl.BlockSpec((seq, W_A), lambda b: (b, 0))] + [pl.BlockSpec((seq * H_A, HD_A), lambda b: (b, 0))] * 2,
        out_shape=[out] + [jax.ShapeDtypeStruct((n_batch * seq * H_A, HD_A), F32)] * 2,
        scratch_shapes=[pltpu.VMEM((H_A * seq, W_A), BF16), pltpu.VMEM((H_A * seq, W_A), BF16)],
        compiler_params=_cparams("arbitrary"), name="ctx_attention",
    )(z, z, z, qn, kn, _head_group_matrix())


NA_QROWS = 4
NA_KROWS = 12


def _na_slab_start(r0, rows):
    first_window = jnp.clip(r0 - WIN_R // 2, 0, rows - WIN_R)
    return jnp.minimum(first_window, rows - NA_KROWS)


def _na_bias_pairs(rpb):
    qc = np.arange(GRID_W)
    kc = np.arange(GRID_W)
    cstart = np.clip(qc - WIN_C // 2, 0, GRID_W - WIN_C)
    col_in = (kc[None, :] >= cstart[:, None]) & (kc[None, :] < cstart[:, None] + WIN_C)
    dc = np.clip(kc[None, :] - qc[:, None], 1 - WIN_C, WIN_C - 1) + WIN_C - 1
    sel_c = jnp.asarray(dc[:, :, None] == np.arange(2 * WIN_C - 1)[None, None, :], F32)
    tab = jnp.einsum("hab,qkb->haqk", rpb, sel_c, precision=HIGHEST)
    tab = jnp.where(jnp.asarray(col_in)[None, None], tab, NEG)
    following = jnp.concatenate([tab[:, 1:], jnp.full_like(tab[:, :1], NEG)], axis=1)
    return jnp.concatenate([tab, following], axis=-1)


def _na_kernel(q_ref, k_ref, v_ref, kc_ref, vc_ref, bias_ref, qn_ref, kn_ref, g_ref, o_ref,
               kn_s, v_s, kc_s, vc_s, *, rows):
    j = pl.program_id(1)
    scale = HD_A ** -0.5

    @pl.when(j == 0)
    def _():
        kn_s[...] = _heads_rms(k_ref[...], kn_ref[...], g_ref[...]).astype(BF16)
        v_s[...] = v_ref[...].astype(BF16)
        kc_s[...] = kc_ref[...].astype(BF16)
        vc_s[...] = vc_ref[...].astype(BF16)

    r0 = j * NA_QROWS
    kr0 = _na_slab_start(r0, rows)
    start = pl.multiple_of(kr0 * GRID_W, GRID_W)
    n_loc = NA_KROWS * GRID_W
    first_half = lax.broadcasted_iota(jnp.int32, (1, 2 * GRID_W), 1) < GRID_W
    plan = []
    for qr in range(NA_QROWS):
        r = r0 + qr
        rs = jnp.clip(r - WIN_R // 2, 0, rows - WIN_R)
        pairs = []
        for p in range(NA_KROWS // 2):
            ka = kr0 + 2 * p
            in0 = jnp.logical_and(ka >= rs, ka < rs + WIN_R).astype(jnp.int32)
            in1 = jnp.logical_and(ka + 1 >= rs, ka + 1 < rs + WIN_R).astype(jnp.int32)
            idx = jnp.clip(ka - r + WIN_R - 1, 0, 2 * WIN_R - 2)
            pairs.append((idx, jnp.where(first_half, in0, in1) != 0))
        plan.append(pairs)

    for h in range(H_A):
        sl = slice(h * HD_A, (h + 1) * HD_A)
        q = (_head_rms(q_ref[:, sl], qn_ref[:, sl]) * scale).astype(BF16)
        s_raw = _dot_nt(q, kn_s[pl.ds(start, n_loc), sl])
        s_ctx = _dot_nt(q, kc_s[:, sl])
        blocks = []
        for qr in range(NA_QROWS):
            qs = slice(qr * GRID_W, (qr + 1) * GRID_W)
            tiles = [s_raw[qs, p * 2 * GRID_W:(p + 1) * 2 * GRID_W] + jnp.where(keep, bias_ref[h, idx], NEG)
                     for p, (idx, keep) in enumerate(plan[qr])]
            blocks.append(jnp.concatenate(tiles, axis=1))
        s_loc = jnp.concatenate(blocks, axis=0)
        m = jnp.max(jnp.maximum(_fold_lanes(s_loc, jnp.maximum), _fold_lanes(s_ctx, jnp.maximum)),
                    axis=-1, keepdims=True)
        p_loc = jnp.exp(s_loc - m)
        p_ctx = jnp.exp(s_ctx - m)
        den = jnp.sum(_fold_lanes(p_loc, jnp.add) + _fold_lanes(p_ctx, jnp.add), axis=-1, keepdims=True)
        o = _dot(p_loc.astype(BF16), v_s[pl.ds(start, n_loc), sl]) + _dot(p_ctx.astype(BF16), vc_s[:, sl])
        o_ref[:, sl] = o / den


def _na_attention(z, row0, n_batch, seq, kc, vc, bias, qn, kn):
    rows = seq // GRID_W
    assert 2 * GRID_W == LANES and rows % NA_QROWS == 0 and NA_KROWS % 2 == 0 and rows >= NA_KROWS
    for r0 in range(0, rows, NA_QROWS):
        lo = int(np.clip(r0 - WIN_R // 2, 0, rows - WIN_R))
        hi = int(np.clip(r0 + NA_QROWS - 1 - WIN_R // 2, 0, rows - WIN_R)) + WIN_R
        assert hi <= min(lo, rows - NA_KROWS) + NA_KROWS
    past = kc.shape[1]
    nblk = rows // NA_QROWS
    tq = NA_QROWS * GRID_W
    blk0 = row0 // tq
    sblk0 = row0 // seq
    full = lambda c: pl.BlockSpec((seq, W_A), lambda b, j: (sblk0 + b, c))
    cspec = pl.BlockSpec((None, past, W_A), lambda b, j: (b, 0, 0))
    wspec = pl.BlockSpec((1, W_A), lambda b, j: (0, 0))
    return pl.pallas_call(
        functools.partial(_na_kernel, rows=rows), grid=(n_batch, nblk),
        in_specs=[pl.BlockSpec((tq, W_A), lambda b, j: (blk0 + b * nblk + j, 0)),
                  full(1), full(2), cspec, cspec,
                  pl.BlockSpec(bias.shape, lambda b, j: (0, 0, 0, 0)),
                  wspec, wspec, pl.BlockSpec((W_A, W_A), lambda b, j: (0, 0))],
        out_specs=pl.BlockSpec((tq, W_A), lambda b, j: (b * nblk + j, 0)),
        out_shape=jax.ShapeDtypeStruct((n_batch * seq, W_A), F32),
        scratch_shapes=[pltpu.VMEM((seq, W_A), BF16), pltpu.VMEM((seq, W_A), BF16),
                        pltpu.VMEM((past, W_A), BF16), pltpu.VMEM((past, W_A), BF16)],
        compiler_params=_cparams("arbitrary", "arbitrary"), name="na_attention",
    )(z, z, z, kc, vc, bias, qn, kn, _head_group_matrix())


def _tri_masks():
    li = lax.broadcasted_iota(jnp.int32, (CHUNK, CHUNK), 0)
    si = lax.broadcasted_iota(jnp.int32, (CHUNK, CHUNK), 1)
    return li >= si, li <= si


def _mlstm_kernel(*refs, nc, has_init):
    if has_init:
        (q_ref, k_ref, v_ref, og_ref, g_ref, nw_ref, c0_ref, n0_ref, m0_ref,
         o_ref, cf_ref, nf_ref, mf_ref, h_s, c_s, n_s, m_s) = refs
    else:
        (q_ref, k_ref, v_ref, og_ref, g_ref, nw_ref,
         o_ref, cf_ref, nf_ref, mf_ref, h_s, c_s, n_s, m_s) = refs
    nd = 2 * H_B
    if has_init:
        for sidx in range(nd):
            c_s[sidx] = c0_ref[sidx].T
        n_s[...] = n0_ref[...]
        m_s[...] = m0_ref[...]
    else:
        c_s[...] = jnp.zeros_like(c_s)
        n_s[...] = jnp.zeros_like(n_s)
        m_s[...] = jnp.zeros_like(m_s)

    causal, anti = _tri_masks()
    tri_f = causal.astype(F32)
    tri_b = anti.astype(F32)
    kscale = HD_B ** -0.5

    def chunk_step(c, carry):
        for d in range(2):
            cc = c if d == 0 else nc - 1 - c
            t0 = pl.multiple_of(cc * CHUNK, CHUNK)
            g = g_ref[pl.ds(t0, CHUNK), :]
            gt = g.T
            ls = _log_sigmoid(g)
            lst = _log_sigmoid(gt)
            tri_c, tri_r, mask_t = (tri_f, tri_b, anti) if d == 0 else (tri_b, tri_f, causal)
            b_cols = _dot_hi(tri_c, ls)
            b_rows = _dot_hi(lst, tri_r)
            last = CHUNK - 1 if d == 0 else 0
            for h in range(H_B):
                ci = (2 * d) * H_B + h
                cf = (2 * d + 1) * H_B + h
                hs = slice(h * HD_B, (h + 1) * HD_B)
                q = q_ref[pl.ds(t0, CHUNK), hs]
                k = k_ref[pl.ds(t0, CHUNK), hs] * kscale
                v = v_ref[pl.ds(t0, CHUNK), hs]
                qb, kb = q.astype(BF16), k.astype(BF16)
                qtb = q.T.astype(BF16)
                vt = v.T
                b_row = b_rows[cf:cf + 1, :]
                i_row = gt[ci:ci + 1, :]
                ib_col = g[:, ci:ci + 1] - b_cols[:, cf:cf + 1]
                sidx = d * H_B + h
                cst = c_s[sidx]
                nst = n_s[sidx:sidx + 1, :]
                mst = m_s[sidx:sidx + 1, 0:1]
                dmat_t = jnp.where(mask_t, b_row + ib_col, -jnp.inf)
                inter = b_row + mst
                mt = jnp.maximum(inter, jnp.max(dmat_t, axis=0, keepdims=True))
                w_t = jnp.exp(dmat_t - mt) * _dot_nt(kb, qb)
                a = jnp.exp(inter - mt)
                num_t = _dot(vt.astype(BF16), w_t.astype(BF16)) + _dot(cst.astype(BF16), qtb) * a
                nq = _dot(jnp.broadcast_to(nst, (SUB, HD_B)).astype(BF16), qtb)[0:1, :]
                den = jnp.sum(w_t, axis=0, keepdims=True) + a * nq
                h_t = num_t / jnp.maximum(jnp.abs(den), jnp.exp(-mt))
                h_s[d, pl.ds(t0, CHUNK), hs] = h_t.T
                bl = b_row[:, last:last + 1]
                dl = bl - b_row + i_row
                m_new = jnp.maximum(bl + mst, jnp.max(dl, axis=-1, keepdims=True))
                wl = jnp.exp(dl - m_new)
                dec = jnp.exp(bl + mst - m_new)
                c_s[sidx] = dec * cst + _dot((vt * wl).astype(BF16), kb)
                wl8 = jnp.broadcast_to(wl, (SUB, CHUNK)).astype(BF16)
                n_s[sidx:sidx + 1, :] = dec * nst + _dot(wl8, kb)[0:1, :]
                m_s[sidx:sidx + 1, :] = jnp.broadcast_to(m_new, (1, LANES))
        return carry

    lax.fori_loop(0, nc, chunk_step, 0)

    for h in range(H_B):
        hs = slice(h * HD_B, (h + 1) * HD_B)
        hsum = h_s[0, :, hs] + h_s[1, :, hs]
        o_ref[:, hs] = _head_rms(hsum, nw_ref[:, hs]) * jax.nn.sigmoid(og_ref[:, hs])
    for sidx in range(nd):
        cf_ref[sidx] = c_s[sidx].T
    nf_ref[...] = n_s[...]
    mf_ref[...] = m_s[...]


def _mlstm(z, g, row0, n_batch, seq, norm_w, init=None):
    sblk0 = row0 // seq
    nd = 2 * H_B
    spec = lambda c: pl.BlockSpec((seq, W_B), lambda b: (sblk0 + b, c))
    in_specs = [spec(3), spec(4), spec(5), spec(6),
                pl.BlockSpec((seq, LANES), lambda b: (sblk0 + b, 0)),
                pl.BlockSpec((1, W_B), lambda b: (0, 0))]
    args = [z, z, z, z, g, norm_w]
    st_specs = [pl.BlockSpec((None, nd, HD_B, HD_B), lambda b: (b, 0, 0, 0)),
                pl.BlockSpec((None, nd, HD_B), lambda b: (b, 0, 0)),
                pl.BlockSpec((None, nd, LANES), lambda b: (b, 0, 0))]
    if init is not None:
        in_specs += st_specs
        args += list(init)
    return pl.pallas_call(
        functools.partial(_mlstm_kernel, nc=seq // CHUNK, has_init=init is not None), grid=(n_batch,),
        in_specs=in_specs,
        out_specs=[pl.BlockSpec((seq, W_B), lambda b: (b, 0))] + st_specs,
        out_shape=[jax.ShapeDtypeStruct((n_batch * seq, W_B), F32),
                   jax.ShapeDtypeStruct((n_batch, nd, HD_B, HD_B), F32),
                   jax.ShapeDtypeStruct((n_batch, nd, HD_B), F32),
                   jax.ShapeDtypeStruct((n_batch, nd, LANES), F32)],
        scratch_shapes=[pltpu.VMEM((2, seq, W_B), F32), pltpu.VMEM((nd, HD_B, HD_B), F32),
                        pltpu.VMEM((nd, HD_B), F32), pltpu.VMEM((nd, LANES), F32)],
        compiler_params=_cparams("arbitrary"), name="mlstm",
    )(*args)


def _rope_tables(seq):
    half = HD_C // 2
    quarter = half // 2
    t = np.arange(seq)
    inv = ROPE_BASE ** (-np.arange(0, half, 2, dtype=np.float64) / half)
    ang_r = (t // GRID_W)[:, None] * inv[None, :]
    ang_c = (t % GRID_W)[:, None] * inv[None, :]
    cos_t = np.concatenate([np.cos(ang_r), np.cos(ang_r), np.cos(ang_c), np.cos(ang_c)], -1)
    sin_t = np.concatenate([-np.sin(ang_r), np.sin(ang_r), -np.sin(ang_c), np.sin(ang_c)], -1)
    assert cos_t.shape == (seq, 4 * quarter)
    return jnp.asarray(cos_t, F32), jnp.asarray(sin_t, F32)


def _rope(x, cos_t, sin_t):
    quarter = HD_C // 4
    lane = lax.broadcasted_iota(jnp.int32, x.shape, 1)
    first = (lane % (2 * quarter)) < quarter
    swapped = jnp.where(first, pltpu.roll(x, HD_C - quarter, 1), pltpu.roll(x, quarter, 1))
    return x * cos_t + swapped * sin_t


def _ret_kernel(*refs, nc, has_init, use_rope):
    refs = list(refs)
    q_ref, k_ref, v_ref, gg_ref, dl_ref, nw_ref = refs[:6]
    pos = 6
    if use_rope:
        cos_ref, sin_ref = refs[pos:pos + 2]
        pos += 2
    if has_init:
        s0_ref = refs[pos]
        pos += 1
    o_ref, sf_ref, h_s, s_s, dk_s = refs[pos:pos + 5]
    if has_init:
        s_s[...] = s0_ref[...]
    else:
        s_s[...] = jnp.zeros_like(s_s)

    causal, anti = _tri_masks()
    li = lax.broadcasted_iota(jnp.int32, (CHUNK, CHUNK), 0).astype(F32)
    si = lax.broadcasted_iota(jnp.int32, (CHUNK, CHUNK), 1).astype(F32)
    lg_all = _log_sigmoid(dl_ref[...])
    kscale = HD_C ** -0.5
    for d in range(2):
        for h in range(H_C):
            sidx = d * H_C + h
            lg = lg_all[sidx:sidx + 1, :]
            if d == 0:
                dk_s[sidx, 0] = jnp.exp(jnp.where(causal, (li - si) * lg, -jnp.inf))
                dk_s[sidx, 1] = jnp.exp((li + 1.0) * lg)
                dk_s[sidx, 2] = jnp.exp((CHUNK - 1.0 - li) * lg)
            else:
                dk_s[sidx, 0] = jnp.exp(jnp.where(anti, (si - li) * lg, -jnp.inf))
                dk_s[sidx, 1] = jnp.exp((CHUNK - li) * lg)
                dk_s[sidx, 2] = jnp.exp(li * lg)

    def chunk_step(c, carry):
        for d in range(2):
            cc = c if d == 0 else nc - 1 - c
            t0 = pl.multiple_of(cc * CHUNK, CHUNK)
            for h in range(H_C):
                sidx = d * H_C + h
                hs = slice(h * HD_C, (h + 1) * HD_C)
                lg = lg_all[sidx:sidx + 1, :]
                q = q_ref[pl.ds(t0, CHUNK), hs]
                k = k_ref[pl.ds(t0, CHUNK), hs] * kscale
                v = v_ref[pl.ds(t0, CHUNK), hs]
                if use_rope:
                    cos_t = cos_ref[pl.ds(t0, CHUNK), :]
                    sin_t = sin_ref[pl.ds(t0, CHUNK), :]
                    q = _rope(q, cos_t, sin_t)
                    k = _rope(k, cos_t, sin_t)
                decay, q_dec, k_dec = dk_s[sidx, 0], dk_s[sidx, 1], dk_s[sidx, 2]
                c_dec = jnp.exp(CHUNK * lg)
                st = s_s[sidx]
                qb, kb, vb = q.astype(BF16), k.astype(BF16), v.astype(BF16)
                att = _dot_nt(qb, kb) * decay
                o = _dot(att.astype(BF16), vb) + _dot(qb, st.astype(BF16)) * q_dec
                h_s[d, pl.ds(t0, CHUNK), hs] = o
                kd = (k * k_dec).T.astype(BF16)
                s_s[sidx] = c_dec * st + _dot(kd, vb)
        return carry

    lax.fori_loop(0, nc, chunk_step, 0)

    for h in range(H_C):
        hs = slice(h * HD_C, (h + 1) * HD_C)
        osum = h_s[0, :, hs] + h_s[1, :, hs]
        gg = gg_ref[:, hs]
        o_ref[:, hs] = _head_rms(osum, nw_ref[:, hs]) * (gg * jax.nn.sigmoid(gg))
    sf_ref[...] = s_s[...]


def _retention(z, row0, n_batch, seq, decay_rep, norm_w, rope=None, init=None):
    sblk0 = row0 // seq
    nd = 2 * H_C
    w_c = H_C * HD_C
    spec = lambda c: pl.BlockSpec((seq, w_c), lambda b: (sblk0 + b, c))
    in_specs = [spec(0), spec(1), spec(2), spec(3),
                pl.BlockSpec((nd, LANES), lambda b: (0, 0)),
                pl.BlockSpec((1, w_c), lambda b: (0, 0))]
    args = [z, z, z, z, decay_rep, norm_w]
    if rope is not None:
        in_specs += [pl.BlockSpec((seq, HD_C), lambda b: (0, 0))] * 2
        args += list(rope)
    st_spec = pl.BlockSpec((None, nd, HD_C, HD_C), lambda b: (b, 0, 0, 0))
    if init is not None:
        in_specs.append(st_spec)
        args.append(init)
    return pl.pallas_call(
        functools.partial(_ret_kernel, nc=seq // CHUNK, has_init=init is not None, use_rope=rope is not None),
        grid=(n_batch,), in_specs=in_specs,
        out_specs=[pl.BlockSpec((seq, w_c), lambda b: (b, 0)), st_spec],
        out_shape=[jax.ShapeDtypeStruct((n_batch * seq, w_c), F32),
                   jax.ShapeDtypeStruct((n_batch, nd, HD_C, HD_C), F32)],
        scratch_shapes=[pltpu.VMEM((2, seq, w_c), F32), pltpu.VMEM((nd, HD_C, HD_C), F32),
                        pltpu.VMEM((nd, 3, CHUNK, CHUNK), F32)],
        compiler_params=_cparams("arbitrary"), name="retention",
    )(*args)


def _dft_tables(n):
    idx = (np.arange(n)[:, None] * np.arange(n)[None, :]) % n
    ang = 2.0 * np.pi * idx / n
    return np.cos(ang) / np.sqrt(n), np.sin(ang) / np.sqrt(n)


def _fnet_kernel(x_ref, cw_ref, sw_ref, cs_ref, ss_ref, o_ref):
    for g in range(N_FG):
        gs = slice(g * FG_W, (g + 1) * FG_W)
        x = x_ref[:, gs].astype(BF16)
        xc = _dot(x, cw_ref[...]).astype(BF16)
        xs = _dot(x, sw_ref[...]).astype(BF16)
        o_ref[:, gs] = _dot(cs_ref[...], xc) - _dot(ss_ref[...], xs)


def _fnet(z, row0, n_batch, seq):
    sblk0 = row0 // seq
    w_d = N_FG * FG_W
    cw, sw = _dft_tables(FG_W)
    cs, ss = _dft_tables(seq)
    tabs = [jnp.asarray(a, F32).astype(BF16) for a in (cw, sw, cs, ss)]
    wspec = pl.BlockSpec((FG_W, FG_W), lambda b: (0, 0))
    sspec = pl.BlockSpec((seq, seq), lambda b: (0, 0))
    return pl.pallas_call(
        _fnet_kernel, grid=(n_batch,),
        in_specs=[pl.BlockSpec((seq, w_d), lambda b: (sblk0 + b, 4)), wspec, wspec, sspec, sspec],
        out_specs=pl.BlockSpec((seq, w_d), lambda b: (b, 0)),
        out_shape=jax.ShapeDtypeStruct((n_batch * seq, w_d), F32),
        compiler_params=_cparams("arbitrary"), name="fnet",
    )(z, *tabs)


SUB = 8
assert D_MODEL == SUB * LANES


def _store_token_major(ref, lead, val):
    n = val.shape[0]
    for c in range(SUB):
        ref[lead + (pl.ds(c, n, stride=SUB), slice(None))] = val[:, c * LANES:(c + 1) * LANES]


def _load_token_major(ref, lead, n):
    return jnp.concatenate([ref[lead + (pl.ds(c, n, stride=SUB), slice(None))] for c in range(SUB)], axis=1)


def _outproj_router_kernel(aa_ref, ab_ref, ba_ref, bb_ref, xa_ref, xb_ref, mod_ref, wa_ref, wb_ref,
                           wrh_ref, wrl_ref, br_ref, x1_ref, h2_ref, ti_ref, tw_ref, rank_ref, cnt_ref, cnt_s):
    i = pl.program_id(0)
    a = _rows_read(i, aa_ref, ab_ref)
    b = _rows_read(i, ba_ref, bb_ref)
    y = _dot(a.astype(BF16), wa_ref[...]) + _dot(b.astype(BF16), wb_ref[...])
    x1 = _rows_read(i, xa_ref, xb_ref) + mod_ref[2:3, :] * y
    x1_ref[...] = x1
    h2 = _rms_mod(x1, mod_ref[3:4, :], mod_ref[4:5, :])
    _store_token_major(h2_ref, (), h2)
    logits = _dot_x3(h2, wrh_ref[...], wrl_ref[...]) + br_ref[...]
    lane = lax.broadcasted_iota(jnp.int32, logits.shape, 1)
    lane_f = lane.astype(F32)
    cur = logits
    vals, picks = [], []
    ti = jnp.zeros(logits.shape, jnp.int32)
    for kk in range(TOP_K):
        mx = jnp.max(cur, axis=-1, keepdims=True)
        idx = jnp.min(jnp.where(cur == mx, lane_f, float(LANES)), axis=-1, keepdims=True)
        ti = jnp.where(lane == kk, idx.astype(jnp.int32), ti)
        pick = lane_f == idx
        cur = jnp.where(pick, -jnp.inf, cur)
        vals.append(mx)
        picks.append(pick)
    es = [jnp.exp(v - vals[0]) for v in vals]
    tot = es[0] + es[1] + es[2] + es[3]
    tw = jnp.zeros(logits.shape, F32)
    for kk in range(TOP_K):
        tw = jnp.where(lane == kk, es[kk] / tot, tw)
    ti_ref[...] = ti
    tw_ref[...] = tw

    @pl.when(i == 0)
    def _():
        cnt_s[...] = jnp.zeros_like(cnt_s)

    onehot = jnp.zeros(logits.shape, F32)
    for pick in picks:
        onehot = onehot + jnp.where(pick, 1.0, 0.0)
    n = logits.shape[0]
    earlier = (lax.broadcasted_iota(jnp.int32, (n, n), 1) < lax.broadcasted_iota(jnp.int32, (n, n), 0))
    before = cnt_s[...] + _dot(jnp.where(earlier, 1.0, 0.0).astype(BF16), onehot.astype(BF16))
    rank = jnp.zeros(logits.shape, jnp.int32)
    for kk, pick in enumerate(picks):
        r_k = jnp.sum(jnp.where(pick, before, 0.0), axis=-1, keepdims=True)
        rank = jnp.where(lane == kk, r_k.astype(jnp.int32), rank)
    rank_ref[...] = rank
    cnt_s[...] = cnt_s[...] + jnp.sum(onehot, axis=0, keepdims=True)
    cnt_ref[...] = cnt_s[...]


def _outproj_router(a, b, x, mod, w_out, wr, br):
    t = N_TILES * TM
    d = x.a.shape[1]
    wid = a.a.shape[1]
    row = lambda w: pl.BlockSpec((TM, w), lambda i: (i, 0))
    const = lambda r, c: pl.BlockSpec((r, c), lambda i: (0, 0))
    wr_hi, wr_lo = _split_bf16(wr)
    return pl.pallas_call(
        _outproj_router_kernel, grid=(N_TILES,),
        in_specs=a.specs() + b.specs() + x.specs() + [
                  pl.BlockSpec((None, 6, d), lambda i: (_cond_row(i), 0, 0)),
                  pl.BlockSpec((wid, d), lambda i: (0, 0)), pl.BlockSpec((wid, d), lambda i: (1, 0)),
                  const(d, LANES), const(d, LANES), const(1, LANES)],
        out_specs=[row(d), pl.BlockSpec((TM * SUB, LANES), lambda i: (i, 0)), row(LANES), row(LANES), row(LANES),
                   const(1, LANES)],
        out_shape=[jax.ShapeDtypeStruct((t, d), F32), jax.ShapeDtypeStruct((t * SUB, LANES), F32),
                   jax.ShapeDtypeStruct((t, LANES), jnp.int32), jax.ShapeDtypeStruct((t, LANES), F32),
                   jax.ShapeDtypeStruct((t, LANES), jnp.int32), jax.ShapeDtypeStruct((1, LANES), F32)],
        scratch_shapes=[pltpu.VMEM((1, LANES), F32)],
        compiler_params=_cparams("arbitrary"), name="outproj_router",
    )(a.a, a.b, b.a, b.b, x.a, x.b, mod, w_out, w_out, wr_hi, wr_lo, br)


N_TILES_MAX = N_TILES * TM * TOP_K // TM_E + N_EXP
PLAN_LANES = 2 * LANES
assert N_TILES_MAX <= PLAN_LANES and N_EXP <= LANES


def _plan_kernel(cnt_ref, ti_ref, rank_ref, dest_ref, meta_ref, start_s):
    i = pl.program_id(0)

    @pl.when(i == 0)
    def _():
        cnt = cnt_ref[...]
        tiles = jnp.floor((cnt + float(TM_E - 1)) * (1.0 / TM_E))
        sub = lax.broadcasted_iota(jnp.int32, (LANES, LANES), 0)
        lane = lax.broadcasted_iota(jnp.int32, (LANES, LANES), 1)
        upto = jnp.where(sub <= lane, 1.0, 0.0).astype(BF16)
        tile_end = _dot(jnp.broadcast_to(tiles, (SUB, LANES)).astype(BF16), upto)[0:1, :]
        tile_start = tile_end - tiles
        start_s[...] = tile_start * float(TM_E * SUB)
        n_tiles = jnp.max(tile_end, axis=-1, keepdims=True)
        used = tiles > 0.0

        def column(row):
            return jnp.sum(jnp.where(sub == lane, jnp.broadcast_to(row, (LANES, LANES)), 0.0), axis=-1, keepdims=True)

        end_c, start_c, tiles_c = column(tile_end), column(tile_start), column(tiles)
        used_b = jnp.broadcast_to(jnp.where(used, 1.0, 0.0), (LANES, LANES))
        pos_c = jnp.sum(jnp.where(lane <= sub, used_b, 0.0), axis=-1, keepdims=True)
        par_c = (pos_c - 1.0) - 2.0 * jnp.floor((pos_c - 1.0) * 0.5)
        nxt_c = jnp.min(jnp.where(jnp.logical_and(lane > sub, used_b > 0.0), lane.astype(F32), float(LANES)),
                        axis=-1, keepdims=True)
        nxt_c = jnp.where(nxt_c < float(LANES), nxt_c, -1.0)

        tid = lax.broadcasted_iota(jnp.int32, (LANES, PLAN_LANES), 1).astype(F32)
        exp_id = lax.broadcasted_iota(jnp.int32, (LANES, PLAN_LANES), 0)
        tid_used = jnp.minimum(tid, n_tiles - 1.0)
        te = jnp.sum(jnp.where(jnp.logical_and(exp_id < N_EXP, end_c <= tid_used), 1.0, 0.0), axis=0, keepdims=True)
        first = jnp.sum(jnp.where(jnp.logical_and(tiles_c > 0.0, start_c == tid), 1.0, 0.0), axis=0, keepdims=True)
        mine = te == exp_id.astype(F32)
        nxt = jnp.sum(jnp.where(mine, nxt_c, 0.0), axis=0, keepdims=True)
        par = jnp.sum(jnp.where(mine, par_c, 0.0), axis=0, keepdims=True)
        last = jnp.where(used, tile_end - 1.0, 0.0)
        last = jnp.concatenate([last, jnp.zeros((1, PLAN_LANES - LANES), F32)], axis=1)

        row_id = lax.broadcasted_iota(jnp.int32, (SUB, PLAN_LANES), 0)
        meta = jnp.zeros((SUB, PLAN_LANES), F32)
        for r, val in enumerate((te, first, nxt, par, last, jnp.broadcast_to(n_tiles, (1, PLAN_LANES)))):
            meta = jnp.where(row_id == r, jnp.broadcast_to(val, (SUB, PLAN_LANES)), meta)
        meta_ref[...] = meta.astype(jnp.int32)

    ti = ti_ref[...]
    rank = rank_ref[...]
    lane = lax.broadcasted_iota(jnp.int32, ti.shape, 1)
    dest = jnp.zeros(ti.shape, jnp.int32)
    for kk in range(TOP_K):
        base = jnp.sum(jnp.where(lane == ti[:, kk:kk + 1], start_s[...], 0.0), axis=-1, keepdims=True)
        dest = jnp.where(lane == kk, base.astype(jnp.int32) + rank[:, kk:kk + 1] * SUB, dest)
    dest_ref[...] = dest.T[0:SUB, :]


def _route_plan(cnt, ti, rank):
    t = ti.shape[0]
    row = pl.BlockSpec((TM, LANES), lambda i: (i, 0))
    dest, meta = pl.pallas_call(
        _plan_kernel, grid=(t // TM,),
        in_specs=[pl.BlockSpec((1, LANES), lambda i: (0, 0)), row, row],
        out_specs=[pl.BlockSpec((SUB, TM), lambda i: (0, i)), pl.BlockSpec((SUB, PLAN_LANES), lambda i: (0, 0))],
        out_shape=[jax.ShapeDtypeStruct((SUB, t), jnp.int32), jax.ShapeDtypeStruct((SUB, PLAN_LANES), jnp.int32)],
        scratch_shapes=[pltpu.VMEM((1, LANES), F32)],
        compiler_params=_cparams("arbitrary"), name="moe_plan",
    )(cnt, ti, rank)
    plan = dict(tile_expert=meta[0, :N_TILES_MAX], first=meta[1, :N_TILES_MAX], tile_next=meta[2, :N_TILES_MAX],
                tile_parity=meta[3, :N_TILES_MAX], last_tile=meta[4, :N_EXP], n_tiles=meta[5, :1])
    return dest, plan


DISPATCH_BLK = 1024


def _dispatch_kernel(lt_ref, nt_ref, *rest):
    dest_refs, (h_ref, xs_ref, h_s, zero_s, sem, hsem) = rest[:TOP_K], rest[TOP_K:]
    i = pl.program_id(0)
    tile_rows = TM_E * SUB
    n_tiles_max = xs_ref.shape[0] // tile_rows
    n_tok = h_s.shape[0] // SUB

    @pl.when(i == 0)
    def _():
        stage = pltpu.make_async_copy(h_ref, h_s, hsem)
        stage.start()
        zero_s[...] = jnp.zeros_like(zero_s)

        def zero_tile(tile):
            r0 = pl.multiple_of(tile * tile_rows, tile_rows)
            return pltpu.make_async_copy(zero_s, xs_ref.at[pl.ds(r0, tile_rows), :], sem)

        def start_unused(j, carry):
            zero_tile(j).start()
            return carry

        def wait_unused(j, carry):
            zero_tile(j).wait()
            return carry

        for e in range(N_EXP):
            zero_tile(lt_ref[e]).start()
        lax.fori_loop(nt_ref[0], n_tiles_max, start_unused, 0)
        for e in range(N_EXP):
            zero_tile(lt_ref[e]).wait()
        lax.fori_loop(nt_ref[0], n_tiles_max, wait_unused, 0)
        stage.wait()

    base = i * DISPATCH_BLK

    def issue(t, carry):
        src = pl.multiple_of((base + t) * SUB, SUB)
        for kk in range(TOP_K):
            row = pl.multiple_of(dest_refs[kk][0, t], SUB)
            pltpu.make_async_copy(h_s.at[pl.ds(src, SUB), :], xs_ref.at[pl.ds(row, SUB), :], sem).start(priority=kk % 2)
        return carry

    lax.fori_loop(0, DISPATCH_BLK, issue, 0, unroll=2)

    @pl.when(i == pl.num_programs(0) - 1)
    def _():
        for kk in range(TOP_K):
            pltpu.make_async_copy(h_s, xs_ref.at[pl.ds(0, n_tok * SUB), :], sem).wait()


def _dispatch(h2, dest, plan, n_rows):
    t = h2.shape[0] // SUB
    nblk = t // DISPATCH_BLK
    return pl.pallas_call(
        _dispatch_kernel,
        grid_spec=pltpu.PrefetchScalarGridSpec(
            num_scalar_prefetch=2, grid=(nblk,),
            in_specs=[pl.BlockSpec((None, 1, DISPATCH_BLK), lambda i, lt, nt: (i, 0, 0), memory_space=pltpu.SMEM)] * TOP_K
                     + [pl.BlockSpec(memory_space=pl.ANY)],
            out_specs=pl.BlockSpec(memory_space=pl.ANY),
            scratch_shapes=[pltpu.VMEM((t * SUB, LANES), F32), pltpu.VMEM((TM_E * SUB, LANES), F32),
                            pltpu.SemaphoreType.DMA(()), pltpu.SemaphoreType.DMA(())]),
        out_shape=jax.ShapeDtypeStruct((n_rows * SUB, LANES), F32),
        compiler_params=_cparams("arbitrary"), name="moe_dispatch",
    )(plan["last_tile"], plan["n_tiles"], *[dest[kk].reshape(nblk, 1, DISPATCH_BLK) for kk in range(TOP_K)], h2)


def _expert_kernel(te_ref, tf_ref, nt_ref, nx_ref, par_ref, xs_ref, w1_ref, b1_ref, w2_ref, b2_ref, ys_ref,
                   w1f, w2f, w1_s, w2_s, wsem, *, layer):
    i = pl.program_id(0)

    def fetch(expert, slot):
        return (pltpu.make_async_copy(w1_ref.at[layer, expert], w1f.at[slot], wsem.at[slot]),
                pltpu.make_async_copy(w2_ref.at[layer, expert], w2f.at[slot], wsem.at[slot]))

    @pl.when(i == 0)
    def _():
        for cp in fetch(te_ref[0], 0):
            cp.start(priority=1)

    @pl.when(i < nt_ref[0])
    def _():
        @pl.when(tf_ref[i] == 1)
        def _():
            slot = par_ref[i]
            for cp in fetch(te_ref[i], slot):
                cp.wait()
            w1_s[...] = w1f[slot].astype(BF16)
            w2_s[...] = w2f[slot].astype(BF16)

            @pl.when(nx_ref[i] >= 0)
            def _():
                for cp in fetch(nx_ref[i], 1 - slot):
                    cp.start(priority=1)

        x = _load_token_major(xs_ref, (), TM_E)
        u = _dot(x.astype(BF16), w1_s[...]) + b1_ref[...]
        g = jnp.minimum(u[:, :D_FF], SWIGLU_LIMIT)
        up = jnp.clip(u[:, D_FF:], -SWIGLU_LIMIT, SWIGLU_LIMIT)
        act = (up + 1.0) * g * jax.nn.sigmoid(SWIGLU_ALPHA * g)
        _store_token_major(ys_ref, (), _dot(act.astype(BF16), w2_s[...]) + b2_ref[...])

    @pl.when(i >= nt_ref[0])
    def _():
        ys_ref[...] = jnp.zeros_like(ys_ref)


def _experts(xs, plan, layer, w1, b1, w2, b2):
    d = D_MODEL
    nt = xs.shape[0] // (TM_E * SUB)
    tile = lambda i, te, tf, ntl, nx, par: (jnp.minimum(i, ntl[0] - 1), 0)
    otile = lambda i, te, tf, ntl, nx, par: (i, 0)
    bmap = lambda i, te, tf, ntl, nx, par: (layer, te[i], 0, 0)
    return pl.pallas_call(
        functools.partial(_expert_kernel, layer=layer),
        grid_spec=pltpu.PrefetchScalarGridSpec(
            num_scalar_prefetch=5, grid=(nt,),
            in_specs=[pl.BlockSpec((TM_E * SUB, LANES), tile),
                      pl.BlockSpec(memory_space=pl.ANY),
                      pl.BlockSpec((None, None, 1, 2 * D_FF), bmap),
                      pl.BlockSpec(memory_space=pl.ANY),
                      pl.BlockSpec((None, None, 1, d), bmap)],
            out_specs=pl.BlockSpec((TM_E * SUB, LANES), otile),
            scratch_shapes=[pltpu.VMEM((2, d, 2 * D_FF), F32), pltpu.VMEM((2, D_FF, d), F32),
                            pltpu.VMEM((d, 2 * D_FF), BF16), pltpu.VMEM((D_FF, d), BF16),
                            pltpu.SemaphoreType.DMA((2,))]),
        out_shape=jax.ShapeDtypeStruct(xs.shape, F32),
        compiler_params=_cparams("arbitrary"), name="moe_experts",
    )(plan["tile_expert"], plan["first"], plan["n_tiles"], plan["tile_next"], plan["tile_parity"],
      xs, w1, b1, w2, b2)


def _combine_kernel(*refs):
    dest_refs = refs[:TOP_K]
    x1_ref, tw_ref, mod_ref, ys_ref, *o_refs, buf, sem = refs[TOP_K:]
    i = pl.program_id(0)
    j = i - 1
    n = pl.num_programs(0) - 1
    tile_rows = TM * SUB

    for s in range(2):
        @pl.when(jnp.logical_and(i < n, i % 2 == s))
        def _():
            def issue(t, carry):
                dst = pl.multiple_of(t * SUB, SUB)
                for kk in range(TOP_K):
                    row = pl.multiple_of(dest_refs[kk][0, t], SUB)
                    pltpu.make_async_copy(ys_ref.at[pl.ds(row, SUB), :], buf.at[s, kk, pl.ds(dst, SUB), :],
                                          sem.at[s]).start(priority=kk % 2)
                return carry

            lax.fori_loop(0, TM, issue, 0, unroll=2)

    @pl.when(j >= 0)
    def _():
        slot = j % 2
        for kk in range(TOP_K):
            pltpu.make_async_copy(ys_ref.at[pl.ds(0, tile_rows), :], buf.at[slot, kk], sem.at[slot]).wait()
        tw = tw_ref[...]
        y = tw[:, 0:1] * _load_token_major(buf, (slot, 0), TM)
        for kk in range(1, TOP_K):
            y = y + tw[:, kk:kk + 1] * _load_token_major(buf, (slot, kk), TM)
        out = x1_ref[...] + mod_ref[5:6, :] * y
        if len(o_refs) == 1:
            o_refs[0][...] = out
        else:
            @pl.when(j < N_CTX_TILES)
            def _():
                o_refs[0][...] = out

            @pl.when(j >= N_CTX_TILES)
            def _():
                o_refs[1][...] = out


def _combine(ys, dest, x1, tw, mod, split):
    t, d = x1.shape
    nblk = t // TM
    prev = lambda i: jnp.maximum(i - 1, 0)
    row = lambda w: pl.BlockSpec((TM, w), lambda i: (prev(i), 0))
    return pl.pallas_call(
        _combine_kernel, grid=(nblk + 1,),
        in_specs=[pl.BlockSpec((None, 1, TM), lambda i: (jnp.minimum(i, nblk - 1), 0, 0), memory_space=pltpu.SMEM)] * TOP_K + [
                  row(d), row(LANES),
                  pl.BlockSpec((None, 6, d), lambda i: (_cond_row(prev(i)), 0, 0)),
                  pl.BlockSpec(memory_space=pl.ANY)],
        out_specs=_Rows(x1, x1, 0).specs(lag=1) if split else row(d),
        out_shape=([jax.ShapeDtypeStruct((N_CTX_TILES * TM, d), F32),
                    jax.ShapeDtypeStruct(((N_TILES - N_CTX_TILES) * TM, d), F32)] if split
                   else jax.ShapeDtypeStruct((t, d), F32)),
        scratch_shapes=[pltpu.VMEM((2, TOP_K, TM * SUB, LANES), F32), pltpu.SemaphoreType.DMA((2,))],
        compiler_params=_cparams("arbitrary"), name="moe_combine",
    )(*[dest[kk].reshape(nblk, 1, TM) for kk in range(TOP_K)], x1, tw, mod, ys)


def _moe(x1, h2, ti, tw, rank, cnt, mod, layer, w1, b1, w2, b2, split):
    depth = w1.shape[0]
    assert x1.shape[0] == N_TILES * TM
    dest, plan = _route_plan(cnt, ti, rank)
    xs = _dispatch(h2, dest, plan, N_TILES_MAX * TM_E)
    ys = _experts(xs, plan, layer, w1, b1.reshape(depth, N_EXP, 1, -1), w2, b2.reshape(depth, N_EXP, 1, -1))
    return _combine(ys, dest, x1, tw, mod, split)


def _pad_lanes(a, value=0.0):
    return jnp.pad(a, ((0, 0), (0, LANES - a.shape[1])), constant_values=value)


def kernel(x_prompt, x_sample, cache_na_k, cache_na_v, state_mlstm_C, state_mlstm_n, state_mlstm_m, state_ret_S, c, c_ctx, w_mod, b_mod, w_in_even, mlstm_gate_b, na_q_norm, na_k_norm, na_rpb, mlstm_norm, w_out_even, w_in_odd, ret_decay, ret_norm, w_out_odd, w_router, b_router, w_moe_in, b_moe_in, w_moe_out, b_moe_out):
    nb_c, s_c, d = x_prompt.shape
    nb_l, s_l, _ = x_sample.shape
    t_c = nb_c * s_c
    t_l = nb_l * s_l
    assert t_c == 4 * SEG and s_l == SEG and d == D_MODEL
    depth = w_mod.shape[0]
    dt = x_prompt.dtype

    x = _Rows(x_prompt.reshape(t_c, d), x_sample.reshape(t_l, d), 0)
    cond = jnp.concatenate([c_ctx[None, :], c, jnp.zeros((N_COND - 1 - nb_l, d), F32)], axis=0)
    mod = _modulation(cond, w_mod, b_mod).reshape(depth, N_COND, 6, d)

    outs = {}
    for l in range(depth):
        e = l // 2
        mod_l = mod[l]
        if l % 2 == 0:
            w_in = w_in_even[e]
            n_main = 3 * W_A + 4 * W_B
            wg = _pad_lanes(w_in[:, n_main:])
            bg = _pad_lanes(mlstm_gate_b[e].reshape(1, 4 * H_B))
            z, g = _inproj(x, mod_l, w_in.astype(BF16), n_main, wg, bg)
            qn = jnp.tile(na_q_norm[e].reshape(1, HD_A), (1, H_A))
            kn = jnp.tile(na_k_norm[e].reshape(1, HD_A), (1, H_A))
            oa_c, ka_c, va_c = _ctx_attention(z, nb_c, s_c, qn, kn)
            past = cache_na_k.shape[2]
            oa_l = _na_attention(z, t_c, nb_l, s_l,
                                 cache_na_k[:, e].reshape(nb_l, past, W_A), cache_na_v[:, e].reshape(nb_l, past, W_A),
                                 _na_bias_pairs(na_rpb[e]), qn, kn)
            nw = mlstm_norm[e].reshape(1, W_B)
            hm_c, c_fin, n_fin, m_fin = _mlstm(z, g, 0, nb_c, s_c, nw)
            init = (state_mlstm_C[:, e].reshape(nb_l, 2 * H_B, HD_B, HD_B),
                    state_mlstm_n[:, e].reshape(nb_l, 2 * H_B, HD_B),
                    jnp.broadcast_to(state_mlstm_m[:, e].reshape(nb_l, 2 * H_B, 1), (nb_l, 2 * H_B, LANES)))
            hm_l = _mlstm(z, g, t_c, nb_l, s_l, nw, init)[0]
            a = _Rows(oa_c, oa_l, 0)
            b = _Rows(hm_c, hm_l, 0)
            w_out = w_out_even[e].astype(BF16)
            outs.setdefault("na_k", []).append(ka_c.reshape(nb_c, s_c, H_A, HD_A))
            outs.setdefault("na_v", []).append(va_c.reshape(nb_c, s_c, H_A, HD_A))
            outs.setdefault("C", []).append(c_fin.reshape(nb_c, 2, H_B, HD_B, HD_B))
            outs.setdefault("n", []).append(n_fin.reshape(nb_c, 2, H_B, HD_B))
            outs.setdefault("m", []).append(m_fin[:, :, 0].reshape(nb_c, 2, H_B))
        else:
            w_c = H_C * HD_C
            z = _inproj(x, mod_l, w_in_odd[e].astype(BF16), 4 * w_c + N_FG * FG_W)
            dl_rep = jnp.broadcast_to(ret_decay[e].reshape(2 * H_C, 1), (2 * H_C, LANES))
            nw = ret_norm[e].reshape(1, w_c)
            hr_c, s_fin = _retention(z, 0, nb_c, s_c, dl_rep, nw)
            hr_l = _retention(z, t_c, nb_l, s_l, dl_rep, nw, rope=_rope_tables(s_l),
                              init=state_ret_S[:, e].reshape(nb_l, 2 * H_C, HD_C, HD_C))[0]
            fd_c = _fnet(z, 0, nb_c, s_c)
            fd_l = _fnet(z, t_c, nb_l, s_l)
            a = _Rows(hr_c, hr_l, 0)
            b = _Rows(fd_c, fd_l, 0)
            w_out = w_out_odd[e].astype(BF16)
            outs.setdefault("S", []).append(s_fin.reshape(nb_c, 2, H_C, HD_C, HD_C))
        wr = _pad_lanes(w_router[l])
        br = _pad_lanes(b_router[l].reshape(1, N_EXP), NEG)
        x1, h2, ti, tw, rank, cnt = _outproj_router(a, b, x, mod_l, w_out, wr, br)
        last = l == depth - 1
        x = _moe(x1, h2, ti, tw, rank, cnt, mod_l, l, w_moe_in, b_moe_in, w_moe_out, b_moe_out, split=last)
        if not last:
            x = _whole(x)

    y_prompt = x[0].reshape(nb_c, s_c, d)
    y_sample = x[1].reshape(nb_l, s_l, d)
    stack = lambda key: jnp.stack(outs[key], axis=1).astype(dt)
    return (y_prompt, y_sample, stack("na_k"), stack("na_v"), stack("C"), stack("n"), stack("m"), stack("S"))
```

```python
import functools
from typing import NamedTuple

import numpy as np
import jax
import jax.numpy as jnp
from jax import lax
from jax.experimental import pallas as pl
from jax.experimental.pallas import tpu as pltpu

F32 = jnp.float32
BF16 = jnp.bfloat16
HIGHEST = lax.Precision.HIGHEST

D_MODEL = 1024
GRID_W = 64
WIN_R = 8
WIN_C = 16
H_A, HD_A = 8, 64
H_B, HD_B = 4, 128
H_C, HD_C = 4, 128
N_FG, FG_W = 4, 128
W_A = H_A * HD_A
W_B = H_B * HD_B
N_EXP = 32
TOP_K = 4
D_FF = D_MODEL
SWIGLU_LIMIT = 7.0
SWIGLU_ALPHA = 1.702
CHUNK = 128
ROPE_BASE = 10000.0
EPS = 1e-6

LANES = 128
SEG = 1024
N_COND = 8
TM = 512
N_TILES = 8 * SEG // TM
N_CTX_TILES = 4 * SEG // TM
TM_E = 256
NEG = -1e30
VMEM_LIMIT = 56 * 1024 * 1024


def _cparams(*sem):
    return pltpu.CompilerParams(dimension_semantics=sem, vmem_limit_bytes=VMEM_LIMIT)


def _cond_row(i):
    return jnp.maximum((i * TM) // SEG - 3, 0)


def _log_sigmoid(x):
    return jnp.minimum(x, 0.0) - jnp.log1p(jnp.exp(-jnp.abs(x)))


def _dot(a, b):
    return jnp.dot(a, b, preferred_element_type=F32)


def _dot_nt(a, b):
    return lax.dot_general(a, b, (((1,), (1,)), ((), ())), preferred_element_type=F32)


def _dot_hi(a, b):
    return jnp.dot(a, b, precision=HIGHEST, preferred_element_type=F32)


def _split_bf16(x):
    hi = x.astype(BF16)
    return hi, (x - hi.astype(F32)).astype(BF16)


def _split3_bf16(x):
    hi = x.astype(BF16)
    rest = x - hi.astype(F32)
    mid = rest.astype(BF16)
    return hi, mid, (rest - mid.astype(F32)).astype(BF16)


def _dot_x3(x, w_hi, w_lo):
    x_hi, x_lo = _split_bf16(x)
    return _dot(x_hi, w_hi) + (_dot(x_lo, w_hi) + _dot(x_hi, w_lo))


def _mod_kernel(cond_ref, w_ref, b_ref, o_ref):
    c = cond_ref[...]
    s = c * jax.nn.sigmoid(c)
    o_ref[...] = _dot_hi(s, w_ref[...]) + b_ref[...]


def _modulation(cond, w_mod, b_mod):
    depth, d, n = w_mod.shape
    tn = 1536
    return pl.pallas_call(
        _mod_kernel,
        grid=(depth, n // tn),
        in_specs=[pl.BlockSpec((N_COND, d), lambda l, j: (0, 0)),
                  pl.BlockSpec((None, d, tn), lambda l, j: (l, 0, j)),
                  pl.BlockSpec((None, 1, tn), lambda l, j: (l, 0, j))],
        out_specs=pl.BlockSpec((None, N_COND, tn), lambda l, j: (l, 0, j)),
        out_shape=jax.ShapeDtypeStruct((depth, N_COND, n), F32),
        compiler_params=_cparams("arbitrary", "arbitrary"),
        name="modulation",
    )(cond, w_mod, b_mod.reshape(depth, 1, n))


def _rms_mod(x, shift, scale):
    h = x * lax.rsqrt(jnp.mean(x * x, axis=-1, keepdims=True) + EPS)
    return h * (1.0 + scale) + shift


class _Rows(NamedTuple):
    a: jax.Array
    b: jax.Array
    off_b: int

    def specs(self, lag=0):
        width = self.a.shape[1]
        off_b = self.off_b
        tile = (lambda i: i) if lag == 0 else (lambda i: jnp.maximum(i - lag, 0))
        return [pl.BlockSpec((TM, width), lambda i: (jnp.minimum(tile(i), N_CTX_TILES - 1), 0)),
                pl.BlockSpec((TM, width), lambda i: (jnp.maximum(tile(i) - N_CTX_TILES, 0) + off_b, 0))]


def _whole(x):
    return _Rows(x, x, N_CTX_TILES)


def _rows_read(tile, ref_a, ref_b):
    return jnp.where(tile < N_CTX_TILES, ref_a[...], ref_b[...])


def _inproj_kernel(xa_ref, xb_ref, mod_ref, w_ref, z_ref):
    x = _rows_read(pl.program_id(0), xa_ref, xb_ref)
    h = _rms_mod(x, mod_ref[0:1, :], mod_ref[1:2, :])
    z_ref[...] = _dot(h.astype(BF16), w_ref[...])


def _inproj_gate_kernel(xa_ref, xb_ref, mod_ref, w_ref, wgh_ref, wgl_ref, bg_ref, z_ref, g_ref):
    x = _rows_read(pl.program_id(0), xa_ref, xb_ref)
    h = _rms_mod(x, mod_ref[0:1, :], mod_ref[1:2, :])
    z_ref[...] = _dot(h.astype(BF16), w_ref[...])
    g_ref[...] = _dot_x3(h, wgh_ref[...], wgl_ref[...]) + bg_ref[...]


def _inproj(x, mod, w, n, wg=None, bg=None):
    t = N_TILES * TM
    d = x.a.shape[1]
    in_specs = x.specs() + [pl.BlockSpec((None, 6, d), lambda i: (_cond_row(i), 0, 0)),
                            pl.BlockSpec((d, n), lambda i: (0, 0))]
    z_spec = pl.BlockSpec((TM, n), lambda i: (i, 0))
    z_shape = jax.ShapeDtypeStruct((t, n), F32)
    if wg is None:
        return pl.pallas_call(
            _inproj_kernel, grid=(N_TILES,), in_specs=in_specs, out_specs=z_spec, out_shape=z_shape,
            compiler_params=_cparams("arbitrary"), name="inproj",
        )(x.a, x.b, mod, w)
    wg_hi, wg_lo = _split_bf16(wg)
    in_specs += [pl.BlockSpec((d, LANES), lambda i: (0, 0))] * 2 + [pl.BlockSpec((1, LANES), lambda i: (0, 0))]
    return pl.pallas_call(
        _inproj_gate_kernel, grid=(N_TILES,), in_specs=in_specs,
        out_specs=[z_spec, pl.BlockSpec((TM, LANES), lambda i: (i, 0))],
        out_shape=[z_shape, jax.ShapeDtypeStruct((t, LANES), F32)],
        compiler_params=_cparams("arbitrary"), name="inproj_gate",
    )(x.a, x.b, mod, w, wg_hi, wg_lo, bg)


def _head_rms(x, w):
    return x * lax.rsqrt(jnp.mean(x * x, axis=-1, keepdims=True) + EPS) * w


def _fold_lanes(x, op):
    parts = [x[:, c * LANES:(c + 1) * LANES] for c in range(x.shape[1] // LANES)]
    while len(parts) > 1:
        parts = [op(parts[c], parts[c + 1]) if c + 1 < len(parts) else parts[c] for c in range(0, len(parts), 2)]
    return parts[0]


def _head_group_matrix():
    head = np.arange(W_A) // HD_A
    return jnp.asarray((head[:, None] == head[None, :]) / HD_A, F32).astype(BF16)


def _heads_rms(x, w_row, g):
    hi, lo = _split_bf16(x * x)
    return x * lax.rsqrt(_dot(hi, g) + _dot(lo, g) + EPS) * w_row


def _ctx_attn_kernel(q_ref, k_ref, v_ref, qn_ref, kn_ref, g_ref, o_ref, ko_ref, vo_ref, km_s, vm_s):
    seq = q_ref.shape[0]
    scale = HD_A ** -0.5
    g = g_ref[...]
    q = _heads_rms(q_ref[...], qn_ref[...] * scale, g).astype(BF16)
    k = _heads_rms(k_ref[...], kn_ref[...], g)
    v = v_ref[...]
    for h in range(H_A):
        sl = slice(h * HD_A, (h + 1) * HD_A)
        ko_ref[pl.ds(h, seq, stride=H_A), :] = k[:, sl]
        vo_ref[pl.ds(h, seq, stride=H_A), :] = v[:, sl]
    kb = k.astype(BF16)
    vb = v.astype(BF16)
    head = lax.broadcasted_iota(jnp.int32, (seq, W_A), 1) // HD_A
    zero = jnp.zeros((seq, W_A), BF16)
    for h in range(H_A):
        km_s[h * seq:(h + 1) * seq, :] = jnp.where(head == h, kb, zero)
        vm_s[h * seq:(h + 1) * seq, :] = jnp.where(head == h, vb, zero)
    s = _dot_nt(q, km_s[...])
    probs = []
    for h in range(H_A):
        s_h = s[:, h * seq:(h + 1) * seq]
        p = jnp.exp(s_h - jnp.max(_fold_lanes(s_h, jnp.maximum), axis=-1, keepdims=True))
        den = jnp.sum(_fold_lanes(p, jnp.add), axis=-1, keepdims=True)
        probs.append((p / den).astype(BF16))
    o_ref[...] = _dot(jnp.concatenate(probs, axis=1), vm_s[...])


def _ctx_attention(z, n_batch, seq, qn, kn):
    spec = lambda c: pl.BlockSpec((seq, W_A), lambda b: (b, c))
    wspec = pl.BlockSpec((1, W_A), lambda b: (0, 0))
    out = jax.ShapeDtypeStruct((n_batch * seq, W_A), F32)
    return pl.pallas_call(
        _ctx_attn_kernel, grid=(n_batch,),
        in_specs=[spec(0), spec(1), spec(2), wspec, wspec, pl.BlockSpec((W_A, W_A), lambda b: (0, 0))],
        out_specs=[pl.BlockSpec((seq, W_A), lambda b: (b, 0))] + [pl.BlockSpec((seq * H_A, HD_A), lambda b: (b, 0))] * 2,
        out_shape=[out] + [jax.ShapeDtypeStruct((n_batch * seq * H_A, HD_A), F32)] * 2,
        scratch_shapes=[pltpu.VMEM((H_A * seq, W_A), BF16), pltpu.VMEM((H_A * seq, W_A), BF16)],
        compiler_params=_cparams("arbitrary"), name="ctx_attention",
    )(z, z, z, qn, kn, _head_group_matrix())


NA_QROWS = 4
NA_KROWS = 12


def _na_slab_start(r0, rows):
    first_window = jnp.clip(r0 - WIN_R // 2, 0, rows - WIN_R)
    return jnp.minimum(first_window, rows - NA_KROWS)


def _na_bias_pairs(rpb):
    qc = np.arange(GRID_W)
    kc = np.arange(GRID_W)
    cstart = np.clip(qc - WIN_C // 2, 0, GRID_W - WIN_C)
    col_in = (kc[None, :] >= cstart[:, None]) & (kc[None, :] < cstart[:, None] + WIN_C)
    dc = np.clip(kc[None, :] - qc[:, None], 1 - WIN_C, WIN_C - 1) + WIN_C - 1
    sel_c = jnp.asarray(dc[:, :, None] == np.arange(2 * WIN_C - 1)[None, None, :], F32)
    tab = jnp.einsum("hab,qkb->haqk", rpb, sel_c, precision=HIGHEST)
    tab = jnp.where(jnp.asarray(col_in)[None, None], tab, NEG)
    following = jnp.concatenate([tab[:, 1:], jnp.full_like(tab[:, :1], NEG)], axis=1)
    return jnp.concatenate([tab, following], axis=-1)


def _na_kernel(q_ref, k_ref, v_ref, kc_ref, vc_ref, bias_ref, qn_ref, kn_ref, g_ref, o_ref,
               kn_s, v_s, kc_s, vc_s, *, rows):
    j = pl.program_id(1)
    scale = HD_A ** -0.5

    @pl.when(j == 0)
    def _():
        kn_s[...] = _heads_rms(k_ref[...], kn_ref[...], g_ref[...]).astype(BF16)
        v_s[...] = v_ref[...].astype(BF16)
        kc_s[...] = kc_ref[...].astype(BF16)
        vc_s[...] = vc_ref[...].astype(BF16)

    r0 = j * NA_QROWS
    kr0 = _na_slab_start(r0, rows)
    start = pl.multiple_of(kr0 * GRID_W, GRID_W)
    n_loc = NA_KROWS * GRID_W
    first_half = lax.broadcasted_iota(jnp.int32, (1, 2 * GRID_W), 1) < GRID_W
    plan = []
    for qr in range(NA_QROWS):
        r = r0 + qr
        rs = jnp.clip(r - WIN_R // 2, 0, rows - WIN_R)
        pairs = []
        for p in range(NA_KROWS // 2):
            ka = kr0 + 2 * p
            in0 = jnp.logical_and(ka >= rs, ka < rs + WIN_R).astype(jnp.int32)
            in1 = jnp.logical_and(ka + 1 >= rs, ka + 1 < rs + WIN_R).astype(jnp.int32)
            idx = jnp.clip(ka - r + WIN_R - 1, 0, 2 * WIN_R - 2)
            pairs.append((idx, jnp.where(first_half, in0, in1) != 0))
        plan.append(pairs)

    for h in range(H_A):
        sl = slice(h * HD_A, (h + 1) * HD_A)
        q = (_head_rms(q_ref[:, sl], qn_ref[:, sl]) * scale).astype(BF16)
        s_raw = _dot_nt(q, kn_s[pl.ds(start, n_loc), sl])
        s_ctx = _dot_nt(q, kc_s[:, sl])
        blocks = []
        for qr in range(NA_QROWS):
            qs = slice(qr * GRID_W, (qr + 1) * GRID_W)
            tiles = [s_raw[qs, p * 2 * GRID_W:(p + 1) * 2 * GRID_W] + jnp.where(keep, bias_ref[h, idx], NEG)
                     for p, (idx, keep) in enumerate(plan[qr])]
            blocks.append(jnp.concatenate(tiles, axis=1))
        s_loc = jnp.concatenate(blocks, axis=0)
        m = jnp.max(jnp.maximum(_fold_lanes(s_loc, jnp.maximum), _fold_lanes(s_ctx, jnp.maximum)),
                    axis=-1, keepdims=True)
        p_loc = jnp.exp(s_loc - m)
        p_ctx = jnp.exp(s_ctx - m)
        den = jnp.sum(_fold_lanes(p_loc, jnp.add) + _fold_lanes(p_ctx, jnp.add), axis=-1, keepdims=True)
        o = _dot(p_loc.astype(BF16), v_s[pl.ds(start, n_loc), sl]) + _dot(p_ctx.astype(BF16), vc_s[:, sl])
        o_ref[:, sl] = o / den


def _na_attention(z, row0, n_batch, seq, kc, vc, bias, qn, kn):
    rows = seq // GRID_W
    assert 2 * GRID_W == LANES and rows % NA_QROWS == 0 and NA_KROWS % 2 == 0 and rows >= NA_KROWS
    for r0 in range(0, rows, NA_QROWS):
        lo = int(np.clip(r0 - WIN_R // 2, 0, rows - WIN_R))
        hi = int(np.clip(r0 + NA_QROWS - 1 - WIN_R // 2, 0, rows - WIN_R)) + WIN_R
        assert hi <= min(lo, rows - NA_KROWS) + NA_KROWS
    past = kc.shape[1]
    nblk = rows // NA_QROWS
    tq = NA_QROWS * GRID_W
    blk0 = row0 // tq
    sblk0 = row0 // seq
    full = lambda c: pl.BlockSpec((seq, W_A), lambda b, j: (sblk0 + b, c))
    cspec = pl.BlockSpec((None, past, W_A), lambda b, j: (b, 0, 0))
    wspec = pl.BlockSpec((1, W_A), lambda b, j: (0, 0))
    return pl.pallas_call(
        functools.partial(_na_kernel, rows=rows), grid=(n_batch, nblk),
        in_specs=[pl.BlockSpec((tq, W_A), lambda b, j: (blk0 + b * nblk + j, 0)),
                  full(1), full(2), cspec, cspec,
                  pl.BlockSpec(bias.shape, lambda b, j: (0, 0, 0, 0)),
                  wspec, wspec, pl.BlockSpec((W_A, W_A), lambda b, j: (0, 0))],
        out_specs=pl.BlockSpec((tq, W_A), lambda b, j: (b * nblk + j, 0)),
        out_shape=jax.ShapeDtypeStruct((n_batch * seq, W_A), F32),
        scratch_shapes=[pltpu.VMEM((seq, W_A), BF16), pltpu.VMEM((seq, W_A), BF16),
                        pltpu.VMEM((past, W_A), BF16), pltpu.VMEM((past, W_A), BF16)],
        compiler_params=_cparams("arbitrary", "arbitrary"), name="na_attention",
    )(z, z, z, kc, vc, bias, qn, kn, _head_group_matrix())


def _tri_masks():
    li = lax.broadcasted_iota(jnp.int32, (CHUNK, CHUNK), 0)
    si = lax.broadcasted_iota(jnp.int32, (CHUNK, CHUNK), 1)
    return li >= si, li <= si


def _mlstm_kernel(*refs, nc, has_init):
    if has_init:
        (q_ref, k_ref, v_ref, og_ref, g_ref, nw_ref, c0_ref, n0_ref, m0_ref,
         o_ref, cf_ref, nf_ref, mf_ref, h_s, c_s, n_s, m_s) = refs
    else:
        (q_ref, k_ref, v_ref, og_ref, g_ref, nw_ref,
         o_ref, cf_ref, nf_ref, mf_ref, h_s, c_s, n_s, m_s) = refs
    nd = 2 * H_B
    if has_init:
        for sidx in range(nd):
            c_s[sidx] = c0_ref[sidx].T
        n_s[...] = n0_ref[...]
        m_s[...] = m0_ref[...]
    else:
        c_s[...] = jnp.zeros_like(c_s)
        n_s[...] = jnp.zeros_like(n_s)
        m_s[...] = jnp.zeros_like(m_s)

    causal, anti = _tri_masks()
    tri_f = jnp.where(causal, 1.0, 0.0).astype(BF16)
    tri_b = jnp.where(anti, 1.0, 0.0).astype(BF16)
    kscale = HD_B ** -0.5

    def chunk_step(c, carry):
        for d in range(2):
            cc = c if d == 0 else nc - 1 - c
            t0 = pl.multiple_of(cc * CHUNK, CHUNK)
            g = g_ref[pl.ds(t0, CHUNK), :]
            gt = g.T
            ls = _log_sigmoid(g)
            lst = _log_sigmoid(gt)
            tri_c, tri_r, mask_t = (tri_f, tri_b, anti) if d == 0 else (tri_b, tri_f, causal)
            b_cols = sum(_dot(tri_c, piece) for piece in _split3_bf16(ls))
            b_rows = sum(_dot(piece, tri_r) for piece in _split3_bf16(lst))
            last = CHUNK - 1 if d == 0 else 0
            for h in range(H_B):
                ci = (2 * d) * H_B + h
                cf = (2 * d + 1) * H_B + h
                hs = slice(h * HD_B, (h + 1) * HD_B)
                q = q_ref[pl.ds(t0, CHUNK), hs]
                k = k_ref[pl.ds(t0, CHUNK), hs] * kscale
                v = v_ref[pl.ds(t0, CHUNK), hs]
                qb, kb = q.astype(BF16), k.astype(BF16)
                qtb = q.T.astype(BF16)
                vt = v.T
                b_row = b_rows[cf:cf + 1, :]
                i_row = gt[ci:ci + 1, :]
                ib_col = g[:, ci:ci + 1] - b_cols[:, cf:cf + 1]
                sidx = d * H_B + h
                cst = c_s[sidx]
                nst = n_s[sidx:sidx + 1, :]
                mst = m_s[sidx:sidx + 1, 0:1]
                dmat_t = jnp.where(mask_t, b_row + ib_col, -jnp.inf)
                inter = b_row + mst
                mt = jnp.maximum(inter, jnp.max(dmat_t, axis=0, keepdims=True))
                w_t = jnp.exp(dmat_t - mt) * _dot_nt(kb, qb)
                a = jnp.exp(inter - mt)
                num_t = _dot(vt.astype(BF16), w_t.astype(BF16)) + _dot(cst.astype(BF16), qtb) * a
                nq = _dot(jnp.broadcast_to(nst, (SUB, HD_B)).astype(BF16), qtb)[0:1, :]
                den = jnp.sum(w_t, axis=0, keepdims=True) + a * nq
                h_t = num_t / jnp.maximum(jnp.abs(den), jnp.exp(-mt))
                h_s[d, pl.ds(t0, CHUNK), hs] = h_t.T
                bl = b_row[:, last:last + 1]
                dl = bl - b_row + i_row
                m_new = jnp.maximum(bl + mst, jnp.max(dl, axis=-1, keepdims=True))
                wl = jnp.exp(dl - m_new)
                dec = jnp.exp(bl + mst - m_new)
                c_s[sidx] = dec * cst + _dot((vt * wl).astype(BF16), kb)
                wl8 = jnp.broadcast_to(wl, (SUB, CHUNK)).astype(BF16)
                n_s[sidx:sidx + 1, :] = dec * nst + _dot(wl8, kb)[0:1, :]
                m_s[sidx:sidx + 1, :] = jnp.broadcast_to(m_new, (1, LANES))
        return carry

    lax.fori_loop(0, nc, chunk_step, 0)

    for h in range(H_B):
        hs = slice(h * HD_B, (h + 1) * HD_B)
        hsum = h_s[0, :, hs] + h_s[1, :, hs]
        o_ref[:, hs] = _head_rms(hsum, nw_ref[:, hs]) * jax.nn.sigmoid(og_ref[:, hs])
    for sidx in range(nd):
        cf_ref[sidx] = c_s[sidx].T
    nf_ref[...] = n_s[...]
    mf_ref[...] = m_s[...]


def _mlstm(z, g, row0, n_batch, seq, norm_w, init=None):
    sblk0 = row0 // seq
    nd = 2 * H_B
    spec = lambda c: pl.BlockSpec((seq, W_B), lambda b: (sblk0 + b, c))
    in_specs = [spec(3), spec(4), spec(5), spec(6),
                pl.BlockSpec((seq, LANES), lambda b: (sblk0 + b, 0)),
                pl.BlockSpec((1, W_B), lambda b: (0, 0))]
    args = [z, z, z, z, g, norm_w]
    st_specs = [pl.BlockSpec((None, nd, HD_B, HD_B), lambda b: (b, 0, 0, 0)),
                pl.BlockSpec((None, nd, HD_B), lambda b: (b, 0, 0)),
                pl.BlockSpec((None, nd, LANES), lambda b: (b, 0, 0))]
    if init is not None:
        in_specs += st_specs
        args += list(init)
    return pl.pallas_call(
        functools.partial(_mlstm_kernel, nc=seq // CHUNK, has_init=init is not None), grid=(n_batch,),
        in_specs=in_specs,
        out_specs=[pl.BlockSpec((seq, W_B), lambda b: (b, 0))] + st_specs,
        out_shape=[jax.ShapeDtypeStruct((n_batch * seq, W_B), F32),
                   jax.ShapeDtypeStruct((n_batch, nd, HD_B, HD_B), F32),
                   jax.ShapeDtypeStruct((n_batch, nd, HD_B), F32),
                   jax.ShapeDtypeStruct((n_batch, nd, LANES), F32)],
        scratch_shapes=[pltpu.VMEM((2, seq, W_B), F32), pltpu.VMEM((nd, HD_B, HD_B), F32),
                        pltpu.VMEM((nd, HD_B), F32), pltpu.VMEM((nd, LANES), F32)],
        compiler_params=_cparams("arbitrary"), name="mlstm",
    )(*args)


def _rope_tables(seq):
    half = HD_C // 2
    quarter = half // 2
    t = np.arange(seq)
    inv = ROPE_BASE ** (-np.arange(0, half, 2, dtype=np.float64) / half)
    ang_r = (t // GRID_W)[:, None] * inv[None, :]
    ang_c = (t % GRID_W)[:, None] * inv[None, :]
    cos_t = np.concatenate([np.cos(ang_r), np.cos(ang_r), np.cos(ang_c), np.cos(ang_c)], -1)
    sin_t = np.concatenate([-np.sin(ang_r), np.sin(ang_r), -np.sin(ang_c), np.sin(ang_c)], -1)
    assert cos_t.shape == (seq, 4 * quarter)
    return jnp.asarray(cos_t, F32), jnp.asarray(sin_t, F32)


def _rope(x, cos_t, sin_t):
    quarter = HD_C // 4
    lane = lax.broadcasted_iota(jnp.int32, x.shape, 1)
    first = (lane % (2 * quarter)) < quarter
    swapped = jnp.where(first, pltpu.roll(x, HD_C - quarter, 1), pltpu.roll(x, quarter, 1))
    return x * cos_t + swapped * sin_t


def _ret_kernel(*refs, nc, has_init, use_rope):
    refs = list(refs)
    q_ref, k_ref, v_ref, gg_ref, dl_ref, nw_ref = refs[:6]
    pos = 6
    if use_rope:
        cos_ref, sin_ref = refs[pos:pos + 2]
        pos += 2
    if has_init:
        s0_ref = refs[pos]
        pos += 1
    o_ref, sf_ref, h_s, s_s, dk_s = refs[pos:pos + 5]
    if has_init:
        s_s[...] = s0_ref[...]
    else:
        s_s[...] = jnp.zeros_like(s_s)

    causal, anti = _tri_masks()
    li = lax.broadcasted_iota(jnp.int32, (CHUNK, CHUNK), 0).astype(F32)
    si = lax.broadcasted_iota(jnp.int32, (CHUNK, CHUNK), 1).astype(F32)
    lg_all = _log_sigmoid(dl_ref[...])
    kscale = HD_C ** -0.5
    for d in range(2):
        for h in range(H_C):
            sidx = d * H_C + h
            lg = lg_all[sidx:sidx + 1, :]
            if d == 0:
                dk_s[sidx, 0] = jnp.exp(jnp.where(causal, (li - si) * lg, -jnp.inf))
                dk_s[sidx, 1] = jnp.exp((li + 1.0) * lg)
                dk_s[sidx, 2] = jnp.exp((CHUNK - 1.0 - li) * lg)
            else:
                dk_s[sidx, 0] = jnp.exp(jnp.where(anti, (si - li) * lg, -jnp.inf))
                dk_s[sidx, 1] = jnp.exp((CHUNK - li) * lg)
                dk_s[sidx, 2] = jnp.exp(li * lg)

    def chunk_step(c, carry):
        for d in range(2):
            cc = c if d == 0 else nc - 1 - c
            t0 = pl.multiple_of(cc * CHUNK, CHUNK)
            for h in range(H_C):
                sidx = d * H_C + h
                hs = slice(h * HD_C, (h + 1) * HD_C)
                lg = lg_all[sidx:sidx + 1, :]
                q = q_ref[pl.ds(t0, CHUNK), hs]
                k = k_ref[pl.ds(t0, CHUNK), hs] * kscale
                v = v_ref[pl.ds(t0, CHUNK), hs]
                if use_rope:
                    cos_t = cos_ref[pl.ds(t0, CHUNK), :]
                    sin_t = sin_ref[pl.ds(t0, CHUNK), :]
                    q = _rope(q, cos_t, sin_t)
                    k = _rope(k, cos_t, sin_t)
                decay, q_dec, k_dec = dk_s[sidx, 0], dk_s[sidx, 1], dk_s[sidx, 2]
                c_dec = jnp.exp(CHUNK * lg)
                st = s_s[sidx]
                qb, kb, vb = q.astype(BF16), k.astype(BF16), v.astype(BF16)
                att = _dot_nt(qb, kb) * decay
                o = _dot(att.astype(BF16), vb) + _dot(qb, st.astype(BF16)) * q_dec
                h_s[d, pl.ds(t0, CHUNK), hs] = o
                kd = (k * k_dec).T.astype(BF16)
                s_s[sidx] = c_dec * st + _dot(kd, vb)
        return carry

    lax.fori_loop(0, nc, chunk_step, 0)

    for h in range(H_C):
        hs = slice(h * HD_C, (h + 1) * HD_C)
        osum = h_s[0, :, hs] + h_s[1, :, hs]
        gg = gg_ref[:, hs]
        o_ref[:, hs] = _head_rms(osum, nw_ref[:, hs]) * (gg * jax.nn.sigmoid(gg))
    sf_ref[...] = s_s[...]


def _retention(z, row0, n_batch, seq, decay_rep, norm_w, rope=None, init=None):
    sblk0 = row0 // seq
    nd = 2 * H_C
    w_c = H_C * HD_C
    spec = lambda c: pl.BlockSpec((seq, w_c), lambda b: (sblk0 + b, c))
    in_specs = [spec(0), spec(1), spec(2), spec(3),
                pl.BlockSpec((nd, LANES), lambda b: (0, 0)),
                pl.BlockSpec((1, w_c), lambda b: (0, 0))]
    args = [z, z, z, z, decay_rep, norm_w]
    if rope is not None:
        in_specs += [pl.BlockSpec((seq, HD_C), lambda b: (0, 0))] * 2
        args += list(rope)
    st_spec = pl.BlockSpec((None, nd, HD_C, HD_C), lambda b: (b, 0, 0, 0))
    if init is not None:
        in_specs.append(st_spec)
        args.append(init)
    return pl.pallas_call(
        functools.partial(_ret_kernel, nc=seq // CHUNK, has_init=init is not None, use_rope=rope is not None),
        grid=(n_batch,), in_specs=in_specs,
        out_specs=[pl.BlockSpec((seq, w_c), lambda b: (b, 0)), st_spec],
        out_shape=[jax.ShapeDtypeStruct((n_batch * seq, w_c), F32),
                   jax.ShapeDtypeStruct((n_batch, nd, HD_C, HD_C), F32)],
        scratch_shapes=[pltpu.VMEM((2, seq, w_c), F32), pltpu.VMEM((nd, HD_C, HD_C), F32),
                        pltpu.VMEM((nd, 3, CHUNK, CHUNK), F32)],
        compiler_params=_cparams("arbitrary"), name="retention",
    )(*args)


def _dft_tables(n):
    idx = (np.arange(n)[:, None] * np.arange(n)[None, :]) % n
    ang = 2.0 * np.pi * idx / n
    return np.cos(ang) / np.sqrt(n), np.sin(ang) / np.sqrt(n)


def _fnet_kernel(x_ref, cw_ref, sw_ref, cs_ref, ss_ref, o_ref):
    for g in range(N_FG):
        gs = slice(g * FG_W, (g + 1) * FG_W)
        x = x_ref[:, gs].astype(BF16)
        xc = _dot(x, cw_ref[...]).astype(BF16)
        xs = _dot(x, sw_ref[...]).astype(BF16)
        o_ref[:, gs] = _dot(cs_ref[...], xc) - _dot(ss_ref[...], xs)


def _fnet(z, row0, n_batch, seq):
    sblk0 = row0 // seq
    w_d = N_FG * FG_W
    cw, sw = _dft_tables(FG_W)
    cs, ss = _dft_tables(seq)
    tabs = [jnp.asarray(a, F32).astype(BF16) for a in (cw, sw, cs, ss)]
    wspec = pl.BlockSpec((FG_W, FG_W), lambda b: (0, 0))
    sspec = pl.BlockSpec((seq, seq), lambda b: (0, 0))
    return pl.pallas_call(
        _fnet_kernel, grid=(n_batch,),
        in_specs=[pl.BlockSpec((seq, w_d), lambda b: (sblk0 + b, 4)), wspec, wspec, sspec, sspec],
        out_specs=pl.BlockSpec((seq, w_d), lambda b: (b, 0)),
        out_shape=jax.ShapeDtypeStruct((n_batch * seq, w_d), F32),
        compiler_params=_cparams("arbitrary"), name="fnet",
    )(z, *tabs)


SUB = 8
assert D_MODEL == SUB * LANES


def _store_token_major(ref, lead, val):
    n = val.shape[0]
    for c in range(SUB):
        ref[lead + (pl.ds(c, n, stride=SUB), slice(None))] = val[:, c * LANES:(c + 1) * LANES]


def _load_token_major(ref, lead, n):
    return jnp.concatenate([ref[lead + (pl.ds(c, n, stride=SUB), slice(None))] for c in range(SUB)], axis=1)


def _outproj_router_kernel(aa_ref, ab_ref, ba_ref, bb_ref, xa_ref, xb_ref, mod_ref, wa_ref, wb_ref,
                           wrh_ref, wrl_ref, br_ref, x1_ref, h2_ref, ti_ref, tw_ref, rank_ref, cnt_ref, cnt_s):
    i = pl.program_id(0)
    a = _rows_read(i, aa_ref, ab_ref)
    b = _rows_read(i, ba_ref, bb_ref)
    y = _dot(a.astype(BF16), wa_ref[...]) + _dot(b.astype(BF16), wb_ref[...])
    x1 = _rows_read(i, xa_ref, xb_ref) + mod_ref[2:3, :] * y
    x1_ref[...] = x1
    h2 = _rms_mod(x1, mod_ref[3:4, :], mod_ref[4:5, :])
    _store_token_major(h2_ref, (), h2)
    logits = _dot_x3(h2, wrh_ref[...], wrl_ref[...]) + br_ref[...]
    lane = lax.broadcasted_iota(jnp.int32, logits.shape, 1)
    lane_f = lane.astype(F32)
    cur = logits
    vals, picks = [], []
    ti = jnp.zeros(logits.shape, jnp.int32)
    for kk in range(TOP_K):
        mx = jnp.max(cur, axis=-1, keepdims=True)
        idx = jnp.min(jnp.where(cur == mx, lane_f, float(LANES)), axis=-1, keepdims=True)
        ti = jnp.where(lane == kk, idx.astype(jnp.int32), ti)
        pick = lane_f == idx
        cur = jnp.where(pick, -jnp.inf, cur)
        vals.append(mx)
        picks.append(pick)
    es = [jnp.exp(v - vals[0]) for v in vals]
    tot = es[0] + es[1] + es[2] + es[3]
    tw = jnp.zeros(logits.shape, F32)
    for kk in range(TOP_K):
        tw = jnp.where(lane == kk, es[kk] / tot, tw)
    ti_ref[...] = ti
    tw_ref[...] = tw

    @pl.when(i == 0)
    def _():
        cnt_s[...] = jnp.zeros_like(cnt_s)

    onehot = jnp.zeros(logits.shape, F32)
    for pick in picks:
        onehot = onehot + jnp.where(pick, 1.0, 0.0)
    n = logits.shape[0]
    earlier = (lax.broadcasted_iota(jnp.int32, (n, n), 1) < lax.broadcasted_iota(jnp.int32, (n, n), 0))
    before = cnt_s[...] + _dot(jnp.where(earlier, 1.0, 0.0).astype(BF16), onehot.astype(BF16))
    rank = jnp.zeros(logits.shape, jnp.int32)
    for kk, pick in enumerate(picks):
        r_k = jnp.sum(jnp.where(pick, before, 0.0), axis=-1, keepdims=True)
        rank = jnp.where(lane == kk, r_k.astype(jnp.int32), rank)
    rank_ref[...] = rank
    cnt_s[...] = cnt_s[...] + jnp.sum(onehot, axis=0, keepdims=True)
    cnt_ref[...] = cnt_s[...]


def _outproj_router(a, b, x, mod, w_out, wr, br):
    t = N_TILES * TM
    d = x.a.shape[1]
    wid = a.a.shape[1]
    row = lambda w: pl.BlockSpec((TM, w), lambda i: (i, 0))
    const = lambda r, c: pl.BlockSpec((r, c), lambda i: (0, 0))
    wr_hi, wr_lo = _split_bf16(wr)
    return pl.pallas_call(
        _outproj_router_kernel, grid=(N_TILES,),
        in_specs=a.specs() + b.specs() + x.specs() + [
                  pl.BlockSpec((None, 6, d), lambda i: (_cond_row(i), 0, 0)),
                  pl.BlockSpec((wid, d), lambda i: (0, 0)), pl.BlockSpec((wid, d), lambda i: (1, 0)),
                  const(d, LANES), const(d, LANES), const(1, LANES)],
        out_specs=[row(d), pl.BlockSpec((TM * SUB, LANES), lambda i: (i, 0)), row(LANES), row(LANES), row(LANES),
                   const(1, LANES)],
        out_shape=[jax.ShapeDtypeStruct((t, d), F32), jax.ShapeDtypeStruct((t * SUB, LANES), F32),
                   jax.ShapeDtypeStruct((t, LANES), jnp.int32), jax.ShapeDtypeStruct((t, LANES), F32),
                   jax.ShapeDtypeStruct((t, LANES), jnp.int32), jax.ShapeDtypeStruct((1, LANES), F32)],
        scratch_shapes=[pltpu.VMEM((1, LANES), F32)],
        compiler_params=_cparams("arbitrary"), name="outproj_router",
    )(a.a, a.b, b.a, b.b, x.a, x.b, mod, w_out, w_out, wr_hi, wr_lo, br)


N_TILES_MAX = N_TILES * TM * TOP_K // TM_E + N_EXP
PLAN_LANES = 2 * LANES
assert N_TILES_MAX <= PLAN_LANES and N_EXP <= LANES


def _plan_kernel(cnt_ref, ti_ref, rank_ref, dest_ref, meta_ref, start_s):
    i = pl.program_id(0)

    @pl.when(i == 0)
    def _():
        cnt = cnt_ref[...]
        tiles = jnp.floor((cnt + float(TM_E - 1)) * (1.0 / TM_E))
        sub = lax.broadcasted_iota(jnp.int32, (LANES, LANES), 0)
        lane = lax.broadcasted_iota(jnp.int32, (LANES, LANES), 1)
        upto = jnp.where(sub <= lane, 1.0, 0.0).astype(BF16)
        tile_end = _dot(jnp.broadcast_to(tiles, (SUB, LANES)).astype(BF16), upto)[0:1, :]
        tile_start = tile_end - tiles
        start_s[...] = tile_start * float(TM_E * SUB)
        n_tiles = jnp.max(tile_end, axis=-1, keepdims=True)
        used = tiles > 0.0

        def column(row):
            return jnp.sum(jnp.where(sub == lane, jnp.broadcast_to(row, (LANES, LANES)), 0.0), axis=-1, keepdims=True)

        end_c, start_c, tiles_c = column(tile_end), column(tile_start), column(tiles)
        used_b = jnp.broadcast_to(jnp.where(used, 1.0, 0.0), (LANES, LANES))
        pos_c = jnp.sum(jnp.where(lane <= sub, used_b, 0.0), axis=-1, keepdims=True)
        par_c = (pos_c - 1.0) - 2.0 * jnp.floor((pos_c - 1.0) * 0.5)
        nxt_c = jnp.min(jnp.where(jnp.logical_and(lane > sub, used_b > 0.0), lane.astype(F32), float(LANES)),
                        axis=-1, keepdims=True)
        nxt_c = jnp.where(nxt_c < float(LANES), nxt_c, -1.0)

        tid = lax.broadcasted_iota(jnp.int32, (LANES, PLAN_LANES), 1).astype(F32)
        exp_id = lax.broadcasted_iota(jnp.int32, (LANES, PLAN_LANES), 0)
        tid_used = jnp.minimum(tid, n_tiles - 1.0)
        te = jnp.sum(jnp.where(jnp.logical_and(exp_id < N_EXP, end_c <= tid_used), 1.0, 0.0), axis=0, keepdims=True)
        first = jnp.sum(jnp.where(jnp.logical_and(tiles_c > 0.0, start_c == tid), 1.0, 0.0), axis=0, keepdims=True)
        mine = te == exp_id.astype(F32)
        nxt = jnp.sum(jnp.where(mine, nxt_c, 0.0), axis=0, keepdims=True)
        par = jnp.sum(jnp.where(mine, par_c, 0.0), axis=0, keepdims=True)
        last = jnp.where(used, tile_end - 1.0, 0.0)
        last = jnp.concatenate([last, jnp.zeros((1, PLAN_LANES - LANES), F32)], axis=1)

        row_id = lax.broadcasted_iota(jnp.int32, (SUB, PLAN_LANES), 0)
        meta = jnp.zeros((SUB, PLAN_LANES), F32)
        for r, val in enumerate((te, first, nxt, par, last, jnp.broadcast_to(n_tiles, (1, PLAN_LANES)))):
            meta = jnp.where(row_id == r, jnp.broadcast_to(val, (SUB, PLAN_LANES)), meta)
        meta_ref[...] = meta.astype(jnp.int32)

    ti = ti_ref[...]
    rank = rank_ref[...]
    lane = lax.broadcasted_iota(jnp.int32, ti.shape, 1)
    dest = jnp.zeros(ti.shape, jnp.int32)
    for kk in range(TOP_K):
        base = jnp.sum(jnp.where(lane == ti[:, kk:kk + 1], start_s[...], 0.0), axis=-1, keepdims=True)
        dest = jnp.where(lane == kk, base.astype(jnp.int32) + rank[:, kk:kk + 1] * SUB, dest)
    dest_ref[...] = dest.T[0:SUB, :]


def _route_plan(cnt, ti, rank):
    t = ti.shape[0]
    row = pl.BlockSpec((TM, LANES), lambda i: (i, 0))
    dest, meta = pl.pallas_call(
        _plan_kernel, grid=(t // TM,),
        in_specs=[pl.BlockSpec((1, LANES), lambda i: (0, 0)), row, row],
        out_specs=[pl.BlockSpec((SUB, TM), lambda i: (0, i)), pl.BlockSpec((SUB, PLAN_LANES), lambda i: (0, 0))],
        out_shape=[jax.ShapeDtypeStruct((SUB, t), jnp.int32), jax.ShapeDtypeStruct((SUB, PLAN_LANES), jnp.int32)],
        scratch_shapes=[pltpu.VMEM((1, LANES), F32)],
        compiler_params=_cparams("arbitrary"), name="moe_plan",
    )(cnt, ti, rank)
    plan = dict(tile_expert=meta[0, :N_TILES_MAX], first=meta[1, :N_TILES_MAX], tile_next=meta[2, :N_TILES_MAX],
                tile_parity=meta[3, :N_TILES_MAX], last_tile=meta[4, :N_EXP], n_tiles=meta[5, :1])
    return dest, plan


DISPATCH_BLK = 1024


def _dispatch_kernel(lt_ref, nt_ref, *rest):
    dest_refs, (h_ref, xs_ref, h_s, zero_s, sem, hsem) = rest[:TOP_K], rest[TOP_K:]
    i = pl.program_id(0)
    tile_rows = TM_E * SUB
    n_tiles_max = xs_ref.shape[0] // tile_rows
    n_tok = h_s.shape[0] // SUB

    blk_rows = DISPATCH_BLK * SUB

    def stage(blk):
        r0 = pl.multiple_of(blk * blk_rows, blk_rows)
        return pltpu.make_async_copy(h_ref.at[pl.ds(r0, blk_rows), :], h_s.at[pl.ds(r0, blk_rows), :], hsem.at[blk])

    @pl.when(i == 0)
    def _():
        for blk in range(n_tok // DISPATCH_BLK):
            stage(blk).start()
        zero_s[...] = jnp.zeros_like(zero_s)

        def zero_tile(tile):
            r0 = pl.multiple_of(tile * tile_rows, tile_rows)
            return pltpu.make_async_copy(zero_s, xs_ref.at[pl.ds(r0, tile_rows), :], sem)

        def start_unused(j, carry):
            zero_tile(j).start()
            return carry

        def wait_unused(j, carry):
            zero_tile(j).wait()
            return carry

        for e in range(N_EXP):
            zero_tile(lt_ref[e]).start()
        lax.fori_loop(nt_ref[0], n_tiles_max, start_unused, 0)
        for e in range(N_EXP):
            zero_tile(lt_ref[e]).wait()
        lax.fori_loop(nt_ref[0], n_tiles_max, wait_unused, 0)

    stage(i).wait()
    base = i * DISPATCH_BLK

    def issue(t, carry):
        src = pl.multiple_of((base + t) * SUB, SUB)
        for kk in range(TOP_K):
            row = pl.multiple_of(dest_refs[kk][0, t], SUB)
            pltpu.make_async_copy(h_s.at[pl.ds(src, SUB), :], xs_ref.at[pl.ds(row, SUB), :], sem).start(priority=kk % 2)
        return carry

    lax.fori_loop(0, DISPATCH_BLK, issue, 0, unroll=2)

    @pl.when(i == pl.num_programs(0) - 1)
    def _():
        for kk in range(TOP_K):
            pltpu.make_async_copy(h_s, xs_ref.at[pl.ds(0, n_tok * SUB), :], sem).wait()


def _dispatch(h2, dest, plan, n_rows):
    t = h2.shape[0] // SUB
    nblk = t // DISPATCH_BLK
    return pl.pallas_call(
        _dispatch_kernel,
        grid_spec=pltpu.PrefetchScalarGridSpec(
            num_scalar_prefetch=2, grid=(nblk,),
            in_specs=[pl.BlockSpec((None, 1, DISPATCH_BLK), lambda i, lt, nt: (i, 0, 0), memory_space=pltpu.SMEM)] * TOP_K
                     + [pl.BlockSpec(memory_space=pl.ANY)],
            out_specs=pl.BlockSpec(memory_space=pl.ANY),
            scratch_shapes=[pltpu.VMEM((t * SUB, LANES), F32), pltpu.VMEM((TM_E * SUB, LANES), F32),
                            pltpu.SemaphoreType.DMA(()), pltpu.SemaphoreType.DMA((nblk,))]),
        out_shape=jax.ShapeDtypeStruct((n_rows * SUB, LANES), F32),
        compiler_params=_cparams("arbitrary"), name="moe_dispatch",
    )(plan["last_tile"], plan["n_tiles"], *[dest[kk].reshape(nblk, 1, DISPATCH_BLK) for kk in range(TOP_K)], h2)


def _expert_kernel(te_ref, tf_ref, nt_ref, nx_ref, par_ref, xs_ref, w1_ref, b1_ref, w2_ref, b2_ref, ys_ref,
                   w1f, w2f, w1_s, w2_s, wsem, *, layer):
    i = pl.program_id(0)

    def fetch(expert, slot):
        return (pltpu.make_async_copy(w1_ref.at[layer, expert], w1f.at[slot], wsem.at[slot]),
                pltpu.make_async_copy(w2_ref.at[layer, expert], w2f.at[slot], wsem.at[slot]))

    @pl.when(i == 0)
    def _():
        for cp in fetch(te_ref[0], 0):
            cp.start(priority=1)

    @pl.when(i < nt_ref[0])
    def _():
        @pl.when(tf_ref[i] == 1)
        def _():
            slot = par_ref[i]
            for cp in fetch(te_ref[i], slot):
                cp.wait()
            w1_s[...] = w1f[slot].astype(BF16)
            w2_s[...] = w2f[slot].astype(BF16)

            @pl.when(nx_ref[i] >= 0)
            def _():
                for cp in fetch(nx_ref[i], 1 - slot):
                    cp.start(priority=1)

        x = _load_token_major(xs_ref, (), TM_E)
        u = _dot(x.astype(BF16), w1_s[...]) + b1_ref[...]
        g = jnp.minimum(u[:, :D_FF], SWIGLU_LIMIT)
        up = jnp.clip(u[:, D_FF:], -SWIGLU_LIMIT, SWIGLU_LIMIT)
        act = (up + 1.0) * g * jax.nn.sigmoid(SWIGLU_ALPHA * g)
        _store_token_major(ys_ref, (), _dot(act.astype(BF16), w2_s[...]) + b2_ref[...])

    @pl.when(i >= nt_ref[0])
    def _():
        ys_ref[...] = jnp.zeros_like(ys_ref)


def _experts(xs, plan, layer, w1, b1, w2, b2):
    d = D_MODEL
    nt = xs.shape[0] // (TM_E * SUB)
    tile = lambda i, te, tf, ntl, nx, par: (jnp.minimum(i, ntl[0] - 1), 0)
    otile = lambda i, te, tf, ntl, nx, par: (i, 0)
    bmap = lambda i, te, tf, ntl, nx, par: (layer, te[i], 0, 0)
    return pl.pallas_call(
        functools.partial(_expert_kernel, layer=layer),
        grid_spec=pltpu.PrefetchScalarGridSpec(
            num_scalar_prefetch=5, grid=(nt,),
            in_specs=[pl.BlockSpec((TM_E * SUB, LANES), tile),
                      pl.BlockSpec(memory_space=pl.ANY),
                      pl.BlockSpec((None, None, 1, 2 * D_FF), bmap),
                      pl.BlockSpec(memory_space=pl.ANY),
                      pl.BlockSpec((None, None, 1, d), bmap)],
            out_specs=pl.BlockSpec((TM_E * SUB, LANES), otile),
            scratch_shapes=[pltpu.VMEM((2, d, 2 * D_FF), F32), pltpu.VMEM((2, D_FF, d), F32),
                            pltpu.VMEM((d, 2 * D_FF), BF16), pltpu.VMEM((D_FF, d), BF16),
                            pltpu.SemaphoreType.DMA((2,))]),
        out_shape=jax.ShapeDtypeStruct(xs.shape, F32),
        compiler_params=_cparams("arbitrary"), name="moe_experts",
    )(plan["tile_expert"], plan["first"], plan["n_tiles"], plan["tile_next"], plan["tile_parity"],
      xs, w1, b1, w2, b2)


def _combine_kernel(*refs):
    dest_refs = refs[:TOP_K]
    x1_ref, tw_ref, mod_ref, ys_ref, *o_refs, buf, sem = refs[TOP_K:]
    i = pl.program_id(0)
    j = i - 1
    n = pl.num_programs(0) - 1
    tile_rows = TM * SUB

    for s in range(2):
        @pl.when(jnp.logical_and(i < n, i % 2 == s))
        def _():
            def issue(t, carry):
                dst = pl.multiple_of(t * SUB, SUB)
                for kk in range(TOP_K):
                    row = pl.multiple_of(dest_refs[kk][0, t], SUB)
                    pltpu.make_async_copy(ys_ref.at[pl.ds(row, SUB), :], buf.at[s, kk, pl.ds(dst, SUB), :],
                                          sem.at[s]).start(priority=kk % 2)
                return carry

            lax.fori_loop(0, TM, issue, 0, unroll=2)

    @pl.when(j >= 0)
    def _():
        slot = j % 2
        for kk in range(TOP_K):
            pltpu.make_async_copy(ys_ref.at[pl.ds(0, tile_rows), :], buf.at[slot, kk], sem.at[slot]).wait()
        tw = tw_ref[...]
        y = tw[:, 0:1] * _load_token_major(buf, (slot, 0), TM)
        for kk in range(1, TOP_K):
            y = y + tw[:, kk:kk + 1] * _load_token_major(buf, (slot, kk), TM)
        out = x1_ref[...] + mod_ref[5:6, :] * y
        if len(o_refs) == 1:
            o_refs[0][...] = out
        else:
            @pl.when(j < N_CTX_TILES)
            def _():
                o_refs[0][...] = out

            @pl.when(j >= N_CTX_TILES)
            def _():
                o_refs[1][...] = out


def _combine(ys, dest, x1, tw, mod, split):
    t, d = x1.shape
    nblk = t // TM
    prev = lambda i: jnp.maximum(i - 1, 0)
    row = lambda w: pl.BlockSpec((TM, w), lambda i: (prev(i), 0))
    return pl.pallas_call(
        _combine_kernel, grid=(nblk + 1,),
        in_specs=[pl.BlockSpec((None, 1, TM), lambda i: (jnp.minimum(i, nblk - 1), 0, 0), memory_space=pltpu.SMEM)] * TOP_K + [
                  row(d), row(LANES),
                  pl.BlockSpec((None, 6, d), lambda i: (_cond_row(prev(i)), 0, 0)),
                  pl.BlockSpec(memory_space=pl.ANY)],
        out_specs=_Rows(x1, x1, 0).specs(lag=1) if split else row(d),
        out_shape=([jax.ShapeDtypeStruct((N_CTX_TILES * TM, d), F32),
                    jax.ShapeDtypeStruct(((N_TILES - N_CTX_TILES) * TM, d), F32)] if split
                   else jax.ShapeDtypeStruct((t, d), F32)),
        scratch_shapes=[pltpu.VMEM((2, TOP_K, TM * SUB, LANES), F32), pltpu.SemaphoreType.DMA((2,))],
        compiler_params=_cparams("arbitrary"), name="moe_combine",
    )(*[dest[kk].reshape(nblk, 1, TM) for kk in range(TOP_K)], x1, tw, mod, ys)


def _moe(x1, h2, ti, tw, rank, cnt, mod, layer, w1, b1, w2, b2, split):
    depth = w1.shape[0]
    assert x1.shape[0] == N_TILES * TM
    dest, plan = _route_plan(cnt, ti, rank)
    xs = _dispatch(h2, dest, plan, N_TILES_MAX * TM_E)
    ys = _experts(xs, plan, layer, w1, b1.reshape(depth, N_EXP, 1, -1), w2, b2.reshape(depth, N_EXP, 1, -1))
    return _combine(ys, dest, x1, tw, mod, split)


def _pad_lanes(a, value=0.0):
    return jnp.pad(a, ((0, 0), (0, LANES - a.shape[1])), constant_values=value)


def kernel(x_prompt, x_sample, cache_na_k, cache_na_v, state_mlstm_C, state_mlstm_n, state_mlstm_m, state_ret_S, c, c_ctx, w_mod, b_mod, w_in_even, mlstm_gate_b, na_q_norm, na_k_norm, na_rpb, mlstm_norm, w_out_even, w_in_odd, ret_decay, ret_norm, w_out_odd, w_router, b_router, w_moe_in, b_moe_in, w_moe_out, b_moe_out):
    nb_c, s_c, d = x_prompt.shape
    nb_l, s_l, _ = x_sample.shape
    t_c = nb_c * s_c
    t_l = nb_l * s_l
    assert t_c == 4 * SEG and s_l == SEG and d == D_MODEL
    depth = w_mod.shape[0]
    dt = x_prompt.dtype

    x = _Rows(x_prompt.reshape(t_c, d), x_sample.reshape(t_l, d), 0)
    cond = jnp.concatenate([c_ctx[None, :], c, jnp.zeros((N_COND - 1 - nb_l, d), F32)], axis=0)
    mod = _modulation(cond, w_mod, b_mod).reshape(depth, N_COND, 6, d)

    outs = {}
    for l in range(depth):
        e = l // 2
        mod_l = mod[l]
        if l % 2 == 0:
            w_in = w_in_even[e]
            n_main = 3 * W_A + 4 * W_B
            wg = _pad_lanes(w_in[:, n_main:])
            bg = _pad_lanes(mlstm_gate_b[e].reshape(1, 4 * H_B))
            z, g = _inproj(x, mod_l, w_in.astype(BF16), n_main, wg, bg)
            qn = jnp.tile(na_q_norm[e].reshape(1, HD_A), (1, H_A))
            kn = jnp.tile(na_k_norm[e].reshape(1, HD_A), (1, H_A))
            oa_c, ka_c, va_c = _ctx_attention(z, nb_c, s_c, qn, kn)
            past = cache_na_k.shape[2]
            oa_l = _na_attention(z, t_c, nb_l, s_l,
                                 cache_na_k[:, e].reshape(nb_l, past, W_A), cache_na_v[:, e].reshape(nb_l, past, W_A),
                                 _na_bias_pairs(na_rpb[e]), qn, kn)
            nw = mlstm_norm[e].reshape(1, W_B)
            hm_c, c_fin, n_fin, m_fin = _mlstm(z, g, 0, nb_c, s_c, nw)
            init = (state_mlstm_C[:, e].reshape(nb_l, 2 * H_B, HD_B, HD_B),
                    state_mlstm_n[:, e].reshape(nb_l, 2 * H_B, HD_B),
                    jnp.broadcast_to(state_mlstm_m[:, e].reshape(nb_l, 2 * H_B, 1), (nb_l, 2 * H_B, LANES)))
            hm_l = _mlstm(z, g, t_c, nb_l, s_l, nw, init)[0]
            a = _Rows(oa_c, oa_l, 0)
            b = _Rows(hm_c, hm_l, 0)
            w_out = w_out_even[e].astype(BF16)
            outs.setdefault("na_k", []).append(ka_c.reshape(nb_c, s_c, H_A, HD_A))
            outs.setdefault("na_v", []).append(va_c.reshape(nb_c, s_c, H_A, HD_A))
            outs.setdefault("C", []).append(c_fin.reshape(nb_c, 2, H_B, HD_B, HD_B))
            outs.setdefault("n", []).append(n_fin.reshape(nb_c, 2, H_B, HD_B))
            outs.setdefault("m", []).append(m_fin[:, :, 0].reshape(nb_c, 2, H_B))
        else:
            w_c = H_C * HD_C
            z = _inproj(x, mod_l, w_in_odd[e].astype(BF16), 4 * w_c + N_FG * FG_W)
            dl_rep = jnp.broadcast_to(ret_decay[e].reshape(2 * H_C, 1), (2 * H_C, LANES))
            nw = ret_norm[e].reshape(1, w_c)
            hr_c, s_fin = _retention(z, 0, nb_c, s_c, dl_rep, nw)
            hr_l = _retention(z, t_c, nb_l, s_l, dl_rep, nw, rope=_rope_tables(s_l),
                              init=state_ret_S[:, e].reshape(nb_l, 2 * H_C, HD_C, HD_C))[0]
            fd_c = _fnet(z, 0, nb_c, s_c)
            fd_l = _fnet(z, t_c, nb_l, s_l)
            a = _Rows(hr_c, hr_l, 0)
            b = _Rows(fd_c, fd_l, 0)
            w_out = w_out_odd[e].astype(BF16)
            outs.setdefault("S", []).append(s_fin.reshape(nb_c, 2, H_C, HD_C, HD_C))
        wr = _pad_lanes(w_router[l])
        br = _pad_lanes(b_router[l].reshape(1, N_EXP), NEG)
        x1, h2, ti, tw, rank, cnt = _outproj_router(a, b, x, mod_l, w_out, wr, br)
        last = l == depth - 1
        x = _moe(x1, h2, ti, tw, rank, cnt, mod_l, l, w_moe_in, b_moe_in, w_moe_out, b_moe_out, split=last)
        if not last:
            x = _whole(x)

    y_prompt = x[0].reshape(nb_c, s_c, d)
    y_sample = x[1].reshape(nb_l, s_l, d)
    stack = lambda key: jnp.stack(outs[key], axis=1).astype(dt)
    return (y_prompt, y_sample, stack("na_k"), stack("na_v"), stack("C"), stack("n"), stack("m"), stack("S"))
```

```python
import functools
from typing import NamedTuple

import numpy as np
import jax
import jax.numpy as jnp
from jax import lax
from jax.experimental import pallas as pl
from jax.experimental.pallas import tpu as pltpu

F32 = jnp.float32
BF16 = jnp.bfloat16
HIGHEST = lax.Precision.HIGHEST

D_MODEL = 1024
GRID_W = 64
WIN_R = 8
WIN_C = 16
H_A, HD_A = 8, 64
H_B, HD_B = 4, 128
H_C, HD_C = 4, 128
N_FG, FG_W = 4, 128
W_A = H_A * HD_A
W_B = H_B * HD_B
N_EXP = 32
TOP_K = 4
D_FF = D_MODEL
SWIGLU_LIMIT = 7.0
SWIGLU_ALPHA = 1.702
CHUNK = 128
ROPE_BASE = 10000.0
EPS = 1e-6

LANES = 128
SEG = 1024
N_COND = 8
TM = 512
N_TILES = 8 * SEG // TM
N_CTX_TILES = 4 * SEG // TM
TM_E = 256
NEG = -1e30
VMEM_LIMIT = 56 * 1024 * 1024


def _cparams(*sem):
    return pltpu.CompilerParams(dimension_semantics=sem, vmem_limit_bytes=VMEM_LIMIT)


def _cond_row(i):
    return jnp.maximum((i * TM) // SEG - 3, 0)


def _log_sigmoid(x):
    return jnp.minimum(x, 0.0) - jnp.log1p(jnp.exp(-jnp.abs(x)))


def _dot(a, b):
    return jnp.dot(a, b, preferred_element_type=F32)


def _dot_nt(a, b):
    return lax.dot_general(a, b, (((1,), (1,)), ((), ())), preferred_element_type=F32)


def _dot_hi(a, b):
    return jnp.dot(a, b, precision=HIGHEST, preferred_element_type=F32)


def _split_bf16(x):
    hi = x.astype(BF16)
    return hi, (x - hi.astype(F32)).astype(BF16)


def _split3_bf16(x):
    hi = x.astype(BF16)
    rest = x - hi.astype(F32)
    mid = rest.astype(BF16)
    return hi, mid, (rest - mid.astype(F32)).astype(BF16)


def _dot_x3(x, w_hi, w_lo):
    x_hi, x_lo = _split_bf16(x)
    return _dot(x_hi, w_hi) + (_dot(x_lo, w_hi) + _dot(x_hi, w_lo))


def _mod_kernel(cond_ref, w_ref, b_ref, o_ref):
    c = cond_ref[...]
    s = c * jax.nn.sigmoid(c)
    o_ref[...] = _dot_hi(s, w_ref[...]) + b_ref[...]


def _modulation(cond, w_mod, b_mod):
    depth, d, n = w_mod.shape
    tn = 1536
    return pl.pallas_call(
        _mod_kernel,
        grid=(depth, n // tn),
        in_specs=[pl.BlockSpec((N_COND, d), lambda l, j: (0, 0)),
                  pl.BlockSpec((None, d, tn), lambda l, j: (l, 0, j)),
                  pl.BlockSpec((None, 1, tn), lambda l, j: (l, 0, j))],
        out_specs=pl.BlockSpec((None, N_COND, tn), lambda l, j: (l, 0, j)),
        out_shape=jax.ShapeDtypeStruct((depth, N_COND, n), F32),
        compiler_params=_cparams("arbitrary", "arbitrary"),
        name="modulation",
    )(cond, w_mod, b_mod.reshape(depth, 1, n))


def _rms_mod(x, shift, scale):
    h = x * lax.rsqrt(jnp.mean(x * x, axis=-1, keepdims=True) + EPS)
    return h * (1.0 + scale) + shift


class _Rows(NamedTuple):
    a: jax.Array
    b: jax.Array
    off_b: int

    def specs(self, lag=0):
        width = self.a.shape[1]
        off_b = self.off_b
        tile = (lambda i: i) if lag == 0 else (lambda i: jnp.maximum(i - lag, 0))
        return [pl.BlockSpec((TM, width), lambda i: (jnp.minimum(tile(i), N_CTX_TILES - 1), 0)),
                pl.BlockSpec((TM, width), lambda i: (jnp.maximum(tile(i) - N_CTX_TILES, 0) + off_b, 0))]


def _whole(x):
    return _Rows(x, x, N_CTX_TILES)


def _rows_read(tile, ref_a, ref_b):
    return jnp.where(tile < N_CTX_TILES, ref_a[...], ref_b[...])


def _inproj_kernel(xa_ref, xb_ref, mod_ref, w_ref, z_ref):
    x = _rows_read(pl.program_id(0), xa_ref, xb_ref)
    h = _rms_mod(x, mod_ref[0:1, :], mod_ref[1:2, :])
    z_ref[...] = _dot(h.astype(BF16), w_ref[...])


def _inproj_gate_kernel(xa_ref, xb_ref, mod_ref, w_ref, wgh_ref, wgl_ref, bg_ref, z_ref, g_ref):
    x = _rows_read(pl.program_id(0), xa_ref, xb_ref)
    h = _rms_mod(x, mod_ref[0:1, :], mod_ref[1:2, :])
    z_ref[...] = _dot(h.astype(BF16), w_ref[...])
    g_ref[...] = _dot_x3(h, wgh_ref[...], wgl_ref[...]) + bg_ref[...]


def _inproj(x, mod, w, n, wg=None, bg=None):
    t = N_TILES * TM
    d = x.a.shape[1]
    in_specs = x.specs() + [pl.BlockSpec((None, 6, d), lambda i: (_cond_row(i), 0, 0)),
                            pl.BlockSpec((d, n), lambda i: (0, 0))]
    z_spec = pl.BlockSpec((TM, n), lambda i: (i, 0))
    z_shape = jax.ShapeDtypeStruct((t, n), F32)
    if wg is None:
        return pl.pallas_call(
            _inproj_kernel, grid=(N_TILES,), in_specs=in_specs, out_specs=z_spec, out_shape=z_shape,
            compiler_params=_cparams("arbitrary"), name="inproj",
        )(x.a, x.b, mod, w)
    wg_hi, wg_lo = _split_bf16(wg)
    in_specs += [pl.BlockSpec((d, LANES), lambda i: (0, 0))] * 2 + [pl.BlockSpec((1, LANES), lambda i: (0, 0))]
    return pl.pallas_call(
        _inproj_gate_kernel, grid=(N_TILES,), in_specs=in_specs,
        out_specs=[z_spec, pl.BlockSpec((TM, LANES), lambda i: (i, 0))],
        out_shape=[z_shape, jax.ShapeDtypeStruct((t, LANES), F32)],
        compiler_params=_cparams("arbitrary"), name="inproj_gate",
    )(x.a, x.b, mod, w, wg_hi, wg_lo, bg)


def _head_rms(x, w):
    return x * lax.rsqrt(jnp.mean(x * x, axis=-1, keepdims=True) + EPS) * w


def _fold_lanes(x, op):
    parts = [x[:, c * LANES:(c + 1) * LANES] for c in range(x.shape[1] // LANES)]
    while len(parts) > 1:
        parts = [op(parts[c], parts[c + 1]) if c + 1 < len(parts) else parts[c] for c in range(0, len(parts), 2)]
    return parts[0]


def _head_group_matrix():
    head = np.arange(W_A) // HD_A
    return jnp.asarray((head[:, None] == head[None, :]) / HD_A, F32).astype(BF16)


def _heads_rms(x, w_row, g):
    hi, lo = _split_bf16(x * x)
    return x * lax.rsqrt(_dot(hi, g) + _dot(lo, g) + EPS) * w_row


def _ctx_attn_kernel(q_ref, k_ref, v_ref, qn_ref, kn_ref, g_ref, o_ref, ko_ref, vo_ref, km_s, vm_s):
    seq = q_ref.shape[0]
    scale = HD_A ** -0.5
    g = g_ref[...]
    q = _heads_rms(q_ref[...], qn_ref[...] * scale, g).astype(BF16)
    k = _heads_rms(k_ref[...], kn_ref[...], g)
    v = v_ref[...]
    for h in range(H_A):
        sl = slice(h * HD_A, (h + 1) * HD_A)
        ko_ref[pl.ds(h, seq, stride=H_A), :] = k[:, sl]
        vo_ref[pl.ds(h, seq, stride=H_A), :] = v[:, sl]
    kb = k.astype(BF16)
    vb = v.astype(BF16)
    head = lax.broadcasted_iota(jnp.int32, (seq, W_A), 1) // HD_A
    zero = jnp.zeros((seq, W_A), BF16)
    for h in range(H_A):
        km_s[h * seq:(h + 1) * seq, :] = jnp.where(head == h, kb, zero)
        vm_s[h * seq:(h + 1) * seq, :] = jnp.where(head == h, vb, zero)
    s = _dot_nt(q, km_s[...])
    probs = []
    for h in range(H_A):
        s_h = s[:, h * seq:(h + 1) * seq]
        p = jnp.exp(s_h - jnp.max(_fold_lanes(s_h, jnp.maximum), axis=-1, keepdims=True))
        den = jnp.sum(_fold_lanes(p, jnp.add), axis=-1, keepdims=True)
        probs.append((p / den).astype(BF16))
    o_ref[...] = _dot(jnp.concatenate(probs, axis=1), vm_s[...])


def _ctx_attention(z, n_batch, seq, qn, kn):
    spec = lambda c: pl.BlockSpec((seq, W_A), lambda b: (b, c))
    wspec = pl.BlockSpec((1, W_A), lambda b: (0, 0))
    out = jax.ShapeDtypeStruct((n_batch * seq, W_A), F32)
    return pl.pallas_call(
        _ctx_attn_kernel, grid=(n_batch,),
        in_specs=[spec(0), spec(1), spec(2), wspec, wspec, pl.BlockSpec((W_A, W_A), lambda b: (0, 0))],
        out_specs=[pl.BlockSpec((seq, W_A), lambda b: (b, 0))] + [pl.BlockSpec((seq * H_A, HD_A), lambda b: (b, 0))] * 2,
        out_shape=[out] + [jax.ShapeDtypeStruct((n_batch * seq * H_A, HD_A), F32)] * 2,
        scratch_shapes=[pltpu.VMEM((H_A * seq, W_A), BF16), pltpu.VMEM((H_A * seq, W_A), BF16)],
        compiler_params=_cparams("arbitrary"), name="ctx_attention",
    )(z, z, z, qn, kn, _head_group_matrix())


NA_QROWS = 4
NA_KROWS = 12


def _na_slab_start(r0, rows):
    first_window = jnp.clip(r0 - WIN_R // 2, 0, rows - WIN_R)
    return jnp.minimum(first_window, rows - NA_KROWS)


def _na_bias_pairs(rpb):
    qc = np.arange(GRID_W)
    kc = np.arange(GRID_W)
    cstart = np.clip(qc - WIN_C // 2, 0, GRID_W - WIN_C)
    col_in = (kc[None, :] >= cstart[:, None]) & (kc[None, :] < cstart[:, None] + WIN_C)
    dc = np.clip(kc[None, :] - qc[:, None], 1 - WIN_C, WIN_C - 1) + WIN_C - 1
    sel_c = jnp.asarray(dc[:, :, None] == np.arange(2 * WIN_C - 1)[None, None, :], F32)
    tab = jnp.einsum("hab,qkb->haqk", rpb, sel_c, precision=HIGHEST)
    tab = jnp.where(jnp.asarray(col_in)[None, None], tab, NEG)
    following = jnp.concatenate([tab[:, 1:], jnp.full_like(tab[:, :1], NEG)], axis=1)
    return jnp.concatenate([tab, following], axis=-1)


def _na_kernel(q_ref, k_ref, v_ref, kc_ref, vc_ref, bias_ref, qn_ref, kn_ref, g_ref, o_ref,
               kn_s, v_s, kc_s, vc_s, *, rows):
    j = pl.program_id(1)
    scale = HD_A ** -0.5

    @pl.when(j == 0)
    def _():
        kn_s[...] = _heads_rms(k_ref[...], kn_ref[...], g_ref[...]).astype(BF16)
        v_s[...] = v_ref[...].astype(BF16)
        kc_s[...] = kc_ref[...].astype(BF16)
        vc_s[...] = vc_ref[...].astype(BF16)

    r0 = j * NA_QROWS
    kr0 = _na_slab_start(r0, rows)
    start = pl.multiple_of(kr0 * GRID_W, GRID_W)
    n_loc = NA_KROWS * GRID_W
    first_half = lax.broadcasted_iota(jnp.int32, (1, 2 * GRID_W), 1) < GRID_W
    plan = []
    for qr in range(NA_QROWS):
        r = r0 + qr
        rs = jnp.clip(r - WIN_R // 2, 0, rows - WIN_R)
        pairs = []
        for p in range(NA_KROWS // 2):
            ka = kr0 + 2 * p
            in0 = jnp.logical_and(ka >= rs, ka < rs + WIN_R).astype(jnp.int32)
            in1 = jnp.logical_and(ka + 1 >= rs, ka + 1 < rs + WIN_R).astype(jnp.int32)
            idx = jnp.clip(ka - r + WIN_R - 1, 0, 2 * WIN_R - 2)
            pairs.append((idx, jnp.where(first_half, in0, in1) != 0))
        plan.append(pairs)

    for h in range(H_A):
        sl = slice(h * HD_A, (h + 1) * HD_A)
        q = (_head_rms(q_ref[:, sl], qn_ref[:, sl]) * scale).astype(BF16)
        s_raw = _dot_nt(q, kn_s[pl.ds(start, n_loc), sl])
        s_ctx = _dot_nt(q, kc_s[:, sl])
        blocks = []
        for qr in range(NA_QROWS):
            qs = slice(qr * GRID_W, (qr + 1) * GRID_W)
            tiles = [s_raw[qs, p * 2 * GRID_W:(p + 1) * 2 * GRID_W] + jnp.where(keep, bias_ref[h, idx], NEG)
                     for p, (idx, keep) in enumerate(plan[qr])]
            blocks.append(jnp.concatenate(tiles, axis=1))
        s_loc = jnp.concatenate(blocks, axis=0)
        m = jnp.max(jnp.maximum(_fold_lanes(s_loc, jnp.maximum), _fold_lanes(s_ctx, jnp.maximum)),
                    axis=-1, keepdims=True)
        p_loc = jnp.exp(s_loc - m)
        p_ctx = jnp.exp(s_ctx - m)
        den = jnp.sum(_fold_lanes(p_loc, jnp.add) + _fold_lanes(p_ctx, jnp.add), axis=-1, keepdims=True)
        o = _dot(p_loc.astype(BF16), v_s[pl.ds(start, n_loc), sl]) + _dot(p_ctx.astype(BF16), vc_s[:, sl])
        o_ref[:, sl] = o / den


def _na_attention(z, row0, n_batch, seq, kc, vc, bias, qn, kn):
    rows = seq // GRID_W
    assert 2 * GRID_W == LANES and rows % NA_QROWS == 0 and NA_KROWS % 2 == 0 and rows >= NA_KROWS
    for r0 in range(0, rows, NA_QROWS):
        lo = int(np.clip(r0 - WIN_R // 2, 0, rows - WIN_R))
        hi = int(np.clip(r0 + NA_QROWS - 1 - WIN_R // 2, 0, rows - WIN_R)) + WIN_R
        assert hi <= min(lo, rows - NA_KROWS) + NA_KROWS
    past = kc.shape[1]
    nblk = rows // NA_QROWS
    tq = NA_QROWS * GRID_W
    blk0 = row0 // tq
    sblk0 = row0 // seq
    full = lambda c: pl.BlockSpec((seq, W_A), lambda b, j: (sblk0 + b, c))
    cspec = pl.BlockSpec((None, past, W_A), lambda b, j: (b, 0, 0))
    wspec = pl.BlockSpec((1, W_A), lambda b, j: (0, 0))
    return pl.pallas_call(
        functools.partial(_na_kernel, rows=rows), grid=(n_batch, nblk),
        in_specs=[pl.BlockSpec((tq, W_A), lambda b, j: (blk0 + b * nblk + j, 0)),
                  full(1), full(2), cspec, cspec,
                  pl.BlockSpec(bias.shape, lambda b, j: (0, 0, 0, 0)),
                  wspec, wspec, pl.BlockSpec((W_A, W_A), lambda b, j: (0, 0))],
        out_specs=pl.BlockSpec((tq, W_A), lambda b, j: (b * nblk + j, 0)),
        out_shape=jax.ShapeDtypeStruct((n_batch * seq, W_A), F32),
        scratch_shapes=[pltpu.VMEM((seq, W_A), BF16), pltpu.VMEM((seq, W_A), BF16),
                        pltpu.VMEM((past, W_A), BF16), pltpu.VMEM((past, W_A), BF16)],
        compiler_params=_cparams("arbitrary", "arbitrary"), name="na_attention",
    )(z, z, z, kc, vc, bias, qn, kn, _head_group_matrix())


def _tri_masks():
    li = lax.broadcasted_iota(jnp.int32, (CHUNK, CHUNK), 0)
    si = lax.broadcasted_iota(jnp.int32, (CHUNK, CHUNK), 1)
    return li >= si, li <= si


def _mlstm_kernel(*refs, nc, has_init):
    if has_init:
        (q_ref, k_ref, v_ref, og_ref, g_ref, nw_ref, c0_ref, n0_ref, m0_ref,
         o_ref, cf_ref, nf_ref, mf_ref, h_s, c_s, n_s, m_s) = refs
    else:
        (q_ref, k_ref, v_ref, og_ref, g_ref, nw_ref,
         o_ref, cf_ref, nf_ref, mf_ref, h_s, c_s, n_s, m_s) = refs
    nd = 2 * H_B
    if has_init:
        for sidx in range(nd):
            c_s[sidx] = c0_ref[sidx].T
        n_s[...] = n0_ref[...]
        m_s[...] = m0_ref[...]
    else:
        c_s[...] = jnp.zeros_like(c_s)
        n_s[...] = jnp.zeros_like(n_s)
        m_s[...] = jnp.zeros_like(m_s)

    causal, anti = _tri_masks()
    tri_f = jnp.where(causal, 1.0, 0.0).astype(BF16)
    tri_b = jnp.where(anti, 1.0, 0.0).astype(BF16)
    kscale = HD_B ** -0.5

    def chunk_step(c, carry):
        for d in range(2):
            cc = c if d == 0 else nc - 1 - c
            t0 = pl.multiple_of(cc * CHUNK, CHUNK)
            g = g_ref[pl.ds(t0, CHUNK), :]
            gt = g.T
            ls = _log_sigmoid(g)
            lst = _log_sigmoid(gt)
            tri_c, tri_r, mask_t = (tri_f, tri_b, anti) if d == 0 else (tri_b, tri_f, causal)
            b_cols = sum(_dot(tri_c, piece) for piece in _split3_bf16(ls))
            b_rows = sum(_dot(piece, tri_r) for piece in _split3_bf16(lst))
            last = CHUNK - 1 if d == 0 else 0
            for h in range(H_B):
                ci = (2 * d) * H_B + h
                cf = (2 * d + 1) * H_B + h
                hs = slice(h * HD_B, (h + 1) * HD_B)
                q = q_ref[pl.ds(t0, CHUNK), hs]
                k = k_ref[pl.ds(t0, CHUNK), hs] * kscale
                v = v_ref[pl.ds(t0, CHUNK), hs]
                qb, kb = q.astype(BF16), k.astype(BF16)
                qtb = q.T.astype(BF16)
                vt = v.T
                b_row = b_rows[cf:cf + 1, :]
                i_row = gt[ci:ci + 1, :]
                ib_col = g[:, ci:ci + 1] - b_cols[:, cf:cf + 1]
                sidx = d * H_B + h
                cst = c_s[sidx]
                nst = n_s[sidx:sidx + 1, :]
                mst = m_s[sidx:sidx + 1, 0:1]
                dmat_t = jnp.where(mask_t, b_row + ib_col, -jnp.inf)
                inter = b_row + mst
                mt = jnp.maximum(inter, jnp.max(dmat_t, axis=0, keepdims=True))
                w_t = jnp.exp(dmat_t - mt) * _dot_nt(kb, qb)
                a = jnp.exp(inter - mt)
                num_t = _dot(vt.astype(BF16), w_t.astype(BF16)) + _dot(cst.astype(BF16), qtb) * a
                nq = _dot(jnp.broadcast_to(nst, (SUB, HD_B)).astype(BF16), qtb)[0:1, :]
                den = jnp.sum(w_t, axis=0, keepdims=True) + a * nq
                h_t = num_t / jnp.maximum(jnp.abs(den), jnp.exp(-mt))
                h_s[d, pl.ds(t0, CHUNK), hs] = h_t.T
                bl = b_row[:, last:last + 1]
                dl = bl - b_row + i_row
                m_new = jnp.maximum(bl + mst, jnp.max(dl, axis=-1, keepdims=True))
                wl = jnp.exp(dl - m_new)
                dec = jnp.exp(bl + mst - m_new)
                c_s[sidx] = dec * cst + _dot((vt * wl).astype(BF16), kb)
                wl8 = jnp.broadcast_to(wl, (SUB, CHUNK)).astype(BF16)
                n_s[sidx:sidx + 1, :] = dec * nst + _dot(wl8, kb)[0:1, :]
                m_s[sidx:sidx + 1, :] = jnp.broadcast_to(m_new, (1, LANES))
        return carry

    lax.fori_loop(0, nc, chunk_step, 0)

    for h in range(H_B):
        hs = slice(h * HD_B, (h + 1) * HD_B)
        hsum = h_s[0, :, hs] + h_s[1, :, hs]
        o_ref[:, hs] = _head_rms(hsum, nw_ref[:, hs]) * jax.nn.sigmoid(og_ref[:, hs])
    for sidx in range(nd):
        cf_ref[sidx] = c_s[sidx].T
    nf_ref[...] = n_s[...]
    mf_ref[...] = m_s[...]


def _mlstm(z, g, row0, n_batch, seq, norm_w, init=None):
    sblk0 = row0 // seq
    nd = 2 * H_B
    spec = lambda c: pl.BlockSpec((seq, W_B), lambda b: (sblk0 + b, c))
    in_specs = [spec(3), spec(4), spec(5), spec(6),
                pl.BlockSpec((seq, LANES), lambda b: (sblk0 + b, 0)),
                pl.BlockSpec((1, W_B), lambda b: (0, 0))]
    args = [z, z, z, z, g, norm_w]
    st_specs = [pl.BlockSpec((None, nd, HD_B, HD_B), lambda b: (b, 0, 0, 0)),
                pl.BlockSpec((None, nd, HD_B), lambda b: (b, 0, 0)),
                pl.BlockSpec((None, nd, LANES), lambda b: (b, 0, 0))]
    if init is not None:
        in_specs += st_specs
        args += list(init)
    return pl.pallas_call(
        functools.partial(_mlstm_kernel, nc=seq // CHUNK, has_init=init is not None), grid=(n_batch,),
        in_specs=in_specs,
        out_specs=[pl.BlockSpec((seq, W_B), lambda b: (b, 0))] + st_specs,
        out_shape=[jax.ShapeDtypeStruct((n_batch * seq, W_B), F32),
                   jax.ShapeDtypeStruct((n_batch, nd, HD_B, HD_B), F32),
                   jax.ShapeDtypeStruct((n_batch, nd, HD_B), F32),
                   jax.ShapeDtypeStruct((n_batch, nd, LANES), F32)],
        scratch_shapes=[pltpu.VMEM((2, seq, W_B), F32), pltpu.VMEM((nd, HD_B, HD_B), F32),
                        pltpu.VMEM((nd, HD_B), F32), pltpu.VMEM((nd, LANES), F32)],
        compiler_params=_cparams("arbitrary"), name="mlstm",
    )(*args)


def _rope_tables(seq):
    half = HD_C // 2
    quarter = half // 2
    t = np.arange(seq)
    inv = ROPE_BASE ** (-np.arange(0, half, 2, dtype=np.float64) / half)
    ang_r = (t // GRID_W)[:, None] * inv[None, :]
    ang_c = (t % GRID_W)[:, None] * inv[None, :]
    cos_t = np.concatenate([np.cos(ang_r), np.cos(ang_r), np.cos(ang_c), np.cos(ang_c)], -1)
    sin_t = np.concatenate([-np.sin(ang_r), np.sin(ang_r), -np.sin(ang_c), np.sin(ang_c)], -1)
    assert cos_t.shape == (seq, 4 * quarter)
    return jnp.asarray(cos_t, F32), jnp.asarray(sin_t, F32)


def _rope(x, cos_t, sin_t):
    quarter = HD_C // 4
    lane = lax.broadcasted_iota(jnp.int32, x.shape, 1)
    first = (lane % (2 * quarter)) < quarter
    swapped = jnp.where(first, pltpu.roll(x, HD_C - quarter, 1), pltpu.roll(x, quarter, 1))
    return x * cos_t + swapped * sin_t


def _ret_kernel(*refs, nc, has_init, use_rope):
    refs = list(refs)
    q_ref, k_ref, v_ref, gg_ref, dl_ref, nw_ref = refs[:6]
    pos = 6
    if use_rope:
        cos_ref, sin_ref = refs[pos:pos + 2]
        pos += 2
    if has_init:
        s0_ref = refs[pos]
        pos += 1
    o_ref, sf_ref, h_s, s_s, dk_s = refs[pos:pos + 5]
    if has_init:
        s_s[...] = s0_ref[...]
    else:
        s_s[...] = jnp.zeros_like(s_s)

    causal, anti = _tri_masks()
    li = lax.broadcasted_iota(jnp.int32, (CHUNK, CHUNK), 0).astype(F32)
    si = lax.broadcasted_iota(jnp.int32, (CHUNK, CHUNK), 1).astype(F32)
    lg_all = _log_sigmoid(dl_ref[...])
    kscale = HD_C ** -0.5
    for d in range(2):
        for h in range(H_C):
            sidx = d * H_C + h
            lg = lg_all[sidx:sidx + 1, :]
            if d == 0:
                dk_s[sidx, 0] = jnp.exp(jnp.where(causal, (li - si) * lg, -jnp.inf))
                dk_s[sidx, 1] = jnp.exp((li + 1.0) * lg)
                dk_s[sidx, 2] = jnp.exp((CHUNK - 1.0 - li) * lg)
            else:
                dk_s[sidx, 0] = jnp.exp(jnp.where(anti, (si - li) * lg, -jnp.inf))
                dk_s[sidx, 1] = jnp.exp((CHUNK - li) * lg)
                dk_s[sidx, 2] = jnp.exp(li * lg)

    def chunk_step(c, carry):
        for d in range(2):
            cc = c if d == 0 else nc - 1 - c
            t0 = pl.multiple_of(cc * CHUNK, CHUNK)
            for h in range(H_C):
                sidx = d * H_C + h
                hs = slice(h * HD_C, (h + 1) * HD_C)
                lg = lg_all[sidx:sidx + 1, :]
                q = q_ref[pl.ds(t0, CHUNK), hs]
                k = k_ref[pl.ds(t0, CHUNK), hs] * kscale
                v = v_ref[pl.ds(t0, CHUNK), hs]
                if use_rope:
                    cos_t = cos_ref[pl.ds(t0, CHUNK), :]
                    sin_t = sin_ref[pl.ds(t0, CHUNK), :]
                    q = _rope(q, cos_t, sin_t)
                    k = _rope(k, cos_t, sin_t)
                decay, q_dec, k_dec = dk_s[sidx, 0], dk_s[sidx, 1], dk_s[sidx, 2]
                c_dec = jnp.exp(CHUNK * lg)
                st = s_s[sidx]
                qb, kb, vb = q.astype(BF16), k.astype(BF16), v.astype(BF16)
                att = _dot_nt(qb, kb) * decay
                o = _dot(att.astype(BF16), vb) + _dot(qb, st.astype(BF16)) * q_dec
                h_s[d, pl.ds(t0, CHUNK), hs] = o
                kd = (k * k_dec).T.astype(BF16)
                s_s[sidx] = c_dec * st + _dot(kd, vb)
        return carry

    lax.fori_loop(0, nc, chunk_step, 0)

    for h in range(H_C):
        hs = slice(h * HD_C, (h + 1) * HD_C)
        osum = h_s[0, :, hs] + h_s[1, :, hs]
        gg = gg_ref[:, hs]
        o_ref[:, hs] = _head_rms(osum, nw_ref[:, hs]) * (gg * jax.nn.sigmoid(gg))
    sf_ref[...] = s_s[...]


def _retention(z, row0, n_batch, seq, decay_rep, norm_w, rope=None, init=None):
    sblk0 = row0 // seq
    nd = 2 * H_C
    w_c = H_C * HD_C
    spec = lambda c: pl.BlockSpec((seq, w_c), lambda b: (sblk0 + b, c))
    in_specs = [spec(0), spec(1), spec(2), spec(3),
                pl.BlockSpec((nd, LANES), lambda b: (0, 0)),
                pl.BlockSpec((1, w_c), lambda b: (0, 0))]
    args = [z, z, z, z, decay_rep, norm_w]
    if rope is not None:
        in_specs += [pl.BlockSpec((seq, HD_C), lambda b: (0, 0))] * 2
        args += list(rope)
    st_spec = pl.BlockSpec((None, nd, HD_C, HD_C), lambda b: (b, 0, 0, 0))
    if init is not None:
        in_specs.append(st_spec)
        args.append(init)
    return pl.pallas_call(
        functools.partial(_ret_kernel, nc=seq // CHUNK, has_init=init is not None, use_rope=rope is not None),
        grid=(n_batch,), in_specs=in_specs,
        out_specs=[pl.BlockSpec((seq, w_c), lambda b: (b, 0)), st_spec],
        out_shape=[jax.ShapeDtypeStruct((n_batch * seq, w_c), F32),
                   jax.ShapeDtypeStruct((n_batch, nd, HD_C, HD_C), F32)],
        scratch_shapes=[pltpu.VMEM((2, seq, w_c), F32), pltpu.VMEM((nd, HD_C, HD_C), F32),
                        pltpu.VMEM((nd, 3, CHUNK, CHUNK), F32)],
        compiler_params=_cparams("arbitrary"), name="retention",
    )(*args)


def _dft_tables(n):
    idx = (np.arange(n)[:, None] * np.arange(n)[None, :]) % n
    ang = 2.0 * np.pi * idx / n
    return np.cos(ang) / np.sqrt(n), np.sin(ang) / np.sqrt(n)


def _fnet_kernel(x_ref, cw_ref, sw_ref, cs_ref, ss_ref, o_ref):
    for g in range(N_FG):
        gs = slice(g * FG_W, (g + 1) * FG_W)
        x = x_ref[:, gs].astype(BF16)
        xc = _dot(x, cw_ref[...]).astype(BF16)
        xs = _dot(x, sw_ref[...]).astype(BF16)
        o_ref[:, gs] = _dot(cs_ref[...], xc) - _dot(ss_ref[...], xs)


def _fnet(z, row0, n_batch, seq):
    sblk0 = row0 // seq
    w_d = N_FG * FG_W
    cw, sw = _dft_tables(FG_W)
    cs, ss = _dft_tables(seq)
    tabs = [jnp.asarray(a, F32).astype(BF16) for a in (cw, sw, cs, ss)]
    wspec = pl.BlockSpec((FG_W, FG_W), lambda b: (0, 0))
    sspec = pl.BlockSpec((seq, seq), lambda b: (0, 0))
    return pl.pallas_call(
        _fnet_kernel, grid=(n_batch,),
        in_specs=[pl.BlockSpec((seq, w_d), lambda b: (sblk0 + b, 4)), wspec, wspec, sspec, sspec],
        out_specs=pl.BlockSpec((seq, w_d), lambda b: (b, 0)),
        out_shape=jax.ShapeDtypeStruct((n_batch * seq, w_d), F32),
        compiler_params=_cparams("arbitrary"), name="fnet",
    )(z, *tabs)


SUB = 8
assert D_MODEL == SUB * LANES


def _store_token_major(ref, lead, val):
    n = val.shape[0]
    for c in range(SUB):
        ref[lead + (pl.ds(c, n, stride=SUB), slice(None))] = val[:, c * LANES:(c + 1) * LANES]


def _load_token_major(ref, lead, n):
    return jnp.concatenate([ref[lead + (pl.ds(c, n, stride=SUB), slice(None))] for c in range(SUB)], axis=1)


def _outproj_router_kernel(aa_ref, ab_ref, ba_ref, bb_ref, xa_ref, xb_ref, mod_ref, wa_ref, wb_ref,
                           wrh_ref, wrl_ref, br_ref, x1_ref, h2_ref, ti_ref, tw_ref, rank_ref, cnt_ref, cnt_s):
    i = pl.program_id(0)
    a = _rows_read(i, aa_ref, ab_ref)
    b = _rows_read(i, ba_ref, bb_ref)
    y = _dot(a.astype(BF16), wa_ref[...]) + _dot(b.astype(BF16), wb_ref[...])
    x1 = _rows_read(i, xa_ref, xb_ref) + mod_ref[2:3, :] * y
    x1_ref[...] = x1
    h2 = _rms_mod(x1, mod_ref[3:4, :], mod_ref[4:5, :])
    _store_token_major(h2_ref, (), h2)
    logits = _dot_x3(h2, wrh_ref[...], wrl_ref[...]) + br_ref[...]
    lane = lax.broadcasted_iota(jnp.int32, logits.shape, 1)
    lane_f = lane.astype(F32)
    cur = logits
    vals, picks = [], []
    ti = jnp.zeros(logits.shape, jnp.int32)
    for kk in range(TOP_K):
        mx = jnp.max(cur, axis=-1, keepdims=True)
        idx = jnp.min(jnp.where(cur == mx, lane_f, float(LANES)), axis=-1, keepdims=True)
        ti = jnp.where(lane == kk, idx.astype(jnp.int32), ti)
        pick = lane_f == idx
        cur = jnp.where(pick, -jnp.inf, cur)
        vals.append(mx)
        picks.append(pick)
    es = [jnp.exp(v - vals[0]) for v in vals]
    tot = es[0] + es[1] + es[2] + es[3]
    tw = jnp.zeros(logits.shape, F32)
    for kk in range(TOP_K):
        tw = jnp.where(lane == kk, es[kk] / tot, tw)
    ti_ref[...] = ti
    tw_ref[...] = tw

    @pl.when(i == 0)
    def _():
        cnt_s[...] = jnp.zeros_like(cnt_s)

    onehot = jnp.zeros(logits.shape, F32)
    for pick in picks:
        onehot = onehot + jnp.where(pick, 1.0, 0.0)
    n = logits.shape[0]
    earlier = (lax.broadcasted_iota(jnp.int32, (n, n), 1) < lax.broadcasted_iota(jnp.int32, (n, n), 0))
    before = cnt_s[...] + _dot(jnp.where(earlier, 1.0, 0.0).astype(BF16), onehot.astype(BF16))
    rank = jnp.zeros(logits.shape, jnp.int32)
    for kk, pick in enumerate(picks):
        r_k = jnp.sum(jnp.where(pick, before, 0.0), axis=-1, keepdims=True)
        rank = jnp.where(lane == kk, r_k.astype(jnp.int32), rank)
    rank_ref[...] = rank
    cnt_s[...] = cnt_s[...] + jnp.sum(onehot, axis=0, keepdims=True)
    cnt_ref[...] = cnt_s[...]


def _outproj_router(a, b, x, mod, w_out, wr, br):
    t = N_TILES * TM
    d = x.a.shape[1]
    wid = a.a.shape[1]
    row = lambda w: pl.BlockSpec((TM, w), lambda i: (i, 0))
    const = lambda r, c: pl.BlockSpec((r, c), lambda i: (0, 0))
    wr_hi, wr_lo = _split_bf16(wr)
    return pl.pallas_call(
        _outproj_router_kernel, grid=(N_TILES,),
        in_specs=a.specs() + b.specs() + x.specs() + [
                  pl.BlockSpec((None, 6, d), lambda i: (_cond_row(i), 0, 0)),
                  pl.BlockSpec((wid, d), lambda i: (0, 0)), pl.BlockSpec((wid, d), lambda i: (1, 0)),
                  const(d, LANES), const(d, LANES), const(1, LANES)],
        out_specs=[row(d), pl.BlockSpec((TM * SUB, LANES), lambda i: (i, 0)), row(LANES), row(LANES), row(LANES),
                   const(1, LANES)],
        out_shape=[jax.ShapeDtypeStruct((t, d), F32), jax.ShapeDtypeStruct((t * SUB, LANES), F32),
                   jax.ShapeDtypeStruct((t, LANES), jnp.int32), jax.ShapeDtypeStruct((t, LANES), F32),
                   jax.ShapeDtypeStruct((t, LANES), jnp.int32), jax.ShapeDtypeStruct((1, LANES), F32)],
        scratch_shapes=[pltpu.VMEM((1, LANES), F32)],
        compiler_params=_cparams("arbitrary"), name="outproj_router",
    )(a.a, a.b, b.a, b.b, x.a, x.b, mod, w_out, w_out, wr_hi, wr_lo, br)


N_TILES_MAX = N_TILES * TM * TOP_K // TM_E + N_EXP
PLAN_LANES = 2 * LANES
assert N_TILES_MAX <= PLAN_LANES and N_EXP <= LANES


def _plan_kernel(cnt_ref, ti_ref, rank_ref, dest_ref, meta_ref, start_s):
    i = pl.program_id(0)

    @pl.when(i == 0)
    def _():
        cnt = cnt_ref[...]
        tiles = jnp.floor((cnt + float(TM_E - 1)) * (1.0 / TM_E))
        sub = lax.broadcasted_iota(jnp.int32, (LANES, LANES), 0)
        lane = lax.broadcasted_iota(jnp.int32, (LANES, LANES), 1)
        upto = jnp.where(sub <= lane, 1.0, 0.0).astype(BF16)
        tile_end = _dot(jnp.broadcast_to(tiles, (SUB, LANES)).astype(BF16), upto)[0:1, :]
        tile_start = tile_end - tiles
        start_s[...] = tile_start * float(TM_E * SUB)
        n_tiles = jnp.max(tile_end, axis=-1, keepdims=True)
        used = tiles > 0.0

        def column(row):
            return jnp.sum(jnp.where(sub == lane, jnp.broadcast_to(row, (LANES, LANES)), 0.0), axis=-1, keepdims=True)

        end_c, start_c, tiles_c = column(tile_end), column(tile_start), column(tiles)
        used_b = jnp.broadcast_to(jnp.where(used, 1.0, 0.0), (LANES, LANES))
        pos_c = jnp.sum(jnp.where(lane <= sub, used_b, 0.0), axis=-1, keepdims=True)
        par_c = (pos_c - 1.0) - 2.0 * jnp.floor((pos_c - 1.0) * 0.5)
        nxt_c = jnp.min(jnp.where(jnp.logical_and(lane > sub, used_b > 0.0), lane.astype(F32), float(LANES)),
                        axis=-1, keepdims=True)
        nxt_c = jnp.where(nxt_c < float(LANES), nxt_c, -1.0)

        tid = lax.broadcasted_iota(jnp.int32, (LANES, PLAN_LANES), 1).astype(F32)
        exp_id = lax.broadcasted_iota(jnp.int32, (LANES, PLAN_LANES), 0)
        tid_used = jnp.minimum(tid, n_tiles - 1.0)
        te = jnp.sum(jnp.where(jnp.logical_and(exp_id < N_EXP, end_c <= tid_used), 1.0, 0.0), axis=0, keepdims=True)
        first = jnp.sum(jnp.where(jnp.logical_and(tiles_c > 0.0, start_c == tid), 1.0, 0.0), axis=0, keepdims=True)
        mine = te == exp_id.astype(F32)
        nxt = jnp.sum(jnp.where(mine, nxt_c, 0.0), axis=0, keepdims=True)
        par = jnp.sum(jnp.where(mine, par_c, 0.0), axis=0, keepdims=True)
        last = jnp.where(used, tile_end - 1.0, 0.0)
        last = jnp.concatenate([last, jnp.zeros((1, PLAN_LANES - LANES), F32)], axis=1)
        rows_used = (jnp.sum(jnp.where(mine, column(cnt), 0.0), axis=0, keepdims=True)
                     - float(TM_E) * (tid[0:1, :] - jnp.sum(jnp.where(mine, start_c, 0.0), axis=0, keepdims=True)))
        half = jnp.where(rows_used <= float(TM_E // 2), 1.0, 0.0)

        row_id = lax.broadcasted_iota(jnp.int32, (SUB, PLAN_LANES), 0)
        meta = jnp.zeros((SUB, PLAN_LANES), F32)
        for r, val in enumerate((te, first, nxt, par, last, jnp.broadcast_to(n_tiles, (1, PLAN_LANES)), half)):
            meta = jnp.where(row_id == r, jnp.broadcast_to(val, (SUB, PLAN_LANES)), meta)
        meta_ref[...] = meta.astype(jnp.int32)

    ti = ti_ref[...]
    rank = rank_ref[...]
    lane = lax.broadcasted_iota(jnp.int32, ti.shape, 1)
    dest = jnp.zeros(ti.shape, jnp.int32)
    for kk in range(TOP_K):
        base = jnp.sum(jnp.where(lane == ti[:, kk:kk + 1], start_s[...], 0.0), axis=-1, keepdims=True)
        dest = jnp.where(lane == kk, base.astype(jnp.int32) + rank[:, kk:kk + 1] * SUB, dest)
    dest_ref[...] = dest.T[0:SUB, :]


def _route_plan(cnt, ti, rank):
    t = ti.shape[0]
    row = pl.BlockSpec((TM, LANES), lambda i: (i, 0))
    dest, meta = pl.pallas_call(
        _plan_kernel, grid=(t // TM,),
        in_specs=[pl.BlockSpec((1, LANES), lambda i: (0, 0)), row, row],
        out_specs=[pl.BlockSpec((SUB, TM), lambda i: (0, i)), pl.BlockSpec((SUB, PLAN_LANES), lambda i: (0, 0))],
        out_shape=[jax.ShapeDtypeStruct((SUB, t), jnp.int32), jax.ShapeDtypeStruct((SUB, PLAN_LANES), jnp.int32)],
        scratch_shapes=[pltpu.VMEM((1, LANES), F32)],
        compiler_params=_cparams("arbitrary"), name="moe_plan",
    )(cnt, ti, rank)
    plan = dict(tile_expert=meta[0, :N_TILES_MAX], first=meta[1, :N_TILES_MAX], tile_next=meta[2, :N_TILES_MAX],
                tile_parity=meta[3, :N_TILES_MAX], last_tile=meta[4, :N_EXP], n_tiles=meta[5, :1],
                tile_half=meta[6, :N_TILES_MAX])
    return dest, plan


DISPATCH_BLK = 1024


def _dispatch_kernel(lt_ref, nt_ref, *rest):
    dest_refs, (h_ref, xs_ref, h_s, zero_s, sem, hsem) = rest[:TOP_K], rest[TOP_K:]
    i = pl.program_id(0)
    tile_rows = TM_E * SUB
    n_tiles_max = xs_ref.shape[0] // tile_rows
    n_tok = h_s.shape[0] // SUB

    blk_rows = DISPATCH_BLK * SUB

    def stage(blk):
        r0 = pl.multiple_of(blk * blk_rows, blk_rows)
        return pltpu.make_async_copy(h_ref.at[pl.ds(r0, blk_rows), :], h_s.at[pl.ds(r0, blk_rows), :], hsem.at[blk])

    @pl.when(i == 0)
    def _():
        for blk in range(n_tok // DISPATCH_BLK):
            stage(blk).start()
        zero_s[...] = jnp.zeros_like(zero_s)

        def zero_tile(tile):
            r0 = pl.multiple_of(tile * tile_rows, tile_rows)
            return pltpu.make_async_copy(zero_s, xs_ref.at[pl.ds(r0, tile_rows), :], sem)

        def start_unused(j, carry):
            zero_tile(j).start()
            return carry

        def wait_unused(j, carry):
            zero_tile(j).wait()
            return carry

        for e in range(N_EXP):
            zero_tile(lt_ref[e]).start()
        lax.fori_loop(nt_ref[0], n_tiles_max, start_unused, 0)
        for e in range(N_EXP):
            zero_tile(lt_ref[e]).wait()
        lax.fori_loop(nt_ref[0], n_tiles_max, wait_unused, 0)

    stage(i).wait()
    base = i * DISPATCH_BLK

    def issue(t, carry):
        src = pl.multiple_of((base + t) * SUB, SUB)
        for kk in range(TOP_K):
            row = pl.multiple_of(dest_refs[kk][0, t], SUB)
            pltpu.make_async_copy(h_s.at[pl.ds(src, SUB), :], xs_ref.at[pl.ds(row, SUB), :], sem).start(priority=kk % 2)
        return carry

    lax.fori_loop(0, DISPATCH_BLK, issue, 0, unroll=2)

    @pl.when(i == pl.num_programs(0) - 1)
    def _():
        for kk in range(TOP_K):
            pltpu.make_async_copy(h_s, xs_ref.at[pl.ds(0, n_tok * SUB), :], sem).wait()


def _dispatch(h2, dest, plan, n_rows):
    t = h2.shape[0] // SUB
    nblk = t // DISPATCH_BLK
    return pl.pallas_call(
        _dispatch_kernel,
        grid_spec=pltpu.PrefetchScalarGridSpec(
            num_scalar_prefetch=2, grid=(nblk,),
            in_specs=[pl.BlockSpec((None, 1, DISPATCH_BLK), lambda i, lt, nt: (i, 0, 0), memory_space=pltpu.SMEM)] * TOP_K
                     + [pl.BlockSpec(memory_space=pl.ANY)],
            out_specs=pl.BlockSpec(memory_space=pl.ANY),
            scratch_shapes=[pltpu.VMEM((t * SUB, LANES), F32), pltpu.VMEM((TM_E * SUB, LANES), F32),
                            pltpu.SemaphoreType.DMA(()), pltpu.SemaphoreType.DMA((nblk,))]),
        out_shape=jax.ShapeDtypeStruct((n_rows * SUB, LANES), F32),
        compiler_params=_cparams("arbitrary"), name="moe_dispatch",
    )(plan["last_tile"], plan["n_tiles"], *[dest[kk].reshape(nblk, 1, DISPATCH_BLK) for kk in range(TOP_K)], h2)


def _expert_kernel(te_ref, tf_ref, nt_ref, nx_ref, par_ref, half_ref, xs_ref, w1_ref, b1_ref, w2_ref, b2_ref, ys_ref,
                   w1f, w2f, w1_s, w2_s, wsem, *, layer):
    i = pl.program_id(0)

    def fetch(expert, slot):
        return (pltpu.make_async_copy(w1_ref.at[layer, expert], w1f.at[slot], wsem.at[slot]),
                pltpu.make_async_copy(w2_ref.at[layer, expert], w2f.at[slot], wsem.at[slot]))

    @pl.when(i == 0)
    def _():
        for cp in fetch(te_ref[0], 0):
            cp.start(priority=1)

    @pl.when(i < nt_ref[0])
    def _():
        @pl.when(tf_ref[i] == 1)
        def _():
            slot = par_ref[i]
            for cp in fetch(te_ref[i], slot):
                cp.wait()
            w1_s[...] = w1f[slot].astype(BF16)
            w2_s[...] = w2f[slot].astype(BF16)

            @pl.when(nx_ref[i] >= 0)
            def _():
                for cp in fetch(nx_ref[i], 1 - slot):
                    cp.start(priority=1)

        def expert_rows(n):
            x = _load_token_major(xs_ref, (), n)
            u = _dot(x.astype(BF16), w1_s[...]) + b1_ref[...]
            g = jnp.minimum(u[:, :D_FF], SWIGLU_LIMIT)
            up = jnp.clip(u[:, D_FF:], -SWIGLU_LIMIT, SWIGLU_LIMIT)
            act = (up + 1.0) * g * jax.nn.sigmoid(SWIGLU_ALPHA * g)
            _store_token_major(ys_ref, (), _dot(act.astype(BF16), w2_s[...]) + b2_ref[...])

        @pl.when(half_ref[i] == 0)
        def _():
            expert_rows(TM_E)

        @pl.when(half_ref[i] == 1)
        def _():
            expert_rows(TM_E // 2)
            ys_ref[pl.ds(TM_E // 2 * SUB, TM_E // 2 * SUB), :] = jnp.zeros((TM_E // 2 * SUB, LANES), F32)

    @pl.when(i >= nt_ref[0])
    def _():
        ys_ref[...] = jnp.zeros_like(ys_ref)


def _experts(xs, plan, layer, w1, b1, w2, b2):
    d = D_MODEL
    nt = xs.shape[0] // (TM_E * SUB)
    tile = lambda i, te, tf, ntl, nx, par, hf: (jnp.minimum(i, ntl[0] - 1), 0)
    otile = lambda i, te, tf, ntl, nx, par, hf: (i, 0)
    bmap = lambda i, te, tf, ntl, nx, par, hf: (layer, te[i], 0, 0)
    return pl.pallas_call(
        functools.partial(_expert_kernel, layer=layer),
        grid_spec=pltpu.PrefetchScalarGridSpec(
            num_scalar_prefetch=6, grid=(nt,),
            in_specs=[pl.BlockSpec((TM_E * SUB, LANES), tile),
                      pl.BlockSpec(memory_space=pl.ANY),
                      pl.BlockSpec((None, None, 1, 2 * D_FF), bmap),
                      pl.BlockSpec(memory_space=pl.ANY),
                      pl.BlockSpec((None, None, 1, d), bmap)],
            out_specs=pl.BlockSpec((TM_E * SUB, LANES), otile),
            scratch_shapes=[pltpu.VMEM((2, d, 2 * D_FF), F32), pltpu.VMEM((2, D_FF, d), F32),
                            pltpu.VMEM((d, 2 * D_FF), BF16), pltpu.VMEM((D_FF, d), BF16),
                            pltpu.SemaphoreType.DMA((2,))]),
        out_shape=jax.ShapeDtypeStruct(xs.shape, F32),
        compiler_params=_cparams("arbitrary"), name="moe_experts",
    )(plan["tile_expert"], plan["first"], plan["n_tiles"], plan["tile_next"], plan["tile_parity"],
      plan["tile_half"], xs, w1, b1, w2, b2)


def _combine_kernel(*refs):
    dest_refs = refs[:TOP_K]
    x1_ref, tw_ref, mod_ref, ys_ref, *o_refs, buf, sem = refs[TOP_K:]
    i = pl.program_id(0)
    j = i - 1
    n = pl.num_programs(0) - 1
    tile_rows = TM * SUB

    for s in range(2):
        @pl.when(jnp.logical_and(i < n, i % 2 == s))
        def _():
            def issue(t, carry):
                dst = pl.multiple_of(t * SUB, SUB)
                for kk in range(TOP_K):
                    row = pl.multiple_of(dest_refs[kk][0, t], SUB)
                    pltpu.make_async_copy(ys_ref.at[pl.ds(row, SUB), :], buf.at[s, kk, pl.ds(dst, SUB), :],
                                          sem.at[s]).start(priority=kk % 2)
                return carry

            lax.fori_loop(0, TM, issue, 0, unroll=2)

    @pl.when(j >= 0)
    def _():
        slot = j % 2
        for kk in range(TOP_K):
            pltpu.make_async_copy(ys_ref.at[pl.ds(0, tile_rows), :], buf.at[slot, kk], sem.at[slot]).wait()
        tw = tw_ref[...]
        y = tw[:, 0:1] * _load_token_major(buf, (slot, 0), TM)
        for kk in range(1, TOP_K):
            y = y + tw[:, kk:kk + 1] * _load_token_major(buf, (slot, kk), TM)
        out = x1_ref[...] + mod_ref[5:6, :] * y
        if len(o_refs) == 1:
            o_refs[0][...] = out
        else:
            @pl.when(j < N_CTX_TILES)
            def _():
                o_refs[0][...] = out

            @pl.when(j >= N_CTX_TILES)
            def _():
                o_refs[1][...] = out


def _combine(ys, dest, x1, tw, mod, split):
    t, d = x1.shape
    nblk = t // TM
    prev = lambda i: jnp.maximum(i - 1, 0)
    row = lambda w: pl.BlockSpec((TM, w), lambda i: (prev(i), 0))
    return pl.pallas_call(
        _combine_kernel, grid=(nblk + 1,),
        in_specs=[pl.BlockSpec((None, 1, TM), lambda i: (jnp.minimum(i, nblk - 1), 0, 0), memory_space=pltpu.SMEM)] * TOP_K + [
                  row(d), row(LANES),
                  pl.BlockSpec((None, 6, d), lambda i: (_cond_row(prev(i)), 0, 0)),
                  pl.BlockSpec(memory_space=pl.ANY)],
        out_specs=_Rows(x1, x1, 0).specs(lag=1) if split else row(d),
        out_shape=([jax.ShapeDtypeStruct((N_CTX_TILES * TM, d), F32),
                    jax.ShapeDtypeStruct(((N_TILES - N_CTX_TILES) * TM, d), F32)] if split
                   else jax.ShapeDtypeStruct((t, d), F32)),
        scratch_shapes=[pltpu.VMEM((2, TOP_K, TM * SUB, LANES), F32), pltpu.SemaphoreType.DMA((2,))],
        compiler_params=_cparams("arbitrary"), name="moe_combine",
    )(*[dest[kk].reshape(nblk, 1, TM) for kk in range(TOP_K)], x1, tw, mod, ys)


def _moe(x1, h2, ti, tw, rank, cnt, mod, layer, w1, b1, w2, b2, split):
    depth = w1.shape[0]
    assert x1.shape[0] == N_TILES * TM
    dest, plan = _route_plan(cnt, ti, rank)
    xs = _dispatch(h2, dest, plan, N_TILES_MAX * TM_E)
    ys = _experts(xs, plan, layer, w1, b1.reshape(depth, N_EXP, 1, -1), w2, b2.reshape(depth, N_EXP, 1, -1))
    return _combine(ys, dest, x1, tw, mod, split)


def _pad_lanes(a, value=0.0):
    return jnp.pad(a, ((0, 0), (0, LANES - a.shape[1])), constant_values=value)


def kernel(x_prompt, x_sample, cache_na_k, cache_na_v, state_mlstm_C, state_mlstm_n, state_mlstm_m, state_ret_S, c, c_ctx, w_mod, b_mod, w_in_even, mlstm_gate_b, na_q_norm, na_k_norm, na_rpb, mlstm_norm, w_out_even, w_in_odd, ret_decay, ret_norm, w_out_odd, w_router, b_router, w_moe_in, b_moe_in, w_moe_out, b_moe_out):
    nb_c, s_c, d = x_prompt.shape
    nb_l, s_l, _ = x_sample.shape
    t_c = nb_c * s_c
    t_l = nb_l * s_l
    assert t_c == 4 * SEG and s_l == SEG and d == D_MODEL
    depth = w_mod.shape[0]
    dt = x_prompt.dtype

    x = _Rows(x_prompt.reshape(t_c, d), x_sample.reshape(t_l, d), 0)
    cond = jnp.concatenate([c_ctx[None, :], c, jnp.zeros((N_COND - 1 - nb_l, d), F32)], axis=0)
    mod = _modulation(cond, w_mod, b_mod).reshape(depth, N_COND, 6, d)

    outs = {}
    for l in range(depth):
        e = l // 2
        mod_l = mod[l]
        if l % 2 == 0:
            w_in = w_in_even[e]
            n_main = 3 * W_A + 4 * W_B
            wg = _pad_lanes(w_in[:, n_main:])
            bg = _pad_lanes(mlstm_gate_b[e].reshape(1, 4 * H_B))
            z, g = _inproj(x, mod_l, w_in.astype(BF16), n_main, wg, bg)
            qn = jnp.tile(na_q_norm[e].reshape(1, HD_A), (1, H_A))
            kn = jnp.tile(na_k_norm[e].reshape(1, HD_A), (1, H_A))
            oa_c, ka_c, va_c = _ctx_attention(z, nb_c, s_c, qn, kn)
            past = cache_na_k.shape[2]
            oa_l = _na_attention(z, t_c, nb_l, s_l,
                                 cache_na_k[:, e].reshape(nb_l, past, W_A), cache_na_v[:, e].reshape(nb_l, past, W_A),
                                 _na_bias_pairs(na_rpb[e]), qn, kn)
            nw = mlstm_norm[e].reshape(1, W_B)
            hm_c, c_fin, n_fin, m_fin = _mlstm(z, g, 0, nb_c, s_c, nw)
            init = (state_mlstm_C[:, e].reshape(nb_l, 2 * H_B, HD_B, HD_B),
                    state_mlstm_n[:, e].reshape(nb_l, 2 * H_B, HD_B),
                    jnp.broadcast_to(state_mlstm_m[:, e].reshape(nb_l, 2 * H_B, 1), (nb_l, 2 * H_B, LANES)))
            hm_l = _mlstm(z, g, t_c, nb_l, s_l, nw, init)[0]
            a = _Rows(oa_c, oa_l, 0)
            b = _Rows(hm_c, hm_l, 0)
            w_out = w_out_even[e].astype(BF16)
            outs.setdefault("na_k", []).append(ka_c.reshape(nb_c, s_c, H_A, HD_A))
            outs.setdefault("na_v", []).append(va_c.reshape(nb_c, s_c, H_A, HD_A))
            outs.setdefault("C", []).append(c_fin.reshape(nb_c, 2, H_B, HD_B, HD_B))
            outs.setdefault("n", []).append(n_fin.reshape(nb_c, 2, H_B, HD_B))
            outs.setdefault("m", []).append(m_fin[:, :, 0].reshape(nb_c, 2, H_B))
        else:
            w_c = H_C * HD_C
            z = _inproj(x, mod_l, w_in_odd[e].astype(BF16), 4 * w_c + N_FG * FG_W)
            dl_rep = jnp.broadcast_to(ret_decay[e].reshape(2 * H_C, 1), (2 * H_C, LANES))
            nw = ret_norm[e].reshape(1, w_c)
            hr_c, s_fin = _retention(z, 0, nb_c, s_c, dl_rep, nw)
            hr_l = _retention(z, t_c, nb_l, s_l, dl_rep, nw, rope=_rope_tables(s_l),
                              init=state_ret_S[:, e].reshape(nb_l, 2 * H_C, HD_C, HD_C))[0]
            fd_c = _fnet(z, 0, nb_c, s_c)
            fd_l = _fnet(z, t_c, nb_l, s_l)
            a = _Rows(hr_c, hr_l, 0)
            b = _Rows(fd_c, fd_l, 0)
            w_out = w_out_odd[e].astype(BF16)
            outs.setdefault("S", []).append(s_fin.reshape(nb_c, 2, H_C, HD_C, HD_C))
        wr = _pad_lanes(w_router[l])
        br = _pad_lanes(b_router[l].reshape(1, N_EXP), NEG)
        x1, h2, ti, tw, rank, cnt = _outproj_router(a, b, x, mod_l, w_out, wr, br)
        last = l == depth - 1
        x = _moe(x1, h2, ti, tw, rank, cnt, mod_l, l, w_moe_in, b_moe_in, w_moe_out, b_moe_out, split=last)
        if not last:
            x = _whole(x)

    y_prompt = x[0].reshape(nb_c, s_c, d)
    y_sample = x[1].reshape(nb_l, s_l, d)
    stack = lambda key: jnp.stack(outs[key], axis=1).astype(dt)
    return (y_prompt, y_sample, stack("na_k"), stack("na_v"), stack("C"), stack("n"), stack("m"), stack("S"))
```

```python
import functools
from typing import NamedTuple

import numpy as np
import jax
import jax.numpy as jnp
from jax import lax
from jax.experimental import pallas as pl
from jax.experimental.pallas import tpu as pltpu

F32 = jnp.float32
BF16 = jnp.bfloat16
HIGHEST = lax.Precision.HIGHEST

D_MODEL = 1024
GRID_W = 64
WIN_R = 8
WIN_C = 16
H_A, HD_A = 8, 64
H_B, HD_B = 4, 128
H_C, HD_C = 4, 128
N_FG, FG_W = 4, 128
W_A = H_A * HD_A
W_B = H_B * HD_B
N_EXP = 32
TOP_K = 4
D_FF = D_MODEL
SWIGLU_LIMIT = 7.0
SWIGLU_ALPHA = 1.702
CHUNK = 128
ROPE_BASE = 10000.0
EPS = 1e-6

LANES = 128
SEG = 1024
N_COND = 8
TM = 512
N_TILES = 8 * SEG // TM
N_CTX_TILES = 4 * SEG // TM
TM_E = 256
NEG = -1e30
VMEM_LIMIT = 56 * 1024 * 1024


def _cparams(*sem):
    return pltpu.CompilerParams(dimension_semantics=sem, vmem_limit_bytes=VMEM_LIMIT)


def _cond_row(i):
    return jnp.maximum((i * TM) // SEG - 3, 0)


def _log_sigmoid(x):
    return jnp.minimum(x, 0.0) - jnp.log1p(jnp.exp(-jnp.abs(x)))


def _dot(a, b):
    return jnp.dot(a, b, preferred_element_type=F32)


def _dot_nt(a, b):
    return lax.dot_general(a, b, (((1,), (1,)), ((), ())), preferred_element_type=F32)


def _dot_hi(a, b):
    return jnp.dot(a, b, precision=HIGHEST, preferred_element_type=F32)


def _split_bf16(x):
    hi = x.astype(BF16)
    return hi, (x - hi.astype(F32)).astype(BF16)


def _split3_bf16(x):
    hi = x.astype(BF16)
    rest = x - hi.astype(F32)
    mid = rest.astype(BF16)
    return hi, mid, (rest - mid.astype(F32)).astype(BF16)


def _dot_x3(x, w_hi, w_lo):
    x_hi, x_lo = _split_bf16(x)
    return _dot(x_hi, w_hi) + (_dot(x_lo, w_hi) + _dot(x_hi, w_lo))


def _mod_kernel(cond_ref, w_ref, b_ref, o_ref):
    c = cond_ref[...]
    s = c * jax.nn.sigmoid(c)
    o_ref[...] = _dot_hi(s, w_ref[...]) + b_ref[...]


def _modulation(cond, w_mod, b_mod):
    depth, d, n = w_mod.shape
    tn = 1536
    return pl.pallas_call(
        _mod_kernel,
        grid=(depth, n // tn),
        in_specs=[pl.BlockSpec((N_COND, d), lambda l, j: (0, 0)),
                  pl.BlockSpec((None, d, tn), lambda l, j: (l, 0, j)),
                  pl.BlockSpec((None, 1, tn), lambda l, j: (l, 0, j))],
        out_specs=pl.BlockSpec((None, N_COND, tn), lambda l, j: (l, 0, j)),
        out_shape=jax.ShapeDtypeStruct((depth, N_COND, n), F32),
        compiler_params=_cparams("arbitrary", "arbitrary"),
        name="modulation",
    )(cond, w_mod, b_mod.reshape(depth, 1, n))


def _rms_mod(x, shift, scale):
    h = x * lax.rsqrt(jnp.mean(x * x, axis=-1, keepdims=True) + EPS)
    return h * (1.0 + scale) + shift


class _Rows(NamedTuple):
    a: jax.Array
    b: jax.Array
    off_b: int

    def specs(self, lag=0):
        width = self.a.shape[1]
        off_b = self.off_b
        tile = (lambda i: i) if lag == 0 else (lambda i: jnp.maximum(i - lag, 0))
        return [pl.BlockSpec((TM, width), lambda i: (jnp.minimum(tile(i), N_CTX_TILES - 1), 0)),
                pl.BlockSpec((TM, width), lambda i: (jnp.maximum(tile(i) - N_CTX_TILES, 0) + off_b, 0))]


def _whole(x):
    return _Rows(x, x, N_CTX_TILES)


def _rows_read(tile, ref_a, ref_b):
    return jnp.where(tile < N_CTX_TILES, ref_a[...], ref_b[...])


def _inproj_kernel(xa_ref, xb_ref, mod_ref, w_ref, z_ref):
    x = _rows_read(pl.program_id(0), xa_ref, xb_ref)
    h = _rms_mod(x, mod_ref[0:1, :], mod_ref[1:2, :])
    z_ref[...] = _dot(h.astype(BF16), w_ref[...])


def _inproj_gate_kernel(xa_ref, xb_ref, mod_ref, w_ref, wgh_ref, wgl_ref, bg_ref, z_ref, g_ref):
    x = _rows_read(pl.program_id(0), xa_ref, xb_ref)
    h = _rms_mod(x, mod_ref[0:1, :], mod_ref[1:2, :])
    z_ref[...] = _dot(h.astype(BF16), w_ref[...])
    g_ref[...] = _dot_x3(h, wgh_ref[...], wgl_ref[...]) + bg_ref[...]


def _inproj(x, mod, w, n, wg=None, bg=None):
    t = N_TILES * TM
    d = x.a.shape[1]
    in_specs = x.specs() + [pl.BlockSpec((None, 6, d), lambda i: (_cond_row(i), 0, 0)),
                            pl.BlockSpec((d, n), lambda i: (0, 0))]
    z_spec = pl.BlockSpec((TM, n), lambda i: (i, 0))
    z_shape = jax.ShapeDtypeStruct((t, n), F32)
    if wg is None:
        return pl.pallas_call(
            _inproj_kernel, grid=(N_TILES,), in_specs=in_specs, out_specs=z_spec, out_shape=z_shape,
            compiler_params=_cparams("arbitrary"), name="inproj",
        )(x.a, x.b, mod, w)
    wg_hi, wg_lo = _split_bf16(wg)
    in_specs += [pl.BlockSpec((d, LANES), lambda i: (0, 0))] * 2 + [pl.BlockSpec((1, LANES), lambda i: (0, 0))]
    return pl.pallas_call(
        _inproj_gate_kernel, grid=(N_TILES,), in_specs=in_specs,
        out_specs=[z_spec, pl.BlockSpec((TM, LANES), lambda i: (i, 0))],
        out_shape=[z_shape, jax.ShapeDtypeStruct((t, LANES), F32)],
        compiler_params=_cparams("arbitrary"), name="inproj_gate",
    )(x.a, x.b, mod, w, wg_hi, wg_lo, bg)


def _head_rms(x, w):
    return x * lax.rsqrt(jnp.mean(x * x, axis=-1, keepdims=True) + EPS) * w


def _fold_lanes(x, op):
    parts = [x[:, c * LANES:(c + 1) * LANES] for c in range(x.shape[1] // LANES)]
    while len(parts) > 1:
        parts = [op(parts[c], parts[c + 1]) if c + 1 < len(parts) else parts[c] for c in range(0, len(parts), 2)]
    return parts[0]


def _head_group_matrix():
    head = np.arange(W_A) // HD_A
    return jnp.asarray((head[:, None] == head[None, :]) / HD_A, F32).astype(BF16)


def _heads_rms(x, w_row, g):
    hi, lo = _split_bf16(x * x)
    return x * lax.rsqrt(_dot(hi, g) + _dot(lo, g) + EPS) * w_row


def _ctx_attn_kernel(q_ref, k_ref, v_ref, qn_ref, kn_ref, g_ref, o_ref, ko_ref, vo_ref, km_s, vm_s):
    seq = q_ref.shape[0]
    scale = HD_A ** -0.5
    g = g_ref[...]
    q = _heads_rms(q_ref[...], qn_ref[...] * scale, g).astype(BF16)
    k = _heads_rms(k_ref[...], kn_ref[...], g)
    v = v_ref[...]
    for h in range(H_A):
        sl = slice(h * HD_A, (h + 1) * HD_A)
        ko_ref[pl.ds(h, seq, stride=H_A), :] = k[:, sl]
        vo_ref[pl.ds(h, seq, stride=H_A), :] = v[:, sl]
    kb = k.astype(BF16)
    vb = v.astype(BF16)
    head = lax.broadcasted_iota(jnp.int32, (seq, W_A), 1) // HD_A
    zero = jnp.zeros((seq, W_A), BF16)
    for h in range(H_A):
        km_s[h * seq:(h + 1) * seq, :] = jnp.where(head == h, kb, zero)
        vm_s[h * seq:(h + 1) * seq, :] = jnp.where(head == h, vb, zero)
    s = _dot_nt(q, km_s[...])
    probs = []
    for h in range(H_A):
        s_h = s[:, h * seq:(h + 1) * seq]
        p = jnp.exp(s_h - jnp.max(_fold_lanes(s_h, jnp.maximum), axis=-1, keepdims=True))
        den = jnp.sum(_fold_lanes(p, jnp.add), axis=-1, keepdims=True)
        probs.append((p / den).astype(BF16))
    o_ref[...] = _dot(jnp.concatenate(probs, axis=1), vm_s[...])


def _ctx_attention(z, n_batch, seq, qn, kn):
    spec = lambda c: pl.BlockSpec((seq, W_A), lambda b: (b, c))
    wspec = pl.BlockSpec((1, W_A), lambda b: (0, 0))
    out = jax.ShapeDtypeStruct((n_batch * seq, W_A), F32)
    return pl.pallas_call(
        _ctx_attn_kernel, grid=(n_batch,),
        in_specs=[spec(0), spec(1), spec(2), wspec, wspec, pl.BlockSpec((W_A, W_A), lambda b: (0, 0))],
        out_specs=[pl.BlockSpec((seq, W_A), lambda b: (b, 0))] + [pl.BlockSpec((seq * H_A, HD_A), lambda b: (b, 0))] * 2,
        out_shape=[out] + [jax.ShapeDtypeStruct((n_batch * seq * H_A, HD_A), F32)] * 2,
        scratch_shapes=[pltpu.VMEM((H_A * seq, W_A), BF16), pltpu.VMEM((H_A * seq, W_A), BF16)],
        compiler_params=_cparams("arbitrary"), name="ctx_attention",
    )(z, z, z, qn, kn, _head_group_matrix())


NA_QROWS = 4
NA_KROWS = 12


def _na_slab_start(r0, rows):
    first_window = jnp.clip(r0 - WIN_R // 2, 0, rows - WIN_R)
    return jnp.minimum(first_window, rows - NA_KROWS)


def _na_bias_pairs(rpb):
    qc = np.arange(GRID_W)
    kc = np.arange(GRID_W)
    cstart = np.clip(qc - WIN_C // 2, 0, GRID_W - WIN_C)
    col_in = (kc[None, :] >= cstart[:, None]) & (kc[None, :] < cstart[:, None] + WIN_C)
    dc = np.clip(kc[None, :] - qc[:, None], 1 - WIN_C, WIN_C - 1) + WIN_C - 1
    sel_c = jnp.asarray(dc[:, :, None] == np.arange(2 * WIN_C - 1)[None, None, :], F32)
    tab = jnp.einsum("hab,qkb->haqk", rpb, sel_c, precision=HIGHEST)
    tab = jnp.where(jnp.asarray(col_in)[None, None], tab, NEG)
    following = jnp.concatenate([tab[:, 1:], jnp.full_like(tab[:, :1], NEG)], axis=1)
    return jnp.concatenate([tab, following], axis=-1)


def _na_kernel(q_ref, k_ref, v_ref, kc_ref, vc_ref, bias_ref, qn_ref, kn_ref, g_ref, o_ref,
               kn_s, v_s, kc_s, vc_s, *, rows):
    j = pl.program_id(1)
    scale = HD_A ** -0.5

    @pl.when(j == 0)
    def _():
        kn_s[...] = _heads_rms(k_ref[...], kn_ref[...], g_ref[...]).astype(BF16)
        v_s[...] = v_ref[...].astype(BF16)
        kc_s[...] = kc_ref[...].astype(BF16)
        vc_s[...] = vc_ref[...].astype(BF16)

    r0 = j * NA_QROWS
    kr0 = _na_slab_start(r0, rows)
    start = pl.multiple_of(kr0 * GRID_W, GRID_W)
    n_loc = NA_KROWS * GRID_W
    first_half = lax.broadcasted_iota(jnp.int32, (1, 2 * GRID_W), 1) < GRID_W
    plan = []
    for qr in range(NA_QROWS):
        r = r0 + qr
        rs = jnp.clip(r - WIN_R // 2, 0, rows - WIN_R)
        pairs = []
        for p in range(NA_KROWS // 2):
            ka = kr0 + 2 * p
            in0 = jnp.logical_and(ka >= rs, ka < rs + WIN_R).astype(jnp.int32)
            in1 = jnp.logical_and(ka + 1 >= rs, ka + 1 < rs + WIN_R).astype(jnp.int32)
            idx = jnp.clip(ka - r + WIN_R - 1, 0, 2 * WIN_R - 2)
            pairs.append((idx, jnp.where(first_half, in0, in1) != 0))
        plan.append(pairs)

    for h in range(H_A):
        sl = slice(h * HD_A, (h + 1) * HD_A)
        q = (_head_rms(q_ref[:, sl], qn_ref[:, sl]) * scale).astype(BF16)
        s_raw = _dot_nt(q, kn_s[pl.ds(start, n_loc), sl])
        s_ctx = _dot_nt(q, kc_s[:, sl])
        blocks = []
        for qr in range(NA_QROWS):
            qs = slice(qr * GRID_W, (qr + 1) * GRID_W)
            tiles = [s_raw[qs, p * 2 * GRID_W:(p + 1) * 2 * GRID_W] + jnp.where(keep, bias_ref[h, idx], NEG)
                     for p, (idx, keep) in enumerate(plan[qr])]
            blocks.append(jnp.concatenate(tiles, axis=1))
        s_loc = jnp.concatenate(blocks, axis=0)
        m = jnp.max(jnp.maximum(_fold_lanes(s_loc, jnp.maximum), _fold_lanes(s_ctx, jnp.maximum)),
                    axis=-1, keepdims=True)
        p_loc = jnp.exp(s_loc - m)
        p_ctx = jnp.exp(s_ctx - m)
        den = jnp.sum(_fold_lanes(p_loc, jnp.add) + _fold_lanes(p_ctx, jnp.add), axis=-1, keepdims=True)
        o = _dot(p_loc.astype(BF16), v_s[pl.ds(start, n_loc), sl]) + _dot(p_ctx.astype(BF16), vc_s[:, sl])
        o_ref[:, sl] = o / den


def _na_attention(z, row0, n_batch, seq, kc, vc, bias, qn, kn):
    rows = seq // GRID_W
    assert 2 * GRID_W == LANES and rows % NA_QROWS == 0 and NA_KROWS % 2 == 0 and rows >= NA_KROWS
    for r0 in range(0, rows, NA_QROWS):
        lo = int(np.clip(r0 - WIN_R // 2, 0, rows - WIN_R))
        hi = int(np.clip(r0 + NA_QROWS - 1 - WIN_R // 2, 0, rows - WIN_R)) + WIN_R
        assert hi <= min(lo, rows - NA_KROWS) + NA_KROWS
    past = kc.shape[1]
    nblk = rows // NA_QROWS
    tq = NA_QROWS * GRID_W
    blk0 = row0 // tq
    sblk0 = row0 // seq
    full = lambda c: pl.BlockSpec((seq, W_A), lambda b, j: (sblk0 + b, c))
    cspec = pl.BlockSpec((None, past, W_A), lambda b, j: (b, 0, 0))
    wspec = pl.BlockSpec((1, W_A), lambda b, j: (0, 0))
    return pl.pallas_call(
        functools.partial(_na_kernel, rows=rows), grid=(n_batch, nblk),
        in_specs=[pl.BlockSpec((tq, W_A), lambda b, j: (blk0 + b * nblk + j, 0)),
                  full(1), full(2), cspec, cspec,
                  pl.BlockSpec(bias.shape, lambda b, j: (0, 0, 0, 0)),
                  wspec, wspec, pl.BlockSpec((W_A, W_A), lambda b, j: (0, 0))],
        out_specs=pl.BlockSpec((tq, W_A), lambda b, j: (b * nblk + j, 0)),
        out_shape=jax.ShapeDtypeStruct((n_batch * seq, W_A), F32),
        scratch_shapes=[pltpu.VMEM((seq, W_A), BF16), pltpu.VMEM((seq, W_A), BF16),
                        pltpu.VMEM((past, W_A), BF16), pltpu.VMEM((past, W_A), BF16)],
        compiler_params=_cparams("arbitrary", "arbitrary"), name="na_attention",
    )(z, z, z, kc, vc, bias, qn, kn, _head_group_matrix())


def _tri_masks():
    li = lax.broadcasted_iota(jnp.int32, (CHUNK, CHUNK), 0)
    si = lax.broadcasted_iota(jnp.int32, (CHUNK, CHUNK), 1)
    return li >= si, li <= si


def _mlstm_kernel(*refs, nc, has_init):
    if has_init:
        (q_ref, k_ref, v_ref, og_ref, g_ref, nw_ref, c0_ref, n0_ref, m0_ref,
         o_ref, cf_ref, nf_ref, mf_ref, h_s, c_s, n_s, m_s) = refs
    else:
        (q_ref, k_ref, v_ref, og_ref, g_ref, nw_ref,
         o_ref, cf_ref, nf_ref, mf_ref, h_s, c_s, n_s, m_s) = refs
    nd = 2 * H_B
    if has_init:
        for sidx in range(nd):
            c_s[sidx] = c0_ref[sidx].T
        n_s[...] = n0_ref[...]
        m_s[...] = m0_ref[...]
    else:
        c_s[...] = jnp.zeros_like(c_s)
        n_s[...] = jnp.zeros_like(n_s)
        m_s[...] = jnp.zeros_like(m_s)

    causal, anti = _tri_masks()
    tri_f = jnp.where(causal, 1.0, 0.0).astype(BF16)
    tri_b = jnp.where(anti, 1.0, 0.0).astype(BF16)
    kscale = HD_B ** -0.5

    def chunk_step(c, carry):
        for d in range(2):
            cc = c if d == 0 else nc - 1 - c
            t0 = pl.multiple_of(cc * CHUNK, CHUNK)
            g = g_ref[pl.ds(t0, CHUNK), :]
            gt = g.T
            ls = _log_sigmoid(g)
            lst = _log_sigmoid(gt)
            tri_c, tri_r, mask_t = (tri_f, tri_b, anti) if d == 0 else (tri_b, tri_f, causal)
            b_cols = sum(_dot(tri_c, piece) for piece in _split3_bf16(ls))
            b_rows = sum(_dot(piece, tri_r) for piece in _split3_bf16(lst))
            last = CHUNK - 1 if d == 0 else 0
            for h in range(H_B):
                ci = (2 * d) * H_B + h
                cf = (2 * d + 1) * H_B + h
                hs = slice(h * HD_B, (h + 1) * HD_B)
                q = q_ref[pl.ds(t0, CHUNK), hs]
                k = k_ref[pl.ds(t0, CHUNK), hs] * kscale
                v = v_ref[pl.ds(t0, CHUNK), hs]
                qb, kb = q.astype(BF16), k.astype(BF16)
                qtb = q.T.astype(BF16)
                vt = v.T
                b_row = b_rows[cf:cf + 1, :]
                i_row = gt[ci:ci + 1, :]
                ib_col = g[:, ci:ci + 1] - b_cols[:, cf:cf + 1]
                sidx = d * H_B + h
                cst = c_s[sidx]
                nst = n_s[sidx:sidx + 1, :]
                mst = m_s[sidx:sidx + 1, 0:1]
                dmat_t = jnp.where(mask_t, b_row + ib_col, -jnp.inf)
                inter = b_row + mst
                mt = jnp.maximum(inter, jnp.max(dmat_t, axis=0, keepdims=True))
                w_t = jnp.exp(dmat_t - mt) * _dot_nt(kb, qb)
                a = jnp.exp(inter - mt)
                num_t = _dot(vt.astype(BF16), w_t.astype(BF16)) + _dot(cst.astype(BF16), qtb) * a
                nq = _dot(jnp.broadcast_to(nst, (SUB, HD_B)).astype(BF16), qtb)[0:1, :]
                den = jnp.sum(w_t, axis=0, keepdims=True) + a * nq
                h_t = num_t / jnp.maximum(jnp.abs(den), jnp.exp(-mt))
                h_s[d, pl.ds(t0, CHUNK), hs] = h_t.T
                bl = b_row[:, last:last + 1]
                dl = bl - b_row + i_row
                m_new = jnp.maximum(bl + mst, jnp.max(dl, axis=-1, keepdims=True))
                wl = jnp.exp(dl - m_new)
                dec = jnp.exp(bl + mst - m_new)
                c_s[sidx] = dec * cst + _dot((vt * wl).astype(BF16), kb)
                wl8 = jnp.broadcast_to(wl, (SUB, CHUNK)).astype(BF16)
                n_s[sidx:sidx + 1, :] = dec * nst + _dot(wl8, kb)[0:1, :]
                m_s[sidx:sidx + 1, :] = jnp.broadcast_to(m_new, (1, LANES))
        return carry

    lax.fori_loop(0, nc, chunk_step, 0, unroll=2)

    for h in range(H_B):
        hs = slice(h * HD_B, (h + 1) * HD_B)
        hsum = h_s[0, :, hs] + h_s[1, :, hs]
        o_ref[:, hs] = _head_rms(hsum, nw_ref[:, hs]) * jax.nn.sigmoid(og_ref[:, hs])
    for sidx in range(nd):
        cf_ref[sidx] = c_s[sidx].T
    nf_ref[...] = n_s[...]
    mf_ref[...] = m_s[...]


def _mlstm(z, g, row0, n_batch, seq, norm_w, init=None):
    sblk0 = row0 // seq
    nd = 2 * H_B
    spec = lambda c: pl.BlockSpec((seq, W_B), lambda b: (sblk0 + b, c))
    in_specs = [spec(3), spec(4), spec(5), spec(6),
                pl.BlockSpec((seq, LANES), lambda b: (sblk0 + b, 0)),
                pl.BlockSpec((1, W_B), lambda b: (0, 0))]
    args = [z, z, z, z, g, norm_w]
    st_specs = [pl.BlockSpec((None, nd, HD_B, HD_B), lambda b: (b, 0, 0, 0)),
                pl.BlockSpec((None, nd, HD_B), lambda b: (b, 0, 0)),
                pl.BlockSpec((None, nd, LANES), lambda b: (b, 0, 0))]
    if init is not None:
        in_specs += st_specs
        args += list(init)
    return pl.pallas_call(
        functools.partial(_mlstm_kernel, nc=seq // CHUNK, has_init=init is not None), grid=(n_batch,),
        in_specs=in_specs,
        out_specs=[pl.BlockSpec((seq, W_B), lambda b: (b, 0))] + st_specs,
        out_shape=[jax.ShapeDtypeStruct((n_batch * seq, W_B), F32),
                   jax.ShapeDtypeStruct((n_batch, nd, HD_B, HD_B), F32),
                   jax.ShapeDtypeStruct((n_batch, nd, HD_B), F32),
                   jax.ShapeDtypeStruct((n_batch, nd, LANES), F32)],
        scratch_shapes=[pltpu.VMEM((2, seq, W_B), F32), pltpu.VMEM((nd, HD_B, HD_B), F32),
                        pltpu.VMEM((nd, HD_B), F32), pltpu.VMEM((nd, LANES), F32)],
        compiler_params=_cparams("arbitrary"), name="mlstm",
    )(*args)


def _rope_tables(seq):
    half = HD_C // 2
    quarter = half // 2
    t = np.arange(seq)
    inv = ROPE_BASE ** (-np.arange(0, half, 2, dtype=np.float64) / half)
    ang_r = (t // GRID_W)[:, None] * inv[None, :]
    ang_c = (t % GRID_W)[:, None] * inv[None, :]
    cos_t = np.concatenate([np.cos(ang_r), np.cos(ang_r), np.cos(ang_c), np.cos(ang_c)], -1)
    sin_t = np.concatenate([-np.sin(ang_r), np.sin(ang_r), -np.sin(ang_c), np.sin(ang_c)], -1)
    assert cos_t.shape == (seq, 4 * quarter)
    return jnp.asarray(cos_t, F32), jnp.asarray(sin_t, F32)


def _rope(x, cos_t, sin_t):
    quarter = HD_C // 4
    lane = lax.broadcasted_iota(jnp.int32, x.shape, 1)
    first = (lane % (2 * quarter)) < quarter
    swapped = jnp.where(first, pltpu.roll(x, HD_C - quarter, 1), pltpu.roll(x, quarter, 1))
    return x * cos_t + swapped * sin_t


def _ret_kernel(*refs, nc, has_init, use_rope):
    refs = list(refs)
    q_ref, k_ref, v_ref, gg_ref, dl_ref, nw_ref = refs[:6]
    pos = 6
    if use_rope:
        cos_ref, sin_ref = refs[pos:pos + 2]
        pos += 2
    if has_init:
        s0_ref = refs[pos]
        pos += 1
    o_ref, sf_ref, h_s, s_s, dk_s = refs[pos:pos + 5]
    if has_init:
        s_s[...] = s0_ref[...]
    else:
        s_s[...] = jnp.zeros_like(s_s)

    causal, anti = _tri_masks()
    li = lax.broadcasted_iota(jnp.int32, (CHUNK, CHUNK), 0).astype(F32)
    si = lax.broadcasted_iota(jnp.int32, (CHUNK, CHUNK), 1).astype(F32)
    lg_all = _log_sigmoid(dl_ref[...])
    kscale = HD_C ** -0.5
    for d in range(2):
        for h in range(H_C):
            sidx = d * H_C + h
            lg = lg_all[sidx:sidx + 1, :]
            if d == 0:
                dk_s[sidx, 0] = jnp.exp(jnp.where(causal, (li - si) * lg, -jnp.inf))
                dk_s[sidx, 1] = jnp.exp((li + 1.0) * lg)
                dk_s[sidx, 2] = jnp.exp((CHUNK - 1.0 - li) * lg)
            else:
                dk_s[sidx, 0] = jnp.exp(jnp.where(anti, (si - li) * lg, -jnp.inf))
                dk_s[sidx, 1] = jnp.exp((CHUNK - li) * lg)
                dk_s[sidx, 2] = jnp.exp(li * lg)

    def chunk_step(c, carry):
        for d in range(2):
            cc = c if d == 0 else nc - 1 - c
            t0 = pl.multiple_of(cc * CHUNK, CHUNK)
            for h in range(H_C):
                sidx = d * H_C + h
                hs = slice(h * HD_C, (h + 1) * HD_C)
                lg = lg_all[sidx:sidx + 1, :]
                q = q_ref[pl.ds(t0, CHUNK), hs]
                k = k_ref[pl.ds(t0, CHUNK), hs] * kscale
                v = v_ref[pl.ds(t0, CHUNK), hs]
                if use_rope:
                    cos_t = cos_ref[pl.ds(t0, CHUNK), :]
                    sin_t = sin_ref[pl.ds(t0, CHUNK), :]
                    q = _rope(q, cos_t, sin_t)
                    k = _rope(k, cos_t, sin_t)
                decay, q_dec, k_dec = dk_s[sidx, 0], dk_s[sidx, 1], dk_s[sidx, 2]
                c_dec = jnp.exp(CHUNK * lg)
                st = s_s[sidx]
                qb, kb, vb = q.astype(BF16), k.astype(BF16), v.astype(BF16)
                att = _dot_nt(qb, kb) * decay
                o = _dot(att.astype(BF16), vb) + _dot(qb, st.astype(BF16)) * q_dec
                h_s[d, pl.ds(t0, CHUNK), hs] = o
                kd = (k * k_dec).T.astype(BF16)
                s_s[sidx] = c_dec * st + _dot(kd, vb)
        return carry

    lax.fori_loop(0, nc, chunk_step, 0, unroll=2)

    for h in range(H_C):
        hs = slice(h * HD_C, (h + 1) * HD_C)
        osum = h_s[0, :, hs] + h_s[1, :, hs]
        gg = gg_ref[:, hs]
        o_ref[:, hs] = _head_rms(osum, nw_ref[:, hs]) * (gg * jax.nn.sigmoid(gg))
    sf_ref[...] = s_s[...]


def _retention(z, row0, n_batch, seq, decay_rep, norm_w, rope=None, init=None):
    sblk0 = row0 // seq
    nd = 2 * H_C
    w_c = H_C * HD_C
    spec = lambda c: pl.BlockSpec((seq, w_c), lambda b: (sblk0 + b, c))
    in_specs = [spec(0), spec(1), spec(2), spec(3),
                pl.BlockSpec((nd, LANES), lambda b: (0, 0)),
                pl.BlockSpec((1, w_c), lambda b: (0, 0))]
    args = [z, z, z, z, decay_rep, norm_w]
    if rope is not None:
        in_specs += [pl.BlockSpec((seq, HD_C), lambda b: (0, 0))] * 2
        args += list(rope)
    st_spec = pl.BlockSpec((None, nd, HD_C, HD_C), lambda b: (b, 0, 0, 0))
    if init is not None:
        in_specs.append(st_spec)
        args.append(init)
    return pl.pallas_call(
        functools.partial(_ret_kernel, nc=seq // CHUNK, has_init=init is not None, use_rope=rope is not None),
        grid=(n_batch,), in_specs=in_specs,
        out_specs=[pl.BlockSpec((seq, w_c), lambda b: (b, 0)), st_spec],
        out_shape=[jax.ShapeDtypeStruct((n_batch * seq, w_c), F32),
                   jax.ShapeDtypeStruct((n_batch, nd, HD_C, HD_C), F32)],
        scratch_shapes=[pltpu.VMEM((2, seq, w_c), F32), pltpu.VMEM((nd, HD_C, HD_C), F32),
                        pltpu.VMEM((nd, 3, CHUNK, CHUNK), F32)],
        compiler_params=_cparams("arbitrary"), name="retention",
    )(*args)


def _dft_tables(n):
    idx = (np.arange(n)[:, None] * np.arange(n)[None, :]) % n
    ang = 2.0 * np.pi * idx / n
    return np.cos(ang) / np.sqrt(n), np.sin(ang) / np.sqrt(n)


def _fnet_kernel(x_ref, cw_ref, sw_ref, cs_ref, ss_ref, o_ref):
    for g in range(N_FG):
        gs = slice(g * FG_W, (g + 1) * FG_W)
        x = x_ref[:, gs].astype(BF16)
        xc = _dot(x, cw_ref[...]).astype(BF16)
        xs = _dot(x, sw_ref[...]).astype(BF16)
        o_ref[:, gs] = _dot(cs_ref[...], xc) - _dot(ss_ref[...], xs)


def _fnet(z, row0, n_batch, seq):
    sblk0 = row0 // seq
    w_d = N_FG * FG_W
    cw, sw = _dft_tables(FG_W)
    cs, ss = _dft_tables(seq)
    tabs = [jnp.asarray(a, F32).astype(BF16) for a in (cw, sw, cs, ss)]
    wspec = pl.BlockSpec((FG_W, FG_W), lambda b: (0, 0))
    sspec = pl.BlockSpec((seq, seq), lambda b: (0, 0))
    return pl.pallas_call(
        _fnet_kernel, grid=(n_batch,),
        in_specs=[pl.BlockSpec((seq, w_d), lambda b: (sblk0 + b, 4)), wspec, wspec, sspec, sspec],
        out_specs=pl.BlockSpec((seq, w_d), lambda b: (b, 0)),
        out_shape=jax.ShapeDtypeStruct((n_batch * seq, w_d), F32),
        compiler_params=_cparams("arbitrary"), name="fnet",
    )(z, *tabs)


SUB = 8
assert D_MODEL == SUB * LANES


def _store_token_major(ref, lead, val):
    n = val.shape[0]
    for c in range(SUB):
        ref[lead + (pl.ds(c, n, stride=SUB), slice(None))] = val[:, c * LANES:(c + 1) * LANES]


def _load_token_major(ref, lead, n):
    return jnp.concatenate([ref[lead + (pl.ds(c, n, stride=SUB), slice(None))] for c in range(SUB)], axis=1)


def _outproj_router_kernel(aa_ref, ab_ref, ba_ref, bb_ref, xa_ref, xb_ref, mod_ref, wa_ref, wb_ref,
                           wrh_ref, wrl_ref, br_ref, x1_ref, h2_ref, ti_ref, tw_ref, rank_ref, cnt_ref, cnt_s):
    i = pl.program_id(0)
    a = _rows_read(i, aa_ref, ab_ref)
    b = _rows_read(i, ba_ref, bb_ref)
    y = _dot(a.astype(BF16), wa_ref[...]) + _dot(b.astype(BF16), wb_ref[...])
    x1 = _rows_read(i, xa_ref, xb_ref) + mod_ref[2:3, :] * y
    x1_ref[...] = x1
    h2 = _rms_mod(x1, mod_ref[3:4, :], mod_ref[4:5, :])
    _store_token_major(h2_ref, (), h2)
    logits = _dot_x3(h2, wrh_ref[...], wrl_ref[...]) + br_ref[...]
    lane = lax.broadcasted_iota(jnp.int32, logits.shape, 1)
    lane_f = lane.astype(F32)
    cur = logits
    vals, picks = [], []
    ti = jnp.zeros(logits.shape, jnp.int32)
    for kk in range(TOP_K):
        mx = jnp.max(cur, axis=-1, keepdims=True)
        idx = jnp.min(jnp.where(cur == mx, lane_f, float(LANES)), axis=-1, keepdims=True)
        ti = jnp.where(lane == kk, idx.astype(jnp.int32), ti)
        pick = lane_f == idx
        cur = jnp.where(pick, -jnp.inf, cur)
        vals.append(mx)
        picks.append(pick)
    es = [jnp.exp(v - vals[0]) for v in vals]
    tot = es[0] + es[1] + es[2] + es[3]
    tw = jnp.zeros(logits.shape, F32)
    for kk in range(TOP_K):
        tw = jnp.where(lane == kk, es[kk] / tot, tw)
    ti_ref[...] = ti
    tw_ref[...] = tw

    @pl.when(i == 0)
    def _():
        cnt_s[...] = jnp.zeros_like(cnt_s)

    onehot = jnp.zeros(logits.shape, F32)
    for pick in picks:
        onehot = onehot + jnp.where(pick, 1.0, 0.0)
    n = logits.shape[0]
    earlier = (lax.broadcasted_iota(jnp.int32, (n, n), 1) < lax.broadcasted_iota(jnp.int32, (n, n), 0))
    before = cnt_s[...] + _dot(jnp.where(earlier, 1.0, 0.0).astype(BF16), onehot.astype(BF16))
    rank = jnp.zeros(logits.shape, jnp.int32)
    for kk, pick in enumerate(picks):
        r_k = jnp.sum(jnp.where(pick, before, 0.0), axis=-1, keepdims=True)
        rank = jnp.where(lane == kk, r_k.astype(jnp.int32), rank)
    rank_ref[...] = rank
    cnt_s[...] = cnt_s[...] + jnp.sum(onehot, axis=0, keepdims=True)
    cnt_ref[...] = cnt_s[...]


def _outproj_router(a, b, x, mod, w_out, wr, br):
    t = N_TILES * TM
    d = x.a.shape[1]
    wid = a.a.shape[1]
    row = lambda w: pl.BlockSpec((TM, w), lambda i: (i, 0))
    const = lambda r, c: pl.BlockSpec((r, c), lambda i: (0, 0))
    wr_hi, wr_lo = _split_bf16(wr)
    return pl.pallas_call(
        _outproj_router_kernel, grid=(N_TILES,),
        in_specs=a.specs() + b.specs() + x.specs() + [
                  pl.BlockSpec((None, 6, d), lambda i: (_cond_row(i), 0, 0)),
                  pl.BlockSpec((wid, d), lambda i: (0, 0)), pl.BlockSpec((wid, d), lambda i: (1, 0)),
                  const(d, LANES), const(d, LANES), const(1, LANES)],
        out_specs=[row(d), pl.BlockSpec((TM * SUB, LANES), lambda i: (i, 0)), row(LANES), row(LANES), row(LANES),
                   const(1, LANES)],
        out_shape=[jax.ShapeDtypeStruct((t, d), F32), jax.ShapeDtypeStruct((t * SUB, LANES), F32),
                   jax.ShapeDtypeStruct((t, LANES), jnp.int32), jax.ShapeDtypeStruct((t, LANES), F32),
                   jax.ShapeDtypeStruct((t, LANES), jnp.int32), jax.ShapeDtypeStruct((1, LANES), F32)],
        scratch_shapes=[pltpu.VMEM((1, LANES), F32)],
        compiler_params=_cparams("arbitrary"), name="outproj_router",
    )(a.a, a.b, b.a, b.b, x.a, x.b, mod, w_out, w_out, wr_hi, wr_lo, br)


N_TILES_MAX = N_TILES * TM * TOP_K // TM_E + N_EXP
PLAN_LANES = 2 * LANES
assert N_TILES_MAX <= PLAN_LANES and N_EXP <= LANES


def _plan_kernel(cnt_ref, ti_ref, rank_ref, dest_ref, meta_ref, start_s):
    i = pl.program_id(0)

    @pl.when(i == 0)
    def _():
        cnt = cnt_ref[...]
        tiles = jnp.floor((cnt + float(TM_E - 1)) * (1.0 / TM_E))
        sub = lax.broadcasted_iota(jnp.int32, (LANES, LANES), 0)
        lane = lax.broadcasted_iota(jnp.int32, (LANES, LANES), 1)
        upto = jnp.where(sub <= lane, 1.0, 0.0).astype(BF16)
        tile_end = _dot(jnp.broadcast_to(tiles, (SUB, LANES)).astype(BF16), upto)[0:1, :]
        tile_start = tile_end - tiles
        start_s[...] = tile_start * float(TM_E * SUB)
        n_tiles = jnp.max(tile_end, axis=-1, keepdims=True)
        used = tiles > 0.0

        def column(row):
            return jnp.sum(jnp.where(sub == lane, jnp.broadcast_to(row, (LANES, LANES)), 0.0), axis=-1, keepdims=True)

        end_c, start_c, tiles_c = column(tile_end), column(tile_start), column(tiles)
        used_b = jnp.broadcast_to(jnp.where(used, 1.0, 0.0), (LANES, LANES))
        pos_c = jnp.sum(jnp.where(lane <= sub, used_b, 0.0), axis=-1, keepdims=True)
        par_c = (pos_c - 1.0) - 2.0 * jnp.floor((pos_c - 1.0) * 0.5)
        nxt_c = jnp.min(jnp.where(jnp.logical_and(lane > sub, used_b > 0.0), lane.astype(F32), float(LANES)),
                        axis=-1, keepdims=True)
        nxt_c = jnp.where(nxt_c < float(LANES), nxt_c, -1.0)

        tid = lax.broadcasted_iota(jnp.int32, (LANES, PLAN_LANES), 1).astype(F32)
        exp_id = lax.broadcasted_iota(jnp.int32, (LANES, PLAN_LANES), 0)
        tid_used = jnp.minimum(tid, n_tiles - 1.0)
        te = jnp.sum(jnp.where(jnp.logical_and(exp_id < N_EXP, end_c <= tid_used), 1.0, 0.0), axis=0, keepdims=True)
        first = jnp.sum(jnp.where(jnp.logical_and(tiles_c > 0.0, start_c == tid), 1.0, 0.0), axis=0, keepdims=True)
        mine = te == exp_id.astype(F32)
        nxt = jnp.sum(jnp.where(mine, nxt_c, 0.0), axis=0, keepdims=True)
        par = jnp.sum(jnp.where(mine, par_c, 0.0), axis=0, keepdims=True)
        last = jnp.where(used, tile_end - 1.0, 0.0)
        last = jnp.concatenate([last, jnp.zeros((1, PLAN_LANES - LANES), F32)], axis=1)
        rows_used = (jnp.sum(jnp.where(mine, column(cnt), 0.0), axis=0, keepdims=True)
                     - float(TM_E) * (tid[0:1, :] - jnp.sum(jnp.where(mine, start_c, 0.0), axis=0, keepdims=True)))
        half = jnp.where(rows_used <= float(TM_E // 2), 1.0, 0.0)

        row_id = lax.broadcasted_iota(jnp.int32, (SUB, PLAN_LANES), 0)
        meta = jnp.zeros((SUB, PLAN_LANES), F32)
        for r, val in enumerate((te, first, nxt, par, last, jnp.broadcast_to(n_tiles, (1, PLAN_LANES)), half)):
            meta = jnp.where(row_id == r, jnp.broadcast_to(val, (SUB, PLAN_LANES)), meta)
        meta_ref[...] = meta.astype(jnp.int32)

    ti = ti_ref[...]
    rank = rank_ref[...]
    lane = lax.broadcasted_iota(jnp.int32, ti.shape, 1)
    dest = jnp.zeros(ti.shape, jnp.int32)
    for kk in range(TOP_K):
        base = jnp.sum(jnp.where(lane == ti[:, kk:kk + 1], start_s[...], 0.0), axis=-1, keepdims=True)
        dest = jnp.where(lane == kk, base.astype(jnp.int32) + rank[:, kk:kk + 1] * SUB, dest)
    dest_ref[...] = dest.T[0:SUB, :]


def _route_plan(cnt, ti, rank):
    t = ti.shape[0]
    row = pl.BlockSpec((TM, LANES), lambda i: (i, 0))
    dest, meta = pl.pallas_call(
        _plan_kernel, grid=(t // TM,),
        in_specs=[pl.BlockSpec((1, LANES), lambda i: (0, 0)), row, row],
        out_specs=[pl.BlockSpec((SUB, TM), lambda i: (0, i)), pl.BlockSpec((SUB, PLAN_LANES), lambda i: (0, 0))],
        out_shape=[jax.ShapeDtypeStruct((SUB, t), jnp.int32), jax.ShapeDtypeStruct((SUB, PLAN_LANES), jnp.int32)],
        scratch_shapes=[pltpu.VMEM((1, LANES), F32)],
        compiler_params=_cparams("arbitrary"), name="moe_plan",
    )(cnt, ti, rank)
    plan = dict(tile_expert=meta[0, :N_TILES_MAX], first=meta[1, :N_TILES_MAX], tile_next=meta[2, :N_TILES_MAX],
                tile_parity=meta[3, :N_TILES_MAX], last_tile=meta[4, :N_EXP], n_tiles=meta[5, :1],
                tile_half=meta[6, :N_TILES_MAX])
    return dest, plan


DISPATCH_BLK = 1024


def _dispatch_kernel(lt_ref, nt_ref, *rest):
    dest_refs, (h_ref, xs_ref, h_s, zero_s, sem, hsem) = rest[:TOP_K], rest[TOP_K:]
    i = pl.program_id(0)
    tile_rows = TM_E * SUB
    n_tiles_max = xs_ref.shape[0] // tile_rows
    n_tok = h_s.shape[0] // SUB

    blk_rows = DISPATCH_BLK * SUB

    def stage(blk):
        r0 = pl.multiple_of(blk * blk_rows, blk_rows)
        return pltpu.make_async_copy(h_ref.at[pl.ds(r0, blk_rows), :], h_s.at[pl.ds(r0, blk_rows), :], hsem.at[blk])

    @pl.when(i == 0)
    def _():
        for blk in range(n_tok // DISPATCH_BLK):
            stage(blk).start()
        zero_s[...] = jnp.zeros_like(zero_s)

        def zero_tile(tile):
            r0 = pl.multiple_of(tile * tile_rows, tile_rows)
            return pltpu.make_async_copy(zero_s, xs_ref.at[pl.ds(r0, tile_rows), :], sem)

        def start_unused(j, carry):
            zero_tile(j).start()
            return carry

        def wait_unused(j, carry):
            zero_tile(j).wait()
            return carry

        for e in range(N_EXP):
            zero_tile(lt_ref[e]).start()
        lax.fori_loop(nt_ref[0], n_tiles_max, start_unused, 0)
        for e in range(N_EXP):
            zero_tile(lt_ref[e]).wait()
        lax.fori_loop(nt_ref[0], n_tiles_max, wait_unused, 0)

    stage(i).wait()
    base = i * DISPATCH_BLK

    def issue(t, carry):
        src = pl.multiple_of((base + t) * SUB, SUB)
        for kk in range(TOP_K):
            row = pl.multiple_of(dest_refs[kk][0, t], SUB)
            pltpu.make_async_copy(h_s.at[pl.ds(src, SUB), :], xs_ref.at[pl.ds(row, SUB), :], sem).start(priority=kk % 2)
        return carry

    lax.fori_loop(0, DISPATCH_BLK, issue, 0, unroll=2)

    @pl.when(i == pl.num_programs(0) - 1)
    def _():
        for kk in range(TOP_K):
            pltpu.make_async_copy(h_s, xs_ref.at[pl.ds(0, n_tok * SUB), :], sem).wait()


def _dispatch(h2, dest, plan, n_rows):
    t = h2.shape[0] // SUB
    nblk = t // DISPATCH_BLK
    return pl.pallas_call(
        _dispatch_kernel,
        grid_spec=pltpu.PrefetchScalarGridSpec(
            num_scalar_prefetch=2, grid=(nblk,),
            in_specs=[pl.BlockSpec((None, 1, DISPATCH_BLK), lambda i, lt, nt: (i, 0, 0), memory_space=pltpu.SMEM)] * TOP_K
                     + [pl.BlockSpec(memory_space=pl.ANY)],
            out_specs=pl.BlockSpec(memory_space=pl.ANY),
            scratch_shapes=[pltpu.VMEM((t * SUB, LANES), F32), pltpu.VMEM((TM_E * SUB, LANES), F32),
                            pltpu.SemaphoreType.DMA(()), pltpu.SemaphoreType.DMA((nblk,))]),
        out_shape=jax.ShapeDtypeStruct((n_rows * SUB, LANES), F32),
        compiler_params=_cparams("arbitrary"), name="moe_dispatch",
    )(plan["last_tile"], plan["n_tiles"], *[dest[kk].reshape(nblk, 1, DISPATCH_BLK) for kk in range(TOP_K)], h2)


def _expert_kernel(te_ref, tf_ref, nt_ref, nx_ref, par_ref, half_ref, xs_ref, w1_ref, b1_ref, w2_ref, b2_ref, ys_ref,
                   w1f, w2f, w1_s, w2_s, wsem, *, layer):
    i = pl.program_id(0)

    def fetch(expert, slot):
        return (pltpu.make_async_copy(w1_ref.at[layer, expert], w1f.at[slot], wsem.at[slot]),
                pltpu.make_async_copy(w2_ref.at[layer, expert], w2f.at[slot], wsem.at[slot]))

    @pl.when(i == 0)
    def _():
        for cp in fetch(te_ref[0], 0):
            cp.start(priority=1)

    @pl.when(i < nt_ref[0])
    def _():
        @pl.when(tf_ref[i] == 1)
        def _():
            slot = par_ref[i]
            for cp in fetch(te_ref[i], slot):
                cp.wait()
            w1_s[...] = w1f[slot].astype(BF16)
            w2_s[...] = w2f[slot].astype(BF16)

            @pl.when(nx_ref[i] >= 0)
            def _():
                for cp in fetch(nx_ref[i], 1 - slot):
                    cp.start(priority=1)

        def expert_rows(n):
            x = _load_token_major(xs_ref, (), n)
            u = _dot(x.astype(BF16), w1_s[...]) + b1_ref[...]
            g = jnp.minimum(u[:, :D_FF], SWIGLU_LIMIT)
            up = jnp.clip(u[:, D_FF:], -SWIGLU_LIMIT, SWIGLU_LIMIT)
            act = (up + 1.0) * g * jax.nn.sigmoid(SWIGLU_ALPHA * g)
            _store_token_major(ys_ref, (), _dot(act.astype(BF16), w2_s[...]) + b2_ref[...])

        @pl.when(half_ref[i] == 0)
        def _():
            expert_rows(TM_E)

        @pl.when(half_ref[i] == 1)
        def _():
            expert_rows(TM_E // 2)
            ys_ref[pl.ds(TM_E // 2 * SUB, TM_E // 2 * SUB), :] = jnp.zeros((TM_E // 2 * SUB, LANES), F32)

    @pl.when(i >= nt_ref[0])
    def _():
        ys_ref[...] = jnp.zeros_like(ys_ref)


def _experts(xs, plan, layer, w1, b1, w2, b2):
    d = D_MODEL
    nt = xs.shape[0] // (TM_E * SUB)
    tile = lambda i, te, tf, ntl, nx, par, hf: (jnp.minimum(i, ntl[0] - 1), 0)
    otile = lambda i, te, tf, ntl, nx, par, hf: (i, 0)
    bmap = lambda i, te, tf, ntl, nx, par, hf: (layer, te[i], 0, 0)
    return pl.pallas_call(
        functools.partial(_expert_kernel, layer=layer),
        grid_spec=pltpu.PrefetchScalarGridSpec(
            num_scalar_prefetch=6, grid=(nt,),
            in_specs=[pl.BlockSpec((TM_E * SUB, LANES), tile),
                      pl.BlockSpec(memory_space=pl.ANY),
                      pl.BlockSpec((None, None, 1, 2 * D_FF), bmap),
                      pl.BlockSpec(memory_space=pl.ANY),
                      pl.BlockSpec((None, None, 1, d), bmap)],
            out_specs=pl.BlockSpec((TM_E * SUB, LANES), otile),
            scratch_shapes=[pltpu.VMEM((2, d, 2 * D_FF), F32), pltpu.VMEM((2, D_FF, d), F32),
                            pltpu.VMEM((d, 2 * D_FF), BF16), pltpu.VMEM((D_FF, d), BF16),
                            pltpu.SemaphoreType.DMA((2,))]),
        out_shape=jax.ShapeDtypeStruct(xs.shape, F32),
        compiler_params=_cparams("arbitrary"), name="moe_experts",
    )(plan["tile_expert"], plan["first"], plan["n_tiles"], plan["tile_next"], plan["tile_parity"],
      plan["tile_half"], xs, w1, b1, w2, b2)


def _combine_kernel(*refs):
    dest_refs = refs[:TOP_K]
    x1_ref, tw_ref, mod_ref, ys_ref, *o_refs, buf, sem = refs[TOP_K:]
    i = pl.program_id(0)
    j = i - 1
    n = pl.num_programs(0) - 1
    tile_rows = TM * SUB

    for s in range(2):
        @pl.when(jnp.logical_and(i < n, i % 2 == s))
        def _():
            def issue(t, carry):
                dst = pl.multiple_of(t * SUB, SUB)
                for kk in range(TOP_K):
                    row = pl.multiple_of(dest_refs[kk][0, t], SUB)
                    pltpu.make_async_copy(ys_ref.at[pl.ds(row, SUB), :], buf.at[s, kk, pl.ds(dst, SUB), :],
                                          sem.at[s]).start(priority=kk % 2)
                return carry

            lax.fori_loop(0, TM, issue, 0, unroll=2)

    @pl.when(j >= 0)
    def _():
        slot = j % 2
        for kk in range(TOP_K):
            pltpu.make_async_copy(ys_ref.at[pl.ds(0, tile_rows), :], buf.at[slot, kk], sem.at[slot]).wait()
        tw = tw_ref[...]
        y = tw[:, 0:1] * _load_token_major(buf, (slot, 0), TM)
        for kk in range(1, TOP_K):
            y = y + tw[:, kk:kk + 1] * _load_token_major(buf, (slot, kk), TM)
        out = x1_ref[...] + mod_ref[5:6, :] * y
        if len(o_refs) == 1:
            o_refs[0][...] = out
        else:
            @pl.when(j < N_CTX_TILES)
            def _():
                o_refs[0][...] = out

            @pl.when(j >= N_CTX_TILES)
            def _():
                o_refs[1][...] = out


def _combine(ys, dest, x1, tw, mod, split):
    t, d = x1.shape
    nblk = t // TM
    prev = lambda i: jnp.maximum(i - 1, 0)
    row = lambda w: pl.BlockSpec((TM, w), lambda i: (prev(i), 0))
    return pl.pallas_call(
        _combine_kernel, grid=(nblk + 1,),
        in_specs=[pl.BlockSpec((None, 1, TM), lambda i: (jnp.minimum(i, nblk - 1), 0, 0), memory_space=pltpu.SMEM)] * TOP_K + [
                  row(d), row(LANES),
                  pl.BlockSpec((None, 6, d), lambda i: (_cond_row(prev(i)), 0, 0)),
                  pl.BlockSpec(memory_space=pl.ANY)],
        out_specs=_Rows(x1, x1, 0).specs(lag=1) if split else row(d),
        out_shape=([jax.ShapeDtypeStruct((N_CTX_TILES * TM, d), F32),
                    jax.ShapeDtypeStruct(((N_TILES - N_CTX_TILES) * TM, d), F32)] if split
                   else jax.ShapeDtypeStruct((t, d), F32)),
        scratch_shapes=[pltpu.VMEM((2, TOP_K, TM * SUB, LANES), F32), pltpu.SemaphoreType.DMA((2,))],
        compiler_params=_cparams("arbitrary"), name="moe_combine",
    )(*[dest[kk].reshape(nblk, 1, TM) for kk in range(TOP_K)], x1, tw, mod, ys)


def _moe(x1, h2, ti, tw, rank, cnt, mod, layer, w1, b1, w2, b2, split):
    depth = w1.shape[0]
    assert x1.shape[0] == N_TILES * TM
    dest, plan = _route_plan(cnt, ti, rank)
    xs = _dispatch(h2, dest, plan, N_TILES_MAX * TM_E)
    ys = _experts(xs, plan, layer, w1, b1.reshape(depth, N_EXP, 1, -1), w2, b2.reshape(depth, N_EXP, 1, -1))
    return _combine(ys, dest, x1, tw, mod, split)


def _pad_lanes(a, value=0.0):
    return jnp.pad(a, ((0, 0), (0, LANES - a.shape[1])), constant_values=value)


def kernel(x_prompt, x_sample, cache_na_k, cache_na_v, state_mlstm_C, state_mlstm_n, state_mlstm_m, state_ret_S, c, c_ctx, w_mod, b_mod, w_in_even, mlstm_gate_b, na_q_norm, na_k_norm, na_rpb, mlstm_norm, w_out_even, w_in_odd, ret_decay, ret_norm, w_out_odd, w_router, b_router, w_moe_in, b_moe_in, w_moe_out, b_moe_out):
    nb_c, s_c, d = x_prompt.shape
    nb_l, s_l, _ = x_sample.shape
    t_c = nb_c * s_c
    t_l = nb_l * s_l
    assert t_c == 4 * SEG and s_l == SEG and d == D_MODEL
    depth = w_mod.shape[0]
    dt = x_prompt.dtype

    x = _Rows(x_prompt.reshape(t_c, d), x_sample.reshape(t_l, d), 0)
    cond = jnp.concatenate([c_ctx[None, :], c, jnp.zeros((N_COND - 1 - nb_l, d), F32)], axis=0)
    mod = _modulation(cond, w_mod, b_mod).reshape(depth, N_COND, 6, d)

    outs = {}
    for l in range(depth):
        e = l // 2
        mod_l = mod[l]
        if l % 2 == 0:
            w_in = w_in_even[e]
            n_main = 3 * W_A + 4 * W_B
            wg = _pad_lanes(w_in[:, n_main:])
            bg = _pad_lanes(mlstm_gate_b[e].reshape(1, 4 * H_B))
            z, g = _inproj(x, mod_l, w_in.astype(BF16), n_main, wg, bg)
            qn = jnp.tile(na_q_norm[e].reshape(1, HD_A), (1, H_A))
            kn = jnp.tile(na_k_norm[e].reshape(1, HD_A), (1, H_A))
            oa_c, ka_c, va_c = _ctx_attention(z, nb_c, s_c, qn, kn)
            past = cache_na_k.shape[2]
            oa_l = _na_attention(z, t_c, nb_l, s_l,
                                 cache_na_k[:, e].reshape(nb_l, past, W_A), cache_na_v[:, e].reshape(nb_l, past, W_A),
                                 _na_bias_pairs(na_rpb[e]), qn, kn)
            nw = mlstm_norm[e].reshape(1, W_B)
            hm_c, c_fin, n_fin, m_fin = _mlstm(z, g, 0, nb_c, s_c, nw)
            init = (state_mlstm_C[:, e].reshape(nb_l, 2 * H_B, HD_B, HD_B),
                    state_mlstm_n[:, e].reshape(nb_l, 2 * H_B, HD_B),
                    jnp.broadcast_to(state_mlstm_m[:, e].reshape(nb_l, 2 * H_B, 1), (nb_l, 2 * H_B, LANES)))
            hm_l = _mlstm(z, g, t_c, nb_l, s_l, nw, init)[0]
            a = _Rows(oa_c, oa_l, 0)
            b = _Rows(hm_c, hm_l, 0)
            w_out = w_out_even[e].astype(BF16)
            outs.setdefault("na_k", []).append(ka_c.reshape(nb_c, s_c, H_A, HD_A))
            outs.setdefault("na_v", []).append(va_c.reshape(nb_c, s_c, H_A, HD_A))
            outs.setdefault("C", []).append(c_fin.reshape(nb_c, 2, H_B, HD_B, HD_B))
            outs.setdefault("n", []).append(n_fin.reshape(nb_c, 2, H_B, HD_B))
            outs.setdefault("m", []).append(m_fin[:, :, 0].reshape(nb_c, 2, H_B))
        else:
            w_c = H_C * HD_C
            z = _inproj(x, mod_l, w_in_odd[e].astype(BF16), 4 * w_c + N_FG * FG_W)
            dl_rep = jnp.broadcast_to(ret_decay[e].reshape(2 * H_C, 1), (2 * H_C, LANES))
            nw = ret_norm[e].reshape(1, w_c)
            hr_c, s_fin = _retention(z, 0, nb_c, s_c, dl_rep, nw)
            hr_l = _retention(z, t_c, nb_l, s_l, dl_rep, nw, rope=_rope_tables(s_l),
                              init=state_ret_S[:, e].reshape(nb_l, 2 * H_C, HD_C, HD_C))[0]
            fd_c = _fnet(z, 0, nb_c, s_c)
            fd_l = _fnet(z, t_c, nb_l, s_l)
            a = _Rows(hr_c, hr_l, 0)
            b = _Rows(fd_c, fd_l, 0)
            w_out = w_out_odd[e].astype(BF16)
            outs.setdefault("S", []).append(s_fin.reshape(nb_c, 2, H_C, HD_C, HD_C))
        wr = _pad_lanes(w_router[l])
        br = _pad_lanes(b_router[l].reshape(1, N_EXP), NEG)
        x1, h2, ti, tw, rank, cnt = _outproj_router(a, b, x, mod_l, w_out, wr, br)
        last = l == depth - 1
        x = _moe(x1, h2, ti, tw, rank, cnt, mod_l, l, w_moe_in, b_moe_in, w_moe_out, b_moe_out, split=last)
        if not last:
            x = _whole(x)

    y_prompt = x[0].reshape(nb_c, s_c, d)
    y_sample = x[1].reshape(nb_l, s_l, d)
    stack = lambda key: jnp.stack(outs[key], axis=1).astype(dt)
    return (y_prompt, y_sample, stack("na_k"), stack("na_v"), stack("C"), stack("n"), stack("m"), stack("S"))
```
